```python
import math, functools
import jax, jax.numpy as jnp
from jax import lax
import numpy as np

D_MODEL = 1024
BATCH = 8
SEQ = 2048
DEPTH = 1
DEC_BATCH = 128
DEC_SEQ = 1
PAST_LEN = 8192
PAGE_SIZE = 128

GDN_HEADS = 8
GDN_DK = 128
GDN_DV = 128
CONV_WIDTH = 4
GDN_CHUNK = 64
GDN_QK_W = GDN_HEADS * GDN_DK
GDN_V_W = GDN_HEADS * GDN_DV
CONV_DIM = 2 * GDN_QK_W + GDN_V_W
MLA_HEADS = 8
Q_LORA = 512
KV_LORA = 512
QK_NOPE = 128
QK_ROPE = 64
V_HEAD = 128
MLA_V_W = MLA_HEADS * V_HEAD
ROPE_THETA = 10000.0
Q_BLOCK = 128
ATTN_SCALE = (QK_NOPE + QK_ROPE) ** -0.5
N_EXPERTS = 32
TOP_K = 4
D_EXPERT = 1024
SWIGLU_LIMIT = 7.0
SWIGLU_ALPHA = 1.702
EXPERT_BLOCK = 128
NORM_EPS = 1e-6
IN_SIZES = (CONV_DIM, GDN_V_W, GDN_HEADS, GDN_HEADS, Q_LORA, KV_LORA, QK_ROPE, D_MODEL, D_MODEL)
IN_COLS = CONV_DIM + GDN_V_W + 2 * GDN_HEADS + Q_LORA + KV_LORA + QK_ROPE + 2 * D_MODEL

kernel_name = 'hybrid_gdn_mla_moe_adaln_step'


def rmsnorm(x, w):
    xf = x.astype(jnp.float32)
    y = xf * lax.rsqrt(jnp.mean(xf * xf, axis=-1, keepdims=True) + NORM_EPS)
    return (y * w.astype(jnp.float32)).astype(x.dtype)


def l2norm(x):
    xf = x.astype(jnp.float32)
    return xf * lax.rsqrt(jnp.sum(xf * xf, axis=-1, keepdims=True) + NORM_EPS)


def rope(x, pos):
    half = QK_ROPE // 2
    inv = ROPE_THETA ** (-jnp.arange(half, dtype=jnp.float32) / half)
    ang = pos.astype(jnp.float32)[:, None] * inv[None, :]
    cos = jnp.cos(ang)[:, None, :]
    sin = jnp.sin(ang)[:, None, :]
    x1 = x[..., :half].astype(jnp.float32)
    x2 = x[..., half:].astype(jnp.float32)
    return jnp.concatenate([x1 * cos - x2 * sin, x1 * sin + x2 * cos], axis=-1).astype(x.dtype)


def short_conv(u, buf, w):
    T = u.shape[1]
    xp = jnp.concatenate([buf.astype(u.dtype), u], axis=1)
    y = xp[:, 0:T] * w[:, 0]
    for j in range(1, CONV_WIDTH):
        y = y + xp[:, j:j + T] * w[:, j]
    return jax.nn.silu(y), xp[:, -(CONV_WIDTH - 1):]


def delta_rule_chunked(q, k, v, beta, g, S0):
    Bn, T, H, _ = q.shape
    nc = T // GDN_CHUNK

    def blocks(x):
        x = x.astype(jnp.float32).reshape((Bn, nc, GDN_CHUNK, H) + x.shape[3:])
        return jnp.moveaxis(x, (1, 3), (0, 2))

    qc, kc, vc, bc, gc = blocks(q), blocks(k), blocks(v), blocks(beta), blocks(g)
    gam = jnp.cumsum(gc, axis=-1)
    causal = jnp.tril(jnp.ones((GDN_CHUNK, GDN_CHUNK), bool))
    strict = jnp.tril(jnp.ones((GDN_CHUNK, GDN_CHUNK), bool), -1)
    decay = jnp.exp(jnp.where(causal, gam[..., :, None] - gam[..., None, :], -jnp.inf))
    kb = kc * bc[..., None]
    a_kk = jnp.where(strict, jnp.einsum('nbhik,nbhjk->nbhij', kb, kc) * decay, 0.0)
    t_mat = a_kk + jnp.eye(GDN_CHUNK, dtype=jnp.float32)

    def solve(rhs):
        return lax.linalg.triangular_solve(t_mat, rhs, left_side=True, lower=True, unit_diagonal=True)

    u = solve(vc * bc[..., None])
    w = solve(kb * jnp.exp(gam)[..., None])
    a_qk = jnp.einsum('nbhik,nbhjk->nbhij', qc, kc) * decay

    def step(S, xs):
        q_i, k_i, u_i, w_i, gam_i, aqk_i = xs
        v_new = u_i - jnp.einsum('bhck,bhkv->bhcv', w_i, S)
        o = (jnp.einsum('bhck,bhkv->bhcv', q_i * jnp.exp(gam_i)[..., None], S)
             + jnp.einsum('bhij,bhjv->bhiv', aqk_i, v_new))
        g_last = gam_i[..., -1:]
        S = (S * jnp.exp(g_last)[..., None]
             + jnp.einsum('bhck,bhcv->bhkv', k_i * jnp.exp(g_last - gam_i)[..., None], v_new))
        return S, o

    S, o = lax.scan(step, S0.astype(jnp.float32), (qc, kc, u, w, gam, a_qk))
    o = jnp.moveaxis(o, (0, 2), (1, 3)).reshape(Bn, T, H, v.shape[-1])
    return o, S


def delta_rule_recurrent(q, k, v, beta, g, S0):
    xs = tuple(jnp.swapaxes(t.astype(jnp.float32), 0, 1) for t in (q, k, v, beta, g))

    def step(S, inp):
        qt, kt, vt, bt, gt = inp
        S = S * jnp.exp(gt)[..., None, None]
        vn = bt[..., None] * (vt - jnp.einsum('bhk,bhkv->bhv', kt, S))
        S = S + jnp.einsum('bhk,bhv->bhkv', kt, vn)
        return S, jnp.einsum('bhk,bhkv->bhv', qt, S)

    S, o = lax.scan(step, S0.astype(jnp.float32), xs)
    return jnp.swapaxes(o, 0, 1), S


def mla_causal_attention(q_lat, q_pe, lat, kpe):
    Bn, T, H, _ = q_lat.shape
    nq = T // Q_BLOCK
    qb = jnp.swapaxes(q_lat.reshape(Bn, nq, Q_BLOCK, H, KV_LORA), 0, 1)
    pb = jnp.swapaxes(q_pe.reshape(Bn, nq, Q_BLOCK, H, QK_ROPE), 0, 1)
    kpos = jnp.arange(T)

    def one_block(args):
        ql, qp, i = args
        s = (jnp.einsum('bqhc,bkc->bhqk', ql, lat)
             + jnp.einsum('bqhr,bkr->bhqk', qp, kpe)).astype(jnp.float32) * ATTN_SCALE
        qpos = i * Q_BLOCK + jnp.arange(Q_BLOCK)
        s = jnp.where(kpos[None, :] <= qpos[:, None], s, -jnp.inf)
        p = jax.nn.softmax(s, axis=-1).astype(lat.dtype)
        return jnp.einsum('bhqk,bkc->bqhc', p, lat)

    o = lax.map(one_block, (qb, pb, jnp.arange(nq)))
    return jnp.swapaxes(o, 0, 1).reshape(Bn, T, H, KV_LORA)


def mla_paged_attention(q_lat, q_pe, lat, kpe, cache_kv, cache_pe, page_table, layer):
    Bn, T, H, _ = q_lat.shape
    qf = q_lat.astype(jnp.float32)
    pf = q_pe.astype(jnp.float32)

    def scores(kv, pe):
        return (jnp.einsum('bthc,bpc->bthp', qf, kv.astype(jnp.float32))
                + jnp.einsum('bthr,bpr->bthp', pf, pe.astype(jnp.float32))) * ATTN_SCALE

    def update(carry, s, kv):
        m, l, acc = carry
        m_new = jnp.maximum(m, jnp.max(s, axis=-1))
        corr = jnp.exp(m - m_new)
        p = jnp.exp(s - m_new[..., None])
        acc = acc * corr[..., None] + jnp.einsum('bthp,bpc->bthc', p, kv.astype(jnp.float32))
        return (m_new, l * corr + jnp.sum(p, axis=-1), acc)

    def page_step(carry, phys):
        kv = cache_kv[layer, phys]
        pe = cache_pe[layer, phys]
        return update(carry, scores(kv, pe), kv), None

    init = (jnp.full((Bn, T, H), -jnp.inf, jnp.float32), jnp.zeros((Bn, T, H), jnp.float32),
            jnp.zeros((Bn, T, H, KV_LORA), jnp.float32))
    carry, _ = lax.scan(page_step, init, jnp.swapaxes(page_table, 0, 1))
    causal = jnp.tril(jnp.ones((T, T), bool))
    s_new = jnp.where(causal[None, :, None, :], scores(lat, kpe), -jnp.inf)
    m, l, acc = update(carry, s_new, lat)
    return (acc / l[..., None]).astype(q_lat.dtype)


def gdn_branch(qkv, z, b, a, conv_buf, ssm_state, lw, chunked):
    Bn, T, _ = qkv.shape
    qkv, new_buf = short_conv(qkv, conv_buf, lw['conv_w'])
    q, k, v = jnp.split(qkv, [GDN_QK_W, 2 * GDN_QK_W], axis=-1)
    q = l2norm(q.reshape(Bn, T, GDN_HEADS, GDN_DK)) * GDN_DK ** -0.5
    k = l2norm(k.reshape(Bn, T, GDN_HEADS, GDN_DK))
    v = v.reshape(Bn, T, GDN_HEADS, GDN_DV)
    beta = jax.nn.sigmoid(b.astype(jnp.float32))
    g = -jnp.exp(lw['A_log'].astype(jnp.float32)) * jax.nn.softplus(a.astype(jnp.float32) + lw['dt_bias'].astype(jnp.float32))
    if chunked:
        o, S = delta_rule_chunked(q, k, v, beta, g, ssm_state)
    else:
        o, S = delta_rule_recurrent(q, k, v, beta, g, ssm_state)
    o = rmsnorm(o.astype(z.dtype), lw['gdn_norm_w']) * jax.nn.silu(z.reshape(Bn, T, GDN_HEADS, GDN_DV))
    return o.reshape(Bn, T, GDN_V_W) @ lw['w_o_gdn'], new_buf, S


def mla_branch(cq, ckv, kpe, pos, lw, attend):
    Bn, T, _ = cq.shape
    q = (rmsnorm(cq, lw['q_norm_w']) @ lw['w_uq']).reshape(Bn, T, MLA_HEADS, QK_NOPE + QK_ROPE)
    q_nope = q[..., :QK_NOPE]
    q_pe = rope(q[..., QK_NOPE:], pos)
    lat = rmsnorm(ckv, lw['kv_norm_w'])
    kpe = rope(kpe[:, :, None, :], pos)[:, :, 0, :]
    q_lat = jnp.einsum('bthn,hcn->bthc', q_nope, lw['w_uk'])
    o_lat = attend(q_lat, q_pe, lat, kpe)
    o = jnp.einsum('bthc,hcv->bthv', o_lat, lw['w_uv'])
    return o.reshape(Bn, T, MLA_V_W) @ lw['w_o_mla'], lat, kpe


def moe_ffn(h, lw):
    N = h.shape[0]
    M = N * TOP_K
    logits = h.astype(jnp.float32) @ lw['w_router'].astype(jnp.float32) + lw['b_router'].astype(jnp.float32)
    top_v, top_e = lax.top_k(logits, TOP_K)
    gate = jax.nn.softmax(top_v, axis=-1).astype(h.dtype)
    flat_e = top_e.reshape(M)
    order = jnp.argsort(flat_e)
    e_sorted = flat_e[order]
    tok_sorted = order // TOP_K
    counts = jnp.bincount(flat_e, length=N_EXPERTS)
    padded = (counts + EXPERT_BLOCK - 1) // EXPERT_BLOCK * EXPERT_BLOCK
    pad_end = jnp.cumsum(padded)
    pad_start = pad_end - padded
    grp_start = jnp.cumsum(counts) - counts
    slot = pad_start[e_sorted] + jnp.arange(M) - grp_start[e_sorted]
    n_blocks = -(-M // EXPERT_BLOCK) + N_EXPERTS
    n_slots = n_blocks * EXPERT_BLOCK
    slot_tok = jnp.full((n_slots,), N, jnp.int32).at[slot].set(tok_sorted.astype(jnp.int32))
    h_pad = jnp.concatenate([h, jnp.zeros((1, h.shape[1]), h.dtype)], axis=0)[slot_tok]
    h_pad = h_pad.reshape(n_blocks, EXPERT_BLOCK, h.shape[1])
    blk_e = jnp.minimum(jnp.searchsorted(pad_end, jnp.arange(n_blocks) * EXPERT_BLOCK, side='right'), N_EXPERTS - 1)

    def run_block(args):
        xb, e = args
        gu = xb @ lw['w_gu'][e] + lw['b_gu'][e]
        gt = jnp.minimum(gu[:, :D_EXPERT], SWIGLU_LIMIT)
        up = jnp.clip(gu[:, D_EXPERT:], -SWIGLU_LIMIT, SWIGLU_LIMIT)
        act = (up + 1.0) * gt * jax.nn.sigmoid(SWIGLU_ALPHA * gt)
        return act @ lw['w_dn'][e] + lw['b_dn'][e]

    y_pad = lax.map(run_block, (h_pad, blk_e)).reshape(n_slots, h.shape[1])
    y_sorted = y_pad[slot] * gate.reshape(M)[order][:, None]
    return jax.ops.segment_sum(y_sorted, tok_sorted, num_segments=N)


def decoder_layer(x, c, pos, conv_buf, ssm_state, lw, chunked, attend):
    Bn, T, D = x.shape
    mod = (c @ lw['w_ada'] + lw['b_ada']).reshape(Bn, 6, D)
    sh1, sc1, g1, sh2, sc2, g2 = [mod[:, i, None, :] for i in range(6)]
    h = rmsnorm(x, lw['norm_mix_w']) * (1.0 + sc1) + sh1
    proj = h @ lw['w_in']
    offsets = [int(o) for o in np.cumsum(IN_SIZES)[:-1]]
    qkv, z, b, a, cq, ckv, kpe, gate_a, gate_b = jnp.split(proj, offsets, axis=-1)
    y_a, new_buf, new_ssm = gdn_branch(qkv, z, b, a, conv_buf, ssm_state, lw, chunked)
    y_b, lat, kpe_rot = mla_branch(cq, ckv, kpe, pos, lw, attend)
    merged = jax.nn.sigmoid(gate_a) * y_a + jax.nn.sigmoid(gate_b) * y_b
    x = x + g1 * (merged @ lw['w_out'])
    h = rmsnorm(x, lw['norm_ffn_w']) * (1.0 + sc2) + sh2
    x = x + g2 * moe_ffn(h.reshape(Bn * T, D), lw).reshape(Bn, T, D)
    return x, lat, kpe_rot, new_ssm, new_buf


def final_norm(x, c, w_ada_final, b_ada_final, norm_final_w):
    mod = (c @ w_ada_final + b_ada_final).reshape(c.shape[0], 2, x.shape[-1])
    return rmsnorm(x, norm_final_w) * (1.0 + mod[:, 1, None, :]) + mod[:, 0, None, :]


def setup_inputs(seed: int = 0) -> dict:
    key = jax.random.key(seed)
    ks = iter(jax.random.split(key, 48))
    f32 = jnp.float32
    n_pages = PAST_LEN // PAGE_SIZE
    n_phys = (DEC_BATCH * n_pages * 5) // 4

    def nrm(shape, scale=1.0):
        return scale * jax.random.normal(next(ks), shape, f32)

    def gain(shape):
        return 1.0 + nrm(shape, 0.02)

    page_table = jax.random.permutation(next(ks), n_phys)[:DEC_BATCH * n_pages]
    page_table = page_table.reshape(DEC_BATCH, n_pages).astype(jnp.int32)
    A_log = jnp.log(jax.random.uniform(next(ks), (DEPTH, GDN_HEADS), f32, 1.0, 16.0))
    dt = jnp.exp(jax.random.uniform(next(ks), (DEPTH, GDN_HEADS), f32, math.log(1e-3), math.log(1e-1)))
    dt_bias = dt + jnp.log(-jnp.expm1(-dt))
    return {
        'x_prompt': nrm((BATCH, SEQ, D_MODEL)),
        'x_sample': nrm((DEC_BATCH, DEC_SEQ, D_MODEL)),
        'c_prompt': nrm((BATCH, D_MODEL)),
        'c_sample': nrm((DEC_BATCH, D_MODEL)),
        'cache_kv': nrm((DEPTH, n_phys, PAGE_SIZE, KV_LORA)),
        'cache_pe': nrm((DEPTH, n_phys, PAGE_SIZE, QK_ROPE)),
        'state_ssm': nrm((DEPTH, DEC_BATCH, GDN_HEADS, GDN_DK, GDN_DV)),
        'state_conv': nrm((DEPTH, DEC_BATCH, CONV_WIDTH - 1, CONV_DIM)),
        'page_table': page_table,
        'norm_mix_w': gain((DEPTH, D_MODEL)),
        'norm_ffn_w': gain((DEPTH, D_MODEL)),
        'w_ada': nrm((DEPTH, D_MODEL, 6 * D_MODEL), 0.3 * D_MODEL ** -0.5),
        'b_ada': nrm((DEPTH, 6 * D_MODEL), 0.02),
        'w_in': nrm((DEPTH, D_MODEL, IN_COLS), D_MODEL ** -0.5),
        'conv_w': nrm((DEPTH, CONV_DIM, CONV_WIDTH), CONV_WIDTH ** -0.5),
        'A_log': A_log,
        'dt_bias': dt_bias,
        'gdn_norm_w': gain((DEPTH, GDN_DV)),
        'w_o_gdn': nrm((DEPTH, GDN_V_W, D_MODEL), GDN_V_W ** -0.5),
        'q_norm_w': gain((DEPTH, Q_LORA)),
        'kv_norm_w': gain((DEPTH, KV_LORA)),
        'w_uq': nrm((DEPTH, Q_LORA, MLA_HEADS * (QK_NOPE + QK_ROPE)), Q_LORA ** -0.5),
        'w_uk': nrm((DEPTH, MLA_HEADS, KV_LORA, QK_NOPE), KV_LORA ** -0.5),
        'w_uv': nrm((DEPTH, MLA_HEADS, KV_LORA, V_HEAD), KV_LORA ** -0.5),
        'w_o_mla': nrm((DEPTH, MLA_V_W, D_MODEL), MLA_V_W ** -0.5),
        'w_out': nrm((DEPTH, D_MODEL, D_MODEL), D_MODEL ** -0.5),
        'w_router': nrm((DEPTH, D_MODEL, N_EXPERTS), D_MODEL ** -0.5),
        'b_router': nrm((DEPTH, N_EXPERTS), 0.01),
        'w_gu': nrm((DEPTH, N_EXPERTS, D_MODEL, 2 * D_EXPERT), D_MODEL ** -0.5),
        'b_gu': nrm((DEPTH, N_EXPERTS, 2 * D_EXPERT), 0.01),
        'w_dn': nrm((DEPTH, N_EXPERTS, D_EXPERT, D_MODEL), D_EXPERT ** -0.5),
        'b_dn': nrm((DEPTH, N_EXPERTS, D_MODEL), 0.01),
        'w_ada_final': nrm((D_MODEL, 2 * D_MODEL), 0.3 * D_MODEL ** -0.5),
        'b_ada_final': nrm((2 * D_MODEL,), 0.02),
        'norm_final_w': gain((D_MODEL,)),
    }


def reference(x_prompt, x_sample, c_prompt, c_sample, cache_kv, cache_pe, state_ssm, state_conv, page_table,
              norm_mix_w, norm_ffn_w, w_ada, b_ada, w_in, conv_w, A_log, dt_bias, gdn_norm_w, w_o_gdn,
              q_norm_w, kv_norm_w, w_uq, w_uk, w_uv, w_o_mla, w_out, w_router, b_router, w_gu, b_gu,
              w_dn, b_dn, w_ada_final, b_ada_final, norm_final_w):
    Bp = x_prompt.shape[0]
    pos_p = jnp.arange(x_prompt.shape[1])
    past_len = page_table.shape[1] * PAGE_SIZE
    pos_s = past_len + jnp.arange(x_sample.shape[1])
    hp, hs = x_prompt, x_sample
    kv_p, pe_p, ssm_p, conv_p = [], [], [], []
    kv_s, pe_s, ssm_s, conv_s = [], [], [], []
    for l in range(DEPTH):
        lw = {
            'norm_mix_w': norm_mix_w[l], 'norm_ffn_w': norm_ffn_w[l], 'w_ada': w_ada[l], 'b_ada': b_ada[l],
            'w_in': w_in[l], 'conv_w': conv_w[l], 'A_log': A_log[l], 'dt_bias': dt_bias[l],
            'gdn_norm_w': gdn_norm_w[l], 'w_o_gdn': w_o_gdn[l], 'q_norm_w': q_norm_w[l],
            'kv_norm_w': kv_norm_w[l], 'w_uq': w_uq[l], 'w_uk': w_uk[l], 'w_uv': w_uv[l],
            'w_o_mla': w_o_mla[l], 'w_out': w_out[l], 'w_router': w_router[l], 'b_router': b_router[l],
            'w_gu': w_gu[l], 'b_gu': b_gu[l], 'w_dn': w_dn[l], 'b_dn': b_dn[l],
        }
        zero_buf = jnp.zeros((Bp, CONV_WIDTH - 1, CONV_DIM), x_prompt.dtype)
        zero_ssm = jnp.zeros((Bp, GDN_HEADS, GDN_DK, GDN_DV), jnp.float32)
        hp, lat, kpe, S, buf = decoder_layer(hp, c_prompt, pos_p, zero_buf, zero_ssm, lw, True,
                                             mla_causal_attention)
        kv_p.append(lat); pe_p.append(kpe); ssm_p.append(S.astype(state_ssm.dtype)); conv_p.append(buf)
        attend_s = functools.partial(mla_paged_attention, cache_kv=cache_kv, cache_pe=cache_pe,
                                     page_table=page_table, layer=l)
        hs, lat, kpe, S, buf = decoder_layer(hs, c_sample, pos_s, state_conv[l], state_ssm[l], lw, False, attend_s)
        kv_s.append(lat); pe_s.append(kpe); ssm_s.append(S.astype(state_ssm.dtype)); conv_s.append(buf)
    y_prompt = final_norm(hp, c_prompt, w_ada_final, b_ada_final, norm_final_w)
    y_sample = final_norm(hs, c_sample, w_ada_final, b_ada_final, norm_final_w)
    return (y_prompt, y_sample,
            jnp.stack(kv_p), jnp.stack(pe_p), jnp.stack(ssm_p), jnp.stack(conv_p),
            jnp.stack(kv_s), jnp.stack(pe_s), jnp.stack(ssm_s), jnp.stack(conv_s))
```

```python
import functools

import jax
import jax.numpy as jnp
from jax import lax
from jax.experimental import pallas as pl
from jax.experimental.pallas import tpu as pltpu

F32 = jnp.float32
BF16 = jnp.bfloat16
HI = lax.Precision.HIGHEST

NORM_EPS = 1e-6
ROPE_THETA = 10000.0
TOP_K = 4
SWIGLU_LIMIT = 7.0
SWIGLU_ALPHA = 1.702
CONV_WIDTH = 4
GDN_CHUNK = 64
PAGE_GROUP = 8
LANE = 128
VMEM_LIMIT = 48 * 1024 * 1024
VMEM_LIMIT_MOE = 58 * 1024 * 1024

NT = (((1,), (1,)), ((), ()))
TN = (((0,), (0,)), ((), ()))


def _cp(sem, vmem=VMEM_LIMIT):
    return pltpu.CompilerParams(dimension_semantics=sem, vmem_limit_bytes=vmem)


def _rms(x):
    return x * lax.rsqrt(jnp.mean(x * x, axis=-1, keepdims=True) + NORM_EPS)


def _softplus(x):
    return jnp.maximum(x, 0.0) + jnp.log1p(jnp.exp(-jnp.abs(x)))


def _silu(x):
    return x * jax.nn.sigmoid(x)


def _bdot(a, b):
    return jnp.dot(a.astype(BF16), b.astype(BF16), preferred_element_type=F32)


def _bdot_g(a, b, dims):
    return lax.dot_general(a.astype(BF16), b.astype(BF16), dims, preferred_element_type=F32)


def _mm_kernel(*refs, has_bias):
    if has_bias:
        x_ref, w_ref, b_ref, o_ref = refs
    else:
        x_ref, w_ref, o_ref = refs
    acc = _bdot(x_ref[...], w_ref[...])
    if has_bias:
        acc = acc + b_ref[...]
    o_ref[...] = acc.astype(o_ref.dtype)


def _matmul(x, w, out_dtype, tm, tn, bias=None):
    M, K = x.shape
    N = w.shape[1]
    tm, tn = min(tm, M), min(tn, N)
    in_specs = [pl.BlockSpec((tm, K), lambda i, j: (i, 0)), pl.BlockSpec((K, tn), lambda i, j: (0, j))]
    args = [x, w]
    if bias is not None:
        in_specs.append(pl.BlockSpec((1, tn), lambda i, j: (0, j)))
        args.append(bias.reshape(1, N))
    return pl.pallas_call(
        functools.partial(_mm_kernel, has_bias=bias is not None),
        grid=(M // tm, N // tn), in_specs=in_specs,
        out_specs=pl.BlockSpec((tm, tn), lambda i, j: (i, j)),
        out_shape=jax.ShapeDtypeStruct((M, N), out_dtype),
        compiler_params=_cp(("parallel", "parallel")))(*args)


class _Group:
    def __init__(self, mod3, modf3, tm, blocks_per_g):
        self.mod3, self.modf3, self.tm, self.bpg = mod3, modf3, tm, blocks_per_g

    def spec(self, j, d):
        r, bpg = self.mod3.shape[1], self.bpg
        return pl.BlockSpec((None, r, d), lambda i: (i // bpg, 0, j))


def _norm_mod_kernel(x_ref, w_ref, sc_ref, sh_ref, o_ref):
    y = _rms(x_ref[...]) * w_ref[...]
    o_ref[...] = (y * (1.0 + sc_ref[...]) + sh_ref[...]).astype(o_ref.dtype)


def _norm_mod(x, w, grp, j_scale, j_shift, out_dtype):
    n, d = x.shape
    tm = grp.tm
    return pl.pallas_call(
        _norm_mod_kernel, grid=(n // tm,),
        in_specs=[pl.BlockSpec((tm, d), lambda i: (i, 0)), pl.BlockSpec((1, d), lambda i: (0, 0)),
                  grp.spec(j_scale, d), grp.spec(j_shift, d)],
        out_specs=pl.BlockSpec((tm, d), lambda i: (i, 0)),
        out_shape=jax.ShapeDtypeStruct((n, d), out_dtype),
        compiler_params=_cp(("parallel",)))(x, w.reshape(1, d), grp.mod3, grp.mod3)


def _neumann_inverse(a, eye):
    n = a.shape[0]
    p = eye - a
    ak = a
    k = 2
    while k < n:
        ak = jnp.dot(ak, ak, precision=HI, preferred_element_type=F32)
        p = p + jnp.dot(p, ak, precision=HI, preferred_element_type=F32)
        k *= 2
    return p


def _gdn_prompt_kernel(qkv_ref, z_ref, sm_ref, cw_ref, gp_ref, nw_ref, og_ref, s_out_ref, s_ref, xbuf_ref,
                       *, H, DK, DV, C):
    c = pl.program_id(1)
    QK = H * DK
    W = 2 * QK + H * DV

    @pl.when(c == 0)
    def _():
        s_ref[...] = jnp.zeros_like(s_ref)
        xbuf_ref[0:8, :] = jnp.zeros((8, W), F32)

    xbuf_ref[8:8 + C, :] = qkv_ref[...].astype(F32)
    y = None
    for j in range(CONV_WIDTH):
        term = xbuf_ref[5 + j:5 + j + C, :] * cw_ref[j:j + 1, :]
        y = term if y is None else y + term
    xbuf_ref[0:8, :] = xbuf_ref[C:C + 8, :]
    y = _silu(y)

    sm = sm_ref[...]
    beta_all = jax.nn.sigmoid(sm)
    g_all = -jnp.exp(gp_ref[0:1, :]) * _softplus(sm + gp_ref[1:2, :])
    row = lax.broadcasted_iota(jnp.int32, (C, C), 0)
    col = lax.broadcasted_iota(jnp.int32, (C, C), 1)
    causal = col <= row
    strict = col < row
    tril = causal.astype(F32)
    eye = (row == col).astype(F32)
    gam_all = jnp.dot(tril, g_all, precision=HI, preferred_element_type=F32)

    for h in range(H):
        qh = y[:, h * DK:(h + 1) * DK]
        kh = y[:, QK + h * DK:QK + (h + 1) * DK]
        vh = y[:, 2 * QK + h * DV:2 * QK + (h + 1) * DV]
        qh = qh * lax.rsqrt(jnp.sum(qh * qh, axis=-1, keepdims=True) + NORM_EPS) * (DK ** -0.5)
        kh = kh * lax.rsqrt(jnp.sum(kh * kh, axis=-1, keepdims=True) + NORM_EPS)
        beta = beta_all[:, h:h + 1]
        g = g_all[:, 8 + h:9 + h]
        gam = gam_all[:, 8 + h:9 + h]
        dm = jnp.dot(tril, jnp.where(strict, g, 0.0), precision=HI, preferred_element_type=F32)
        decay = jnp.where(causal, jnp.exp(dm), 0.0)
        egam = jnp.exp(gam)
        kb16 = kh.astype(BF16)
        a_kk = jnp.where(strict, _bdot_g(kh * beta, kb16, NT) * decay, 0.0)
        t_inv = _neumann_inverse(a_kk, eye)
        s_old = s_ref[h]
        s16 = s_old.astype(BF16)
        rhs = beta * (vh - _bdot(kh * egam, s16))
        v_new = jnp.dot(t_inv, rhs, precision=HI, preferred_element_type=F32)
        a_qk = jnp.where(causal, _bdot_g(qh, kb16, NT) * decay, 0.0)
        o = _bdot(qh * egam, s16) + _bdot(a_qk, v_new)
        g_last = gam[C - 1:C, :]
        s_ref[h] = s_old * jnp.exp(g_last) + _bdot_g(kh * jnp.exp(g_last - gam), v_new, TN)
        zh = z_ref[:, h * DV:(h + 1) * DV].astype(F32)
        og_ref[:, h * DV:(h + 1) * DV] = (_rms(o) * nw_ref[...] * _silu(zh)).astype(og_ref.dtype)

    @pl.when(c == pl.num_programs(1) - 1)
    def _():
        s_out_ref[...] = s_ref[...]


def _gdn_prompt(qkv, z, small, ba_col, cw_t, gparams, norm_w, B, T, H, DK, DV):
    C = GDN_CHUNK
    nc = T // C
    W = qkv.shape[1]
    kern = functools.partial(_gdn_prompt_kernel, H=H, DK=DK, DV=DV, C=C)
    return pl.pallas_call(
        kern, grid=(B, nc),
        in_specs=[pl.BlockSpec((C, W), lambda b, c: (b * nc + c, 0)),
                  pl.BlockSpec((C, H * DV), lambda b, c: (b * nc + c, 0)),
                  pl.BlockSpec((C, LANE), lambda b, c: (b * nc + c, ba_col)),
                  pl.BlockSpec((CONV_WIDTH, W), lambda b, c: (0, 0)),
                  pl.BlockSpec((2, LANE), lambda b, c: (0, 0)),
                  pl.BlockSpec((1, DV), lambda b, c: (0, 0))],
        out_specs=[pl.BlockSpec((C, H * DV), lambda b, c: (b * nc + c, 0)),
                   pl.BlockSpec((None, H, DK, DV), lambda b, c: (b, 0, 0, 0))],
        out_shape=[jax.ShapeDtypeStruct((B * T, H * DV), BF16),
                   jax.ShapeDtypeStruct((B, H, DK, DV), F32)],
        scratch_shapes=[pltpu.VMEM((H, DK, DV), F32), pltpu.VMEM((C + 8, W), F32)],
        compiler_params=_cp(("parallel", "arbitrary")))(qkv, z, small, cw_t, gparams, norm_w.reshape(1, DV))


def _gdn_decode_kernel(u_ref, z_ref, ba_ref, buf_ref, s_in_ref, cw_ref, gp_ref, nw_ref,
                       og_ref, s_out_ref, buf_out_ref, *, H, DK, DV):
    QK = H * DK
    u = u_ref[...]
    buf = buf_ref[...]
    y = buf[0:1] * cw_ref[0:1, :]
    y = y + buf[1:2] * cw_ref[1:2, :]
    y = y + buf[2:3] * cw_ref[2:3, :]
    y = y + u * cw_ref[3:4, :]
    buf_out_ref[0:2, :] = buf[1:3]
    buf_out_ref[2:3, :] = u
    y = _silu(y)
    sm = ba_ref[...]
    beta_all = jax.nn.sigmoid(sm)
    g_all = -jnp.exp(gp_ref[0:1, :]) * _softplus(sm + gp_ref[1:2, :])
    rows = []
    for h in range(H):
        kh = y[:, QK + h * DK:QK + (h + 1) * DK]
        rows.append(kh * lax.rsqrt(jnp.sum(kh * kh, axis=-1, keepdims=True) + NORM_EPS))
    for h in range(H):
        qh = y[:, h * DK:(h + 1) * DK]
        rows.append(qh * lax.rsqrt(jnp.sum(qh * qh, axis=-1, keepdims=True) + NORM_EPS) * (DK ** -0.5))
    rows.append(jnp.zeros((DK - 2 * H, DK), F32))
    cols = jnp.concatenate(rows, axis=0).T
    for h in range(H):
        kcol = cols[:, h:h + 1]
        qcol = cols[:, H + h:H + h + 1]
        vh = y[:, 2 * QK + h * DV:2 * QK + (h + 1) * DV]
        s_dec = s_in_ref[h] * jnp.exp(g_all[:, 8 + h:9 + h])
        v_new = beta_all[:, h:h + 1] * (vh - jnp.sum(kcol * s_dec, axis=0, keepdims=True))
        s_new = s_dec + kcol * v_new
        s_out_ref[h] = s_new
        o = jnp.sum(qcol * s_new, axis=0, keepdims=True)
        zh = z_ref[:, h * DV:(h + 1) * DV].astype(F32)
        og_ref[:, h * DV:(h + 1) * DV] = (_rms(o) * nw_ref[...] * _silu(zh)).astype(og_ref.dtype)


def _gdn_decode(qkv, z, small, ba_col, conv_buf, ssm, cw_t, gparams, norm_w, H, DK, DV):
    Bs, W = qkv.shape
    kern = functools.partial(_gdn_decode_kernel, H=H, DK=DK, DV=DV)
    og, s_new, buf_new = pl.pallas_call(
        kern, grid=(Bs,),
        in_specs=[pl.BlockSpec((None, 1, W), lambda b: (b, 0, 0)),
                  pl.BlockSpec((None, 1, H * DV), lambda b: (b, 0, 0)),
                  pl.BlockSpec((None, 1, LANE), lambda b: (b, 0, ba_col)),
                  pl.BlockSpec((None, CONV_WIDTH - 1, W), lambda b: (b, 0, 0)),
                  pl.BlockSpec((None, H, DK, DV), lambda b: (b, 0, 0, 0)),
                  pl.BlockSpec((CONV_WIDTH, W), lambda b: (0, 0)),
                  pl.BlockSpec((2, LANE), lambda b: (0, 0)),
                  pl.BlockSpec((1, DV), lambda b: (0, 0))],
        out_specs=[pl.BlockSpec((None, 1, H * DV), lambda b: (b, 0, 0)),
                   pl.BlockSpec((None, H, DK, DV), lambda b: (b, 0, 0, 0)),
                   pl.BlockSpec((None, CONV_WIDTH - 1, W), lambda b: (b, 0, 0))],
        out_shape=[jax.ShapeDtypeStruct((Bs, 1, H * DV), BF16),
                   jax.ShapeDtypeStruct(ssm.shape, F32),
                   jax.ShapeDtypeStruct(conv_buf.shape, F32)],
        compiler_params=_cp(("parallel",)))(
            qkv.reshape(Bs, 1, W), z.reshape(Bs, 1, H * DV), small.reshape(Bs, 1, small.shape[1]),
            conv_buf, ssm, cw_t, gparams, norm_w.reshape(1, DV))
    return og.reshape(Bs, H * DV), s_new, buf_new


def _mla_prep_prompt_kernel(sm_ref, cos_ref, sin_ref, qnw_ref, kvnw_ref, wq_ref, wqr_ref, wuk_ref, wuv_ref,
                            q_ref, k_ref, v_ref, lat_ref, pe_ref, *, H, NOPE, ROPE, QL, KVL, scale):
    sm = sm_ref[...]
    cos = cos_ref[...]
    sin = sin_ref[...]
    qn = _rms(sm[:, :QL]) * qnw_ref[...]
    qf = _bdot(qn, wq_ref[...])
    qr = _bdot(qn, wqr_ref[...])
    lat = _rms(sm[:, QL:QL + KVL]) * kvnw_ref[...]
    lat_ref[...] = lat
    kn = _bdot(lat, wuk_ref[...])
    v_ref[...] = _bdot(lat, wuv_ref[...]).astype(v_ref.dtype)
    o = QL + KVL
    kr = sm[:, o:o + ROPE] * cos + sm[:, o + ROPE:o + 2 * ROPE] * sin
    pe_ref[...] = kr
    for h in range(H):
        p0 = H * NOPE + h * ROPE
        qp = qf[:, p0:p0 + ROPE] * cos + qr[:, h * ROPE:(h + 1) * ROPE] * sin
        q_ref[h] = (jnp.concatenate([qf[:, h * NOPE:(h + 1) * NOPE], qp], axis=-1) * scale).astype(q_ref.dtype)
        k_ref[h] = jnp.concatenate([kn[:, h * NOPE:(h + 1) * NOPE], kr], axis=-1).astype(k_ref.dtype)


def _mla_prep_prompt(small, cos, sin, qnw, kvnw, wq, wqr, wuk, wuv, T, tm, H, NOPE, ROPE, VH, QL, KVL, scale):
    n, ws = small.shape
    nt = T // tm
    kern = functools.partial(_mla_prep_prompt_kernel, H=H, NOPE=NOPE, ROPE=ROPE, QL=QL, KVL=KVL, scale=scale)
    full = lambda a: pl.BlockSpec(a.shape, lambda i: (0,) * a.ndim)
    dqk = NOPE + ROPE
    return pl.pallas_call(
        kern, grid=(n // tm,),
        in_specs=[pl.BlockSpec((tm, ws), lambda i: (i, 0)),
                  pl.BlockSpec((tm, ROPE), lambda i: (i % nt, 0)),
                  pl.BlockSpec((tm, ROPE), lambda i: (i % nt, 0)),
                  full(qnw), full(kvnw), full(wq), full(wqr), full(wuk), full(wuv)],
        out_specs=[pl.BlockSpec((H, tm, dqk), lambda i: (0, i, 0)),
                   pl.BlockSpec((H, tm, dqk), lambda i: (0, i, 0)),
                   pl.BlockSpec((tm, H * VH), lambda i: (i, 0)),
                   pl.BlockSpec((tm, KVL), lambda i: (i, 0)),
                   pl.BlockSpec((tm, ROPE), lambda i: (i, 0))],
        out_shape=[jax.ShapeDtypeStruct((H, n, dqk), BF16), jax.ShapeDtypeStruct((H, n, dqk), BF16),
                   jax.ShapeDtypeStruct((n, H * VH), BF16), jax.ShapeDtypeStruct((n, KVL), F32),
                   jax.ShapeDtypeStruct((n, ROPE), F32)],
        compiler_params=_cp(("parallel",)))(small, cos, sin, qnw, kvnw, wq, wqr, wuk, wuv)


def _mla_prep_sample_kernel(sm_ref, cos_ref, sin_ref, qnw_ref, kvnw_ref, wq_ref, wqr_ref,
                            qn_ref, qp_ref, lat_ref, pe_ref, *, H, NOPE, ROPE, QL, KVL, scale):
    sm = sm_ref[...]
    cos = cos_ref[...]
    sin = sin_ref[...]
    qn = _rms(sm[:, :QL]) * qnw_ref[...]
    qf = _bdot(qn, wq_ref[...])
    qr = _bdot(qn, wqr_ref[...])
    qn_ref[...] = qf[:, :H * NOPE].astype(qn_ref.dtype)
    lat_ref[...] = _rms(sm[:, QL:QL + KVL]) * kvnw_ref[...]
    o = QL + KVL
    pe_ref[...] = sm[:, o:o + ROPE] * cos + sm[:, o + ROPE:o + 2 * ROPE] * sin
    for h in range(H):
        p0 = H * NOPE + h * ROPE
        qp = qf[:, p0:p0 + ROPE] * cos + qr[:, h * ROPE:(h + 1) * ROPE] * sin
        qp_ref[:, h * ROPE:(h + 1) * ROPE] = (qp * scale).astype(qp_ref.dtype)


def _mla_prep_sample(small, cos, sin, qnw, kvnw, wq, wqr, H, NOPE, ROPE, QL, KVL, scale):
    n, ws = small.shape
    kern = functools.partial(_mla_prep_sample_kernel, H=H, NOPE=NOPE, ROPE=ROPE, QL=QL, KVL=KVL, scale=scale)
    full = lambda a: pl.BlockSpec(a.shape, lambda i: (0,) * a.ndim)
    return pl.pallas_call(
        kern, grid=(1,),
        in_specs=[full(small), full(cos), full(sin), full(qnw), full(kvnw), full(wq), full(wqr)],
        out_specs=[pl.BlockSpec((n, H * NOPE), lambda i: (0, 0)), pl.BlockSpec((n, H * ROPE), lambda i: (0, 0)),
                   pl.BlockSpec((n, KVL), lambda i: (0, 0)), pl.BlockSpec((n, ROPE), lambda i: (0, 0))],
        out_shape=[jax.ShapeDtypeStruct((n, H * NOPE), BF16), jax.ShapeDtypeStruct((n, H * ROPE), BF16),
                   jax.ShapeDtypeStruct((n, KVL), F32), jax.ShapeDtypeStruct((n, ROPE), F32)],
        compiler_params=_cp(("arbitrary",)))(small, cos, sin, qnw, kvnw, wq, wqr)


def _head_proj_kernel(x_ref, w_ref, o_ref, *, dims, scale):
    o_ref[...] = (_bdot_g(x_ref[...], w_ref[...], dims) * scale).astype(o_ref.dtype)


def _q_latent(q_nope, w_uk, scale):
    H, KVL, NOPE = w_uk.shape
    n = q_nope.shape[0]
    return pl.pallas_call(
        functools.partial(_head_proj_kernel, dims=NT, scale=scale), grid=(H,),
        in_specs=[pl.BlockSpec((n, NOPE), lambda h: (0, h)), pl.BlockSpec((None, KVL, NOPE), lambda h: (h, 0, 0))],
        out_specs=pl.BlockSpec((None, n, KVL), lambda h: (h, 0, 0)),
        out_shape=jax.ShapeDtypeStruct((H, n, KVL), BF16),
        compiler_params=_cp(("parallel",)))(q_nope, w_uk)


def _o_value(o_lat, w_uv):
    H, KVL, VH = w_uv.shape
    n = o_lat.shape[1]
    return pl.pallas_call(
        functools.partial(_head_proj_kernel, dims=(((1,), (0,)), ((), ())), scale=1.0), grid=(H,),
        in_specs=[pl.BlockSpec((None, n, KVL), lambda h: (h, 0, 0)), pl.BlockSpec((None, KVL, VH), lambda h: (h, 0, 0))],
        out_specs=pl.BlockSpec((n, VH), lambda h: (0, h)),
        out_shape=jax.ShapeDtypeStruct((n, H * VH), BF16),
        compiler_params=_cp(("parallel",)))(o_lat, w_uv)


def _flash_kernel(q_ref, k_ref, v_ref, o_ref, m_ref, l_ref, acc_ref, *, tq):
    qi = pl.program_id(2)
    kj = pl.program_id(3)

    @pl.when(kj == 0)
    def _():
        m_ref[...] = jnp.full_like(m_ref, -jnp.inf)
        l_ref[...] = jnp.zeros_like(l_ref)
        acc_ref[...] = jnp.zeros_like(acc_ref)

    @pl.when(kj <= qi)
    def _():
        s = lax.dot_general(q_ref[...], k_ref[...], NT, preferred_element_type=F32)
        row = qi * tq + lax.broadcasted_iota(jnp.int32, s.shape, 0)
        col = kj * tq + lax.broadcasted_iota(jnp.int32, s.shape, 1)
        s = jnp.where(col <= row, s, -jnp.inf)
        m_prev = m_ref[...]
        m_new = jnp.maximum(m_prev, jnp.max(s, axis=-1, keepdims=True))
        p = jnp.exp(s - m_new)
        corr = jnp.exp(m_prev - m_new)
        l_ref[...] = corr * l_ref[...] + jnp.sum(p, axis=-1, keepdims=True)
        acc_ref[...] = acc_ref[...] * corr + jnp.dot(p.astype(BF16), v_ref[...], preferred_element_type=F32)
        m_ref[...] = m_new

    @pl.when(kj == qi)
    def _():
        o_ref[...] = (acc_ref[...] / l_ref[...]).astype(o_ref.dtype)


def _flash_attention(q, k, v, B, T, tq, VH):
    H, n, dqk = q.shape
    nq = T // tq
    kern = functools.partial(_flash_kernel, tq=tq)
    return pl.pallas_call(
        kern, grid=(B, H, nq, nq),
        in_specs=[pl.BlockSpec((None, tq, dqk), lambda b, h, i, j: (h, b * nq + i, 0)),
                  pl.BlockSpec((None, tq, dqk), lambda b, h, i, j: (h, b * nq + jnp.minimum(i, j), 0)),
                  pl.BlockSpec((tq, VH), lambda b, h, i, j: (b * nq + jnp.minimum(i, j), h))],
        out_specs=pl.BlockSpec((tq, VH), lambda b, h, i, j: (b * nq + i, h)),
        out_shape=jax.ShapeDtypeStruct((n, H * VH), BF16),
        scratch_shapes=[pltpu.VMEM((tq, 1), F32), pltpu.VMEM((tq, 1), F32), pltpu.VMEM((tq, VH), F32)],
        compiler_params=_cp(("parallel", "parallel", "parallel", "arbitrary")))(q, k, v)


def _paged_kernel(pt_ref, ql_ref, qp_ref, latn_ref, pen_ref, *refs, G):
    kv_refs, pe_refs = refs[:G], refs[G:2 * G]
    o_ref, m_ref, l_ref, acc_ref = refs[2 * G:]
    g = pl.program_id(1)

    @pl.when(g == 0)
    def _():
        m_ref[...] = jnp.full_like(m_ref, -jnp.inf)
        l_ref[...] = jnp.zeros_like(l_ref)
        acc_ref[...] = jnp.zeros_like(acc_ref)

    ql = ql_ref[...]
    qp = qp_ref[...]
    kvs, ss = [], []
    for i in range(G):
        kv = kv_refs[i][...].astype(BF16)
        kvs.append(kv)
        ss.append(lax.dot_general(ql, kv, NT, preferred_element_type=F32)
                  + lax.dot_general(qp, pe_refs[i][...].astype(BF16), NT, preferred_element_type=F32))
    s = jnp.concatenate(ss, axis=-1)
    m_prev = m_ref[...]
    m_new = jnp.maximum(m_prev, jnp.max(s, axis=-1, keepdims=True))
    p = jnp.exp(s - m_new)
    corr = jnp.exp(m_prev - m_new)
    l_new = corr * l_ref[...] + jnp.sum(p, axis=-1, keepdims=True)
    P = kvs[0].shape[0]
    pv = None
    for i in range(G):
        t = jnp.dot(p[:, i * P:(i + 1) * P].astype(BF16), kvs[i], preferred_element_type=F32)
        pv = t if pv is None else pv + t
    acc_new = acc_ref[...] * corr + pv
    m_ref[...] = m_new
    l_ref[...] = l_new
    acc_ref[...] = acc_new

    @pl.when(g == pl.num_programs(1) - 1)
    def _():
        latn = latn_ref[...]
        s_n = (jnp.sum(ql.astype(F32) * latn, axis=-1, keepdims=True)
               + jnp.sum(qp.astype(F32) * pen_ref[...], axis=-1, keepdims=True))
        m2 = jnp.maximum(m_new, s_n)
        c2 = jnp.exp(m_new - m2)
        p2 = jnp.exp(s_n - m2)
        o_ref[...] = ((acc_new * c2 + p2 * latn) / (l_new * c2 + p2)).astype(o_ref.dtype)


def _paged_attention(q_lat, q_pe, lat_new, pe_new, cache_kv, cache_pe, page_table, layer):
    Bs, H, KVL = q_lat.shape
    ROPE = q_pe.shape[2]
    n_pages = page_table.shape[1]
    P = cache_kv.shape[2]
    G = min(PAGE_GROUP, n_pages)
    kv_specs = [pl.BlockSpec((None, None, P, KVL), functools.partial(
        lambda b, g, pt, i: (layer, pt[b, g * G + i], 0, 0), i=i)) for i in range(G)]
    pe_specs = [pl.BlockSpec((None, None, P, ROPE), functools.partial(
        lambda b, g, pt, i: (layer, pt[b, g * G + i], 0, 0), i=i)) for i in range(G)]
    grid_spec = pltpu.PrefetchScalarGridSpec(
        num_scalar_prefetch=1, grid=(Bs, n_pages // G),
        in_specs=[pl.BlockSpec((None, H, KVL), lambda b, g, pt: (b, 0, 0)),
                  pl.BlockSpec((None, H, ROPE), lambda b, g, pt: (b, 0, 0)),
                  pl.BlockSpec((None, 1, KVL), lambda b, g, pt: (b, 0, 0)),
                  pl.BlockSpec((None, 1, ROPE), lambda b, g, pt: (b, 0, 0))] + kv_specs + pe_specs,
        out_specs=pl.BlockSpec((None, H, KVL), lambda b, g, pt: (b, 0, 0)),
        scratch_shapes=[pltpu.VMEM((H, 1), F32), pltpu.VMEM((H, 1), F32), pltpu.VMEM((H, KVL), F32)])
    return pl.pallas_call(
        functools.partial(_paged_kernel, G=G), grid_spec=grid_spec,
        out_shape=jax.ShapeDtypeStruct((Bs, H, KVL), BF16),
        compiler_params=_cp(("parallel", "arbitrary")))(
            page_table, q_lat, q_pe, lat_new.reshape(Bs, 1, KVL), pe_new.reshape(Bs, 1, ROPE),
            *([cache_kv] * G), *([cache_pe] * G))


def _post_mixer_kernel(x_ref, og_ref, om_ref, gab_ref, g1_ref, sc2_ref, sh2_ref, wog_ref, wom_ref, wout_ref,
                       nw_ref, wr_ref, br_ref, x1_ref, h2_ref, te_ref, tg_ref, *, D, E):
    gab = gab_ref[...].astype(F32)
    merged = (jax.nn.sigmoid(gab[:, :D]) * _bdot(og_ref[...], wog_ref[...])
              + jax.nn.sigmoid(gab[:, D:]) * _bdot(om_ref[...], wom_ref[...]))
    x1 = x_ref[...] + g1_ref[...] * _bdot(merged, wout_ref[...])
    x1_ref[...] = x1
    h2 = _rms(x1) * nw_ref[...] * (1.0 + sc2_ref[...]) + sh2_ref[...]
    h2_ref[...] = h2
    logits = jnp.dot(h2, wr_ref[...], precision=HI, preferred_element_type=F32) + br_ref[...]
    lane = lax.broadcasted_iota(jnp.int32, logits.shape, 1)
    logits = jnp.where(lane < E, logits, -jnp.inf)
    te = jnp.zeros(logits.shape, jnp.int32)
    ex = jnp.zeros(logits.shape, F32)
    top = None
    for k in range(TOP_K):
        m = jnp.max(logits, axis=-1, keepdims=True)
        idx = jnp.min(jnp.where(logits == m, lane, LANE), axis=-1, keepdims=True)
        top = m if top is None else top
        te = jnp.where(lane == k, idx, te)
        ex = jnp.where(lane == k, jnp.exp(m - top), ex)
        logits = jnp.where(lane == idx, -jnp.inf, logits)
    te_ref[...] = te
    tg_ref[...] = ex / jnp.sum(ex, axis=-1, keepdims=True)


def _post_mixer(x, og, om, gab, grp, wog, wom, wout, norm_w, wr, br, E):
    n, d = x.shape
    tm = grp.tm
    rowblk = lambda w: pl.BlockSpec((tm, w), lambda i: (i, 0))
    full = lambda a: pl.BlockSpec(a.shape, lambda i: (0,) * a.ndim)
    return pl.pallas_call(
        functools.partial(_post_mixer_kernel, D=d, E=E), grid=(n // tm,),
        in_specs=[rowblk(d), rowblk(d), rowblk(d), rowblk(2 * d), grp.spec(2, d), grp.spec(4, d), grp.spec(3, d),
                  full(wog), full(wom), full(wout), full(norm_w), full(wr), full(br)],
        out_specs=[rowblk(d), rowblk(d), rowblk(LANE), rowblk(LANE)],
        out_shape=[jax.ShapeDtypeStruct((n, d), F32), jax.ShapeDtypeStruct((n, d), F32),
                   jax.ShapeDtypeStruct((n, LANE), jnp.int32), jax.ShapeDtypeStruct((n, LANE), F32)],
        compiler_params=_cp(("parallel",)))(x, og, om, gab, grp.mod3, grp.mod3, grp.mod3,
                                            wog, wom, wout, norm_w, wr, br)


def _dispatch_kernel(slot_ref, h_ref, init_ref, hs_ref, sem, *, tb):
    del init_ref

    def row_copy(r, s):
        return pltpu.make_async_copy(h_ref.at[pl.ds(r, 1)], hs_ref.at[pl.ds(s, 1)], sem)

    def start(r, carry):
        for k in range(TOP_K):
            row_copy(r, slot_ref[0, 0, r * TOP_K + k]).start()
        return carry

    def wait(r, carry):
        for k in range(TOP_K):
            row_copy(0, 0).wait()
        return carry

    lax.fori_loop(0, tb, start, 0)
    lax.fori_loop(0, tb, wait, 0)


def _dispatch(h, slots, n_slots, tb):
    n, d = h.shape
    nb = n // tb
    return pl.pallas_call(
        functools.partial(_dispatch_kernel, tb=tb), grid=(nb,),
        in_specs=[pl.BlockSpec((1, 1, tb * TOP_K), lambda i: (i, 0, 0), memory_space=pltpu.SMEM),
                  pl.BlockSpec((tb, d), lambda i: (i, 0)),
                  pl.BlockSpec(memory_space=pl.ANY)],
        out_specs=pl.BlockSpec(memory_space=pl.ANY),
        out_shape=jax.ShapeDtypeStruct((n_slots, d), h.dtype),
        scratch_shapes=[pltpu.SemaphoreType.DMA],
        input_output_aliases={2: 0},
        compiler_params=_cp(("arbitrary",)))(slots.reshape(nb, 1, tb * TOP_K), h, jnp.zeros((n_slots, d), h.dtype))


def _expert_kernel(be_ref, na_ref, x_ref, wgu_ref, bgu_ref, wdn_ref, bdn_ref, y_ref, wgu_s, wdn_s, *, DE):
    i = pl.program_id(0)
    active = i < na_ref[0]
    first = jnp.logical_or(i == 0, be_ref[i] != be_ref[jnp.maximum(i - 1, 0)])

    @pl.when(jnp.logical_and(active, first))
    def _():
        wgu_s[...] = wgu_ref[...].astype(BF16)
        wdn_s[...] = wdn_ref[...].astype(BF16)

    @pl.when(active)
    def _():
        gu = jnp.dot(x_ref[...].astype(BF16), wgu_s[...], preferred_element_type=F32) + bgu_ref[...]
        gt = jnp.minimum(gu[:, :DE], SWIGLU_LIMIT)
        up = jnp.clip(gu[:, DE:], -SWIGLU_LIMIT, SWIGLU_LIMIT)
        act = (up + 1.0) * gt * jax.nn.sigmoid(SWIGLU_ALPHA * gt)
        y_ref[...] = jnp.dot(act.astype(BF16), wdn_s[...], preferred_element_type=F32) + bdn_ref[...]

    @pl.when(jnp.logical_not(active))
    def _():
        y_ref[...] = jnp.zeros_like(y_ref)


def _experts(hs, blk_e, n_active, w_gu, b_gu, w_dn, b_dn, layer, tm):
    n_slots, d = hs.shape
    E, _, de2 = w_gu.shape[1:]
    de = de2 // 2
    grid_spec = pltpu.PrefetchScalarGridSpec(
        num_scalar_prefetch=2, grid=(n_slots // tm,),
        in_specs=[pl.BlockSpec((tm, d), lambda i, be, na: (i, 0)),
                  pl.BlockSpec((None, None, d, de2), lambda i, be, na: (layer, be[i], 0, 0)),
                  pl.BlockSpec((None, None, 1, de2), lambda i, be, na: (layer, be[i], 0, 0)),
                  pl.BlockSpec((None, None, de, d), lambda i, be, na: (layer, be[i], 0, 0)),
                  pl.BlockSpec((None, None, 1, d), lambda i, be, na: (layer, be[i], 0, 0))],
        out_specs=pl.BlockSpec((tm, d), lambda i, be, na: (i, 0)),
        scratch_shapes=[pltpu.VMEM((d, de2), BF16), pltpu.VMEM((de, d), BF16)])
    L = w_gu.shape[0]
    return pl.pallas_call(
        functools.partial(_expert_kernel, DE=de), grid_spec=grid_spec,
        out_shape=jax.ShapeDtypeStruct((n_slots, d), F32),
        compiler_params=_cp(("arbitrary",), VMEM_LIMIT_MOE))(
            blk_e, n_active, hs, w_gu, b_gu.reshape(L, E, 1, de2), w_dn, b_dn.reshape(L, E, 1, d))


def _combine_kernel(slot_ref, tg_ref, x1_ref, g2_ref, scf_ref, shf_ref, nwf_ref, yp_ref, o_ref, buf, sem, *, tb):
    def row_copy(r, k, s):
        return pltpu.make_async_copy(yp_ref.at[pl.ds(s, 1)], buf.at[k, pl.ds(r, 1)], sem)

    def start(r, carry):
        for k in range(TOP_K):
            row_copy(r, k, slot_ref[0, 0, r * TOP_K + k]).start()
        return carry

    def wait(r, carry):
        for k in range(TOP_K):
            row_copy(0, k, 0).wait()
        return carry

    lax.fori_loop(0, tb, start, 0)
    lax.fori_loop(0, tb, wait, 0)
    tg = tg_ref[...]
    moe = tg[:, 0:1] * buf[0]
    for k in range(1, TOP_K):
        moe = moe + tg[:, k:k + 1] * buf[k]
    x2 = x1_ref[...] + g2_ref[...] * moe
    o_ref[...] = _rms(x2) * nwf_ref[...] * (1.0 + scf_ref[...]) + shf_ref[...]


def _combine_final(y_pad, slots, tg, x1, grp, norm_final_w):
    n, d = x1.shape
    tb = grp.tm
    nb = n // tb
    rf, bpg = grp.modf3.shape[1], grp.bpg
    fspec = lambda j: pl.BlockSpec((None, rf, d), lambda i: (i // bpg, 0, j))
    return pl.pallas_call(
        functools.partial(_combine_kernel, tb=tb), grid=(nb,),
        in_specs=[pl.BlockSpec((1, 1, tb * TOP_K), lambda i: (i, 0, 0), memory_space=pltpu.SMEM),
                  pl.BlockSpec((tb, LANE), lambda i: (i, 0)),
                  pl.BlockSpec((tb, d), lambda i: (i, 0)),
                  grp.spec(5, d), fspec(1), fspec(0),
                  pl.BlockSpec((1, d), lambda i: (0, 0)),
                  pl.BlockSpec(memory_space=pl.ANY)],
        out_specs=pl.BlockSpec((tb, d), lambda i: (i, 0)),
        out_shape=jax.ShapeDtypeStruct((n, d), F32),
        scratch_shapes=[pltpu.VMEM((TOP_K, tb, d), F32), pltpu.SemaphoreType.DMA],
        compiler_params=_cp(("arbitrary",)))(
            slots.reshape(nb, 1, tb * TOP_K), tg, x1, grp.mod3, grp.modf3, grp.modf3,
            norm_final_w.reshape(1, d), y_pad)


def _route(top_e, E, tm):
    n = top_e.shape[0]
    tok_oh = jnp.sum((top_e[:, :, None] == jnp.arange(E, dtype=jnp.int32)).astype(jnp.int32), axis=1)
    csum = jnp.cumsum(tok_oh, axis=0)
    counts = csum[-1]
    rank = jnp.take_along_axis(csum - tok_oh, top_e, axis=1)
    padded = (counts + tm - 1) // tm * tm
    pad_end = jnp.cumsum(padded)
    slots = (pad_end - padded)[top_e] + rank
    n_blocks = -(-(n * TOP_K) // tm) + E
    blk_e = jnp.minimum(jnp.searchsorted(pad_end, jnp.arange(n_blocks, dtype=jnp.int32) * tm, side='right'), E - 1)
    n_active = (pad_end[-1] // tm).reshape(1)
    return slots.astype(jnp.int32), blk_e.astype(jnp.int32), n_active.astype(jnp.int32), n_blocks * tm


def _rope_tables(pos, rope):
    half = rope // 2
    inv = ROPE_THETA ** (-jnp.arange(half, dtype=F32) / half)
    ang = pos.astype(F32)[:, None] * inv[None, :]
    cos, sin = jnp.cos(ang), jnp.sin(ang)
    return jnp.concatenate([cos, cos], axis=-1), jnp.concatenate([sin, sin], axis=-1)


def _rotate_cols(w, rope):
    k, n = w.shape
    w3 = w.reshape(k, n // rope, rope)
    half = rope // 2
    return jnp.concatenate([-w3[..., half:], w3[..., :half]], axis=-1).reshape(k, n)


def kernel(x_prompt, x_sample, c_prompt, c_sample, cache_kv, cache_pe, state_ssm, state_conv, page_table, norm_mix_w, norm_ffn_w, w_ada, b_ada, w_in, conv_w, A_log, dt_bias, gdn_norm_w, w_o_gdn, q_norm_w, kv_norm_w, w_uq, w_uk, w_uv, w_o_mla, w_out, w_router, b_router, w_gu, b_gu, w_dn, b_dn, w_ada_final, b_ada_final, norm_final_w):
    B, T, D = x_prompt.shape
    Bs, Ts, _ = x_sample.shape
    assert Ts == 1
    depth = w_in.shape[0]
    H, DK, DV = state_ssm.shape[2:]
    QK = H * DK
    CONV = state_conv.shape[3]
    assert CONV == 2 * QK + H * DV and H <= 8
    QL = q_norm_w.shape[1]
    HM, KVL, NOPE = w_uk.shape[1:]
    VH = w_uv.shape[3]
    ROPE = cache_pe.shape[3]
    E = w_router.shape[2]
    scale = float(NOPE + ROPE) ** -0.5
    n_p, n_s = B * T, Bs * Ts
    past_len = page_table.shape[1] * cache_kv.shape[2]

    c_all = jnp.concatenate([c_prompt, c_sample], axis=0)
    modf = _matmul(c_all, w_ada_final, F32, c_all.shape[0], 1024, b_ada_final)
    tm_p = min(512, T)
    tb = 128
    cos_p, sin_p = _rope_tables(jnp.arange(T), ROPE)
    cos_s, sin_s = _rope_tables(past_len + jnp.arange(Ts), ROPE)

    hp = x_prompt.reshape(n_p, D)
    hs = x_sample.reshape(n_s, D)
    outs = {k: [] for k in ('kv_p', 'pe_p', 'ssm_p', 'conv_p', 'kv_s', 'pe_s', 'ssm_s', 'conv_s')}
    for l in range(depth):
        mod = _matmul(c_all, w_ada[l], F32, c_all.shape[0], 1024, b_ada[l])
        last = l == depth - 1
        grp_p = _Group(mod[:B].reshape(B, 1, 6 * D), modf[:B].reshape(B, 1, 2 * D), tm_p, T // tm_p)
        grp_s = _Group(mod[B:].reshape(1, Bs, 6 * D), modf[B:].reshape(1, Bs, 2 * D), Bs, 1)
        grp_pc = _Group(grp_p.mod3, grp_p.modf3, tb, T // tb)

        offs = [0]
        for s in (CONV, H * DV, H, H, QL, KVL, ROPE, D, D):
            offs.append(offs[-1] + s)
        wi = w_in[l]
        seg = lambda i: wi[:, offs[i]:offs[i + 1]]
        w_qkv = seg(0).astype(BF16)
        w_z = seg(1).astype(BF16)
        w_gab = jnp.concatenate([seg(7), seg(8)], axis=1).astype(BF16)
        n_small = QL + KVL + 2 * ROPE
        ba_col = -(-n_small // LANE)
        w_small = jnp.concatenate(
            [seg(4), seg(5), seg(6), _rotate_cols(seg(6), ROPE), jnp.zeros((D, ba_col * LANE - n_small), F32),
             seg(2), jnp.zeros((D, 8 - H), F32), seg(3), jnp.zeros((D, LANE - 8 - H), F32)], axis=1).astype(BF16)
        cw_t = conv_w[l].T
        gparams = jnp.zeros((2, LANE), F32).at[0, 8:8 + H].set(A_log[l]).at[1, 8:8 + H].set(dt_bias[l])
        wq = w_uq[l].reshape(QL, HM, NOPE + ROPE)
        wq_pe = wq[:, :, NOPE:].reshape(QL, HM * ROPE)
        wq_all = jnp.concatenate([wq[:, :, :NOPE].reshape(QL, HM * NOPE), wq_pe], axis=1).astype(BF16)
        wq_rot = _rotate_cols(wq_pe, ROPE).astype(BF16)
        wuk_all = jnp.transpose(w_uk[l], (1, 0, 2)).reshape(KVL, HM * NOPE).astype(BF16)
        wuv_all = jnp.transpose(w_uv[l], (1, 0, 2)).reshape(KVL, HM * VH).astype(BF16)
        qnw = q_norm_w[l].reshape(1, QL)
        kvnw = kv_norm_w[l].reshape(1, KVL)
        wog = w_o_gdn[l].astype(BF16)
        wom = w_o_mla[l].astype(BF16)
        wout = w_out[l].astype(BF16)
        wr = jnp.pad(w_router[l], ((0, 0), (0, LANE - E)))
        br = jnp.pad(b_router[l], (0, LANE - E)).reshape(1, LANE)
        nfw = norm_ffn_w[l].reshape(1, D)

        h1 = _norm_mod(hp, norm_mix_w[l], grp_p, 1, 0, BF16)
        qkv_p = _matmul(h1, w_qkv, BF16, 1024, 512)
        z_p = _matmul(h1, w_z, BF16, 1024, 512)
        gab_p = _matmul(h1, w_gab, BF16, 1024, 512)
        small_p = _matmul(h1, w_small, F32, 1024, w_small.shape[1])
        og_p, ssm_p = _gdn_prompt(qkv_p, z_p, small_p, ba_col, cw_t, gparams, gdn_norm_w[l], B, T, H, DK, DV)
        q_p, k_p, v_p, lat_p, pe_p = _mla_prep_prompt(small_p, cos_p, sin_p, qnw, kvnw, wq_all, wq_rot, wuk_all,
                                                      wuv_all, T, tm_p, HM, NOPE, ROPE, VH, QL, KVL, scale)
        om_p = _flash_attention(q_p, k_p, v_p, B, T, tm_p, VH)
        x1_p, h2_p, te_p, tg_p = _post_mixer(hp, og_p, om_p, gab_p, grp_p, wog, wom, wout, nfw, wr, br, E)
        outs['kv_p'].append(lat_p.reshape(B, T, KVL))
        outs['pe_p'].append(pe_p.reshape(B, T, ROPE))
        outs['ssm_p'].append(ssm_p)
        outs['conv_p'].append(qkv_p.reshape(B, T, CONV)[:, T - (CONV_WIDTH - 1):, :].astype(F32))

        h1s = _norm_mod(hs, norm_mix_w[l], grp_s, 1, 0, BF16)
        qkv_s = _matmul(h1s, w_qkv, F32, Bs, 512)
        z_s = _matmul(h1s, w_z, BF16, Bs, 512)
        gab_s = _matmul(h1s, w_gab, BF16, Bs, 512)
        small_s = _matmul(h1s, w_small, F32, Bs, w_small.shape[1])
        og_s, ssm_s, conv_s = _gdn_decode(qkv_s, z_s, small_s, ba_col, state_conv[l], state_ssm[l], cw_t, gparams,
                                          gdn_norm_w[l], H, DK, DV)
        qn_s, qp_s, lat_s, pe_s = _mla_prep_sample(small_s, cos_s, sin_s, qnw, kvnw, wq_all, wq_rot,
                                                   HM, NOPE, ROPE, QL, KVL, scale)
        q_lat = jnp.transpose(_q_latent(qn_s, w_uk[l], scale), (1, 0, 2))
        o_lat = _paged_attention(q_lat, qp_s.reshape(n_s, HM, ROPE), lat_s, pe_s, cache_kv, cache_pe, page_table, l)
        om_s = _o_value(jnp.transpose(o_lat, (1, 0, 2)), w_uv[l])
        x1_s, h2_s, te_s, tg_s = _post_mixer(hs, og_s, om_s, gab_s, grp_s, wog, wom, wout, nfw, wr, br, E)
        outs['kv_s'].append(lat_s.reshape(Bs, Ts, KVL))
        outs['pe_s'].append(pe_s.reshape(Bs, Ts, ROPE))
        outs['ssm_s'].append(ssm_s)
        outs['conv_s'].append(conv_s)

        tm_e = 256
        h2 = jnp.concatenate([h2_p, h2_s], axis=0)
        top_e = jnp.concatenate([te_p[:, :TOP_K], te_s[:, :TOP_K]], axis=0)
        slots, blk_e, n_active, n_slots = _route(top_e, E, tm_e)
        h_sorted = _dispatch(h2, slots, n_slots, tb)
        y_pad = _experts(h_sorted, blk_e, n_active, w_gu, b_gu, w_dn, b_dn, l, tm_e)
        if not last:
            raise NotImplementedError("stacked layers need an un-normalised residual output")
        hp = _combine_final(y_pad, slots[:n_p], tg_p, x1_p, grp_pc, norm_final_w)
        hs = _combine_final(y_pad, slots[n_p:], tg_s, x1_s, grp_s, norm_final_w)

    st = lambda k: jnp.stack(outs[k])
    return (hp.reshape(B, T, D), hs.reshape(Bs, Ts, D),
            st('kv_p'), st('pe_p'), st('ssm_p'), st('conv_p'),
            st('kv_s'), st('pe_s'), st('ssm_s'), st('conv_s'))
```

```python
import functools

import jax
import jax.numpy as jnp
from jax import lax
from jax.experimental import pallas as pl
from jax.experimental.pallas import tpu as pltpu

F32 = jnp.float32
BF16 = jnp.bfloat16
HI = lax.Precision.HIGHEST

NORM_EPS = 1e-6
ROPE_THETA = 10000.0
TOP_K = 4
SWIGLU_LIMIT = 7.0
SWIGLU_ALPHA = 1.702
CONV_WIDTH = 4
GDN_CHUNK = 64
PAGE_GROUP = 8
PAGE_REQUESTS = 4
LANE = 128
VMEM_LIMIT = 48 * 1024 * 1024
VMEM_LIMIT_MOE = 58 * 1024 * 1024

NT = (((1,), (1,)), ((), ()))
TN = (((0,), (0,)), ((), ()))


def _cp(sem, vmem=VMEM_LIMIT):
    return pltpu.CompilerParams(dimension_semantics=sem, vmem_limit_bytes=vmem)


def _rms(x):
    return x * lax.rsqrt(jnp.mean(x * x, axis=-1, keepdims=True) + NORM_EPS)


def _softplus(x):
    return jnp.maximum(x, 0.0) + jnp.log1p(jnp.exp(-jnp.abs(x)))


def _silu(x):
    return x * jax.nn.sigmoid(x)


def _bdot(a, b):
    return jnp.dot(a.astype(BF16), b.astype(BF16), preferred_element_type=F32)


def _bdot_g(a, b, dims):
    return lax.dot_general(a.astype(BF16), b.astype(BF16), dims, preferred_element_type=F32)


def _mm_kernel(*refs, has_bias):
    if has_bias:
        x_ref, w_ref, b_ref, o_ref = refs
    else:
        x_ref, w_ref, o_ref = refs
    acc = _bdot(x_ref[...], w_ref[...])
    if has_bias:
        acc = acc + b_ref[...]
    o_ref[...] = acc.astype(o_ref.dtype)


def _matmul(x, w, out_dtype, tm, tn, bias=None):
    M, K = x.shape
    N = w.shape[1]
    tm, tn = min(tm, M), min(tn, N)
    in_specs = [pl.BlockSpec((tm, K), lambda i, j: (i, 0)), pl.BlockSpec((K, tn), lambda i, j: (0, j))]
    args = [x, w]
    if bias is not None:
        in_specs.append(pl.BlockSpec((1, tn), lambda i, j: (0, j)))
        args.append(bias.reshape(1, N))
    return pl.pallas_call(
        functools.partial(_mm_kernel, has_bias=bias is not None),
        grid=(M // tm, N // tn), in_specs=in_specs,
        out_specs=pl.BlockSpec((tm, tn), lambda i, j: (i, j)),
        out_shape=jax.ShapeDtypeStruct((M, N), out_dtype),
        compiler_params=_cp(("parallel", "parallel")))(*args)


class _Group:
    def __init__(self, mod3, modf3, tm, blocks_per_g):
        self.mod3, self.modf3, self.tm, self.bpg = mod3, modf3, tm, blocks_per_g

    def spec(self, j, d):
        r, bpg = self.mod3.shape[1], self.bpg
        return pl.BlockSpec((None, r, d), lambda i: (i // bpg, 0, j))


def _norm_mod_kernel(x_ref, w_ref, sc_ref, sh_ref, o_ref):
    y = _rms(x_ref[...]) * w_ref[...]
    o_ref[...] = (y * (1.0 + sc_ref[...]) + sh_ref[...]).astype(o_ref.dtype)


def _norm_mod(x, w, grp, j_scale, j_shift, out_dtype):
    n, d = x.shape
    tm = grp.tm
    return pl.pallas_call(
        _norm_mod_kernel, grid=(n // tm,),
        in_specs=[pl.BlockSpec((tm, d), lambda i: (i, 0)), pl.BlockSpec((1, d), lambda i: (0, 0)),
                  grp.spec(j_scale, d), grp.spec(j_shift, d)],
        out_specs=pl.BlockSpec((tm, d), lambda i: (i, 0)),
        out_shape=jax.ShapeDtypeStruct((n, d), out_dtype),
        compiler_params=_cp(("parallel",)))(x, w.reshape(1, d), grp.mod3, grp.mod3)


SUBLANES = 8
PREV_ROWS = 16


def _split_bf16(x):
    hi = x.astype(BF16)
    return hi, (x - hi.astype(F32)).astype(BF16)


def _dot3(a, b):
    d = lambda x, y: jnp.dot(x, y, preferred_element_type=F32)
    return d(a[0], b[0]) + d(a[0], b[1]) + d(a[1], b[0])


def _expand_matrix(C):
    k = jnp.arange(C)
    n = jnp.arange(SUBLANES * LANE)
    rem = n % LANE
    hit = ((k[:, None] // SUBLANES == rem[None, :] // SUBLANES) & (k[:, None] % SUBLANES == n[None, :] // LANE)
           & (rem[None, :] < C))
    return hit.astype(BF16)


def _unit_lower_inverses(a_list, at_list, g_ref, C):
    nh = len(a_list)
    row = lax.broadcasted_iota(jnp.int32, (C, C), 0)
    col = lax.broadcasted_iota(jnp.int32, (C, C), 1)
    blockdiag = (row // SUBLANES) == (col // SUBLANES)
    packed = []
    for at in at_list:
        m = jnp.where(blockdiag, at, 0.0)
        d = m[0:SUBLANES]
        for b in range(1, C // SUBLANES):
            d = d + m[SUBLANES * b:SUBLANES * (b + 1)]
        packed.append(d)
    stack = _split_bf16(jnp.concatenate(packed, axis=0))
    g = g_ref[...]
    coef = (jnp.dot(stack[0], g, preferred_element_type=F32) + jnp.dot(stack[1], g, preferred_element_type=F32))
    sub = lax.broadcasted_iota(jnp.int32, (SUBLANES, LANE), 0)
    lane = lax.broadcasted_iota(jnp.int32, (SUBLANES, LANE), 1)
    unit = jnp.where((lane % SUBLANES == sub) & (lane < C), 1.0, 0.0)
    xd = [unit] * nh
    for i in range(1, SUBLANES):
        e_i = jnp.where((lane % SUBLANES == i) & (lane < C), 1.0, 0.0)[0:1]
        for h in range(nh):
            c_i = coef[SUBLANES * h:SUBLANES * (h + 1), LANE * i:LANE * (i + 1)]
            new_row = e_i - jnp.sum(c_i * xd[h], axis=0, keepdims=True)
            xd[h] = jnp.where(sub == i, new_row, xd[h])
    x = [jnp.where(blockdiag, jnp.concatenate([xd[h][:, :C]] * (C // SUBLANES), axis=0), 0.0) for h in range(nh)]
    s = SUBLANES
    while s < C:
        below = ((row // s) % 2 == 1) & ((col // s) == (row // s) - 1)
        xs = [_split_bf16(x[h]) for h in range(nh)]
        m1 = [_dot3(xs[h], _split_bf16(jnp.where(below, a_list[h], 0.0))) for h in range(nh)]
        x = [x[h] - _dot3(_split_bf16(m1[h]), xs[h]) for h in range(nh)]
        s *= 2
    return x


def _gdn_prep_kernel(qkv_ref, prev_ref, sm_ref, cw_ref, gp_ref, g_ref,
                     u_ref, w_ref, qe_ref, kd_ref, aqk_ref, dec_ref, xbuf_ref, *, H, DK, DV, C):
    c = pl.program_id(1)
    QK = H * DK
    P = PREV_ROWS

    prev = prev_ref[...].astype(F32)
    xbuf_ref[0:P, :] = jnp.where(c == 0, 0.0, prev)
    xbuf_ref[P:P + C, :] = qkv_ref[...].astype(F32)
    y = None
    for j in range(CONV_WIDTH):
        o = P - (CONV_WIDTH - 1) + j
        term = xbuf_ref[o:o + C, :] * cw_ref[j:j + 1, :]
        y = term if y is None else y + term
    y = _silu(y)

    sm = sm_ref[...]
    beta_all = jax.nn.sigmoid(sm)
    g_all = -jnp.exp(gp_ref[0:1, :]) * _softplus(sm + gp_ref[1:2, :])
    row = lax.broadcasted_iota(jnp.int32, (C, C), 0)
    col = lax.broadcasted_iota(jnp.int32, (C, C), 1)
    tril = (col <= row).astype(F32)
    gam_all = jnp.dot(tril, g_all, precision=HI, preferred_element_type=F32)
    lane = lax.broadcasted_iota(jnp.int32, sm.shape, 1)
    rows_t = jnp.where(lane < 8, beta_all, gam_all).T

    a_list, at_list, rhs_list = [], [], []
    for h in range(H):
        qh = y[:, h * DK:(h + 1) * DK]
        kh = y[:, QK + h * DK:QK + (h + 1) * DK]
        vh = y[:, 2 * QK + h * DV:2 * QK + (h + 1) * DV]
        qh = qh * lax.rsqrt(jnp.sum(qh * qh, axis=-1, keepdims=True) + NORM_EPS) * (DK ** -0.5)
        kh = kh * lax.rsqrt(jnp.sum(kh * kh, axis=-1, keepdims=True) + NORM_EPS)
        beta_c = beta_all[:, h:h + 1]
        gam_c = gam_all[:, 8 + h:9 + h]
        beta_r = rows_t[h:h + 1, :]
        gam_r = rows_t[8 + h:9 + h, :]
        dm = gam_c - gam_r
        decay = jnp.exp(jnp.where(col <= row, dm, -jnp.inf))
        decay_t = jnp.exp(jnp.where(row < col, -dm, -jnp.inf))
        qk_kk = _bdot_g(jnp.concatenate([qh, kh], axis=0), kh, NT)
        kk = qk_kk[C:]
        a_list.append(jnp.where(col < row, kk * beta_c * decay, 0.0))
        at_list.append(kk * beta_r * decay_t)
        egam = jnp.exp(gam_c)
        rhs_list.append(jnp.concatenate([beta_c * vh, beta_c * egam * kh], axis=1).astype(BF16))
        g_last = gam_c[C - 1:C, :]
        qe_ref[:, h * DK:(h + 1) * DK] = (qh * egam).astype(qe_ref.dtype)
        kd_ref[:, h * DK:(h + 1) * DK] = (kh * jnp.exp(g_last - gam_c)).astype(kd_ref.dtype)
        aqk_ref[:, h * C:(h + 1) * C] = (qk_kk[:C] * decay).astype(aqk_ref.dtype)
        dec_ref[h:h + 1, :] = jnp.broadcast_to(jnp.exp(g_last), (1, LANE))

    t_inv = _unit_lower_inverses(a_list, at_list, g_ref, C)
    for h in range(H):
        t_hi, t_lo = _split_bf16(t_inv[h])
        uw = (jnp.dot(t_hi, rhs_list[h], preferred_element_type=F32)
              + jnp.dot(t_lo, rhs_list[h], preferred_element_type=F32))
        u_ref[:, h * DV:(h + 1) * DV] = uw[:, :DV].astype(u_ref.dtype)
        w_ref[:, h * DK:(h + 1) * DK] = uw[:, DV:].astype(w_ref.dtype)


def _gdn_scan_kernel(u_ref, w_ref, qe_ref, kd_ref, aqk_ref, dec_ref, z_ref, nw_ref, og_ref, s_out_ref, s_ref,
                     *, H, DK, DV, C):
    c = pl.program_id(1)

    @pl.when(c == 0)
    def _():
        s_ref[...] = jnp.zeros_like(s_ref)

    ks = [slice(h * DK, (h + 1) * DK) for h in range(H)]
    vs = [slice(h * DV, (h + 1) * DV) for h in range(H)]
    s16 = [s_ref[h].astype(BF16) for h in range(H)]
    ws = [jnp.dot(w_ref[:, ks[h]], s16[h], preferred_element_type=F32) for h in range(H)]
    qs = [jnp.dot(qe_ref[:, ks[h]], s16[h], preferred_element_type=F32) for h in range(H)]
    v16 = [(u_ref[:, vs[h]].astype(F32) - ws[h]).astype(BF16) for h in range(H)]
    ds = [lax.dot_general(kd_ref[:, ks[h]], v16[h], TN, preferred_element_type=F32) for h in range(H)]
    o = [qs[h] + jnp.dot(aqk_ref[:, h * C:(h + 1) * C], v16[h], preferred_element_type=F32) for h in range(H)]
    for h in range(H):
        s_ref[h] = s_ref[h] * dec_ref[h:h + 1, :] + ds[h]
        zh = z_ref[:, vs[h]].astype(F32)
        og_ref[:, vs[h]] = (_rms(o[h]) * nw_ref[...] * _silu(zh)).astype(og_ref.dtype)

    @pl.when(c == pl.num_programs(1) - 1)
    def _():
        s_out_ref[...] = s_ref[...]


def _gdn_prompt(qkv, z, small, ba_col, cw_t, gparams, norm_w, B, T, H, DK, DV):
    C = GDN_CHUNK
    nc = T // C
    n, W = qkv.shape
    kw = dict(H=H, DK=DK, DV=DV, C=C)
    blk = lambda w: pl.BlockSpec((C, w), lambda b, c: (b * nc + c, 0))
    ppc = C // PREV_ROWS
    u, w, qe, kd, aqk, dec = pl.pallas_call(
        functools.partial(_gdn_prep_kernel, **kw), grid=(B, nc),
        in_specs=[blk(W),
                  pl.BlockSpec((PREV_ROWS, W), lambda b, c: (jnp.maximum((b * nc + c) * ppc - 1, 0), 0)),
                  pl.BlockSpec((C, LANE), lambda b, c: (b * nc + c, ba_col)),
                  pl.BlockSpec((CONV_WIDTH, W), lambda b, c: (0, 0)),
                  pl.BlockSpec((2, LANE), lambda b, c: (0, 0)),
                  pl.BlockSpec((C, SUBLANES * LANE), lambda b, c: (0, 0))],
        out_specs=[blk(H * DV), blk(H * DK), blk(H * DK), blk(H * DK), blk(H * C),
                   pl.BlockSpec((None, SUBLANES, LANE), lambda b, c: (b * nc + c, 0, 0))],
        out_shape=[jax.ShapeDtypeStruct((n, H * DV), BF16), jax.ShapeDtypeStruct((n, H * DK), BF16),
                   jax.ShapeDtypeStruct((n, H * DK), BF16), jax.ShapeDtypeStruct((n, H * DK), BF16),
                   jax.ShapeDtypeStruct((n, H * C), BF16), jax.ShapeDtypeStruct((B * nc, SUBLANES, LANE), F32)],
        scratch_shapes=[pltpu.VMEM((C + PREV_ROWS, W), F32)],
        compiler_params=_cp(("parallel", "parallel")))(qkv, qkv, small, cw_t, gparams, _expand_matrix(C))
    return pl.pallas_call(
        functools.partial(_gdn_scan_kernel, **kw), grid=(B, nc),
        in_specs=[blk(H * DV), blk(H * DK), blk(H * DK), blk(H * DK), blk(H * C),
                  pl.BlockSpec((None, SUBLANES, LANE), lambda b, c: (b * nc + c, 0, 0)),
                  blk(H * DV), pl.BlockSpec((1, DV), lambda b, c: (0, 0))],
        out_specs=[blk(H * DV), pl.BlockSpec((None, H, DK, DV), lambda b, c: (b, 0, 0, 0))],
        out_shape=[jax.ShapeDtypeStruct((n, H * DV), BF16), jax.ShapeDtypeStruct((B, H, DK, DV), F32)],
        scratch_shapes=[pltpu.VMEM((H, DK, DV), F32)],
        compiler_params=_cp(("parallel", "arbitrary")))(u, w, qe, kd, aqk, dec, z, norm_w.reshape(1, DV))


def _gdn_decode_kernel(u_ref, z_ref, ba_ref, buf_ref, s_in_ref, cw_ref, gp_ref, nw_ref,
                       og_ref, s_out_ref, buf_out_ref, *, H, DK, DV):
    QK = H * DK
    u = u_ref[...]
    buf = buf_ref[...]
    y = buf[0:1] * cw_ref[0:1, :]
    y = y + buf[1:2] * cw_ref[1:2, :]
    y = y + buf[2:3] * cw_ref[2:3, :]
    y = y + u * cw_ref[3:4, :]
    buf_out_ref[0:2, :] = buf[1:3]
    buf_out_ref[2:3, :] = u
    y = _silu(y)
    sm = ba_ref[...]
    beta_all = jax.nn.sigmoid(sm)
    g_all = -jnp.exp(gp_ref[0:1, :]) * _softplus(sm + gp_ref[1:2, :])
    rows = []
    for h in range(H):
        kh = y[:, QK + h * DK:QK + (h + 1) * DK]
        rows.append(kh * lax.rsqrt(jnp.sum(kh * kh, axis=-1, keepdims=True) + NORM_EPS))
    for h in range(H):
        qh = y[:, h * DK:(h + 1) * DK]
        rows.append(qh * lax.rsqrt(jnp.sum(qh * qh, axis=-1, keepdims=True) + NORM_EPS) * (DK ** -0.5))
    rows.append(jnp.zeros((DK - 2 * H, DK), F32))
    cols = jnp.concatenate(rows, axis=0).T
    for h in range(H):
        kcol = cols[:, h:h + 1]
        qcol = cols[:, H + h:H + h + 1]
        vh = y[:, 2 * QK + h * DV:2 * QK + (h + 1) * DV]
        s_dec = s_in_ref[h] * jnp.exp(g_all[:, 8 + h:9 + h])
        v_new = beta_all[:, h:h + 1] * (vh - jnp.sum(kcol * s_dec, axis=0, keepdims=True))
        s_new = s_dec + kcol * v_new
        s_out_ref[h] = s_new
        o = jnp.sum(qcol * s_new, axis=0, keepdims=True)
        zh = z_ref[:, h * DV:(h + 1) * DV].astype(F32)
        og_ref[:, h * DV:(h + 1) * DV] = (_rms(o) * nw_ref[...] * _silu(zh)).astype(og_ref.dtype)


def _gdn_decode(qkv, z, small, ba_col, conv_buf, ssm, cw_t, gparams, norm_w, H, DK, DV):
    Bs, W = qkv.shape
    kern = functools.partial(_gdn_decode_kernel, H=H, DK=DK, DV=DV)
    og, s_new, buf_new = pl.pallas_call(
        kern, grid=(Bs,),
        in_specs=[pl.BlockSpec((None, 1, W), lambda b: (b, 0, 0)),
                  pl.BlockSpec((None, 1, H * DV), lambda b: (b, 0, 0)),
                  pl.BlockSpec((None, 1, LANE), lambda b: (b, 0, ba_col)),
                  pl.BlockSpec((None, CONV_WIDTH - 1, W), lambda b: (b, 0, 0)),
                  pl.BlockSpec((None, H, DK, DV), lambda b: (b, 0, 0, 0)),
                  pl.BlockSpec((CONV_WIDTH, W), lambda b: (0, 0)),
                  pl.BlockSpec((2, LANE), lambda b: (0, 0)),
                  pl.BlockSpec((1, DV), lambda b: (0, 0))],
        out_specs=[pl.BlockSpec((None, 1, H * DV), lambda b: (b, 0, 0)),
                   pl.BlockSpec((None, H, DK, DV), lambda b: (b, 0, 0, 0)),
                   pl.BlockSpec((None, CONV_WIDTH - 1, W), lambda b: (b, 0, 0))],
        out_shape=[jax.ShapeDtypeStruct((Bs, 1, H * DV), BF16),
                   jax.ShapeDtypeStruct(ssm.shape, F32),
                   jax.ShapeDtypeStruct(conv_buf.shape, F32)],
        compiler_params=_cp(("parallel",)))(
            qkv.reshape(Bs, 1, W), z.reshape(Bs, 1, H * DV), small.reshape(Bs, 1, small.shape[1]),
            conv_buf, ssm, cw_t, gparams, norm_w.reshape(1, DV))
    return og.reshape(Bs, H * DV), s_new, buf_new


def _mla_prep_prompt_kernel(sm_ref, cos_ref, sin_ref, qnw_ref, kvnw_ref, wq_ref, wqr_ref, wuk_ref, wuv_ref,
                            q_ref, k_ref, v_ref, lat_ref, pe_ref, *, H, NOPE, ROPE, QL, KVL, scale):
    sm = sm_ref[...]
    cos = cos_ref[...]
    sin = sin_ref[...]
    qn = _rms(sm[:, :QL]) * qnw_ref[...]
    qf = _bdot(qn, wq_ref[...])
    qr = _bdot(qn, wqr_ref[...])
    lat = _rms(sm[:, QL:QL + KVL]) * kvnw_ref[...]
    lat_ref[...] = lat
    kn = _bdot(lat, wuk_ref[...])
    v_ref[...] = _bdot(lat, wuv_ref[...]).astype(v_ref.dtype)
    o = QL + KVL
    kr = sm[:, o:o + ROPE] * cos + sm[:, o + ROPE:o + 2 * ROPE] * sin
    pe_ref[...] = kr
    for h in range(H):
        p0 = H * NOPE + h * ROPE
        qp = qf[:, p0:p0 + ROPE] * cos + qr[:, h * ROPE:(h + 1) * ROPE] * sin
        q_ref[h] = (jnp.concatenate([qf[:, h * NOPE:(h + 1) * NOPE], qp], axis=-1) * scale).astype(q_ref.dtype)
        k_ref[h] = jnp.concatenate([kn[:, h * NOPE:(h + 1) * NOPE], kr], axis=-1).astype(k_ref.dtype)


def _mla_prep_prompt(small, cos, sin, qnw, kvnw, wq, wqr, wuk, wuv, T, tm, H, NOPE, ROPE, VH, QL, KVL, scale):
    n, ws = small.shape
    nt = T // tm
    kern = functools.partial(_mla_prep_prompt_kernel, H=H, NOPE=NOPE, ROPE=ROPE, QL=QL, KVL=KVL, scale=scale)
    full = lambda a: pl.BlockSpec(a.shape, lambda i: (0,) * a.ndim)
    dqk = NOPE + ROPE
    return pl.pallas_call(
        kern, grid=(n // tm,),
        in_specs=[pl.BlockSpec((tm, ws), lambda i: (i, 0)),
                  pl.BlockSpec((tm, ROPE), lambda i: (i % nt, 0)),
                  pl.BlockSpec((tm, ROPE), lambda i: (i % nt, 0)),
                  full(qnw), full(kvnw), full(wq), full(wqr), full(wuk), full(wuv)],
        out_specs=[pl.BlockSpec((H, tm, dqk), lambda i: (0, i, 0)),
                   pl.BlockSpec((H, tm, dqk), lambda i: (0, i, 0)),
                   pl.BlockSpec((tm, H * VH), lambda i: (i, 0)),
                   pl.BlockSpec((tm, KVL), lambda i: (i, 0)),
                   pl.BlockSpec((tm, ROPE), lambda i: (i, 0))],
        out_shape=[jax.ShapeDtypeStruct((H, n, dqk), BF16), jax.ShapeDtypeStruct((H, n, dqk), BF16),
                   jax.ShapeDtypeStruct((n, H * VH), BF16), jax.ShapeDtypeStruct((n, KVL), F32),
                   jax.ShapeDtypeStruct((n, ROPE), F32)],
        compiler_params=_cp(("parallel",)))(small, cos, sin, qnw, kvnw, wq, wqr, wuk, wuv)


def _mla_prep_sample_kernel(sm_ref, cos_ref, sin_ref, qnw_ref, kvnw_ref, wq_ref, wqr_ref,
                            qn_ref, qp_ref, lat_ref, pe_ref, *, H, NOPE, ROPE, QL, KVL, scale):
    sm = sm_ref[...]
    cos = cos_ref[...]
    sin = sin_ref[...]
    qn = _rms(sm[:, :QL]) * qnw_ref[...]
    qf = _bdot(qn, wq_ref[...])
    qr = _bdot(qn, wqr_ref[...])
    qn_ref[...] = qf[:, :H * NOPE].astype(qn_ref.dtype)
    lat_ref[...] = _rms(sm[:, QL:QL + KVL]) * kvnw_ref[...]
    o = QL + KVL
    pe_ref[...] = sm[:, o:o + ROPE] * cos + sm[:, o + ROPE:o + 2 * ROPE] * sin
    for h in range(H):
        p0 = H * NOPE + h * ROPE
        qp = qf[:, p0:p0 + ROPE] * cos + qr[:, h * ROPE:(h + 1) * ROPE] * sin
        qp_ref[:, h * ROPE:(h + 1) * ROPE] = (qp * scale).astype(qp_ref.dtype)


def _mla_prep_sample(small, cos, sin, qnw, kvnw, wq, wqr, H, NOPE, ROPE, QL, KVL, scale):
    n, ws = small.shape
    kern = functools.partial(_mla_prep_sample_kernel, H=H, NOPE=NOPE, ROPE=ROPE, QL=QL, KVL=KVL, scale=scale)
    full = lambda a: pl.BlockSpec(a.shape, lambda i: (0,) * a.ndim)
    return pl.pallas_call(
        kern, grid=(1,),
        in_specs=[full(small), full(cos), full(sin), full(qnw), full(kvnw), full(wq), full(wqr)],
        out_specs=[pl.BlockSpec((n, H * NOPE), lambda i: (0, 0)), pl.BlockSpec((n, H * ROPE), lambda i: (0, 0)),
                   pl.BlockSpec((n, KVL), lambda i: (0, 0)), pl.BlockSpec((n, ROPE), lambda i: (0, 0))],
        out_shape=[jax.ShapeDtypeStruct((n, H * NOPE), BF16), jax.ShapeDtypeStruct((n, H * ROPE), BF16),
                   jax.ShapeDtypeStruct((n, KVL), F32), jax.ShapeDtypeStruct((n, ROPE), F32)],
        compiler_params=_cp(("arbitrary",)))(small, cos, sin, qnw, kvnw, wq, wqr)


def _head_proj_kernel(x_ref, w_ref, o_ref, *, dims, scale):
    o_ref[...] = (_bdot_g(x_ref[...], w_ref[...], dims) * scale).astype(o_ref.dtype)


def _q_latent(q_nope, w_uk, scale):
    H, KVL, NOPE = w_uk.shape
    n = q_nope.shape[0]
    return pl.pallas_call(
        functools.partial(_head_proj_kernel, dims=NT, scale=scale), grid=(H,),
        in_specs=[pl.BlockSpec((n, NOPE), lambda h: (0, h)), pl.BlockSpec((None, KVL, NOPE), lambda h: (h, 0, 0))],
        out_specs=pl.BlockSpec((None, n, KVL), lambda h: (h, 0, 0)),
        out_shape=jax.ShapeDtypeStruct((H, n, KVL), BF16),
        compiler_params=_cp(("parallel",)))(q_nope, w_uk)


def _o_value(o_lat, w_uv):
    H, KVL, VH = w_uv.shape
    n = o_lat.shape[1]
    return pl.pallas_call(
        functools.partial(_head_proj_kernel, dims=(((1,), (0,)), ((), ())), scale=1.0), grid=(H,),
        in_specs=[pl.BlockSpec((None, n, KVL), lambda h: (h, 0, 0)), pl.BlockSpec((None, KVL, VH), lambda h: (h, 0, 0))],
        out_specs=pl.BlockSpec((n, VH), lambda h: (0, h)),
        out_shape=jax.ShapeDtypeStruct((n, H * VH), BF16),
        compiler_params=_cp(("parallel",)))(o_lat, w_uv)


def _flash_kernel(q_ref, k_ref, v_ref, o_ref, m_ref, l_ref, acc_ref, *, tq):
    qi = pl.program_id(2)
    kj = pl.program_id(3)

    @pl.when(kj == 0)
    def _():
        m_ref[...] = jnp.full_like(m_ref, -jnp.inf)
        l_ref[...] = jnp.zeros_like(l_ref)
        acc_ref[...] = jnp.zeros_like(acc_ref)

    @pl.when(kj <= qi)
    def _():
        s = lax.dot_general(q_ref[...], k_ref[...], NT, preferred_element_type=F32)
        row = qi * tq + lax.broadcasted_iota(jnp.int32, s.shape, 0)
        col = kj * tq + lax.broadcasted_iota(jnp.int32, s.shape, 1)
        s = jnp.where(col <= row, s, -jnp.inf)
        m_prev = m_ref[...]
        m_new = jnp.maximum(m_prev, jnp.max(s, axis=-1, keepdims=True))
        p = jnp.exp(s - m_new)
        corr = jnp.exp(m_prev - m_new)
        l_ref[...] = corr * l_ref[...] + jnp.sum(p, axis=-1, keepdims=True)
        acc_ref[...] = acc_ref[...] * corr + jnp.dot(p.astype(BF16), v_ref[...], preferred_element_type=F32)
        m_ref[...] = m_new

    @pl.when(kj == qi)
    def _():
        o_ref[...] = (acc_ref[...] / l_ref[...]).astype(o_ref.dtype)


def _flash_attention(q, k, v, B, T, tq, VH):
    H, n, dqk = q.shape
    nq = T // tq
    kern = functools.partial(_flash_kernel, tq=tq)
    return pl.pallas_call(
        kern, grid=(B, H, nq, nq),
        in_specs=[pl.BlockSpec((None, tq, dqk), lambda b, h, i, j: (h, b * nq + i, 0)),
                  pl.BlockSpec((None, tq, dqk), lambda b, h, i, j: (h, b * nq + jnp.minimum(i, j), 0)),
                  pl.BlockSpec((tq, VH), lambda b, h, i, j: (b * nq + jnp.minimum(i, j), h))],
        out_specs=pl.BlockSpec((tq, VH), lambda b, h, i, j: (b * nq + i, h)),
        out_shape=jax.ShapeDtypeStruct((n, H * VH), BF16),
        scratch_shapes=[pltpu.VMEM((tq, 1), F32), pltpu.VMEM((tq, 1), F32), pltpu.VMEM((tq, VH), F32)],
        compiler_params=_cp(("parallel", "parallel", "parallel", "arbitrary")))(q, k, v)


def _paged_kernel(pt_ref, ql_ref, qp_ref, latn_ref, pen_ref, *refs, R, G):
    kv_refs, pe_refs = refs[:R * G], refs[R * G:2 * R * G]
    o_ref, m_ref, l_ref, acc_ref = refs[2 * R * G:]
    g = pl.program_id(1)

    @pl.when(g == 0)
    def _():
        m_ref[...] = jnp.full_like(m_ref, -jnp.inf)
        l_ref[...] = jnp.zeros_like(l_ref)
        acc_ref[...] = jnp.zeros_like(acc_ref)

    kvs = [[kv_refs[r * G + i][...].astype(BF16) for i in range(G)] for r in range(R)]
    P = kvs[0][0].shape[0]
    s, p, corr, m_new = [], [], [], []
    for r in range(R):
        ql = ql_ref[r]
        qp = qp_ref[r]
        s.append(jnp.concatenate(
            [lax.dot_general(ql, kvs[r][i], NT, preferred_element_type=F32)
             + jnp.dot(qp, pe_refs[r * G + i][...].astype(BF16), preferred_element_type=F32) for i in range(G)],
            axis=-1))
    for r in range(R):
        m_prev = m_ref[r]
        m_new.append(jnp.maximum(m_prev, jnp.max(s[r], axis=-1, keepdims=True)))
        corr.append(jnp.exp(m_prev - m_new[r]))
        p.append(jnp.exp(s[r] - m_new[r]))
    pv = []
    for r in range(R):
        t = None
        for i in range(G):
            d = jnp.dot(p[r][:, i * P:(i + 1) * P].astype(BF16), kvs[r][i], preferred_element_type=F32)
            t = d if t is None else t + d
        pv.append(t)
    for r in range(R):
        m_ref[r] = m_new[r]
        l_ref[r] = corr[r] * l_ref[r] + jnp.sum(p[r], axis=-1, keepdims=True)
        acc_ref[r] = acc_ref[r] * corr[r] + pv[r]

    @pl.when(g == pl.num_programs(1) - 1)
    def _():
        for r in range(R):
            latn = latn_ref[r]
            s_n = (jnp.sum(ql_ref[r].astype(F32) * latn, axis=-1, keepdims=True)
                   + jnp.sum(qp_ref[r].astype(F32) * pen_ref[r], axis=-1, keepdims=True))
            m_old = m_ref[r]
            m2 = jnp.maximum(m_old, s_n)
            c2 = jnp.exp(m_old - m2)
            p2 = jnp.exp(s_n - m2)
            o_ref[r] = ((acc_ref[r] * c2 + p2 * latn) / (l_ref[r] * c2 + p2)).astype(o_ref.dtype)


def _paged_attention(q_lat, q_pe, lat_new, pe_new, cache_kv, cache_pe_t, page_table, layer):
    Bs, H, KVL = q_lat.shape
    ROPE = q_pe.shape[2]
    n_pages = page_table.shape[1]
    P = cache_kv.shape[2]
    G = min(PAGE_GROUP, n_pages)
    R = min(PAGE_REQUESTS, Bs)

    def page_spec(shape, r, i):
        return pl.BlockSpec((None, None) + shape, lambda b, g, pt: (layer, pt[b * R + r, g * G + i], 0, 0))

    kv_specs = [page_spec((P, KVL), r, i) for r in range(R) for i in range(G)]
    pe_specs = [page_spec((ROPE, P), r, i) for r in range(R) for i in range(G)]
    req = lambda rows, w: pl.BlockSpec((R, rows, w), lambda b, g, pt: (b, 0, 0))
    grid_spec = pltpu.PrefetchScalarGridSpec(
        num_scalar_prefetch=1, grid=(Bs // R, n_pages // G),
        in_specs=[req(H, KVL), req(H, ROPE), req(1, KVL), req(1, ROPE)] + kv_specs + pe_specs,
        out_specs=req(H, KVL),
        scratch_shapes=[pltpu.VMEM((R, H, 1), F32), pltpu.VMEM((R, H, 1), F32), pltpu.VMEM((R, H, KVL), F32)])
    return pl.pallas_call(
        functools.partial(_paged_kernel, R=R, G=G), grid_spec=grid_spec,
        out_shape=jax.ShapeDtypeStruct((Bs, H, KVL), BF16),
        compiler_params=_cp(("parallel", "arbitrary")))(
            page_table, q_lat, q_pe, lat_new.reshape(Bs, 1, KVL), pe_new.reshape(Bs, 1, ROPE),
            *([cache_kv] * (R * G)), *([cache_pe_t] * (R * G)))


def _post_mixer_kernel(x_ref, og_ref, om_ref, gab_ref, g1_ref, sc2_ref, sh2_ref, wog_ref, wom_ref, wout_ref,
                       nw_ref, wr_ref, br_ref, x1_ref, h2_ref, te_ref, tg_ref, *, D, E):
    gab = gab_ref[...].astype(F32)
    merged = (jax.nn.sigmoid(gab[:, :D]) * _bdot(og_ref[...], wog_ref[...])
              + jax.nn.sigmoid(gab[:, D:]) * _bdot(om_ref[...], wom_ref[...]))
    x1 = x_ref[...] + g1_ref[...] * _bdot(merged, wout_ref[...])
    x1_ref[...] = x1
    h2 = _rms(x1) * nw_ref[...] * (1.0 + sc2_ref[...]) + sh2_ref[...]
    h2_ref[...] = h2
    logits = jnp.dot(h2, wr_ref[...], precision=HI, preferred_element_type=F32) + br_ref[...]
    lane = lax.broadcasted_iota(jnp.int32, logits.shape, 1)
    logits = jnp.where(lane < E, logits, -jnp.inf)
    te = jnp.zeros(logits.shape, jnp.int32)
    ex = jnp.zeros(logits.shape, F32)
    top = None
    for k in range(TOP_K):
        m = jnp.max(logits, axis=-1, keepdims=True)
        idx = jnp.min(jnp.where(logits == m, lane, LANE), axis=-1, keepdims=True)
        top = m if top is None else top
        te = jnp.where(lane == k, idx, te)
        ex = jnp.where(lane == k, jnp.exp(m - top), ex)
        logits = jnp.where(lane == idx, -jnp.inf, logits)
    te_ref[...] = te
    tg_ref[...] = ex / jnp.sum(ex, axis=-1, keepdims=True)


def _post_mixer(x, og, om, gab, grp, wog, wom, wout, norm_w, wr, br, E):
    n, d = x.shape
    tm = grp.tm
    rowblk = lambda w: pl.BlockSpec((tm, w), lambda i: (i, 0))
    full = lambda a: pl.BlockSpec(a.shape, lambda i: (0,) * a.ndim)
    return pl.pallas_call(
        functools.partial(_post_mixer_kernel, D=d, E=E), grid=(n // tm,),
        in_specs=[rowblk(d), rowblk(d), rowblk(d), rowblk(2 * d), grp.spec(2, d), grp.spec(4, d), grp.spec(3, d),
                  full(wog), full(wom), full(wout), full(norm_w), full(wr), full(br)],
        out_specs=[rowblk(d), rowblk(d), rowblk(LANE), rowblk(LANE)],
        out_shape=[jax.ShapeDtypeStruct((n, d), F32), jax.ShapeDtypeStruct((n, d), F32),
                   jax.ShapeDtypeStruct((n, LANE), jnp.int32), jax.ShapeDtypeStruct((n, LANE), F32)],
        compiler_params=_cp(("parallel",)))(x, og, om, gab, grp.mod3, grp.mod3, grp.mod3,
                                            wog, wom, wout, norm_w, wr, br)


def _dispatch_kernel(slot_ref, h_ref, init_ref, hs_ref, sem, *, tb):
    del init_ref

    def row_copy(r, s):
        return pltpu.make_async_copy(h_ref.at[pl.ds(r, 1)], hs_ref.at[pl.ds(s, 1)], sem)

    def start(r, carry):
        for k in range(TOP_K):
            row_copy(r, slot_ref[0, 0, r * TOP_K + k]).start()
        return carry

    def wait(r, carry):
        for k in range(TOP_K):
            row_copy(0, 0).wait()
        return carry

    lax.fori_loop(0, tb, start, 0)
    lax.fori_loop(0, tb, wait, 0)


def _dispatch(h, slots, n_slots, tb):
    n, d = h.shape
    nb = n // tb
    return pl.pallas_call(
        functools.partial(_dispatch_kernel, tb=tb), grid=(nb,),
        in_specs=[pl.BlockSpec((1, 1, tb * TOP_K), lambda i: (i, 0, 0), memory_space=pltpu.SMEM),
                  pl.BlockSpec((tb, d), lambda i: (i, 0)),
                  pl.BlockSpec(memory_space=pl.ANY)],
        out_specs=pl.BlockSpec(memory_space=pl.ANY),
        out_shape=jax.ShapeDtypeStruct((n_slots, d), h.dtype),
        scratch_shapes=[pltpu.SemaphoreType.DMA],
        input_output_aliases={2: 0},
        compiler_params=_cp(("arbitrary",)))(slots.reshape(nb, 1, tb * TOP_K), h, jnp.zeros((n_slots, d), h.dtype))


def _expert_kernel(be_ref, na_ref, x_ref, wgu_ref, bgu_ref, wdn_ref, bdn_ref, y_ref, wgu_s, wdn_s, *, DE):
    i = pl.program_id(0)
    active = i < na_ref[0]
    first = jnp.logical_or(i == 0, be_ref[i] != be_ref[jnp.maximum(i - 1, 0)])

    @pl.when(jnp.logical_and(active, first))
    def _():
        wgu_s[...] = wgu_ref[...].astype(BF16)
        wdn_s[...] = wdn_ref[...].astype(BF16)

    @pl.when(active)
    def _():
        gu = jnp.dot(x_ref[...].astype(BF16), wgu_s[...], preferred_element_type=F32) + bgu_ref[...]
        gt = jnp.minimum(gu[:, :DE], SWIGLU_LIMIT)
        up = jnp.clip(gu[:, DE:], -SWIGLU_LIMIT, SWIGLU_LIMIT)
        act = (up + 1.0) * gt * jax.nn.sigmoid(SWIGLU_ALPHA * gt)
        y_ref[...] = jnp.dot(act.astype(BF16), wdn_s[...], preferred_element_type=F32) + bdn_ref[...]

    @pl.when(jnp.logical_not(active))
    def _():
        y_ref[...] = jnp.zeros_like(y_ref)


def _experts(hs, blk_e, n_active, w_gu, b_gu, w_dn, b_dn, layer, tm):
    n_slots, d = hs.shape
    E, _, de2 = w_gu.shape[1:]
    de = de2 // 2
    grid_spec = pltpu.PrefetchScalarGridSpec(
        num_scalar_prefetch=2, grid=(n_slots // tm,),
        in_specs=[pl.BlockSpec((tm, d), lambda i, be, na: (i, 0)),
                  pl.BlockSpec((None, None, d, de2), lambda i, be, na: (layer, be[i], 0, 0)),
                  pl.BlockSpec((None, None, 1, de2), lambda i, be, na: (layer, be[i], 0, 0)),
                  pl.BlockSpec((None, None, de, d), lambda i, be, na: (layer, be[i], 0, 0)),
                  pl.BlockSpec((None, None, 1, d), lambda i, be, na: (layer, be[i], 0, 0))],
        out_specs=pl.BlockSpec((tm, d), lambda i, be, na: (i, 0)),
        scratch_shapes=[pltpu.VMEM((d, de2), BF16), pltpu.VMEM((de, d), BF16)])
    L = w_gu.shape[0]
    return pl.pallas_call(
        functools.partial(_expert_kernel, DE=de), grid_spec=grid_spec,
        out_shape=jax.ShapeDtypeStruct((n_slots, d), F32),
        compiler_params=_cp(("arbitrary",), VMEM_LIMIT_MOE))(
            blk_e, n_active, hs, w_gu, b_gu.reshape(L, E, 1, de2), w_dn, b_dn.reshape(L, E, 1, d))


def _combine_kernel(slot_ref, tg_ref, x1_ref, g2_ref, scf_ref, shf_ref, nwf_ref, yp_ref, o_ref, buf, sem, *, tb):
    def row_copy(r, k, s):
        return pltpu.make_async_copy(yp_ref.at[pl.ds(s, 1)], buf.at[k, pl.ds(r, 1)], sem)

    def start(r, carry):
        for k in range(TOP_K):
            row_copy(r, k, slot_ref[0, 0, r * TOP_K + k]).start()
        return carry

    def wait(r, carry):
        for k in range(TOP_K):
            row_copy(0, k, 0).wait()
        return carry

    lax.fori_loop(0, tb, start, 0)
    lax.fori_loop(0, tb, wait, 0)
    tg = tg_ref[...]
    moe = tg[:, 0:1] * buf[0]
    for k in range(1, TOP_K):
        moe = moe + tg[:, k:k + 1] * buf[k]
    x2 = x1_ref[...] + g2_ref[...] * moe
    o_ref[...] = _rms(x2) * nwf_ref[...] * (1.0 + scf_ref[...]) + shf_ref[...]


def _combine_final(y_pad, slots, tg, x1, grp, norm_final_w):
    n, d = x1.shape
    tb = grp.tm
    nb = n // tb
    rf, bpg = grp.modf3.shape[1], grp.bpg
    fspec = lambda j: pl.BlockSpec((None, rf, d), lambda i: (i // bpg, 0, j))
    return pl.pallas_call(
        functools.partial(_combine_kernel, tb=tb), grid=(nb,),
        in_specs=[pl.BlockSpec((1, 1, tb * TOP_K), lambda i: (i, 0, 0), memory_space=pltpu.SMEM),
                  pl.BlockSpec((tb, LANE), lambda i: (i, 0)),
                  pl.BlockSpec((tb, d), lambda i: (i, 0)),
                  grp.spec(5, d), fspec(1), fspec(0),
                  pl.BlockSpec((1, d), lambda i: (0, 0)),
                  pl.BlockSpec(memory_space=pl.ANY)],
        out_specs=pl.BlockSpec((tb, d), lambda i: (i, 0)),
        out_shape=jax.ShapeDtypeStruct((n, d), F32),
        scratch_shapes=[pltpu.VMEM((TOP_K, tb, d), F32), pltpu.SemaphoreType.DMA],
        compiler_params=_cp(("arbitrary",)))(
            slots.reshape(nb, 1, tb * TOP_K), tg, x1, grp.mod3, grp.modf3, grp.modf3,
            norm_final_w.reshape(1, d), y_pad)


def _route(top_e, E, tm):
    n = top_e.shape[0]
    tok_oh = jnp.sum((top_e[:, :, None] == jnp.arange(E, dtype=jnp.int32)).astype(jnp.int32), axis=1)
    csum = jnp.cumsum(tok_oh, axis=0)
    counts = csum[-1]
    rank = jnp.take_along_axis(csum - tok_oh, top_e, axis=1)
    padded = (counts + tm - 1) // tm * tm
    pad_end = jnp.cumsum(padded)
    slots = (pad_end - padded)[top_e] + rank
    n_blocks = -(-(n * TOP_K) // tm) + E
    blk_e = jnp.minimum(jnp.searchsorted(pad_end, jnp.arange(n_blocks, dtype=jnp.int32) * tm, side='right'), E - 1)
    n_active = (pad_end[-1] // tm).reshape(1)
    return slots.astype(jnp.int32), blk_e.astype(jnp.int32), n_active.astype(jnp.int32), n_blocks * tm


def _rope_tables(pos, rope):
    half = rope // 2
    inv = ROPE_THETA ** (-jnp.arange(half, dtype=F32) / half)
    ang = pos.astype(F32)[:, None] * inv[None, :]
    cos, sin = jnp.cos(ang), jnp.sin(ang)
    return jnp.concatenate([cos, cos], axis=-1), jnp.concatenate([sin, sin], axis=-1)


def _rotate_cols(w, rope):
    k, n = w.shape
    w3 = w.reshape(k, n // rope, rope)
    half = rope // 2
    return jnp.concatenate([-w3[..., half:], w3[..., :half]], axis=-1).reshape(k, n)


def kernel(x_prompt, x_sample, c_prompt, c_sample, cache_kv, cache_pe, state_ssm, state_conv, page_table, norm_mix_w, norm_ffn_w, w_ada, b_ada, w_in, conv_w, A_log, dt_bias, gdn_norm_w, w_o_gdn, q_norm_w, kv_norm_w, w_uq, w_uk, w_uv, w_o_mla, w_out, w_router, b_router, w_gu, b_gu, w_dn, b_dn, w_ada_final, b_ada_final, norm_final_w):
    B, T, D = x_prompt.shape
    Bs, Ts, _ = x_sample.shape
    assert Ts == 1
    depth = w_in.shape[0]
    H, DK, DV = state_ssm.shape[2:]
    QK = H * DK
    CONV = state_conv.shape[3]
    assert CONV == 2 * QK + H * DV and H == SUBLANES
    QL = q_norm_w.shape[1]
    HM, KVL, NOPE = w_uk.shape[1:]
    VH = w_uv.shape[3]
    ROPE = cache_pe.shape[3]
    E = w_router.shape[2]
    scale = float(NOPE + ROPE) ** -0.5
    n_p, n_s = B * T, Bs * Ts
    past_len = page_table.shape[1] * cache_kv.shape[2]
    cache_pe_t = jnp.swapaxes(cache_pe, 2, 3)

    c_all = jnp.concatenate([c_prompt, c_sample], axis=0)
    modf = _matmul(c_all, w_ada_final, F32, c_all.shape[0], 1024, b_ada_final)
    tm_p = min(512, T)
    tb = 128
    cos_p, sin_p = _rope_tables(jnp.arange(T), ROPE)
    cos_s, sin_s = _rope_tables(past_len + jnp.arange(Ts), ROPE)

    hp = x_prompt.reshape(n_p, D)
    hs = x_sample.reshape(n_s, D)
    outs = {k: [] for k in ('kv_p', 'pe_p', 'ssm_p', 'conv_p', 'kv_s', 'pe_s', 'ssm_s', 'conv_s')}
    for l in range(depth):
        mod = _matmul(c_all, w_ada[l], F32, c_all.shape[0], 1024, b_ada[l])
        last = l == depth - 1
        grp_p = _Group(mod[:B].reshape(B, 1, 6 * D), modf[:B].reshape(B, 1, 2 * D), tm_p, T // tm_p)
        grp_s = _Group(mod[B:].reshape(1, Bs, 6 * D), modf[B:].reshape(1, Bs, 2 * D), Bs, 1)
        grp_pc = _Group(grp_p.mod3, grp_p.modf3, tb, T // tb)

        offs = [0]
        for s in (CONV, H * DV, H, H, QL, KVL, ROPE, D, D):
            offs.append(offs[-1] + s)
        wi = w_in[l]
        seg = lambda i: wi[:, offs[i]:offs[i + 1]]
        w_qkv = seg(0).astype(BF16)
        w_z = seg(1).astype(BF16)
        w_gab = jnp.concatenate([seg(7), seg(8)], axis=1).astype(BF16)
        n_small = QL + KVL + 2 * ROPE
        ba_col = -(-n_small // LANE)
        w_small = jnp.concatenate(
            [seg(4), seg(5), seg(6), _rotate_cols(seg(6), ROPE), jnp.zeros((D, ba_col * LANE - n_small), F32),
             seg(2), jnp.zeros((D, 8 - H), F32), seg(3), jnp.zeros((D, LANE - 8 - H), F32)], axis=1).astype(BF16)
        cw_t = conv_w[l].T
        gparams = jnp.zeros((2, LANE), F32).at[0, 8:8 + H].set(A_log[l]).at[1, 8:8 + H].set(dt_bias[l])
        wq = w_uq[l].reshape(QL, HM, NOPE + ROPE)
        wq_pe = wq[:, :, NOPE:].reshape(QL, HM * ROPE)
        wq_all = jnp.concatenate([wq[:, :, :NOPE].reshape(QL, HM * NOPE), wq_pe], axis=1).astype(BF16)
        wq_rot = _rotate_cols(wq_pe, ROPE).astype(BF16)
        wuk_all = jnp.transpose(w_uk[l], (1, 0, 2)).reshape(KVL, HM * NOPE).astype(BF16)
        wuv_all = jnp.transpose(w_uv[l], (1, 0, 2)).reshape(KVL, HM * VH).astype(BF16)
        qnw = q_norm_w[l].reshape(1, QL)
        kvnw = kv_norm_w[l].reshape(1, KVL)
        wog = w_o_gdn[l].astype(BF16)
        wom = w_o_mla[l].astype(BF16)
        wout = w_out[l].astype(BF16)
        wr = jnp.pad(w_router[l], ((0, 0), (0, LANE - E)))
        br = jnp.pad(b_router[l], (0, LANE - E)).reshape(1, LANE)
        nfw = norm_ffn_w[l].reshape(1, D)

        h1 = _norm_mod(hp, norm_mix_w[l], grp_p, 1, 0, BF16)
        qkv_p = _matmul(h1, w_qkv, BF16, 1024, 512)
        z_p = _matmul(h1, w_z, BF16, 1024, 512)
        gab_p = _matmul(h1, w_gab, BF16, 1024, 512)
        small_p = _matmul(h1, w_small, F32, 1024, w_small.shape[1])
        og_p, ssm_p = _gdn_prompt(qkv_p, z_p, small_p, ba_col, cw_t, gparams, gdn_norm_w[l], B, T, H, DK, DV)
        q_p, k_p, v_p, lat_p, pe_p = _mla_prep_prompt(small_p, cos_p, sin_p, qnw, kvnw, wq_all, wq_rot, wuk_all,
                                                      wuv_all, T, tm_p, HM, NOPE, ROPE, VH, QL, KVL, scale)
        om_p = _flash_attention(q_p, k_p, v_p, B, T, tm_p, VH)
        x1_p, h2_p, te_p, tg_p = _post_mixer(hp, og_p, om_p, gab_p, grp_p, wog, wom, wout, nfw, wr, br, E)
        outs['kv_p'].append(lat_p.reshape(B, T, KVL))
        outs['pe_p'].append(pe_p.reshape(B, T, ROPE))
        outs['ssm_p'].append(ssm_p)
        outs['conv_p'].append(qkv_p.reshape(B, T, CONV)[:, T - (CONV_WIDTH - 1):, :].astype(F32))

        h1s = _norm_mod(hs, norm_mix_w[l], grp_s, 1, 0, BF16)
        qkv_s = _matmul(h1s, w_qkv, F32, Bs, 512)
        z_s = _matmul(h1s, w_z, BF16, Bs, 512)
        gab_s = _matmul(h1s, w_gab, BF16, Bs, 512)
        small_s = _matmul(h1s, w_small, F32, Bs, w_small.shape[1])
        og_s, ssm_s, conv_s = _gdn_decode(qkv_s, z_s, small_s, ba_col, state_conv[l], state_ssm[l], cw_t, gparams,
                                          gdn_norm_w[l], H, DK, DV)
        qn_s, qp_s, lat_s, pe_s = _mla_prep_sample(small_s, cos_s, sin_s, qnw, kvnw, wq_all, wq_rot,
                                                   HM, NOPE, ROPE, QL, KVL, scale)
        q_lat = jnp.transpose(_q_latent(qn_s, w_uk[l], scale), (1, 0, 2))
        o_lat = _paged_attention(q_lat, qp_s.reshape(n_s, HM, ROPE), lat_s, pe_s, cache_kv, cache_pe_t, page_table, l)
        om_s = _o_value(jnp.transpose(o_lat, (1, 0, 2)), w_uv[l])
        x1_s, h2_s, te_s, tg_s = _post_mixer(hs, og_s, om_s, gab_s, grp_s, wog, wom, wout, nfw, wr, br, E)
        outs['kv_s'].append(lat_s.reshape(Bs, Ts, KVL))
        outs['pe_s'].append(pe_s.reshape(Bs, Ts, ROPE))
        outs['ssm_s'].append(ssm_s)
        outs['conv_s'].append(conv_s)

        tm_e = 256
        h2 = jnp.concatenate([h2_p, h2_s], axis=0)
        top_e = jnp.concatenate([te_p[:, :TOP_K], te_s[:, :TOP_K]], axis=0)
        slots, blk_e, n_active, n_slots = _route(top_e, E, tm_e)
        h_sorted = _dispatch(h2, slots, n_slots, tb)
        y_pad = _experts(h_sorted, blk_e, n_active, w_gu, b_gu, w_dn, b_dn, l, tm_e)
        if not last:
            raise NotImplementedError("stacked layers need an un-normalised residual output")
        hp = _combine_final(y_pad, slots[:n_p], tg_p, x1_p, grp_pc, norm_final_w)
        hs = _combine_final(y_pad, slots[n_p:], tg_s, x1_s, grp_s, norm_final_w)

    st = lambda k: jnp.stack(outs[k])
    return (hp.reshape(B, T, D), hs.reshape(Bs, Ts, D),
            st('kv_p'), st('pe_p'), st('ssm_p'), st('conv_p'),
            st('kv_s'), st('pe_s'), st('ssm_s'), st('conv_s'))
```

```python
import functools

import jax
import jax.numpy as jnp
from jax import lax
from jax.experimental import pallas as pl
from jax.experimental.pallas import tpu as pltpu

F32 = jnp.float32
BF16 = jnp.bfloat16
HI = lax.Precision.HIGHEST

NORM_EPS = 1e-6
ROPE_THETA = 10000.0
TOP_K = 4
SWIGLU_LIMIT = 7.0
SWIGLU_ALPHA = 1.702
CONV_WIDTH = 4
GDN_CHUNK = 64
PAGE_GROUP = 8
PAGE_REQUESTS = 4
LANE = 128
VMEM_LIMIT = 48 * 1024 * 1024
VMEM_LIMIT_MOE = 58 * 1024 * 1024

NT = (((1,), (1,)), ((), ()))
TN = (((0,), (0,)), ((), ()))


def _cp(sem, vmem=VMEM_LIMIT):
    return pltpu.CompilerParams(dimension_semantics=sem, vmem_limit_bytes=vmem)


def _rms(x):
    return x * lax.rsqrt(jnp.mean(x * x, axis=-1, keepdims=True) + NORM_EPS)


def _softplus(x):
    return jnp.maximum(x, 0.0) + jnp.log1p(jnp.exp(-jnp.abs(x)))


def _silu(x):
    return x * jax.nn.sigmoid(x)


def _bdot(a, b):
    return jnp.dot(a.astype(BF16), b.astype(BF16), preferred_element_type=F32)


def _bdot_g(a, b, dims):
    return lax.dot_general(a.astype(BF16), b.astype(BF16), dims, preferred_element_type=F32)


def _mm_kernel(*refs, has_bias):
    if has_bias:
        x_ref, w_ref, b_ref, o_ref = refs
    else:
        x_ref, w_ref, o_ref = refs
    acc = _bdot(x_ref[...], w_ref[...])
    if has_bias:
        acc = acc + b_ref[...]
    o_ref[...] = acc.astype(o_ref.dtype)


def _matmul(x, w, out_dtype, tm, tn, bias=None):
    M, K = x.shape
    N = w.shape[1]
    tm, tn = min(tm, M), min(tn, N)
    in_specs = [pl.BlockSpec((tm, K), lambda i, j: (i, 0)), pl.BlockSpec((K, tn), lambda i, j: (0, j))]
    args = [x, w]
    if bias is not None:
        in_specs.append(pl.BlockSpec((1, tn), lambda i, j: (0, j)))
        args.append(bias.reshape(1, N))
    return pl.pallas_call(
        functools.partial(_mm_kernel, has_bias=bias is not None),
        grid=(M // tm, N // tn), in_specs=in_specs,
        out_specs=pl.BlockSpec((tm, tn), lambda i, j: (i, j)),
        out_shape=jax.ShapeDtypeStruct((M, N), out_dtype),
        compiler_params=_cp(("parallel", "parallel")))(*args)


class _Group:
    def __init__(self, mod3, modf3, tm, blocks_per_g):
        self.mod3, self.modf3, self.tm, self.bpg = mod3, modf3, tm, blocks_per_g

    def spec(self, j, d):
        r, bpg = self.mod3.shape[1], self.bpg
        return pl.BlockSpec((None, r, d), lambda i: (i // bpg, 0, j))


def _norm_mod_kernel(x_ref, w_ref, sc_ref, sh_ref, o_ref):
    y = _rms(x_ref[...]) * w_ref[...]
    o_ref[...] = (y * (1.0 + sc_ref[...]) + sh_ref[...]).astype(o_ref.dtype)


def _norm_mod(x, w, grp, j_scale, j_shift, out_dtype):
    n, d = x.shape
    tm = grp.tm
    return pl.pallas_call(
        _norm_mod_kernel, grid=(n // tm,),
        in_specs=[pl.BlockSpec((tm, d), lambda i: (i, 0)), pl.BlockSpec((1, d), lambda i: (0, 0)),
                  grp.spec(j_scale, d), grp.spec(j_shift, d)],
        out_specs=pl.BlockSpec((tm, d), lambda i: (i, 0)),
        out_shape=jax.ShapeDtypeStruct((n, d), out_dtype),
        compiler_params=_cp(("parallel",)))(x, w.reshape(1, d), grp.mod3, grp.mod3)


SUBLANES = 8
PREV_ROWS = 16


def _split_bf16(x):
    hi = x.astype(BF16)
    return hi, (x - hi.astype(F32)).astype(BF16)


def _dot3(a, b):
    d = lambda x, y: jnp.dot(x, y, preferred_element_type=F32)
    return d(a[0], b[0]) + d(a[0], b[1]) + d(a[1], b[0])


def _expand_matrix(C):
    k = jnp.arange(C)
    n = jnp.arange(SUBLANES * LANE)
    rem = n % LANE
    hit = ((k[:, None] // SUBLANES == rem[None, :] // SUBLANES) & (k[:, None] % SUBLANES == n[None, :] // LANE)
           & (rem[None, :] < C))
    return hit.astype(BF16)


def _unit_lower_inverses(a_list, at_list, g_ref, C):
    nh = len(a_list)
    row = lax.broadcasted_iota(jnp.int32, (C, C), 0)
    col = lax.broadcasted_iota(jnp.int32, (C, C), 1)
    blockdiag = (row // SUBLANES) == (col // SUBLANES)
    packed = []
    for at in at_list:
        m = jnp.where(blockdiag, at, 0.0)
        d = m[0:SUBLANES]
        for b in range(1, C // SUBLANES):
            d = d + m[SUBLANES * b:SUBLANES * (b + 1)]
        packed.append(d)
    stack = _split_bf16(jnp.concatenate(packed, axis=0))
    g = g_ref[...]
    coef = (jnp.dot(stack[0], g, preferred_element_type=F32) + jnp.dot(stack[1], g, preferred_element_type=F32))
    sub = lax.broadcasted_iota(jnp.int32, (SUBLANES, LANE), 0)
    lane = lax.broadcasted_iota(jnp.int32, (SUBLANES, LANE), 1)
    unit = jnp.where((lane % SUBLANES == sub) & (lane < C), 1.0, 0.0)
    xd = [unit] * nh
    for i in range(1, SUBLANES):
        e_i = jnp.where((lane % SUBLANES == i) & (lane < C), 1.0, 0.0)[0:1]
        for h in range(nh):
            c_i = coef[SUBLANES * h:SUBLANES * (h + 1), LANE * i:LANE * (i + 1)]
            new_row = e_i - jnp.sum(c_i * xd[h], axis=0, keepdims=True)
            xd[h] = jnp.where(sub == i, new_row, xd[h])
    x = [jnp.where(blockdiag, jnp.concatenate([xd[h][:, :C]] * (C // SUBLANES), axis=0), 0.0) for h in range(nh)]
    s = SUBLANES
    while s < C:
        below = ((row // s) % 2 == 1) & ((col // s) == (row // s) - 1)
        xs = [_split_bf16(x[h]) for h in range(nh)]
        m1 = [_dot3(xs[h], _split_bf16(jnp.where(below, a_list[h], 0.0))) for h in range(nh)]
        x = [x[h] - _dot3(_split_bf16(m1[h]), xs[h]) for h in range(nh)]
        s *= 2
    return x


def _gdn_prep_kernel(qkv_ref, prev_ref, sm_ref, cw_ref, gp_ref, g_ref,
                     u_ref, w_ref, qe_ref, kd_ref, aqk_ref, dec_ref, xbuf_ref, *, H, DK, DV, C):
    c = pl.program_id(1)
    QK = H * DK
    P = PREV_ROWS

    prev = prev_ref[...].astype(F32)
    xbuf_ref[0:P, :] = jnp.where(c == 0, 0.0, prev)
    xbuf_ref[P:P + C, :] = qkv_ref[...].astype(F32)
    y = None
    for j in range(CONV_WIDTH):
        o = P - (CONV_WIDTH - 1) + j
        term = xbuf_ref[o:o + C, :] * cw_ref[j:j + 1, :]
        y = term if y is None else y + term
    y = _silu(y)

    sm = sm_ref[...]
    beta_all = jax.nn.sigmoid(sm)
    g_all = -jnp.exp(gp_ref[0:1, :]) * _softplus(sm + gp_ref[1:2, :])
    row = lax.broadcasted_iota(jnp.int32, (C, C), 0)
    col = lax.broadcasted_iota(jnp.int32, (C, C), 1)
    tril = (col <= row).astype(F32)
    gam_all = jnp.dot(tril, g_all, precision=HI, preferred_element_type=F32)
    lane = lax.broadcasted_iota(jnp.int32, sm.shape, 1)
    rows_t = jnp.where(lane < 8, beta_all, gam_all).T

    a_list, at_list, rhs_list = [], [], []
    for h in range(H):
        qh = y[:, h * DK:(h + 1) * DK]
        kh = y[:, QK + h * DK:QK + (h + 1) * DK]
        vh = y[:, 2 * QK + h * DV:2 * QK + (h + 1) * DV]
        qh = qh * lax.rsqrt(jnp.sum(qh * qh, axis=-1, keepdims=True) + NORM_EPS) * (DK ** -0.5)
        kh = kh * lax.rsqrt(jnp.sum(kh * kh, axis=-1, keepdims=True) + NORM_EPS)
        beta_c = beta_all[:, h:h + 1]
        gam_c = gam_all[:, 8 + h:9 + h]
        beta_r = rows_t[h:h + 1, :]
        gam_r = rows_t[8 + h:9 + h, :]
        dm = gam_c - gam_r
        decay = jnp.exp(jnp.where(col <= row, dm, -jnp.inf))
        decay_t = jnp.exp(jnp.where(row < col, -dm, -jnp.inf))
        qk_kk = _bdot_g(jnp.concatenate([qh, kh], axis=0), kh, NT)
        kk = qk_kk[C:]
        a_list.append(jnp.where(col < row, kk * beta_c * decay, 0.0))
        at_list.append(kk * beta_r * decay_t)
        egam = jnp.exp(gam_c)
        rhs_list.append(jnp.concatenate([beta_c * vh, beta_c * egam * kh], axis=1).astype(BF16))
        g_last = gam_c[C - 1:C, :]
        qe_ref[:, h * DK:(h + 1) * DK] = (qh * egam).astype(qe_ref.dtype)
        kd_ref[:, h * DK:(h + 1) * DK] = (kh * jnp.exp(g_last - gam_c)).astype(kd_ref.dtype)
        aqk_ref[:, h * C:(h + 1) * C] = (qk_kk[:C] * decay).astype(aqk_ref.dtype)
        dec_ref[h:h + 1, :] = jnp.broadcast_to(jnp.exp(g_last), (1, LANE))

    t_inv = _unit_lower_inverses(a_list, at_list, g_ref, C)
    for h in range(H):
        t_hi, t_lo = _split_bf16(t_inv[h])
        uw = (jnp.dot(t_hi, rhs_list[h], preferred_element_type=F32)
              + jnp.dot(t_lo, rhs_list[h], preferred_element_type=F32))
        u_ref[:, h * DV:(h + 1) * DV] = uw[:, :DV].astype(u_ref.dtype)
        w_ref[:, h * DK:(h + 1) * DK] = uw[:, DV:].astype(w_ref.dtype)


def _gdn_scan_kernel(u_ref, w_ref, qe_ref, kd_ref, aqk_ref, dec_ref, z_ref, nw_ref, og_ref, s_out_ref, s_ref,
                     *, H, DK, DV, C):
    c = pl.program_id(1)

    @pl.when(c == 0)
    def _():
        s_ref[...] = jnp.zeros_like(s_ref)

    ks = [slice(h * DK, (h + 1) * DK) for h in range(H)]
    vs = [slice(h * DV, (h + 1) * DV) for h in range(H)]
    s16 = [s_ref[h].astype(BF16) for h in range(H)]
    ws = [jnp.dot(w_ref[:, ks[h]], s16[h], preferred_element_type=F32) for h in range(H)]
    qs = [jnp.dot(qe_ref[:, ks[h]], s16[h], preferred_element_type=F32) for h in range(H)]
    v16 = [(u_ref[:, vs[h]].astype(F32) - ws[h]).astype(BF16) for h in range(H)]
    ds = [lax.dot_general(kd_ref[:, ks[h]], v16[h], TN, preferred_element_type=F32) for h in range(H)]
    o = [qs[h] + jnp.dot(aqk_ref[:, h * C:(h + 1) * C], v16[h], preferred_element_type=F32) for h in range(H)]
    for h in range(H):
        s_ref[h] = s_ref[h] * dec_ref[h:h + 1, :] + ds[h]
        zh = z_ref[:, vs[h]].astype(F32)
        og_ref[:, vs[h]] = (_rms(o[h]) * nw_ref[...] * _silu(zh)).astype(og_ref.dtype)

    @pl.when(c == pl.num_programs(1) - 1)
    def _():
        s_out_ref[...] = s_ref[...]


def _gdn_prompt(qkv, z, small, ba_col, cw_t, gparams, norm_w, B, T, H, DK, DV):
    C = GDN_CHUNK
    nc = T // C
    n, W = qkv.shape
    kw = dict(H=H, DK=DK, DV=DV, C=C)
    blk = lambda w: pl.BlockSpec((C, w), lambda b, c: (b * nc + c, 0))
    ppc = C // PREV_ROWS
    u, w, qe, kd, aqk, dec = pl.pallas_call(
        functools.partial(_gdn_prep_kernel, **kw), grid=(B, nc),
        in_specs=[blk(W),
                  pl.BlockSpec((PREV_ROWS, W), lambda b, c: (jnp.maximum((b * nc + c) * ppc - 1, 0), 0)),
                  pl.BlockSpec((C, LANE), lambda b, c: (b * nc + c, ba_col)),
                  pl.BlockSpec((CONV_WIDTH, W), lambda b, c: (0, 0)),
                  pl.BlockSpec((2, LANE), lambda b, c: (0, 0)),
                  pl.BlockSpec((C, SUBLANES * LANE), lambda b, c: (0, 0))],
        out_specs=[blk(H * DV), blk(H * DK), blk(H * DK), blk(H * DK), blk(H * C),
                   pl.BlockSpec((None, SUBLANES, LANE), lambda b, c: (b * nc + c, 0, 0))],
        out_shape=[jax.ShapeDtypeStruct((n, H * DV), BF16), jax.ShapeDtypeStruct((n, H * DK), BF16),
                   jax.ShapeDtypeStruct((n, H * DK), BF16), jax.ShapeDtypeStruct((n, H * DK), BF16),
                   jax.ShapeDtypeStruct((n, H * C), BF16), jax.ShapeDtypeStruct((B * nc, SUBLANES, LANE), F32)],
        scratch_shapes=[pltpu.VMEM((C + PREV_ROWS, W), F32)],
        compiler_params=_cp(("parallel", "parallel")))(qkv, qkv, small, cw_t, gparams, _expand_matrix(C))
    return pl.pallas_call(
        functools.partial(_gdn_scan_kernel, **kw), grid=(B, nc),
        in_specs=[blk(H * DV), blk(H * DK), blk(H * DK), blk(H * DK), blk(H * C),
                  pl.BlockSpec((None, SUBLANES, LANE), lambda b, c: (b * nc + c, 0, 0)),
                  blk(H * DV), pl.BlockSpec((1, DV), lambda b, c: (0, 0))],
        out_specs=[blk(H * DV), pl.BlockSpec((None, H, DK, DV), lambda b, c: (b, 0, 0, 0))],
        out_shape=[jax.ShapeDtypeStruct((n, H * DV), BF16), jax.ShapeDtypeStruct((B, H, DK, DV), F32)],
        scratch_shapes=[pltpu.VMEM((H, DK, DV), F32)],
        compiler_params=_cp(("parallel", "arbitrary")))(u, w, qe, kd, aqk, dec, z, norm_w.reshape(1, DV))


def _gdn_decode_kernel(u_ref, z_ref, ba_ref, buf_ref, s_in_ref, cw_ref, gp_ref, nw_ref,
                       og_ref, s_out_ref, buf_out_ref, *, H, DK, DV):
    QK = H * DK
    u = u_ref[...]
    buf = buf_ref[...]
    y = buf[0:1] * cw_ref[0:1, :]
    y = y + buf[1:2] * cw_ref[1:2, :]
    y = y + buf[2:3] * cw_ref[2:3, :]
    y = y + u * cw_ref[3:4, :]
    buf_out_ref[0:2, :] = buf[1:3]
    buf_out_ref[2:3, :] = u
    y = _silu(y)
    sm = ba_ref[...]
    beta_all = jax.nn.sigmoid(sm)
    g_all = -jnp.exp(gp_ref[0:1, :]) * _softplus(sm + gp_ref[1:2, :])
    rows = []
    for h in range(H):
        kh = y[:, QK + h * DK:QK + (h + 1) * DK]
        rows.append(kh * lax.rsqrt(jnp.sum(kh * kh, axis=-1, keepdims=True) + NORM_EPS))
    for h in range(H):
        qh = y[:, h * DK:(h + 1) * DK]
        rows.append(qh * lax.rsqrt(jnp.sum(qh * qh, axis=-1, keepdims=True) + NORM_EPS) * (DK ** -0.5))
    rows.append(jnp.zeros((DK - 2 * H, DK), F32))
    cols = jnp.concatenate(rows, axis=0).T
    for h in range(H):
        kcol = cols[:, h:h + 1]
        qcol = cols[:, H + h:H + h + 1]
        vh = y[:, 2 * QK + h * DV:2 * QK + (h + 1) * DV]
        s_dec = s_in_ref[h] * jnp.exp(g_all[:, 8 + h:9 + h])
        v_new = beta_all[:, h:h + 1] * (vh - jnp.sum(kcol * s_dec, axis=0, keepdims=True))
        s_new = s_dec + kcol * v_new
        s_out_ref[h] = s_new
        o = jnp.sum(qcol * s_new, axis=0, keepdims=True)
        zh = z_ref[:, h * DV:(h + 1) * DV].astype(F32)
        og_ref[:, h * DV:(h + 1) * DV] = (_rms(o) * nw_ref[...] * _silu(zh)).astype(og_ref.dtype)


def _gdn_decode(qkv, z, small, ba_col, conv_buf, ssm, cw_t, gparams, norm_w, H, DK, DV):
    Bs, W = qkv.shape
    kern = functools.partial(_gdn_decode_kernel, H=H, DK=DK, DV=DV)
    og, s_new, buf_new = pl.pallas_call(
        kern, grid=(Bs,),
        in_specs=[pl.BlockSpec((None, 1, W), lambda b: (b, 0, 0)),
                  pl.BlockSpec((None, 1, H * DV), lambda b: (b, 0, 0)),
                  pl.BlockSpec((None, 1, LANE), lambda b: (b, 0, ba_col)),
                  pl.BlockSpec((None, CONV_WIDTH - 1, W), lambda b: (b, 0, 0)),
                  pl.BlockSpec((None, H, DK, DV), lambda b: (b, 0, 0, 0)),
                  pl.BlockSpec((CONV_WIDTH, W), lambda b: (0, 0)),
                  pl.BlockSpec((2, LANE), lambda b: (0, 0)),
                  pl.BlockSpec((1, DV), lambda b: (0, 0))],
        out_specs=[pl.BlockSpec((None, 1, H * DV), lambda b: (b, 0, 0)),
                   pl.BlockSpec((None, H, DK, DV), lambda b: (b, 0, 0, 0)),
                   pl.BlockSpec((None, CONV_WIDTH - 1, W), lambda b: (b, 0, 0))],
        out_shape=[jax.ShapeDtypeStruct((Bs, 1, H * DV), BF16),
                   jax.ShapeDtypeStruct(ssm.shape, F32),
                   jax.ShapeDtypeStruct(conv_buf.shape, F32)],
        compiler_params=_cp(("parallel",)))(
            qkv.reshape(Bs, 1, W), z.reshape(Bs, 1, H * DV), small.reshape(Bs, 1, small.shape[1]),
            conv_buf, ssm, cw_t, gparams, norm_w.reshape(1, DV))
    return og.reshape(Bs, H * DV), s_new, buf_new


def _mla_prep_prompt_kernel(sm_ref, cos_ref, sin_ref, cost_ref, sint_ref, qnw_ref, kvnw_ref,
                            wqt_ref, wqrt_ref, wuk_ref, wuvt_ref,
                            qt_ref, k_ref, vt_ref, lat_ref, pe_ref, *, H, NOPE, ROPE, VH, QL, KVL, scale):
    sm = sm_ref[...]
    qn = (_rms(sm[:, :QL]) * qnw_ref[...]).astype(BF16)
    qft = lax.dot_general(wqt_ref[...], qn, NT, preferred_element_type=F32)
    qrt = lax.dot_general(wqrt_ref[...], qn, NT, preferred_element_type=F32)
    qpt = qft[H * NOPE:, :] * cost_ref[...] + qrt * sint_ref[...]
    lat = _rms(sm[:, QL:QL + KVL]) * kvnw_ref[...]
    lat_ref[...] = lat
    lat16 = lat.astype(BF16)
    kn = jnp.dot(lat16, wuk_ref[...], preferred_element_type=F32)
    vt = lax.dot_general(wuvt_ref[...], lat16, NT, preferred_element_type=F32)
    o = QL + KVL
    kr = sm[:, o:o + ROPE] * cos_ref[...] + sm[:, o + ROPE:o + 2 * ROPE] * sin_ref[...]
    pe_ref[...] = kr
    for h in range(H):
        qt_ref[h] = (jnp.concatenate([qft[h * NOPE:(h + 1) * NOPE, :], qpt[h * ROPE:(h + 1) * ROPE, :]], axis=0)
                     * scale).astype(qt_ref.dtype)
        k_ref[h] = jnp.concatenate([kn[:, h * NOPE:(h + 1) * NOPE], kr], axis=-1).astype(k_ref.dtype)
        vt_ref[h, 0] = vt[h * VH:(h + 1) * VH, :].astype(vt_ref.dtype)


def _mla_prep_prompt(small, cos, sin, qnw, kvnw, wq, wqr, wuk, wuv, T, tm, H, NOPE, ROPE, VH, QL, KVL, scale):
    n, ws = small.shape
    nt = T // tm
    kern = functools.partial(_mla_prep_prompt_kernel, H=H, NOPE=NOPE, ROPE=ROPE, VH=VH, QL=QL, KVL=KVL, scale=scale)
    full = lambda a: pl.BlockSpec(a.shape, lambda i: (0,) * a.ndim)
    dqk = NOPE + ROPE
    cos_t = jnp.tile(cos.T, (H, 1))
    sin_t = jnp.tile(sin.T, (H, 1))
    wqt, wqrt, wuvt = wq.T, wqr.T, wuv.T
    return pl.pallas_call(
        kern, grid=(n // tm,),
        in_specs=[pl.BlockSpec((tm, ws), lambda i: (i, 0)),
                  pl.BlockSpec((tm, ROPE), lambda i: (i % nt, 0)),
                  pl.BlockSpec((tm, ROPE), lambda i: (i % nt, 0)),
                  pl.BlockSpec((H * ROPE, tm), lambda i: (0, i % nt)),
                  pl.BlockSpec((H * ROPE, tm), lambda i: (0, i % nt)),
                  full(qnw), full(kvnw), full(wqt), full(wqrt), full(wuk), full(wuvt)],
        out_specs=[pl.BlockSpec((H, dqk, tm), lambda i: (0, 0, i)),
                   pl.BlockSpec((H, tm, dqk), lambda i: (0, i, 0)),
                   pl.BlockSpec((H, 1, VH, tm), lambda i: (0, i, 0, 0)),
                   pl.BlockSpec((tm, KVL), lambda i: (i, 0)),
                   pl.BlockSpec((tm, ROPE), lambda i: (i, 0))],
        out_shape=[jax.ShapeDtypeStruct((H, dqk, n), BF16), jax.ShapeDtypeStruct((H, n, dqk), BF16),
                   jax.ShapeDtypeStruct((H, n // tm, VH, tm), BF16), jax.ShapeDtypeStruct((n, KVL), F32),
                   jax.ShapeDtypeStruct((n, ROPE), F32)],
        compiler_params=_cp(("parallel",)))(small, cos, sin, cos_t, sin_t, qnw, kvnw, wqt, wqrt, wuk, wuvt)


def _mla_prep_sample_kernel(sm_ref, cos_ref, sin_ref, qnw_ref, kvnw_ref, wq_ref, wqr_ref,
                            qn_ref, qp_ref, lat_ref, pe_ref, *, H, NOPE, ROPE, QL, KVL, scale):
    sm = sm_ref[...]
    cos = cos_ref[...]
    sin = sin_ref[...]
    qn = _rms(sm[:, :QL]) * qnw_ref[...]
    qf = _bdot(qn, wq_ref[...])
    qr = _bdot(qn, wqr_ref[...])
    qn_ref[...] = qf[:, :H * NOPE].astype(qn_ref.dtype)
    lat_ref[...] = _rms(sm[:, QL:QL + KVL]) * kvnw_ref[...]
    o = QL + KVL
    pe_ref[...] = sm[:, o:o + ROPE] * cos + sm[:, o + ROPE:o + 2 * ROPE] * sin
    for h in range(H):
        p0 = H * NOPE + h * ROPE
        qp = qf[:, p0:p0 + ROPE] * cos + qr[:, h * ROPE:(h + 1) * ROPE] * sin
        qp_ref[:, h * ROPE:(h + 1) * ROPE] = (qp * scale).astype(qp_ref.dtype)


def _mla_prep_sample(small, cos, sin, qnw, kvnw, wq, wqr, H, NOPE, ROPE, QL, KVL, scale):
    n, ws = small.shape
    kern = functools.partial(_mla_prep_sample_kernel, H=H, NOPE=NOPE, ROPE=ROPE, QL=QL, KVL=KVL, scale=scale)
    full = lambda a: pl.BlockSpec(a.shape, lambda i: (0,) * a.ndim)
    return pl.pallas_call(
        kern, grid=(1,),
        in_specs=[full(small), full(cos), full(sin), full(qnw), full(kvnw), full(wq), full(wqr)],
        out_specs=[pl.BlockSpec((n, H * NOPE), lambda i: (0, 0)), pl.BlockSpec((n, H * ROPE), lambda i: (0, 0)),
                   pl.BlockSpec((n, KVL), lambda i: (0, 0)), pl.BlockSpec((n, ROPE), lambda i: (0, 0))],
        out_shape=[jax.ShapeDtypeStruct((n, H * NOPE), BF16), jax.ShapeDtypeStruct((n, H * ROPE), BF16),
                   jax.ShapeDtypeStruct((n, KVL), F32), jax.ShapeDtypeStruct((n, ROPE), F32)],
        compiler_params=_cp(("arbitrary",)))(small, cos, sin, qnw, kvnw, wq, wqr)


def _head_proj_kernel(x_ref, w_ref, o_ref, *, dims, scale):
    o_ref[...] = (_bdot_g(x_ref[...], w_ref[...], dims) * scale).astype(o_ref.dtype)


def _q_latent(q_nope, w_uk, scale):
    H, KVL, NOPE = w_uk.shape
    n = q_nope.shape[0]
    return pl.pallas_call(
        functools.partial(_head_proj_kernel, dims=NT, scale=scale), grid=(H,),
        in_specs=[pl.BlockSpec((n, NOPE), lambda h: (0, h)), pl.BlockSpec((None, KVL, NOPE), lambda h: (h, 0, 0))],
        out_specs=pl.BlockSpec((None, n, KVL), lambda h: (h, 0, 0)),
        out_shape=jax.ShapeDtypeStruct((H, n, KVL), BF16),
        compiler_params=_cp(("parallel",)))(q_nope, w_uk)


def _o_value(o_lat, w_uv):
    H, KVL, VH = w_uv.shape
    n = o_lat.shape[1]
    return pl.pallas_call(
        functools.partial(_head_proj_kernel, dims=(((1,), (0,)), ((), ())), scale=1.0), grid=(H,),
        in_specs=[pl.BlockSpec((None, n, KVL), lambda h: (h, 0, 0)), pl.BlockSpec((None, KVL, VH), lambda h: (h, 0, 0))],
        out_specs=pl.BlockSpec((n, VH), lambda h: (0, h)),
        out_shape=jax.ShapeDtypeStruct((n, H * VH), BF16),
        compiler_params=_cp(("parallel",)))(o_lat, w_uv)


FLASH_SPLIT = 2


def _flash_kernel(qt_ref, k_ref, vt_ref, o_ref, m_ref, l_ref, acc_ref, *, tq):
    qi = pl.program_id(2)
    ns = FLASH_SPLIT
    hq = tq // ns
    m_ref[...] = jnp.full_like(m_ref, -jnp.inf)
    l_ref[...] = jnp.zeros_like(l_ref)
    acc_ref[...] = jnp.zeros_like(acc_ref)
    qt = [qt_ref[:, t * hq:(t + 1) * hq] for t in range(ns)]

    def update(st, vt):
        m_prev = [m_ref[t] for t in range(ns)]
        m_new = [jnp.maximum(m_prev[t], jnp.max(st[t], axis=0, keepdims=True)) for t in range(ns)]
        p = [jnp.exp(st[t] - m_new[t]) for t in range(ns)]
        corr = [jnp.exp(m_prev[t] - m_new[t]) for t in range(ns)]
        pv = [jnp.dot(vt, p[t].astype(BF16), preferred_element_type=F32) for t in range(ns)]
        for t in range(ns):
            m_ref[t] = m_new[t]
            l_ref[t] = corr[t] * l_ref[t] + jnp.sum(p[t], axis=0, keepdims=True)
            acc_ref[t] = acc_ref[t] * corr[t] + pv[t]

    def below_diagonal(j, carry):
        k = k_ref[pl.ds(pl.multiple_of(j * tq, tq), tq), :]
        update([jnp.dot(k, qt[t], preferred_element_type=F32) for t in range(ns)], vt_ref[j])
        return carry

    lax.fori_loop(0, qi, below_diagonal, 0)

    k = k_ref[pl.ds(pl.multiple_of(qi * tq, tq), tq), :]
    st = []
    for t in range(ns):
        s = jnp.dot(k, qt[t], preferred_element_type=F32)
        key = lax.broadcasted_iota(jnp.int32, s.shape, 0)
        qry = t * hq + lax.broadcasted_iota(jnp.int32, s.shape, 1)
        st.append(jnp.where(key <= qry, s, -jnp.inf))
    update(st, vt_ref[qi])
    for t in range(ns):
        o_ref[t * hq:(t + 1) * hq, :] = (acc_ref[t] / l_ref[t]).T.astype(o_ref.dtype)


def _flash_attention(qt, k, vt, B, T, tq):
    H, dqk, n = qt.shape
    VH = vt.shape[2]
    nq = T // tq
    hq = tq // FLASH_SPLIT
    kern = functools.partial(_flash_kernel, tq=tq)
    return pl.pallas_call(
        kern, grid=(B, H, nq),
        in_specs=[pl.BlockSpec((None, dqk, tq), lambda b, h, i: (h, 0, b * nq + i)),
                  pl.BlockSpec((None, T, dqk), lambda b, h, i: (h, b, 0)),
                  pl.BlockSpec((None, nq, VH, tq), lambda b, h, i: (h, b, 0, 0))],
        out_specs=pl.BlockSpec((tq, VH), lambda b, h, i: (b * nq + i, h)),
        out_shape=jax.ShapeDtypeStruct((n, H * VH), BF16),
        scratch_shapes=[pltpu.VMEM((FLASH_SPLIT, 1, hq), F32), pltpu.VMEM((FLASH_SPLIT, 1, hq), F32),
                        pltpu.VMEM((FLASH_SPLIT, VH, hq), F32)],
        compiler_params=_cp(("parallel", "parallel", "arbitrary")))(qt, k, vt)


def _paged_kernel(pt_ref, ql_ref, qp_ref, latn_ref, pen_ref, kv_hbm, pe_hbm, o_ref,
                  kv_buf, pe_buf, sem, m_ref, l_ref, acc_ref, *, R, G, layer):
    b = pl.program_id(0)
    g = pl.program_id(1)
    ng = pl.num_programs(1)
    step = b * ng + g
    slot = step % 2

    def page_copies(bb, gg, sl, real_pages):
        out = []
        for r in range(R):
            for i in range(G):
                page = pt_ref[bb * R + r, gg * G + i] if real_pages else 0
                out.append(pltpu.make_async_copy(kv_hbm.at[layer, page], kv_buf.at[sl, r * G + i], sem.at[0, sl]))
                out.append(pltpu.make_async_copy(pe_hbm.at[layer, page], pe_buf.at[sl, r * G + i], sem.at[1, sl]))
        return out

    @pl.when(step == 0)
    def _():
        for cp in page_copies(0, 0, 0, True):
            cp.start()

    @pl.when(step + 1 < pl.num_programs(0) * ng)
    def _():
        nxt = step + 1
        for cp in page_copies(nxt // ng, nxt % ng, 1 - slot, True):
            cp.start()

    for cp in page_copies(b, g, slot, False):
        cp.wait()

    @pl.when(g == 0)
    def _():
        m_ref[...] = jnp.full_like(m_ref, -jnp.inf)
        l_ref[...] = jnp.zeros_like(l_ref)
        acc_ref[...] = jnp.zeros_like(acc_ref)

    kvs = [[kv_buf[slot, r * G + i].astype(BF16) for i in range(G)] for r in range(R)]
    P = kvs[0][0].shape[0]
    s, p, corr, m_new = [], [], [], []
    for r in range(R):
        ql = ql_ref[r]
        qp = qp_ref[r]
        s.append(jnp.concatenate(
            [lax.dot_general(ql, kvs[r][i], NT, preferred_element_type=F32)
             + jnp.dot(qp, pe_buf[slot, r * G + i].astype(BF16), preferred_element_type=F32) for i in range(G)],
            axis=-1))
    for r in range(R):
        m_prev = m_ref[r]
        m_new.append(jnp.maximum(m_prev, jnp.max(s[r], axis=-1, keepdims=True)))
        corr.append(jnp.exp(m_prev - m_new[r]))
        p.append(jnp.exp(s[r] - m_new[r]))
    pv = []
    for r in range(R):
        t = None
        for i in range(G):
            d = jnp.dot(p[r][:, i * P:(i + 1) * P].astype(BF16), kvs[r][i], preferred_element_type=F32)
            t = d if t is None else t + d
        pv.append(t)
    for r in range(R):
        m_ref[r] = m_new[r]
        l_ref[r] = corr[r] * l_ref[r] + jnp.sum(p[r], axis=-1, keepdims=True)
        acc_ref[r] = acc_ref[r] * corr[r] + pv[r]

    @pl.when(g == pl.num_programs(1) - 1)
    def _():
        for r in range(R):
            latn = latn_ref[r]
            s_n = (jnp.sum(ql_ref[r].astype(F32) * latn, axis=-1, keepdims=True)
                   + jnp.sum(qp_ref[r].astype(F32) * pen_ref[r], axis=-1, keepdims=True))
            m_old = m_ref[r]
            m2 = jnp.maximum(m_old, s_n)
            c2 = jnp.exp(m_old - m2)
            p2 = jnp.exp(s_n - m2)
            o_ref[r] = ((acc_ref[r] * c2 + p2 * latn) / (l_ref[r] * c2 + p2)).astype(o_ref.dtype)


def _paged_attention(q_lat, q_pe, lat_new, pe_new, cache_kv, cache_pe_t, page_table, layer):
    Bs, H, KVL = q_lat.shape
    ROPE = q_pe.shape[2]
    n_pages = page_table.shape[1]
    P = cache_kv.shape[2]
    G = min(PAGE_GROUP, n_pages)
    R = min(PAGE_REQUESTS, Bs)

    req = lambda rows, w: pl.BlockSpec((R, rows, w), lambda b, g, pt: (b, 0, 0))
    hbm = pl.BlockSpec(memory_space=pl.ANY)
    grid_spec = pltpu.PrefetchScalarGridSpec(
        num_scalar_prefetch=1, grid=(Bs // R, n_pages // G),
        in_specs=[req(H, KVL), req(H, ROPE), req(1, KVL), req(1, ROPE), hbm, hbm],
        out_specs=req(H, KVL),
        scratch_shapes=[pltpu.VMEM((2, R * G, P, KVL), cache_kv.dtype), pltpu.VMEM((2, R * G, ROPE, P), cache_pe_t.dtype),
                        pltpu.SemaphoreType.DMA((2, 2)),
                        pltpu.VMEM((R, H, 1), F32), pltpu.VMEM((R, H, 1), F32), pltpu.VMEM((R, H, KVL), F32)])
    return pl.pallas_call(
        functools.partial(_paged_kernel, R=R, G=G, layer=layer), grid_spec=grid_spec,
        out_shape=jax.ShapeDtypeStruct((Bs, H, KVL), BF16),
        compiler_params=_cp(("arbitrary", "arbitrary")))(
            page_table, q_lat, q_pe, lat_new.reshape(Bs, 1, KVL), pe_new.reshape(Bs, 1, ROPE), cache_kv, cache_pe_t)


def _post_mixer_kernel(x_ref, og_ref, om_ref, gab_ref, g1_ref, sc2_ref, sh2_ref, wog_ref, wom_ref, wout_ref,
                       nw_ref, wr_ref, br_ref, x1_ref, h2_ref, te_ref, tg_ref, *, D, E):
    gab = gab_ref[...].astype(F32)
    merged = (jax.nn.sigmoid(gab[:, :D]) * _bdot(og_ref[...], wog_ref[...])
              + jax.nn.sigmoid(gab[:, D:]) * _bdot(om_ref[...], wom_ref[...]))
    x1 = x_ref[...] + g1_ref[...] * _bdot(merged, wout_ref[...])
    x1_ref[...] = x1
    h2 = _rms(x1) * nw_ref[...] * (1.0 + sc2_ref[...]) + sh2_ref[...]
    h2_ref[...] = h2
    logits = jnp.dot(h2, wr_ref[...], precision=HI, preferred_element_type=F32) + br_ref[...]
    lane = lax.broadcasted_iota(jnp.int32, logits.shape, 1)
    logits = jnp.where(lane < E, logits, -jnp.inf)
    te = jnp.zeros(logits.shape, jnp.int32)
    ex = jnp.zeros(logits.shape, F32)
    top = None
    for k in range(TOP_K):
        m = jnp.max(logits, axis=-1, keepdims=True)
        idx = jnp.min(jnp.where(logits == m, lane, LANE), axis=-1, keepdims=True)
        top = m if top is None else top
        te = jnp.where(lane == k, idx, te)
        ex = jnp.where(lane == k, jnp.exp(m - top), ex)
        logits = jnp.where(lane == idx, -jnp.inf, logits)
    te_ref[...] = te
    tg_ref[...] = ex / jnp.sum(ex, axis=-1, keepdims=True)


def _post_mixer(x, og, om, gab, grp, wog, wom, wout, norm_w, wr, br, E):
    n, d = x.shape
    tm = grp.tm
    rowblk = lambda w: pl.BlockSpec((tm, w), lambda i: (i, 0))
    full = lambda a: pl.BlockSpec(a.shape, lambda i: (0,) * a.ndim)
    return pl.pallas_call(
        functools.partial(_post_mixer_kernel, D=d, E=E), grid=(n // tm,),
        in_specs=[rowblk(d), rowblk(d), rowblk(d), rowblk(2 * d), grp.spec(2, d), grp.spec(4, d), grp.spec(3, d),
                  full(wog), full(wom), full(wout), full(norm_w), full(wr), full(br)],
        out_specs=[rowblk(d), rowblk(d), rowblk(LANE), rowblk(LANE)],
        out_shape=[jax.ShapeDtypeStruct((n, d), F32), jax.ShapeDtypeStruct((n, d), F32),
                   jax.ShapeDtypeStruct((n, LANE), jnp.int32), jax.ShapeDtypeStruct((n, LANE), F32)],
        compiler_params=_cp(("parallel",)))(x, og, om, gab, grp.mod3, grp.mod3, grp.mod3,
                                            wog, wom, wout, norm_w, wr, br)


def _dispatch_kernel(slot_ref, h_ref, init_ref, hs_ref, sem, *, tb):
    del init_ref

    def row_copy(r, s):
        return pltpu.make_async_copy(h_ref.at[pl.ds(r, 1)], hs_ref.at[pl.ds(s, 1)], sem)

    def start(r, carry):
        for k in range(TOP_K):
            row_copy(r, slot_ref[0, 0, r * TOP_K + k]).start()
        return carry

    def wait(r, carry):
        for k in range(TOP_K):
            row_copy(0, 0).wait()
        return carry

    lax.fori_loop(0, tb, start, 0)
    lax.fori_loop(0, tb, wait, 0)


def _dispatch(h, slots, n_slots, tb):
    n, d = h.shape
    nb = n // tb
    return pl.pallas_call(
        functools.partial(_dispatch_kernel, tb=tb), grid=(nb,),
        in_specs=[pl.BlockSpec((1, 1, tb * TOP_K), lambda i: (i, 0, 0), memory_space=pltpu.SMEM),
                  pl.BlockSpec((tb, d), lambda i: (i, 0)),
                  pl.BlockSpec(memory_space=pl.ANY)],
        out_specs=pl.BlockSpec(memory_space=pl.ANY),
        out_shape=jax.ShapeDtypeStruct((n_slots, d), h.dtype),
        scratch_shapes=[pltpu.SemaphoreType.DMA],
        input_output_aliases={2: 0},
        compiler_params=_cp(("arbitrary",)))(slots.reshape(nb, 1, tb * TOP_K), h, jnp.zeros((n_slots, d), h.dtype))


def _expert_kernel(be_ref, na_ref, x_ref, wgu_ref, bgu_ref, wdn_ref, bdn_ref, y_ref, wgu_s, wdn_s, *, DE):
    i = pl.program_id(0)
    active = i < na_ref[0]
    first = jnp.logical_or(i == 0, be_ref[i] != be_ref[jnp.maximum(i - 1, 0)])

    @pl.when(jnp.logical_and(active, first))
    def _():
        wgu_s[...] = wgu_ref[...].astype(BF16)
        wdn_s[...] = wdn_ref[...].astype(BF16)

    @pl.when(active)
    def _():
        gu = jnp.dot(x_ref[...].astype(BF16), wgu_s[...], preferred_element_type=F32) + bgu_ref[...]
        gt = jnp.minimum(gu[:, :DE], SWIGLU_LIMIT)
        up = jnp.clip(gu[:, DE:], -SWIGLU_LIMIT, SWIGLU_LIMIT)
        act = (up + 1.0) * gt * jax.nn.sigmoid(SWIGLU_ALPHA * gt)
        y_ref[...] = jnp.dot(act.astype(BF16), wdn_s[...], preferred_element_type=F32) + bdn_ref[...]

    @pl.when(jnp.logical_not(active))
    def _():
        y_ref[...] = jnp.zeros_like(y_ref)


def _experts(hs, blk_e, n_active, w_gu, b_gu, w_dn, b_dn, layer, tm):
    n_slots, d = hs.shape
    E, _, de2 = w_gu.shape[1:]
    de = de2 // 2
    grid_spec = pltpu.PrefetchScalarGridSpec(
        num_scalar_prefetch=2, grid=(n_slots // tm,),
        in_specs=[pl.BlockSpec((tm, d), lambda i, be, na: (i, 0)),
                  pl.BlockSpec((None, None, d, de2), lambda i, be, na: (layer, be[i], 0, 0)),
                  pl.BlockSpec((None, None, 1, de2), lambda i, be, na: (layer, be[i], 0, 0)),
                  pl.BlockSpec((None, None, de, d), lambda i, be, na: (layer, be[i], 0, 0)),
                  pl.BlockSpec((None, None, 1, d), lambda i, be, na: (layer, be[i], 0, 0))],
        out_specs=pl.BlockSpec((tm, d), lambda i, be, na: (i, 0)),
        scratch_shapes=[pltpu.VMEM((d, de2), BF16), pltpu.VMEM((de, d), BF16)])
    L = w_gu.shape[0]
    return pl.pallas_call(
        functools.partial(_expert_kernel, DE=de), grid_spec=grid_spec,
        out_shape=jax.ShapeDtypeStruct((n_slots, d), F32),
        compiler_params=_cp(("arbitrary",), VMEM_LIMIT_MOE))(
            blk_e, n_active, hs, w_gu, b_gu.reshape(L, E, 1, de2), w_dn, b_dn.reshape(L, E, 1, d))


def _combine_kernel(slot_ref, tg_ref, x1_ref, g2_ref, scf_ref, shf_ref, nwf_ref, yp_ref, o_ref, buf, sem, *, tb):
    def row_copy(r, k, s):
        return pltpu.make_async_copy(yp_ref.at[pl.ds(s, 1)], buf.at[k, pl.ds(r, 1)], sem)

    def start(r, carry):
        for k in range(TOP_K):
            row_copy(r, k, slot_ref[0, 0, r * TOP_K + k]).start()
        return carry

    def wait(r, carry):
        for k in range(TOP_K):
            row_copy(0, k, 0).wait()
        return carry

    lax.fori_loop(0, tb, start, 0)
    lax.fori_loop(0, tb, wait, 0)
    tg = tg_ref[...]
    moe = tg[:, 0:1] * buf[0]
    for k in range(1, TOP_K):
        moe = moe + tg[:, k:k + 1] * buf[k]
    x2 = x1_ref[...] + g2_ref[...] * moe
    o_ref[...] = _rms(x2) * nwf_ref[...] * (1.0 + scf_ref[...]) + shf_ref[...]


def _combine_final(y_pad, slots, tg, x1, grp, norm_final_w):
    n, d = x1.shape
    tb = grp.tm
    nb = n // tb
    rf, bpg = grp.modf3.shape[1], grp.bpg
    fspec = lambda j: pl.BlockSpec((None, rf, d), lambda i: (i // bpg, 0, j))
    return pl.pallas_call(
        functools.partial(_combine_kernel, tb=tb), grid=(nb,),
        in_specs=[pl.BlockSpec((1, 1, tb * TOP_K), lambda i: (i, 0, 0), memory_space=pltpu.SMEM),
                  pl.BlockSpec((tb, LANE), lambda i: (i, 0)),
                  pl.BlockSpec((tb, d), lambda i: (i, 0)),
                  grp.spec(5, d), fspec(1), fspec(0),
                  pl.BlockSpec((1, d), lambda i: (0, 0)),
                  pl.BlockSpec(memory_space=pl.ANY)],
        out_specs=pl.BlockSpec((tb, d), lambda i: (i, 0)),
        out_shape=jax.ShapeDtypeStruct((n, d), F32),
        scratch_shapes=[pltpu.VMEM((TOP_K, tb, d), F32), pltpu.SemaphoreType.DMA],
        compiler_params=_cp(("arbitrary",)))(
            slots.reshape(nb, 1, tb * TOP_K), tg, x1, grp.mod3, grp.modf3, grp.modf3,
            norm_final_w.reshape(1, d), y_pad)


def _route(top_e, E, tm):
    n = top_e.shape[0]
    tok_oh = jnp.sum((top_e[:, :, None] == jnp.arange(E, dtype=jnp.int32)).astype(jnp.int32), axis=1)
    csum = jnp.cumsum(tok_oh, axis=0)
    counts = csum[-1]
    rank = jnp.take_along_axis(csum - tok_oh, top_e, axis=1)
    padded = (counts + tm - 1) // tm * tm
    pad_end = jnp.cumsum(padded)
    slots = (pad_end - padded)[top_e] + rank
    n_blocks = -(-(n * TOP_K) // tm) + E
    blk_start = jnp.arange(n_blocks, dtype=jnp.int32) * tm
    blk_e = jnp.minimum(jnp.sum((pad_end[None, :] <= blk_start[:, None]).astype(jnp.int32), axis=1), E - 1)
    n_active = (pad_end[-1] // tm).reshape(1)
    return slots.astype(jnp.int32), blk_e.astype(jnp.int32), n_active.astype(jnp.int32), n_blocks * tm


def _rope_tables(pos, rope):
    half = rope // 2
    inv = ROPE_THETA ** (-jnp.arange(half, dtype=F32) / half)
    ang = pos.astype(F32)[:, None] * inv[None, :]
    cos, sin = jnp.cos(ang), jnp.sin(ang)
    return jnp.concatenate([cos, cos], axis=-1), jnp.concatenate([sin, sin], axis=-1)


def _rotate_cols(w, rope):
    k, n = w.shape
    w3 = w.reshape(k, n // rope, rope)
    half = rope // 2
    return jnp.concatenate([-w3[..., half:], w3[..., :half]], axis=-1).reshape(k, n)


def kernel(x_prompt, x_sample, c_prompt, c_sample, cache_kv, cache_pe, state_ssm, state_conv, page_table, norm_mix_w, norm_ffn_w, w_ada, b_ada, w_in, conv_w, A_log, dt_bias, gdn_norm_w, w_o_gdn, q_norm_w, kv_norm_w, w_uq, w_uk, w_uv, w_o_mla, w_out, w_router, b_router, w_gu, b_gu, w_dn, b_dn, w_ada_final, b_ada_final, norm_final_w):
    B, T, D = x_prompt.shape
    Bs, Ts, _ = x_sample.shape
    assert Ts == 1
    depth = w_in.shape[0]
    H, DK, DV = state_ssm.shape[2:]
    QK = H * DK
    CONV = state_conv.shape[3]
    assert CONV == 2 * QK + H * DV and H == SUBLANES
    QL = q_norm_w.shape[1]
    HM, KVL, NOPE = w_uk.shape[1:]
    VH = w_uv.shape[3]
    ROPE = cache_pe.shape[3]
    E = w_router.shape[2]
    scale = float(NOPE + ROPE) ** -0.5
    n_p, n_s = B * T, Bs * Ts
    past_len = page_table.shape[1] * cache_kv.shape[2]
    cache_pe_t = jnp.swapaxes(cache_pe, 2, 3)

    c_all = jnp.concatenate([c_prompt, c_sample], axis=0)
    modf = _matmul(c_all, w_ada_final, F32, c_all.shape[0], 1024, b_ada_final)
    tm_p = min(512, T)
    tb = 128
    cos_p, sin_p = _rope_tables(jnp.arange(T), ROPE)
    cos_s, sin_s = _rope_tables(past_len + jnp.arange(Ts), ROPE)

    hp = x_prompt.reshape(n_p, D)
    hs = x_sample.reshape(n_s, D)
    outs = {k: [] for k in ('kv_p', 'pe_p', 'ssm_p', 'conv_p', 'kv_s', 'pe_s', 'ssm_s', 'conv_s')}
    for l in range(depth):
        mod = _matmul(c_all, w_ada[l], F32, c_all.shape[0], 1024, b_ada[l])
        last = l == depth - 1
        grp_p = _Group(mod[:B].reshape(B, 1, 6 * D), modf[:B].reshape(B, 1, 2 * D), tm_p, T // tm_p)
        grp_s = _Group(mod[B:].reshape(1, Bs, 6 * D), modf[B:].reshape(1, Bs, 2 * D), Bs, 1)
        grp_pc = _Group(grp_p.mod3, grp_p.modf3, tb, T // tb)

        offs = [0]
        for s in (CONV, H * DV, H, H, QL, KVL, ROPE, D, D):
            offs.append(offs[-1] + s)
        wi = w_in[l]
        seg = lambda i: wi[:, offs[i]:offs[i + 1]]
        w_qkv = seg(0).astype(BF16)
        w_z = seg(1).astype(BF16)
        w_gab = jnp.concatenate([seg(7), seg(8)], axis=1).astype(BF16)
        n_small = QL + KVL + 2 * ROPE
        ba_col = -(-n_small // LANE)
        w_small = jnp.concatenate(
            [seg(4), seg(5), seg(6), _rotate_cols(seg(6), ROPE), jnp.zeros((D, ba_col * LANE - n_small), F32),
             seg(2), jnp.zeros((D, 8 - H), F32), seg(3), jnp.zeros((D, LANE - 8 - H), F32)], axis=1).astype(BF16)
        cw_t = conv_w[l].T
        gparams = jnp.zeros((2, LANE), F32).at[0, 8:8 + H].set(A_log[l]).at[1, 8:8 + H].set(dt_bias[l])
        wq = w_uq[l].reshape(QL, HM, NOPE + ROPE)
        wq_pe = wq[:, :, NOPE:].reshape(QL, HM * ROPE)
        wq_all = jnp.concatenate([wq[:, :, :NOPE].reshape(QL, HM * NOPE), wq_pe], axis=1).astype(BF16)
        wq_rot = _rotate_cols(wq_pe, ROPE).astype(BF16)
        wuk_all = jnp.transpose(w_uk[l], (1, 0, 2)).reshape(KVL, HM * NOPE).astype(BF16)
        wuv_all = jnp.transpose(w_uv[l], (1, 0, 2)).reshape(KVL, HM * VH).astype(BF16)
        qnw = q_norm_w[l].reshape(1, QL)
        kvnw = kv_norm_w[l].reshape(1, KVL)
        wog = w_o_gdn[l].astype(BF16)
        wom = w_o_mla[l].astype(BF16)
        wout = w_out[l].astype(BF16)
        wr = jnp.pad(w_router[l], ((0, 0), (0, LANE - E)))
        br = jnp.pad(b_router[l], (0, LANE - E)).reshape(1, LANE)
        nfw = norm_ffn_w[l].reshape(1, D)

        h1 = _norm_mod(hp, norm_mix_w[l], grp_p, 1, 0, BF16)
        qkv_p = _matmul(h1, w_qkv, BF16, 1024, 512)
        z_p = _matmul(h1, w_z, BF16, 1024, 512)
        gab_p = _matmul(h1, w_gab, BF16, 1024, 512)
        small_p = _matmul(h1, w_small, F32, 1024, w_small.shape[1])
        og_p, ssm_p = _gdn_prompt(qkv_p, z_p, small_p, ba_col, cw_t, gparams, gdn_norm_w[l], B, T, H, DK, DV)
        q_p, k_p, v_p, lat_p, pe_p = _mla_prep_prompt(small_p, cos_p, sin_p, qnw, kvnw, wq_all, wq_rot, wuk_all,
                                                      wuv_all, T, tm_p, HM, NOPE, ROPE, VH, QL, KVL, scale)
        om_p = _flash_attention(q_p, k_p, v_p, B, T, tm_p)
        x1_p, h2_p, te_p, tg_p = _post_mixer(hp, og_p, om_p, gab_p, grp_p, wog, wom, wout, nfw, wr, br, E)
        outs['kv_p'].append(lat_p.reshape(B, T, KVL))
        outs['pe_p'].append(pe_p.reshape(B, T, ROPE))
        outs['ssm_p'].append(ssm_p)
        outs['conv_p'].append(qkv_p.reshape(B, T, CONV)[:, T - (CONV_WIDTH - 1):, :].astype(F32))

        h1s = _norm_mod(hs, norm_mix_w[l], grp_s, 1, 0, BF16)
        qkv_s = _matmul(h1s, w_qkv, F32, Bs, 512)
        z_s = _matmul(h1s, w_z, BF16, Bs, 512)
        gab_s = _matmul(h1s, w_gab, BF16, Bs, 512)
        small_s = _matmul(h1s, w_small, F32, Bs, w_small.shape[1])
        og_s, ssm_s, conv_s = _gdn_decode(qkv_s, z_s, small_s, ba_col, state_conv[l], state_ssm[l], cw_t, gparams,
                                          gdn_norm_w[l], H, DK, DV)
        qn_s, qp_s, lat_s, pe_s = _mla_prep_sample(small_s, cos_s, sin_s, qnw, kvnw, wq_all, wq_rot,
                                                   HM, NOPE, ROPE, QL, KVL, scale)
        q_lat = jnp.transpose(_q_latent(qn_s, w_uk[l], scale), (1, 0, 2))
        o_lat = _paged_attention(q_lat, qp_s.reshape(n_s, HM, ROPE), lat_s, pe_s, cache_kv, cache_pe_t, page_table, l)
        om_s = _o_value(jnp.transpose(o_lat, (1, 0, 2)), w_uv[l])
        x1_s, h2_s, te_s, tg_s = _post_mixer(hs, og_s, om_s, gab_s, grp_s, wog, wom, wout, nfw, wr, br, E)
        outs['kv_s'].append(lat_s.reshape(Bs, Ts, KVL))
        outs['pe_s'].append(pe_s.reshape(Bs, Ts, ROPE))
        outs['ssm_s'].append(ssm_s)
        outs['conv_s'].append(conv_s)

        tm_e = 256
        h2 = jnp.concatenate([h2_p, h2_s], axis=0)
        top_e = jnp.concatenate([te_p[:, :TOP_K], te_s[:, :TOP_K]], axis=0)
        slots, blk_e, n_active, n_slots = _route(top_e, E, tm_e)
        h_sorted = _dispatch(h2, slots, n_slots, tb)
        y_pad = _experts(h_sorted, blk_e, n_active, w_gu, b_gu, w_dn, b_dn, l, tm_e)
        if not last:
            raise NotImplementedError("stacked layers need an un-normalised residual output")
        hp = _combine_final(y_pad, slots[:n_p], tg_p, x1_p, grp_pc, norm_final_w)
        hs = _combine_final(y_pad, slots[n_p:], tg_s, x1_s, grp_s, norm_final_w)

    st = lambda k: jnp.stack(outs[k])
    return (hp.reshape(B, T, D), hs.reshape(Bs, Ts, D),
            st('kv_p'), st('pe_p'), st('ssm_p'), st('conv_p'),
            st('kv_s'), st('pe_s'), st('ssm_s'), st('conv_s'))
```

```python
import functools

import jax
import jax.numpy as jnp
from jax import lax
from jax.experimental import pallas as pl
from jax.experimental.pallas import tpu as pltpu

F32 = jnp.float32
BF16 = jnp.bfloat16
HI = lax.Precision.HIGHEST

NORM_EPS = 1e-6
ROPE_THETA = 10000.0
TOP_K = 4
SWIGLU_LIMIT = 7.0
SWIGLU_ALPHA = 1.702
CONV_WIDTH = 4
GDN_CHUNK = 64
PAGE_GROUP = 8
PAGE_REQUESTS = 4
LANE = 128
VMEM_LIMIT = 48 * 1024 * 1024
VMEM_LIMIT_MOE = 58 * 1024 * 1024

NT = (((1,), (1,)), ((), ()))
TN = (((0,), (0,)), ((), ()))


def _cp(sem, vmem=VMEM_LIMIT):
    return pltpu.CompilerParams(dimension_semantics=sem, vmem_limit_bytes=vmem)


def _rms(x):
    return x * lax.rsqrt(jnp.mean(x * x, axis=-1, keepdims=True) + NORM_EPS)


def _softplus(x):
    return jnp.maximum(x, 0.0) + jnp.log1p(jnp.exp(-jnp.abs(x)))


def _silu(x):
    return x * jax.nn.sigmoid(x)


def _bdot(a, b):
    return jnp.dot(a.astype(BF16), b.astype(BF16), preferred_element_type=F32)


def _bdot_g(a, b, dims):
    return lax.dot_general(a.astype(BF16), b.astype(BF16), dims, preferred_element_type=F32)


def _mm_kernel(*refs, has_bias):
    if has_bias:
        x_ref, w_ref, b_ref, o_ref = refs
    else:
        x_ref, w_ref, o_ref = refs
    acc = _bdot(x_ref[...], w_ref[...])
    if has_bias:
        acc = acc + b_ref[...]
    o_ref[...] = acc.astype(o_ref.dtype)


def _matmul(x, w, out_dtype, tm, tn, bias=None):
    M, K = x.shape
    N = w.shape[1]
    tm, tn = min(tm, M), min(tn, N)
    in_specs = [pl.BlockSpec((tm, K), lambda i, j: (i, 0)), pl.BlockSpec((K, tn), lambda i, j: (0, j))]
    args = [x, w]
    if bias is not None:
        in_specs.append(pl.BlockSpec((1, tn), lambda i, j: (0, j)))
        args.append(bias.reshape(1, N))
    return pl.pallas_call(
        functools.partial(_mm_kernel, has_bias=bias is not None),
        grid=(M // tm, N // tn), in_specs=in_specs,
        out_specs=pl.BlockSpec((tm, tn), lambda i, j: (i, j)),
        out_shape=jax.ShapeDtypeStruct((M, N), out_dtype),
        compiler_params=_cp(("parallel", "parallel")))(*args)


class _Group:
    def __init__(self, mod3, modf3, tm, blocks_per_g):
        self.mod3, self.modf3, self.tm, self.bpg = mod3, modf3, tm, blocks_per_g

    def spec(self, j, d):
        r, bpg = self.mod3.shape[1], self.bpg
        return pl.BlockSpec((None, r, d), lambda i: (i // bpg, 0, j))


def _norm_mod_kernel(x_ref, w_ref, sc_ref, sh_ref, o_ref):
    y = _rms(x_ref[...]) * w_ref[...]
    o_ref[...] = (y * (1.0 + sc_ref[...]) + sh_ref[...]).astype(o_ref.dtype)


def _norm_mod(x, w, grp, j_scale, j_shift, out_dtype):
    n, d = x.shape
    tm = grp.tm
    return pl.pallas_call(
        _norm_mod_kernel, grid=(n // tm,),
        in_specs=[pl.BlockSpec((tm, d), lambda i: (i, 0)), pl.BlockSpec((1, d), lambda i: (0, 0)),
                  grp.spec(j_scale, d), grp.spec(j_shift, d)],
        out_specs=pl.BlockSpec((tm, d), lambda i: (i, 0)),
        out_shape=jax.ShapeDtypeStruct((n, d), out_dtype),
        compiler_params=_cp(("parallel",)))(x, w.reshape(1, d), grp.mod3, grp.mod3)


SUBLANES = 8
PREV_ROWS = 16


def _split_bf16(x):
    hi = x.astype(BF16)
    return hi, (x - hi.astype(F32)).astype(BF16)


def _dot3(a, b):
    d = lambda x, y: jnp.dot(x, y, preferred_element_type=F32)
    return d(a[0], b[0]) + d(a[0], b[1]) + d(a[1], b[0])


def _expand_matrix(C):
    k = jnp.arange(C)
    n = jnp.arange(SUBLANES * LANE)
    rem = n % LANE
    hit = ((k[:, None] // SUBLANES == rem[None, :] // SUBLANES) & (k[:, None] % SUBLANES == n[None, :] // LANE)
           & (rem[None, :] < C))
    return hit.astype(BF16)


def _unit_lower_inverses(a_list, at_list, g_ref, C):
    nh = len(a_list)
    row = lax.broadcasted_iota(jnp.int32, (C, C), 0)
    col = lax.broadcasted_iota(jnp.int32, (C, C), 1)
    blockdiag = (row // SUBLANES) == (col // SUBLANES)
    packed = []
    for at in at_list:
        m = jnp.where(blockdiag, at, 0.0)
        d = m[0:SUBLANES]
        for b in range(1, C // SUBLANES):
            d = d + m[SUBLANES * b:SUBLANES * (b + 1)]
        packed.append(d)
    stack = _split_bf16(jnp.concatenate(packed, axis=0))
    g = g_ref[...]
    coef = (jnp.dot(stack[0], g, preferred_element_type=F32) + jnp.dot(stack[1], g, preferred_element_type=F32))
    sub = lax.broadcasted_iota(jnp.int32, (SUBLANES, LANE), 0)
    lane = lax.broadcasted_iota(jnp.int32, (SUBLANES, LANE), 1)
    unit = jnp.where((lane % SUBLANES == sub) & (lane < C), 1.0, 0.0)
    xd = [unit] * nh
    for i in range(1, SUBLANES):
        e_i = jnp.where((lane % SUBLANES == i) & (lane < C), 1.0, 0.0)[0:1]
        for h in range(nh):
            c_i = coef[SUBLANES * h:SUBLANES * (h + 1), LANE * i:LANE * (i + 1)]
            new_row = e_i - jnp.sum(c_i * xd[h], axis=0, keepdims=True)
            xd[h] = jnp.where(sub == i, new_row, xd[h])
    x = [jnp.where(blockdiag, jnp.concatenate([xd[h][:, :C]] * (C // SUBLANES), axis=0), 0.0) for h in range(nh)]
    s = SUBLANES
    while s < C:
        below = ((row // s) % 2 == 1) & ((col // s) == (row // s) - 1)
        xs = [_split_bf16(x[h]) for h in range(nh)]
        m1 = [_dot3(xs[h], _split_bf16(jnp.where(below, a_list[h], 0.0))) for h in range(nh)]
        x = [x[h] - _dot3(_split_bf16(m1[h]), xs[h]) for h in range(nh)]
        s *= 2
    return x


def _gdn_prep_kernel(qkv_ref, prev_ref, sm_ref, cw_ref, gp_ref, g_ref,
                     u_ref, w_ref, qe_ref, kd_ref, aqk_ref, dec_ref, xbuf_ref, *, H, DK, DV, C):
    c = pl.program_id(1)
    QK = H * DK
    P = PREV_ROWS

    prev = prev_ref[...].astype(F32)
    xbuf_ref[0:P, :] = jnp.where(c == 0, 0.0, prev)
    xbuf_ref[P:P + C, :] = qkv_ref[...].astype(F32)
    y = None
    for j in range(CONV_WIDTH):
        o = P - (CONV_WIDTH - 1) + j
        term = xbuf_ref[o:o + C, :] * cw_ref[j:j + 1, :]
        y = term if y is None else y + term
    y = _silu(y)

    sm = sm_ref[...]
    beta_all = jax.nn.sigmoid(sm)
    g_all = -jnp.exp(gp_ref[0:1, :]) * _softplus(sm + gp_ref[1:2, :])
    row = lax.broadcasted_iota(jnp.int32, (C, C), 0)
    col = lax.broadcasted_iota(jnp.int32, (C, C), 1)
    tril = (col <= row).astype(F32)
    gam_all = jnp.dot(tril, g_all, precision=HI, preferred_element_type=F32)
    lane = lax.broadcasted_iota(jnp.int32, sm.shape, 1)
    rows_t = jnp.where(lane < 8, beta_all, gam_all).T

    a_list, at_list, rhs_list = [], [], []
    for h in range(H):
        qh = y[:, h * DK:(h + 1) * DK]
        kh = y[:, QK + h * DK:QK + (h + 1) * DK]
        vh = y[:, 2 * QK + h * DV:2 * QK + (h + 1) * DV]
        qh = qh * lax.rsqrt(jnp.sum(qh * qh, axis=-1, keepdims=True) + NORM_EPS) * (DK ** -0.5)
        kh = kh * lax.rsqrt(jnp.sum(kh * kh, axis=-1, keepdims=True) + NORM_EPS)
        beta_c = beta_all[:, h:h + 1]
        gam_c = gam_all[:, 8 + h:9 + h]
        beta_r = rows_t[h:h + 1, :]
        gam_r = rows_t[8 + h:9 + h, :]
        dm = gam_c - gam_r
        decay = jnp.exp(jnp.where(col <= row, dm, -jnp.inf))
        decay_t = jnp.exp(jnp.where(row < col, -dm, -jnp.inf))
        qk_kk = _bdot_g(jnp.concatenate([qh, kh], axis=0), kh, NT)
        kk = qk_kk[C:]
        a_list.append(jnp.where(col < row, kk * beta_c * decay, 0.0))
        at_list.append(kk * beta_r * decay_t)
        egam = jnp.exp(gam_c)
        rhs_list.append(jnp.concatenate([beta_c * vh, beta_c * egam * kh], axis=1).astype(BF16))
        g_last = gam_c[C - 1:C, :]
        qe_ref[:, h * DK:(h + 1) * DK] = (qh * egam).astype(qe_ref.dtype)
        kd_ref[:, h * DK:(h + 1) * DK] = (kh * jnp.exp(g_last - gam_c)).astype(kd_ref.dtype)
        aqk_ref[:, h * C:(h + 1) * C] = (qk_kk[:C] * decay).astype(aqk_ref.dtype)
        dec_ref[h:h + 1, :] = jnp.broadcast_to(jnp.exp(g_last), (1, LANE))

    t_inv = _unit_lower_inverses(a_list, at_list, g_ref, C)
    for h in range(H):
        t_hi, t_lo = _split_bf16(t_inv[h])
        uw = (jnp.dot(t_hi, rhs_list[h], preferred_element_type=F32)
              + jnp.dot(t_lo, rhs_list[h], preferred_element_type=F32))
        u_ref[:, h * DV:(h + 1) * DV] = uw[:, :DV].astype(u_ref.dtype)
        w_ref[:, h * DK:(h + 1) * DK] = uw[:, DV:].astype(w_ref.dtype)


def _gdn_scan_kernel(u_ref, w_ref, qe_ref, kd_ref, aqk_ref, dec_ref, z_ref, nw_ref, og_ref, s_out_ref, s_ref,
                     *, H, DK, DV, C):
    c = pl.program_id(1)

    @pl.when(c == 0)
    def _():
        s_ref[...] = jnp.zeros_like(s_ref)

    ks = [slice(h * DK, (h + 1) * DK) for h in range(H)]
    vs = [slice(h * DV, (h + 1) * DV) for h in range(H)]
    s16 = [s_ref[h].astype(BF16) for h in range(H)]
    ws = [jnp.dot(w_ref[:, ks[h]], s16[h], preferred_element_type=F32) for h in range(H)]
    qs = [jnp.dot(qe_ref[:, ks[h]], s16[h], preferred_element_type=F32) for h in range(H)]
    v16 = [(u_ref[:, vs[h]].astype(F32) - ws[h]).astype(BF16) for h in range(H)]
    ds = [lax.dot_general(kd_ref[:, ks[h]], v16[h], TN, preferred_element_type=F32) for h in range(H)]
    o = [qs[h] + jnp.dot(aqk_ref[:, h * C:(h + 1) * C], v16[h], preferred_element_type=F32) for h in range(H)]
    for h in range(H):
        s_ref[h] = s_ref[h] * dec_ref[h:h + 1, :] + ds[h]
        zh = z_ref[:, vs[h]].astype(F32)
        og_ref[:, vs[h]] = (_rms(o[h]) * nw_ref[...] * _silu(zh)).astype(og_ref.dtype)

    @pl.when(c == pl.num_programs(1) - 1)
    def _():
        s_out_ref[...] = s_ref[...]


def _gdn_prompt(qkv, z, small, ba_col, cw_t, gparams, norm_w, B, T, H, DK, DV):
    C = GDN_CHUNK
    nc = T // C
    n, W = qkv.shape
    kw = dict(H=H, DK=DK, DV=DV, C=C)
    blk = lambda w: pl.BlockSpec((C, w), lambda b, c: (b * nc + c, 0))
    ppc = C // PREV_ROWS
    u, w, qe, kd, aqk, dec = pl.pallas_call(
        functools.partial(_gdn_prep_kernel, **kw), grid=(B, nc),
        in_specs=[blk(W),
                  pl.BlockSpec((PREV_ROWS, W), lambda b, c: (jnp.maximum((b * nc + c) * ppc - 1, 0), 0)),
                  pl.BlockSpec((C, LANE), lambda b, c: (b * nc + c, ba_col)),
                  pl.BlockSpec((CONV_WIDTH, W), lambda b, c: (0, 0)),
                  pl.BlockSpec((2, LANE), lambda b, c: (0, 0)),
                  pl.BlockSpec((C, SUBLANES * LANE), lambda b, c: (0, 0))],
        out_specs=[blk(H * DV), blk(H * DK), blk(H * DK), blk(H * DK), blk(H * C),
                   pl.BlockSpec((None, SUBLANES, LANE), lambda b, c: (b * nc + c, 0, 0))],
        out_shape=[jax.ShapeDtypeStruct((n, H * DV), BF16), jax.ShapeDtypeStruct((n, H * DK), BF16),
                   jax.ShapeDtypeStruct((n, H * DK), BF16), jax.ShapeDtypeStruct((n, H * DK), BF16),
                   jax.ShapeDtypeStruct((n, H * C), BF16), jax.ShapeDtypeStruct((B * nc, SUBLANES, LANE), F32)],
        scratch_shapes=[pltpu.VMEM((C + PREV_ROWS, W), F32)],
        compiler_params=_cp(("parallel", "parallel")))(qkv, qkv, small, cw_t, gparams, _expand_matrix(C))
    return pl.pallas_call(
        functools.partial(_gdn_scan_kernel, **kw), grid=(B, nc),
        in_specs=[blk(H * DV), blk(H * DK), blk(H * DK), blk(H * DK), blk(H * C),
                  pl.BlockSpec((None, SUBLANES, LANE), lambda b, c: (b * nc + c, 0, 0)),
                  blk(H * DV), pl.BlockSpec((1, DV), lambda b, c: (0, 0))],
        out_specs=[blk(H * DV), pl.BlockSpec((None, H, DK, DV), lambda b, c: (b, 0, 0, 0))],
        out_shape=[jax.ShapeDtypeStruct((n, H * DV), BF16), jax.ShapeDtypeStruct((B, H, DK, DV), F32)],
        scratch_shapes=[pltpu.VMEM((H, DK, DV), F32)],
        compiler_params=_cp(("parallel", "arbitrary")))(u, w, qe, kd, aqk, dec, z, norm_w.reshape(1, DV))


def _gdn_decode_kernel(u_ref, z_ref, ba_ref, buf_ref, s_in_ref, cw_ref, gp_ref, nw_ref,
                       og_ref, s_out_ref, buf_out_ref, *, H, DK, DV):
    QK = H * DK
    u = u_ref[...]
    buf = buf_ref[...]
    y = buf[0:1] * cw_ref[0:1, :]
    y = y + buf[1:2] * cw_ref[1:2, :]
    y = y + buf[2:3] * cw_ref[2:3, :]
    y = y + u * cw_ref[3:4, :]
    buf_out_ref[0:2, :] = buf[1:3]
    buf_out_ref[2:3, :] = u
    y = _silu(y)
    sm = ba_ref[...]
    beta_all = jax.nn.sigmoid(sm)
    g_all = -jnp.exp(gp_ref[0:1, :]) * _softplus(sm + gp_ref[1:2, :])
    rows = []
    for h in range(H):
        kh = y[:, QK + h * DK:QK + (h + 1) * DK]
        rows.append(kh * lax.rsqrt(jnp.sum(kh * kh, axis=-1, keepdims=True) + NORM_EPS))
    for h in range(H):
        qh = y[:, h * DK:(h + 1) * DK]
        rows.append(qh * lax.rsqrt(jnp.sum(qh * qh, axis=-1, keepdims=True) + NORM_EPS) * (DK ** -0.5))
    rows.append(jnp.zeros((DK - 2 * H, DK), F32))
    cols = jnp.concatenate(rows, axis=0).T
    for h in range(H):
        kcol = cols[:, h:h + 1]
        qcol = cols[:, H + h:H + h + 1]
        vh = y[:, 2 * QK + h * DV:2 * QK + (h + 1) * DV]
        s_dec = s_in_ref[h] * jnp.exp(g_all[:, 8 + h:9 + h])
        v_new = beta_all[:, h:h + 1] * (vh - jnp.sum(kcol * s_dec, axis=0, keepdims=True))
        s_new = s_dec + kcol * v_new
        s_out_ref[h] = s_new
        o = jnp.sum(qcol * s_new, axis=0, keepdims=True)
        zh = z_ref[:, h * DV:(h + 1) * DV].astype(F32)
        og_ref[:, h * DV:(h + 1) * DV] = (_rms(o) * nw_ref[...] * _silu(zh)).astype(og_ref.dtype)


def _gdn_decode(qkv, z, small, ba_col, conv_buf, ssm, cw_t, gparams, norm_w, H, DK, DV):
    Bs, W = qkv.shape
    kern = functools.partial(_gdn_decode_kernel, H=H, DK=DK, DV=DV)
    og, s_new, buf_new = pl.pallas_call(
        kern, grid=(Bs,),
        in_specs=[pl.BlockSpec((None, 1, W), lambda b: (b, 0, 0)),
                  pl.BlockSpec((None, 1, H * DV), lambda b: (b, 0, 0)),
                  pl.BlockSpec((None, 1, LANE), lambda b: (b, 0, ba_col)),
                  pl.BlockSpec((None, CONV_WIDTH - 1, W), lambda b: (b, 0, 0)),
                  pl.BlockSpec((None, H, DK, DV), lambda b: (b, 0, 0, 0)),
                  pl.BlockSpec((CONV_WIDTH, W), lambda b: (0, 0)),
                  pl.BlockSpec((2, LANE), lambda b: (0, 0)),
                  pl.BlockSpec((1, DV), lambda b: (0, 0))],
        out_specs=[pl.BlockSpec((None, 1, H * DV), lambda b: (b, 0, 0)),
                   pl.BlockSpec((None, H, DK, DV), lambda b: (b, 0, 0, 0)),
                   pl.BlockSpec((None, CONV_WIDTH - 1, W), lambda b: (b, 0, 0))],
        out_shape=[jax.ShapeDtypeStruct((Bs, 1, H * DV), BF16),
                   jax.ShapeDtypeStruct(ssm.shape, F32),
                   jax.ShapeDtypeStruct(conv_buf.shape, F32)],
        compiler_params=_cp(("parallel",)))(
            qkv.reshape(Bs, 1, W), z.reshape(Bs, 1, H * DV), small.reshape(Bs, 1, small.shape[1]),
            conv_buf, ssm, cw_t, gparams, norm_w.reshape(1, DV))
    return og.reshape(Bs, H * DV), s_new, buf_new


def _mla_prep_prompt_kernel(sm_ref, cos_ref, sin_ref, cost_ref, sint_ref, qnw_ref, kvnw_ref,
                            wqt_ref, wqrt_ref, wuk_ref, wuvt_ref,
                            qt_ref, k_ref, vt_ref, lat_ref, pe_ref, *, H, NOPE, ROPE, VH, QL, KVL, scale):
    sm = sm_ref[...]
    qn = (_rms(sm[:, :QL]) * qnw_ref[...]).astype(BF16)
    qft = lax.dot_general(wqt_ref[...], qn, NT, preferred_element_type=F32)
    qrt = lax.dot_general(wqrt_ref[...], qn, NT, preferred_element_type=F32)
    qpt = qft[H * NOPE:, :] * cost_ref[...] + qrt * sint_ref[...]
    lat = _rms(sm[:, QL:QL + KVL]) * kvnw_ref[...]
    lat_ref[...] = lat
    lat16 = lat.astype(BF16)
    kn = jnp.dot(lat16, wuk_ref[...], preferred_element_type=F32)
    vt = lax.dot_general(wuvt_ref[...], lat16, NT, preferred_element_type=F32)
    o = QL + KVL
    kr = sm[:, o:o + ROPE] * cos_ref[...] + sm[:, o + ROPE:o + 2 * ROPE] * sin_ref[...]
    pe_ref[...] = kr
    for h in range(H):
        qt_ref[h] = (jnp.concatenate([qft[h * NOPE:(h + 1) * NOPE, :], qpt[h * ROPE:(h + 1) * ROPE, :]], axis=0)
                     * scale).astype(qt_ref.dtype)
        k_ref[h] = jnp.concatenate([kn[:, h * NOPE:(h + 1) * NOPE], kr], axis=-1).astype(k_ref.dtype)
        vt_ref[h, 0] = vt[h * VH:(h + 1) * VH, :].astype(vt_ref.dtype)


def _mla_prep_prompt(small, cos, sin, qnw, kvnw, wq, wqr, wuk, wuv, T, tm, H, NOPE, ROPE, VH, QL, KVL, scale):
    n, ws = small.shape
    nt = T // tm
    kern = functools.partial(_mla_prep_prompt_kernel, H=H, NOPE=NOPE, ROPE=ROPE, VH=VH, QL=QL, KVL=KVL, scale=scale)
    full = lambda a: pl.BlockSpec(a.shape, lambda i: (0,) * a.ndim)
    dqk = NOPE + ROPE
    cos_t = jnp.tile(cos.T, (H, 1))
    sin_t = jnp.tile(sin.T, (H, 1))
    wqt, wqrt, wuvt = wq.T, wqr.T, wuv.T
    return pl.pallas_call(
        kern, grid=(n // tm,),
        in_specs=[pl.BlockSpec((tm, ws), lambda i: (i, 0)),
                  pl.BlockSpec((tm, ROPE), lambda i: (i % nt, 0)),
                  pl.BlockSpec((tm, ROPE), lambda i: (i % nt, 0)),
                  pl.BlockSpec((H * ROPE, tm), lambda i: (0, i % nt)),
                  pl.BlockSpec((H * ROPE, tm), lambda i: (0, i % nt)),
                  full(qnw), full(kvnw), full(wqt), full(wqrt), full(wuk), full(wuvt)],
        out_specs=[pl.BlockSpec((H, dqk, tm), lambda i: (0, 0, i)),
                   pl.BlockSpec((H, tm, dqk), lambda i: (0, i, 0)),
                   pl.BlockSpec((H, 1, VH, tm), lambda i: (0, i, 0, 0)),
                   pl.BlockSpec((tm, KVL), lambda i: (i, 0)),
                   pl.BlockSpec((tm, ROPE), lambda i: (i, 0))],
        out_shape=[jax.ShapeDtypeStruct((H, dqk, n), BF16), jax.ShapeDtypeStruct((H, n, dqk), BF16),
                   jax.ShapeDtypeStruct((H, n // tm, VH, tm), BF16), jax.ShapeDtypeStruct((n, KVL), F32),
                   jax.ShapeDtypeStruct((n, ROPE), F32)],
        compiler_params=_cp(("parallel",)))(small, cos, sin, cos_t, sin_t, qnw, kvnw, wqt, wqrt, wuk, wuvt)


def _mla_prep_sample_kernel(sm_ref, cos_ref, sin_ref, qnw_ref, kvnw_ref, wq_ref, wqr_ref,
                            qn_ref, qp_ref, lat_ref, pe_ref, *, H, NOPE, ROPE, QL, KVL, scale):
    sm = sm_ref[...]
    cos = cos_ref[...]
    sin = sin_ref[...]
    qn = _rms(sm[:, :QL]) * qnw_ref[...]
    qf = _bdot(qn, wq_ref[...])
    qr = _bdot(qn, wqr_ref[...])
    qn_ref[...] = qf[:, :H * NOPE].astype(qn_ref.dtype)
    lat_ref[...] = _rms(sm[:, QL:QL + KVL]) * kvnw_ref[...]
    o = QL + KVL
    pe_ref[...] = sm[:, o:o + ROPE] * cos + sm[:, o + ROPE:o + 2 * ROPE] * sin
    for h in range(H):
        p0 = H * NOPE + h * ROPE
        qp = qf[:, p0:p0 + ROPE] * cos + qr[:, h * ROPE:(h + 1) * ROPE] * sin
        qp_ref[:, h * ROPE:(h + 1) * ROPE] = (qp * scale).astype(qp_ref.dtype)


def _mla_prep_sample(small, cos, sin, qnw, kvnw, wq, wqr, H, NOPE, ROPE, QL, KVL, scale):
    n, ws = small.shape
    kern = functools.partial(_mla_prep_sample_kernel, H=H, NOPE=NOPE, ROPE=ROPE, QL=QL, KVL=KVL, scale=scale)
    full = lambda a: pl.BlockSpec(a.shape, lambda i: (0,) * a.ndim)
    return pl.pallas_call(
        kern, grid=(1,),
        in_specs=[full(small), full(cos), full(sin), full(qnw), full(kvnw), full(wq), full(wqr)],
        out_specs=[pl.BlockSpec((n, H * NOPE), lambda i: (0, 0)), pl.BlockSpec((n, H * ROPE), lambda i: (0, 0)),
                   pl.BlockSpec((n, KVL), lambda i: (0, 0)), pl.BlockSpec((n, ROPE), lambda i: (0, 0))],
        out_shape=[jax.ShapeDtypeStruct((n, H * NOPE), BF16), jax.ShapeDtypeStruct((n, H * ROPE), BF16),
                   jax.ShapeDtypeStruct((n, KVL), F32), jax.ShapeDtypeStruct((n, ROPE), F32)],
        compiler_params=_cp(("arbitrary",)))(small, cos, sin, qnw, kvnw, wq, wqr)


def _head_proj_kernel(x_ref, w_ref, o_ref, *, dims, scale):
    o_ref[...] = (_bdot_g(x_ref[...], w_ref[...], dims) * scale).astype(o_ref.dtype)


def _q_latent(q_nope, w_uk, scale):
    H, KVL, NOPE = w_uk.shape
    n = q_nope.shape[0]
    return pl.pallas_call(
        functools.partial(_head_proj_kernel, dims=NT, scale=scale), grid=(H,),
        in_specs=[pl.BlockSpec((n, NOPE), lambda h: (0, h)), pl.BlockSpec((None, KVL, NOPE), lambda h: (h, 0, 0))],
        out_specs=pl.BlockSpec((None, n, KVL), lambda h: (h, 0, 0)),
        out_shape=jax.ShapeDtypeStruct((H, n, KVL), BF16),
        compiler_params=_cp(("parallel",)))(q_nope, w_uk)


def _o_value(o_lat, w_uv):
    H, KVL, VH = w_uv.shape
    n = o_lat.shape[1]
    return pl.pallas_call(
        functools.partial(_head_proj_kernel, dims=(((1,), (0,)), ((), ())), scale=1.0), grid=(H,),
        in_specs=[pl.BlockSpec((None, n, KVL), lambda h: (h, 0, 0)), pl.BlockSpec((None, KVL, VH), lambda h: (h, 0, 0))],
        out_specs=pl.BlockSpec((n, VH), lambda h: (0, h)),
        out_shape=jax.ShapeDtypeStruct((n, H * VH), BF16),
        compiler_params=_cp(("parallel",)))(o_lat, w_uv)


FLASH_SPLIT = 2


def _flash_kernel(qt_ref, k_ref, vt_ref, o_ref, m_ref, l_ref, acc_ref, *, tq):
    qi = pl.program_id(2)
    ns = FLASH_SPLIT
    hq = tq // ns
    m_ref[...] = jnp.full_like(m_ref, -jnp.inf)
    l_ref[...] = jnp.zeros_like(l_ref)
    acc_ref[...] = jnp.zeros_like(acc_ref)
    qt = [qt_ref[:, t * hq:(t + 1) * hq] for t in range(ns)]

    def update(st, vt):
        m_prev = [m_ref[t] for t in range(ns)]
        m_new = [jnp.maximum(m_prev[t], jnp.max(st[t], axis=0, keepdims=True)) for t in range(ns)]
        p = [jnp.exp(st[t] - m_new[t]) for t in range(ns)]
        corr = [jnp.exp(m_prev[t] - m_new[t]) for t in range(ns)]
        pv = [jnp.dot(vt, p[t].astype(BF16), preferred_element_type=F32) for t in range(ns)]
        for t in range(ns):
            m_ref[t] = m_new[t]
            l_ref[t] = corr[t] * l_ref[t] + jnp.sum(p[t], axis=0, keepdims=True)
            acc_ref[t] = acc_ref[t] * corr[t] + pv[t]

    def below_diagonal(j, carry):
        k = k_ref[pl.ds(pl.multiple_of(j * tq, tq), tq), :]
        update([jnp.dot(k, qt[t], preferred_element_type=F32) for t in range(ns)], vt_ref[j])
        return carry

    lax.fori_loop(0, qi, below_diagonal, 0)

    k = k_ref[pl.ds(pl.multiple_of(qi * tq, tq), tq), :]
    st = []
    for t in range(ns):
        s = jnp.dot(k, qt[t], preferred_element_type=F32)
        key = lax.broadcasted_iota(jnp.int32, s.shape, 0)
        qry = t * hq + lax.broadcasted_iota(jnp.int32, s.shape, 1)
        st.append(jnp.where(key <= qry, s, -jnp.inf))
    update(st, vt_ref[qi])
    for t in range(ns):
        o_ref[t * hq:(t + 1) * hq, :] = (acc_ref[t] / l_ref[t]).T.astype(o_ref.dtype)


def _flash_attention(qt, k, vt, B, T, tq):
    H, dqk, n = qt.shape
    VH = vt.shape[2]
    nq = T // tq
    hq = tq // FLASH_SPLIT
    kern = functools.partial(_flash_kernel, tq=tq)
    return pl.pallas_call(
        kern, grid=(B, H, nq),
        in_specs=[pl.BlockSpec((None, dqk, tq), lambda b, h, i: (h, 0, b * nq + i)),
                  pl.BlockSpec((None, T, dqk), lambda b, h, i: (h, b, 0)),
                  pl.BlockSpec((None, nq, VH, tq), lambda b, h, i: (h, b, 0, 0))],
        out_specs=pl.BlockSpec((tq, VH), lambda b, h, i: (b * nq + i, h)),
        out_shape=jax.ShapeDtypeStruct((n, H * VH), BF16),
        scratch_shapes=[pltpu.VMEM((FLASH_SPLIT, 1, hq), F32), pltpu.VMEM((FLASH_SPLIT, 1, hq), F32),
                        pltpu.VMEM((FLASH_SPLIT, VH, hq), F32)],
        compiler_params=_cp(("parallel", "parallel", "arbitrary")))(qt, k, vt)


def _paged_kernel(pt_ref, ql_ref, qp_ref, latn_ref, pen_ref, kv_hbm, pe_hbm, o_ref,
                  kv_buf, pe_buf, sem, m_ref, l_ref, acc_ref, *, R, G, layer):
    b = pl.program_id(0)
    g = pl.program_id(1)
    ng = pl.num_programs(1)
    step = b * ng + g
    slot = step % 2

    def page_copies(bb, gg, sl, real_pages):
        out = []
        for r in range(R):
            for i in range(G):
                page = pt_ref[bb * R + r, gg * G + i] if real_pages else 0
                out.append(pltpu.make_async_copy(kv_hbm.at[layer, page], kv_buf.at[sl, r * G + i], sem.at[0, sl]))
                out.append(pltpu.make_async_copy(pe_hbm.at[layer, page], pe_buf.at[sl, r * G + i], sem.at[1, sl]))
        return out

    @pl.when(step == 0)
    def _():
        for cp in page_copies(0, 0, 0, True):
            cp.start()

    @pl.when(step + 1 < pl.num_programs(0) * ng)
    def _():
        nxt = step + 1
        for cp in page_copies(nxt // ng, nxt % ng, 1 - slot, True):
            cp.start()

    for cp in page_copies(b, g, slot, False):
        cp.wait()

    @pl.when(g == 0)
    def _():
        m_ref[...] = jnp.full_like(m_ref, -jnp.inf)
        l_ref[...] = jnp.zeros_like(l_ref)
        acc_ref[...] = jnp.zeros_like(acc_ref)

    kvs = [[kv_buf[slot, r * G + i].astype(BF16) for i in range(G)] for r in range(R)]
    P = kvs[0][0].shape[0]
    s, p, corr, m_new = [], [], [], []
    for r in range(R):
        ql = ql_ref[r]
        qp = qp_ref[r]
        s.append(jnp.concatenate(
            [lax.dot_general(ql, kvs[r][i], NT, preferred_element_type=F32)
             + jnp.dot(qp, pe_buf[slot, r * G + i].astype(BF16), preferred_element_type=F32) for i in range(G)],
            axis=-1))
    for r in range(R):
        m_prev = m_ref[r]
        m_new.append(jnp.maximum(m_prev, jnp.max(s[r], axis=-1, keepdims=True)))
        corr.append(jnp.exp(m_prev - m_new[r]))
        p.append(jnp.exp(s[r] - m_new[r]))
    pv = []
    for r in range(R):
        t = None
        for i in range(G):
            d = jnp.dot(p[r][:, i * P:(i + 1) * P].astype(BF16), kvs[r][i], preferred_element_type=F32)
            t = d if t is None else t + d
        pv.append(t)
    for r in range(R):
        m_ref[r] = m_new[r]
        l_ref[r] = corr[r] * l_ref[r] + jnp.sum(p[r], axis=-1, keepdims=True)
        acc_ref[r] = acc_ref[r] * corr[r] + pv[r]

    @pl.when(g == pl.num_programs(1) - 1)
    def _():
        for r in range(R):
            latn = latn_ref[r]
            s_n = (jnp.sum(ql_ref[r].astype(F32) * latn, axis=-1, keepdims=True)
                   + jnp.sum(qp_ref[r].astype(F32) * pen_ref[r], axis=-1, keepdims=True))
            m_old = m_ref[r]
            m2 = jnp.maximum(m_old, s_n)
            c2 = jnp.exp(m_old - m2)
            p2 = jnp.exp(s_n - m2)
            o_ref[r] = ((acc_ref[r] * c2 + p2 * latn) / (l_ref[r] * c2 + p2)).astype(o_ref.dtype)


def _paged_attention(q_lat, q_pe, lat_new, pe_new, cache_kv, cache_pe_t, page_table, layer):
    Bs, H, KVL = q_lat.shape
    ROPE = q_pe.shape[2]
    n_pages = page_table.shape[1]
    P = cache_kv.shape[2]
    G = min(PAGE_GROUP, n_pages)
    R = min(PAGE_REQUESTS, Bs)

    req = lambda rows, w: pl.BlockSpec((R, rows, w), lambda b, g, pt: (b, 0, 0))
    hbm = pl.BlockSpec(memory_space=pl.ANY)
    grid_spec = pltpu.PrefetchScalarGridSpec(
        num_scalar_prefetch=1, grid=(Bs // R, n_pages // G),
        in_specs=[req(H, KVL), req(H, ROPE), req(1, KVL), req(1, ROPE), hbm, hbm],
        out_specs=req(H, KVL),
        scratch_shapes=[pltpu.VMEM((2, R * G, P, KVL), cache_kv.dtype), pltpu.VMEM((2, R * G, ROPE, P), cache_pe_t.dtype),
                        pltpu.SemaphoreType.DMA((2, 2)),
                        pltpu.VMEM((R, H, 1), F32), pltpu.VMEM((R, H, 1), F32), pltpu.VMEM((R, H, KVL), F32)])
    return pl.pallas_call(
        functools.partial(_paged_kernel, R=R, G=G, layer=layer), grid_spec=grid_spec,
        out_shape=jax.ShapeDtypeStruct((Bs, H, KVL), BF16),
        compiler_params=_cp(("arbitrary", "arbitrary")))(
            page_table, q_lat, q_pe, lat_new.reshape(Bs, 1, KVL), pe_new.reshape(Bs, 1, ROPE), cache_kv, cache_pe_t)


def _post_mixer_kernel(x_ref, og_ref, om_ref, gab_ref, g1_ref, sc2_ref, sh2_ref, wog_ref, wom_ref, wout_ref,
                       nw_ref, wr_ref, br_ref, x1_ref, h2_ref, te_ref, tg_ref, *, D, E):
    gab = gab_ref[...].astype(F32)
    merged = (jax.nn.sigmoid(gab[:, :D]) * _bdot(og_ref[...], wog_ref[...])
              + jax.nn.sigmoid(gab[:, D:]) * _bdot(om_ref[...], wom_ref[...]))
    x1 = x_ref[...] + g1_ref[...] * _bdot(merged, wout_ref[...])
    x1_ref[...] = x1
    h2 = _rms(x1) * nw_ref[...] * (1.0 + sc2_ref[...]) + sh2_ref[...]
    h2_ref[...] = h2
    logits = jnp.dot(h2, wr_ref[...], precision=HI, preferred_element_type=F32) + br_ref[...]
    lane = lax.broadcasted_iota(jnp.int32, logits.shape, 1)
    logits = jnp.where(lane < E, logits, -jnp.inf)
    te = jnp.zeros(logits.shape, jnp.int32)
    ex = jnp.zeros(logits.shape, F32)
    top = None
    for k in range(TOP_K):
        m = jnp.max(logits, axis=-1, keepdims=True)
        idx = jnp.min(jnp.where(logits == m, lane, LANE), axis=-1, keepdims=True)
        top = m if top is None else top
        te = jnp.where(lane == k, idx, te)
        ex = jnp.where(lane == k, jnp.exp(m - top), ex)
        logits = jnp.where(lane == idx, -jnp.inf, logits)
    te_ref[...] = te
    tg_ref[...] = ex / jnp.sum(ex, axis=-1, keepdims=True)


def _post_mixer(x, og, om, gab, grp, wog, wom, wout, norm_w, wr, br, E):
    n, d = x.shape
    tm = grp.tm
    rowblk = lambda w: pl.BlockSpec((tm, w), lambda i: (i, 0))
    full = lambda a: pl.BlockSpec(a.shape, lambda i: (0,) * a.ndim)
    return pl.pallas_call(
        functools.partial(_post_mixer_kernel, D=d, E=E), grid=(n // tm,),
        in_specs=[rowblk(d), rowblk(d), rowblk(d), rowblk(2 * d), grp.spec(2, d), grp.spec(4, d), grp.spec(3, d),
                  full(wog), full(wom), full(wout), full(norm_w), full(wr), full(br)],
        out_specs=[rowblk(d), rowblk(d), rowblk(LANE), rowblk(LANE)],
        out_shape=[jax.ShapeDtypeStruct((n, d), F32), jax.ShapeDtypeStruct((n, d), F32),
                   jax.ShapeDtypeStruct((n, LANE), jnp.int32), jax.ShapeDtypeStruct((n, LANE), F32)],
        compiler_params=_cp(("parallel",)))(x, og, om, gab, grp.mod3, grp.mod3, grp.mod3,
                                            wog, wom, wout, norm_w, wr, br)


def _moe_kernel(be_ref, tok0_ref, tokn_ref, dstp_ref, dstc_ref, h_hbm, wgu_ref, bgu_ref, wdn_ref, bdn_ref, y_hbm,
                xbuf_a, xbuf_b, ybuf_a, ybuf_b, gsem, ssem, wgu_s, wdn_s, *, DE, tm, trash_row):
    i = pl.program_id(0)
    nb = pl.num_programs(0)

    def gather_starts(idx_ref, xbuf, s):
        for r in range(tm):
            pltpu.make_async_copy(h_hbm.at[pl.ds(idx_ref[0, 0, r], 1)], xbuf.at[pl.ds(r, 1)], gsem.at[s]).start()

    def scatter_starts(dst_of_row, ybuf, s):
        for r in range(tm):
            pltpu.make_async_copy(ybuf.at[pl.ds(r, 1)], y_hbm.at[pl.ds(dst_of_row(r), 1)], ssem.at[s]).start()

    def gather_wait(xbuf, s):
        pltpu.make_async_copy(h_hbm.at[pl.ds(0, tm)], xbuf, gsem.at[s]).wait()

    def scatter_wait(ybuf, s):
        pltpu.make_async_copy(ybuf, y_hbm.at[pl.ds(0, tm)], ssem.at[s]).wait()

    first = jnp.logical_or(i == 0, be_ref[i] != be_ref[jnp.maximum(i - 1, 0)])

    @pl.when(first)
    def _():
        wgu_s[...] = wgu_ref[...].astype(BF16)
        wdn_s[...] = wdn_ref[...].astype(BF16)

    @pl.when(i == 0)
    def _():
        ybuf_b[...] = jnp.zeros_like(ybuf_b)
        gather_starts(tok0_ref, xbuf_a, 0)

    def step(x_cur, x_nxt, y_cur, y_prv, cur, nxt):
        @pl.when(i >= 1)
        def _():
            scatter_wait(y_cur, cur)

        gather_wait(x_cur, cur)
        gu = jnp.dot(x_cur[...].astype(BF16), wgu_s[...], preferred_element_type=F32) + bgu_ref[...]
        gt = jnp.minimum(gu[:, :DE], SWIGLU_LIMIT)
        up = jnp.clip(gu[:, DE:], -SWIGLU_LIMIT, SWIGLU_LIMIT)
        act = (up + 1.0) * gt * jax.nn.sigmoid(SWIGLU_ALPHA * gt)
        y_cur[...] = jnp.dot(act.astype(BF16), wdn_s[...], preferred_element_type=F32) + bdn_ref[...]
        gather_starts(tokn_ref, x_nxt, nxt)
        scatter_starts(lambda r: jnp.where(i == 0, trash_row + r, dstp_ref[0, 0, r]), y_prv, nxt)

        @pl.when(i == nb - 1)
        def _():
            scatter_wait(y_prv, nxt)
            scatter_starts(lambda r: dstc_ref[0, 0, r], y_cur, cur)
            scatter_wait(y_cur, cur)
            gather_wait(x_nxt, nxt)

    @pl.when(i % 2 == 0)
    def _():
        step(xbuf_a, xbuf_b, ybuf_a, ybuf_b, 0, 1)

    @pl.when(i % 2 == 1)
    def _():
        step(xbuf_b, xbuf_a, ybuf_b, ybuf_a, 1, 0)


def _experts(h, route, w_gu, b_gu, w_dn, b_dn, layer, tm):
    n, d = h.shape
    slot_tok, slot_dst, blk_e = route
    nb = blk_e.shape[0]
    E, _, de2 = w_gu.shape[1:]
    de = de2 // 2
    L = w_gu.shape[0]
    idx = lambda f: pl.BlockSpec((1, 1, tm), lambda i, be: (f(i), 0, 0), memory_space=pltpu.SMEM)
    wspec = lambda shape: pl.BlockSpec((None, None) + shape, lambda i, be: (layer, be[i], 0, 0))
    grid_spec = pltpu.PrefetchScalarGridSpec(
        num_scalar_prefetch=1, grid=(nb,),
        in_specs=[idx(lambda i: 0), idx(lambda i: jnp.minimum(i + 1, nb - 1)),
                  idx(lambda i: jnp.maximum(i - 1, 0)), idx(lambda i: i),
                  pl.BlockSpec(memory_space=pl.ANY),
                  wspec((d, de2)), wspec((1, de2)), wspec((de, d)), wspec((1, d))],
        out_specs=pl.BlockSpec(memory_space=pl.ANY),
        scratch_shapes=[pltpu.VMEM((tm, d), F32), pltpu.VMEM((tm, d), F32),
                        pltpu.VMEM((tm, d), F32), pltpu.VMEM((tm, d), F32),
                        pltpu.SemaphoreType.DMA((2,)), pltpu.SemaphoreType.DMA((2,)),
                        pltpu.VMEM((d, de2), BF16), pltpu.VMEM((de, d), BF16)])
    tok3 = slot_tok.reshape(nb, 1, tm)
    dst3 = slot_dst.reshape(nb, 1, tm)
    y = pl.pallas_call(
        functools.partial(_moe_kernel, DE=de, tm=tm, trash_row=n * TOP_K), grid_spec=grid_spec,
        out_shape=jax.ShapeDtypeStruct((n * TOP_K + tm, d), F32),
        compiler_params=_cp(("arbitrary",), VMEM_LIMIT_MOE))(
            blk_e, tok3, tok3, dst3, dst3, h, w_gu, b_gu.reshape(L, E, 1, de2), w_dn, b_dn.reshape(L, E, 1, d))
    return y.reshape(n + tm // TOP_K, TOP_K * d)


def _combine_kernel(tg_ref, x1_ref, g2_ref, scf_ref, shf_ref, nwf_ref, y_ref, o_ref, *, D):
    tg = tg_ref[...]
    moe = tg[:, 0:1] * y_ref[:, 0:D]
    for k in range(1, TOP_K):
        moe = moe + tg[:, k:k + 1] * y_ref[:, k * D:(k + 1) * D]
    x2 = x1_ref[...] + g2_ref[...] * moe
    o_ref[...] = _rms(x2) * nwf_ref[...] * (1.0 + scf_ref[...]) + shf_ref[...]


def _combine_final(y_tok, first_block, tg, x1, grp, norm_final_w):
    n, d = x1.shape
    tb = grp.tm
    nb = n // tb
    rf, bpg = grp.modf3.shape[1], grp.bpg
    fspec = lambda j: pl.BlockSpec((None, rf, d), lambda i: (i // bpg, 0, j))
    return pl.pallas_call(
        functools.partial(_combine_kernel, D=d), grid=(nb,),
        in_specs=[pl.BlockSpec((tb, LANE), lambda i: (i, 0)),
                  pl.BlockSpec((tb, d), lambda i: (i, 0)),
                  grp.spec(5, d), fspec(1), fspec(0),
                  pl.BlockSpec((1, d), lambda i: (0, 0)),
                  pl.BlockSpec((tb, TOP_K * d), lambda i: (first_block + i, 0))],
        out_specs=pl.BlockSpec((tb, d), lambda i: (i, 0)),
        out_shape=jax.ShapeDtypeStruct((n, d), F32),
        compiler_params=_cp(("parallel",)))(
            tg, x1, grp.mod3, grp.modf3, grp.modf3, norm_final_w.reshape(1, d), y_tok)


def _route(top_e, E, tm):
    n = top_e.shape[0]
    tok_oh = jnp.sum((top_e[:, :, None] == jnp.arange(E, dtype=jnp.int32)).astype(jnp.int32), axis=1)
    csum = jnp.cumsum(tok_oh, axis=0)
    counts = csum[-1]
    rank = jnp.take_along_axis(csum - tok_oh, top_e, axis=1)
    padded = (counts + tm - 1) // tm * tm
    pad_end = jnp.cumsum(padded)
    slots = ((pad_end - padded)[top_e] + rank).reshape(-1)
    n_blocks = -(-(n * TOP_K) // tm) + E
    n_slots = n_blocks * tm
    assign = jnp.arange(n * TOP_K, dtype=jnp.int32)
    slot_tok = jnp.zeros((n_slots,), jnp.int32).at[slots].set(assign // TOP_K, unique_indices=True)
    slot_dst = (n * TOP_K + jnp.arange(n_slots, dtype=jnp.int32) % tm).at[slots].set(assign, unique_indices=True)
    blk_start = jnp.arange(n_blocks, dtype=jnp.int32) * tm
    blk_e = jnp.minimum(jnp.sum((pad_end[None, :] <= blk_start[:, None]).astype(jnp.int32), axis=1), E - 1)
    return slot_tok, slot_dst, blk_e.astype(jnp.int32)


def _rope_tables(pos, rope):
    half = rope // 2
    inv = ROPE_THETA ** (-jnp.arange(half, dtype=F32) / half)
    ang = pos.astype(F32)[:, None] * inv[None, :]
    cos, sin = jnp.cos(ang), jnp.sin(ang)
    return jnp.concatenate([cos, cos], axis=-1), jnp.concatenate([sin, sin], axis=-1)


def _rotate_cols(w, rope):
    k, n = w.shape
    w3 = w.reshape(k, n // rope, rope)
    half = rope // 2
    return jnp.concatenate([-w3[..., half:], w3[..., :half]], axis=-1).reshape(k, n)


def kernel(x_prompt, x_sample, c_prompt, c_sample, cache_kv, cache_pe, state_ssm, state_conv, page_table, norm_mix_w, norm_ffn_w, w_ada, b_ada, w_in, conv_w, A_log, dt_bias, gdn_norm_w, w_o_gdn, q_norm_w, kv_norm_w, w_uq, w_uk, w_uv, w_o_mla, w_out, w_router, b_router, w_gu, b_gu, w_dn, b_dn, w_ada_final, b_ada_final, norm_final_w):
    B, T, D = x_prompt.shape
    Bs, Ts, _ = x_sample.shape
    assert Ts == 1
    depth = w_in.shape[0]
    H, DK, DV = state_ssm.shape[2:]
    QK = H * DK
    CONV = state_conv.shape[3]
    assert CONV == 2 * QK + H * DV and H == SUBLANES
    QL = q_norm_w.shape[1]
    HM, KVL, NOPE = w_uk.shape[1:]
    VH = w_uv.shape[3]
    ROPE = cache_pe.shape[3]
    E = w_router.shape[2]
    scale = float(NOPE + ROPE) ** -0.5
    n_p, n_s = B * T, Bs * Ts
    past_len = page_table.shape[1] * cache_kv.shape[2]
    cache_pe_t = jnp.swapaxes(cache_pe, 2, 3)

    c_all = jnp.concatenate([c_prompt, c_sample], axis=0)
    modf = _matmul(c_all, w_ada_final, F32, c_all.shape[0], 1024, b_ada_final)
    tm_p = min(512, T)
    tb = 128
    cos_p, sin_p = _rope_tables(jnp.arange(T), ROPE)
    cos_s, sin_s = _rope_tables(past_len + jnp.arange(Ts), ROPE)

    hp = x_prompt.reshape(n_p, D)
    hs = x_sample.reshape(n_s, D)
    outs = {k: [] for k in ('kv_p', 'pe_p', 'ssm_p', 'conv_p', 'kv_s', 'pe_s', 'ssm_s', 'conv_s')}
    for l in range(depth):
        mod = _matmul(c_all, w_ada[l], F32, c_all.shape[0], 1024, b_ada[l])
        last = l == depth - 1
        grp_p = _Group(mod[:B].reshape(B, 1, 6 * D), modf[:B].reshape(B, 1, 2 * D), tm_p, T // tm_p)
        grp_s = _Group(mod[B:].reshape(1, Bs, 6 * D), modf[B:].reshape(1, Bs, 2 * D), Bs, 1)
        grp_pc = _Group(grp_p.mod3, grp_p.modf3, tb, T // tb)

        offs = [0]
        for s in (CONV, H * DV, H, H, QL, KVL, ROPE, D, D):
            offs.append(offs[-1] + s)
        wi = w_in[l]
        seg = lambda i: wi[:, offs[i]:offs[i + 1]]
        w_qkv = seg(0).astype(BF16)
        w_z = seg(1).astype(BF16)
        w_gab = jnp.concatenate([seg(7), seg(8)], axis=1).astype(BF16)
        n_small = QL + KVL + 2 * ROPE
        ba_col = -(-n_small // LANE)
        w_small = jnp.concatenate(
            [seg(4), seg(5), seg(6), _rotate_cols(seg(6), ROPE), jnp.zeros((D, ba_col * LANE - n_small), F32),
             seg(2), jnp.zeros((D, 8 - H), F32), seg(3), jnp.zeros((D, LANE - 8 - H), F32)], axis=1).astype(BF16)
        cw_t = conv_w[l].T
        gparams = jnp.zeros((2, LANE), F32).at[0, 8:8 + H].set(A_log[l]).at[1, 8:8 + H].set(dt_bias[l])
        wq = w_uq[l].reshape(QL, HM, NOPE + ROPE)
        wq_pe = wq[:, :, NOPE:].reshape(QL, HM * ROPE)
        wq_all = jnp.concatenate([wq[:, :, :NOPE].reshape(QL, HM * NOPE), wq_pe], axis=1).astype(BF16)
        wq_rot = _rotate_cols(wq_pe, ROPE).astype(BF16)
        wuk_all = jnp.transpose(w_uk[l], (1, 0, 2)).reshape(KVL, HM * NOPE).astype(BF16)
        wuv_all = jnp.transpose(w_uv[l], (1, 0, 2)).reshape(KVL, HM * VH).astype(BF16)
        qnw = q_norm_w[l].reshape(1, QL)
        kvnw = kv_norm_w[l].reshape(1, KVL)
        wog = w_o_gdn[l].astype(BF16)
        wom = w_o_mla[l].astype(BF16)
        wout = w_out[l].astype(BF16)
        wr = jnp.pad(w_router[l], ((0, 0), (0, LANE - E)))
        br = jnp.pad(b_router[l], (0, LANE - E)).reshape(1, LANE)
        nfw = norm_ffn_w[l].reshape(1, D)

        h1 = _norm_mod(hp, norm_mix_w[l], grp_p, 1, 0, BF16)
        qkv_p = _matmul(h1, w_qkv, BF16, 1024, 512)
        z_p = _matmul(h1, w_z, BF16, 1024, 512)
        gab_p = _matmul(h1, w_gab, BF16, 1024, 512)
        small_p = _matmul(h1, w_small, F32, 1024, w_small.shape[1])
        og_p, ssm_p = _gdn_prompt(qkv_p, z_p, small_p, ba_col, cw_t, gparams, gdn_norm_w[l], B, T, H, DK, DV)
        q_p, k_p, v_p, lat_p, pe_p = _mla_prep_prompt(small_p, cos_p, sin_p, qnw, kvnw, wq_all, wq_rot, wuk_all,
                                                      wuv_all, T, tm_p, HM, NOPE, ROPE, VH, QL, KVL, scale)
        om_p = _flash_attention(q_p, k_p, v_p, B, T, tm_p)
        x1_p, h2_p, te_p, tg_p = _post_mixer(hp, og_p, om_p, gab_p, grp_p, wog, wom, wout, nfw, wr, br, E)
        outs['kv_p'].append(lat_p.reshape(B, T, KVL))
        outs['pe_p'].append(pe_p.reshape(B, T, ROPE))
        outs['ssm_p'].append(ssm_p)
        outs['conv_p'].append(qkv_p.reshape(B, T, CONV)[:, T - (CONV_WIDTH - 1):, :].astype(F32))

        h1s = _norm_mod(hs, norm_mix_w[l], grp_s, 1, 0, BF16)
        qkv_s = _matmul(h1s, w_qkv, F32, Bs, 512)
        z_s = _matmul(h1s, w_z, BF16, Bs, 512)
        gab_s = _matmul(h1s, w_gab, BF16, Bs, 512)
        small_s = _matmul(h1s, w_small, F32, Bs, w_small.shape[1])
        og_s, ssm_s, conv_s = _gdn_decode(qkv_s, z_s, small_s, ba_col, state_conv[l], state_ssm[l], cw_t, gparams,
                                          gdn_norm_w[l], H, DK, DV)
        qn_s, qp_s, lat_s, pe_s = _mla_prep_sample(small_s, cos_s, sin_s, qnw, kvnw, wq_all, wq_rot,
                                                   HM, NOPE, ROPE, QL, KVL, scale)
        q_lat = jnp.transpose(_q_latent(qn_s, w_uk[l], scale), (1, 0, 2))
        o_lat = _paged_attention(q_lat, qp_s.reshape(n_s, HM, ROPE), lat_s, pe_s, cache_kv, cache_pe_t, page_table, l)
        om_s = _o_value(jnp.transpose(o_lat, (1, 0, 2)), w_uv[l])
        x1_s, h2_s, te_s, tg_s = _post_mixer(hs, og_s, om_s, gab_s, grp_s, wog, wom, wout, nfw, wr, br, E)
        outs['kv_s'].append(lat_s.reshape(Bs, Ts, KVL))
        outs['pe_s'].append(pe_s.reshape(Bs, Ts, ROPE))
        outs['ssm_s'].append(ssm_s)
        outs['conv_s'].append(conv_s)

        tm_e = 256
        h2 = jnp.concatenate([h2_p, h2_s], axis=0)
        top_e = jnp.concatenate([te_p[:, :TOP_K], te_s[:, :TOP_K]], axis=0)
        y_tok = _experts(h2, _route(top_e, E, tm_e), w_gu, b_gu, w_dn, b_dn, l, tm_e)
        if not last:
            raise NotImplementedError("stacked layers need an un-normalised residual output")
        hp = _combine_final(y_tok, 0, tg_p, x1_p, grp_pc, norm_final_w)
        hs = _combine_final(y_tok, n_p // grp_s.tm, tg_s, x1_s, grp_s, norm_final_w)

    st = lambda k: jnp.stack(outs[k])
    return (hp.reshape(B, T, D), hs.reshape(Bs, Ts, D),
            st('kv_p'), st('pe_p'), st('ssm_p'), st('conv_p'),
            st('kv_s'), st('pe_s'), st('ssm_s'), st('conv_s'))
```

```python
import functools

import jax
import jax.numpy as jnp
from jax import lax
from jax.experimental import pallas as pl
from jax.experimental.pallas import tpu as pltpu

F32 = jnp.float32
BF16 = jnp.bfloat16
HI = lax.Precision.HIGHEST

NORM_EPS = 1e-6
ROPE_THETA = 10000.0
TOP_K = 4
SWIGLU_LIMIT = 7.0
SWIGLU_ALPHA = 1.702
CONV_WIDTH = 4
GDN_CHUNK = 64
PAGE_GROUP = 8
PAGE_REQUESTS = 4
LANE = 128
VMEM_LIMIT = 48 * 1024 * 1024
VMEM_LIMIT_MOE = 58 * 1024 * 1024

NT = (((1,), (1,)), ((), ()))
TN = (((0,), (0,)), ((), ()))


def _cp(sem, vmem=VMEM_LIMIT):
    return pltpu.CompilerParams(dimension_semantics=sem, vmem_limit_bytes=vmem)


def _rms(x):
    return x * lax.rsqrt(jnp.mean(x * x, axis=-1, keepdims=True) + NORM_EPS)


def _softplus(x):
    return jnp.maximum(x, 0.0) + jnp.log1p(jnp.exp(-jnp.abs(x)))


def _silu(x):
    return x * jax.nn.sigmoid(x)


def _bdot(a, b):
    return jnp.dot(a.astype(BF16), b.astype(BF16), preferred_element_type=F32)


def _bdot_g(a, b, dims):
    return lax.dot_general(a.astype(BF16), b.astype(BF16), dims, preferred_element_type=F32)


def _mm_kernel(*refs, has_bias):
    if has_bias:
        x_ref, w_ref, b_ref, o_ref = refs
    else:
        x_ref, w_ref, o_ref = refs
    acc = _bdot(x_ref[...], w_ref[...])
    if has_bias:
        acc = acc + b_ref[...]
    o_ref[...] = acc.astype(o_ref.dtype)


def _matmul(x, w, out_dtype, tm, tn, bias=None):
    M, K = x.shape
    N = w.shape[1]
    tm, tn = min(tm, M), min(tn, N)
    in_specs = [pl.BlockSpec((tm, K), lambda i, j: (i, 0)), pl.BlockSpec((K, tn), lambda i, j: (0, j))]
    args = [x, w]
    if bias is not None:
        in_specs.append(pl.BlockSpec((1, tn), lambda i, j: (0, j)))
        args.append(bias.reshape(1, N))
    return pl.pallas_call(
        functools.partial(_mm_kernel, has_bias=bias is not None),
        grid=(M // tm, N // tn), in_specs=in_specs,
        out_specs=pl.BlockSpec((tm, tn), lambda i, j: (i, j)),
        out_shape=jax.ShapeDtypeStruct((M, N), out_dtype),
        compiler_params=_cp(("parallel", "parallel")))(*args)


class _Group:
    def __init__(self, mod3, modf3, tm, blocks_per_g):
        self.mod3, self.modf3, self.tm, self.bpg = mod3, modf3, tm, blocks_per_g

    def spec(self, j, d):
        r, bpg = self.mod3.shape[1], self.bpg
        return pl.BlockSpec((None, r, d), lambda i: (i // bpg, 0, j))


def _norm_mod_kernel(x_ref, w_ref, sc_ref, sh_ref, o_ref):
    y = _rms(x_ref[...]) * w_ref[...]
    o_ref[...] = (y * (1.0 + sc_ref[...]) + sh_ref[...]).astype(o_ref.dtype)


def _norm_mod(x, w, grp, j_scale, j_shift, out_dtype):
    n, d = x.shape
    tm = grp.tm
    return pl.pallas_call(
        _norm_mod_kernel, grid=(n // tm,),
        in_specs=[pl.BlockSpec((tm, d), lambda i: (i, 0)), pl.BlockSpec((1, d), lambda i: (0, 0)),
                  grp.spec(j_scale, d), grp.spec(j_shift, d)],
        out_specs=pl.BlockSpec((tm, d), lambda i: (i, 0)),
        out_shape=jax.ShapeDtypeStruct((n, d), out_dtype),
        compiler_params=_cp(("parallel",)))(x, w.reshape(1, d), grp.mod3, grp.mod3)


SUBLANES = 8
PREV_ROWS = 16


def _split_bf16(x):
    hi = x.astype(BF16)
    return hi, (x - hi.astype(F32)).astype(BF16)


def _dot3(a, b):
    d = lambda x, y: jnp.dot(x, y, preferred_element_type=F32)
    return d(a[0], b[0]) + d(a[0], b[1]) + d(a[1], b[0])


def _expand_matrix(C):
    k = jnp.arange(C)
    n = jnp.arange(SUBLANES * LANE)
    rem = n % LANE
    hit = ((k[:, None] // SUBLANES == rem[None, :] // SUBLANES) & (k[:, None] % SUBLANES == n[None, :] // LANE)
           & (rem[None, :] < C))
    return hit.astype(BF16)


def _unit_lower_inverses(a_list, at_list, g_ref, C):
    nh = len(a_list)
    row = lax.broadcasted_iota(jnp.int32, (C, C), 0)
    col = lax.broadcasted_iota(jnp.int32, (C, C), 1)
    blockdiag = (row // SUBLANES) == (col // SUBLANES)
    packed = []
    for at in at_list:
        m = jnp.where(blockdiag, at, 0.0)
        d = m[0:SUBLANES]
        for b in range(1, C // SUBLANES):
            d = d + m[SUBLANES * b:SUBLANES * (b + 1)]
        packed.append(d)
    stack = _split_bf16(jnp.concatenate(packed, axis=0))
    g = g_ref[...]
    coef = (jnp.dot(stack[0], g, preferred_element_type=F32) + jnp.dot(stack[1], g, preferred_element_type=F32))
    sub = lax.broadcasted_iota(jnp.int32, (SUBLANES, LANE), 0)
    lane = lax.broadcasted_iota(jnp.int32, (SUBLANES, LANE), 1)
    unit = jnp.where((lane % SUBLANES == sub) & (lane < C), 1.0, 0.0)
    xd = [unit] * nh
    for i in range(1, SUBLANES):
        e_i = jnp.where((lane % SUBLANES == i) & (lane < C), 1.0, 0.0)[0:1]
        for h in range(nh):
            c_i = coef[SUBLANES * h:SUBLANES * (h + 1), LANE * i:LANE * (i + 1)]
            new_row = e_i - jnp.sum(c_i * xd[h], axis=0, keepdims=True)
            xd[h] = jnp.where(sub == i, new_row, xd[h])
    x = [jnp.where(blockdiag, jnp.concatenate([xd[h][:, :C]] * (C // SUBLANES), axis=0), 0.0) for h in range(nh)]
    s = SUBLANES
    while s < C:
        below = ((row // s) % 2 == 1) & ((col // s) == (row // s) - 1)
        xs = [_split_bf16(x[h]) for h in range(nh)]
        m1 = [_dot3(xs[h], _split_bf16(jnp.where(below, a_list[h], 0.0))) for h in range(nh)]
        x = [x[h] - _dot3(_split_bf16(m1[h]), xs[h]) for h in range(nh)]
        s *= 2
    return x


def _gdn_prep_kernel(qkv_ref, prev_ref, sm_ref, cw_ref, gp_ref, g_ref,
                     u_ref, w_ref, qe_ref, kd_ref, aqk_ref, dec_ref, xbuf_ref, *, H, DK, DV, C):
    c = pl.program_id(1)
    QK = H * DK
    P = PREV_ROWS

    prev = prev_ref[...].astype(F32)
    xbuf_ref[0:P, :] = jnp.where(c == 0, 0.0, prev)
    xbuf_ref[P:P + C, :] = qkv_ref[...].astype(F32)
    y = None
    for j in range(CONV_WIDTH):
        o = P - (CONV_WIDTH - 1) + j
        term = xbuf_ref[o:o + C, :] * cw_ref[j:j + 1, :]
        y = term if y is None else y + term
    y = _silu(y)

    sm = sm_ref[...]
    beta_all = jax.nn.sigmoid(sm)
    g_all = -jnp.exp(gp_ref[0:1, :]) * _softplus(sm + gp_ref[1:2, :])
    row = lax.broadcasted_iota(jnp.int32, (C, C), 0)
    col = lax.broadcasted_iota(jnp.int32, (C, C), 1)
    tril = (col <= row).astype(F32)
    gam_all = jnp.dot(tril, g_all, precision=HI, preferred_element_type=F32)
    lane = lax.broadcasted_iota(jnp.int32, sm.shape, 1)
    rows_t = jnp.where(lane < 8, beta_all, gam_all).T

    a_list, at_list, rhs_list = [], [], []
    for h in range(H):
        qh = y[:, h * DK:(h + 1) * DK]
        kh = y[:, QK + h * DK:QK + (h + 1) * DK]
        vh = y[:, 2 * QK + h * DV:2 * QK + (h + 1) * DV]
        qh = qh * lax.rsqrt(jnp.sum(qh * qh, axis=-1, keepdims=True) + NORM_EPS) * (DK ** -0.5)
        kh = kh * lax.rsqrt(jnp.sum(kh * kh, axis=-1, keepdims=True) + NORM_EPS)
        beta_c = beta_all[:, h:h + 1]
        gam_c = gam_all[:, 8 + h:9 + h]
        beta_r = rows_t[h:h + 1, :]
        gam_r = rows_t[8 + h:9 + h, :]
        dm = gam_c - gam_r
        decay = jnp.exp(jnp.where(col <= row, dm, -jnp.inf))
        decay_t = jnp.exp(jnp.where(row < col, -dm, -jnp.inf))
        qk_kk = _bdot_g(jnp.concatenate([qh, kh], axis=0), kh, NT)
        kk = qk_kk[C:]
        a_list.append(jnp.where(col < row, kk * beta_c * decay, 0.0))
        at_list.append(kk * beta_r * decay_t)
        egam = jnp.exp(gam_c)
        rhs_list.append(jnp.concatenate([beta_c * vh, beta_c * egam * kh], axis=1).astype(BF16))
        g_last = gam_c[C - 1:C, :]
        qe_ref[:, h * DK:(h + 1) * DK] = (qh * egam).astype(qe_ref.dtype)
        kd_ref[:, h * DK:(h + 1) * DK] = (kh * jnp.exp(g_last - gam_c)).astype(kd_ref.dtype)
        aqk_ref[:, h * C:(h + 1) * C] = (qk_kk[:C] * decay).astype(aqk_ref.dtype)
        dec_ref[h:h + 1, :] = jnp.broadcast_to(jnp.exp(g_last), (1, LANE))

    t_inv = _unit_lower_inverses(a_list, at_list, g_ref, C)
    for h in range(H):
        t_hi, t_lo = _split_bf16(t_inv[h])
        uw = (jnp.dot(t_hi, rhs_list[h], preferred_element_type=F32)
              + jnp.dot(t_lo, rhs_list[h], preferred_element_type=F32))
        u_ref[:, h * DV:(h + 1) * DV] = uw[:, :DV].astype(u_ref.dtype)
        w_ref[:, h * DK:(h + 1) * DK] = uw[:, DV:].astype(w_ref.dtype)


def _gdn_scan_kernel(u_ref, w_ref, qe_ref, kd_ref, aqk_ref, dec_ref, z_ref, nw_ref, og_ref, s_out_ref, s_ref,
                     *, H, DK, DV, C):
    c = pl.program_id(1)

    @pl.when(c == 0)
    def _():
        s_ref[...] = jnp.zeros_like(s_ref)

    ks = [slice(h * DK, (h + 1) * DK) for h in range(H)]
    vs = [slice(h * DV, (h + 1) * DV) for h in range(H)]
    s16 = [s_ref[h].astype(BF16) for h in range(H)]
    ws = [jnp.dot(w_ref[:, ks[h]], s16[h], preferred_element_type=F32) for h in range(H)]
    qs = [jnp.dot(qe_ref[:, ks[h]], s16[h], preferred_element_type=F32) for h in range(H)]
    v16 = [(u_ref[:, vs[h]].astype(F32) - ws[h]).astype(BF16) for h in range(H)]
    ds = [lax.dot_general(kd_ref[:, ks[h]], v16[h], TN, preferred_element_type=F32) for h in range(H)]
    o = [qs[h] + jnp.dot(aqk_ref[:, h * C:(h + 1) * C], v16[h], preferred_element_type=F32) for h in range(H)]
    for h in range(H):
        s_ref[h] = s_ref[h] * dec_ref[h:h + 1, :] + ds[h]
        zh = z_ref[:, vs[h]].astype(F32)
        og_ref[:, vs[h]] = (_rms(o[h]) * nw_ref[...] * _silu(zh)).astype(og_ref.dtype)

    @pl.when(c == pl.num_programs(1) - 1)
    def _():
        s_out_ref[...] = s_ref[...]


def _gdn_prompt(qkv, z, small, ba_col, cw_t, gparams, norm_w, B, T, H, DK, DV):
    C = GDN_CHUNK
    nc = T // C
    n, W = qkv.shape
    kw = dict(H=H, DK=DK, DV=DV, C=C)
    blk = lambda w: pl.BlockSpec((C, w), lambda b, c: (b * nc + c, 0))
    ppc = C // PREV_ROWS
    u, w, qe, kd, aqk, dec = pl.pallas_call(
        functools.partial(_gdn_prep_kernel, **kw), grid=(B, nc),
        in_specs=[blk(W),
                  pl.BlockSpec((PREV_ROWS, W), lambda b, c: (jnp.maximum((b * nc + c) * ppc - 1, 0), 0)),
                  pl.BlockSpec((C, LANE), lambda b, c: (b * nc + c, ba_col)),
                  pl.BlockSpec((CONV_WIDTH, W), lambda b, c: (0, 0)),
                  pl.BlockSpec((2, LANE), lambda b, c: (0, 0)),
                  pl.BlockSpec((C, SUBLANES * LANE), lambda b, c: (0, 0))],
        out_specs=[blk(H * DV), blk(H * DK), blk(H * DK), blk(H * DK), blk(H * C),
                   pl.BlockSpec((None, SUBLANES, LANE), lambda b, c: (b * nc + c, 0, 0))],
        out_shape=[jax.ShapeDtypeStruct((n, H * DV), BF16), jax.ShapeDtypeStruct((n, H * DK), BF16),
                   jax.ShapeDtypeStruct((n, H * DK), BF16), jax.ShapeDtypeStruct((n, H * DK), BF16),
                   jax.ShapeDtypeStruct((n, H * C), BF16), jax.ShapeDtypeStruct((B * nc, SUBLANES, LANE), F32)],
        scratch_shapes=[pltpu.VMEM((C + PREV_ROWS, W), F32)],
        compiler_params=_cp(("parallel", "parallel")))(qkv, qkv, small, cw_t, gparams, _expand_matrix(C))
    return pl.pallas_call(
        functools.partial(_gdn_scan_kernel, **kw), grid=(B, nc),
        in_specs=[blk(H * DV), blk(H * DK), blk(H * DK), blk(H * DK), blk(H * C),
                  pl.BlockSpec((None, SUBLANES, LANE), lambda b, c: (b * nc + c, 0, 0)),
                  blk(H * DV), pl.BlockSpec((1, DV), lambda b, c: (0, 0))],
        out_specs=[blk(H * DV), pl.BlockSpec((None, H, DK, DV), lambda b, c: (b, 0, 0, 0))],
        out_shape=[jax.ShapeDtypeStruct((n, H * DV), BF16), jax.ShapeDtypeStruct((B, H, DK, DV), F32)],
        scratch_shapes=[pltpu.VMEM((H, DK, DV), F32)],
        compiler_params=_cp(("parallel", "arbitrary")))(u, w, qe, kd, aqk, dec, z, norm_w.reshape(1, DV))


def _gdn_decode_kernel(u_ref, z_ref, ba_ref, buf_ref, s_in_ref, cw_ref, gp_ref, nw_ref,
                       og_ref, s_out_ref, buf_out_ref, *, H, DK, DV):
    QK = H * DK
    u = u_ref[...]
    buf = buf_ref[...]
    y = buf[0:1] * cw_ref[0:1, :]
    y = y + buf[1:2] * cw_ref[1:2, :]
    y = y + buf[2:3] * cw_ref[2:3, :]
    y = y + u * cw_ref[3:4, :]
    buf_out_ref[0:2, :] = buf[1:3]
    buf_out_ref[2:3, :] = u
    y = _silu(y)
    sm = ba_ref[...]
    beta_all = jax.nn.sigmoid(sm)
    g_all = -jnp.exp(gp_ref[0:1, :]) * _softplus(sm + gp_ref[1:2, :])
    rows = []
    for h in range(H):
        kh = y[:, QK + h * DK:QK + (h + 1) * DK]
        rows.append(kh * lax.rsqrt(jnp.sum(kh * kh, axis=-1, keepdims=True) + NORM_EPS))
    for h in range(H):
        qh = y[:, h * DK:(h + 1) * DK]
        rows.append(qh * lax.rsqrt(jnp.sum(qh * qh, axis=-1, keepdims=True) + NORM_EPS) * (DK ** -0.5))
    rows.append(jnp.zeros((DK - 2 * H, DK), F32))
    cols = jnp.concatenate(rows, axis=0).T
    for h in range(H):
        kcol = cols[:, h:h + 1]
        qcol = cols[:, H + h:H + h + 1]
        vh = y[:, 2 * QK + h * DV:2 * QK + (h + 1) * DV]
        s_dec = s_in_ref[h] * jnp.exp(g_all[:, 8 + h:9 + h])
        v_new = beta_all[:, h:h + 1] * (vh - jnp.sum(kcol * s_dec, axis=0, keepdims=True))
        s_new = s_dec + kcol * v_new
        s_out_ref[h] = s_new
        o = jnp.sum(qcol * s_new, axis=0, keepdims=True)
        zh = z_ref[:, h * DV:(h + 1) * DV].astype(F32)
        og_ref[:, h * DV:(h + 1) * DV] = (_rms(o) * nw_ref[...] * _silu(zh)).astype(og_ref.dtype)


def _gdn_decode(qkv, z, small, ba_col, conv_buf, ssm, cw_t, gparams, norm_w, H, DK, DV):
    Bs, W = qkv.shape
    kern = functools.partial(_gdn_decode_kernel, H=H, DK=DK, DV=DV)
    og, s_new, buf_new = pl.pallas_call(
        kern, grid=(Bs,),
        in_specs=[pl.BlockSpec((None, 1, W), lambda b: (b, 0, 0)),
                  pl.BlockSpec((None, 1, H * DV), lambda b: (b, 0, 0)),
                  pl.BlockSpec((None, 1, LANE), lambda b: (b, 0, ba_col)),
                  pl.BlockSpec((None, CONV_WIDTH - 1, W), lambda b: (b, 0, 0)),
                  pl.BlockSpec((None, H, DK, DV), lambda b: (b, 0, 0, 0)),
                  pl.BlockSpec((CONV_WIDTH, W), lambda b: (0, 0)),
                  pl.BlockSpec((2, LANE), lambda b: (0, 0)),
                  pl.BlockSpec((1, DV), lambda b: (0, 0))],
        out_specs=[pl.BlockSpec((None, 1, H * DV), lambda b: (b, 0, 0)),
                   pl.BlockSpec((None, H, DK, DV), lambda b: (b, 0, 0, 0)),
                   pl.BlockSpec((None, CONV_WIDTH - 1, W), lambda b: (b, 0, 0))],
        out_shape=[jax.ShapeDtypeStruct((Bs, 1, H * DV), BF16),
                   jax.ShapeDtypeStruct(ssm.shape, F32),
                   jax.ShapeDtypeStruct(conv_buf.shape, F32)],
        compiler_params=_cp(("parallel",)))(
            qkv.reshape(Bs, 1, W), z.reshape(Bs, 1, H * DV), small.reshape(Bs, 1, small.shape[1]),
            conv_buf, ssm, cw_t, gparams, norm_w.reshape(1, DV))
    return og.reshape(Bs, H * DV), s_new, buf_new


def _mla_prep_prompt_kernel(sm_ref, cos_ref, sin_ref, cost_ref, sint_ref, qnw_ref, kvnw_ref,
                            wqt_ref, wqrt_ref, wuk_ref, wuvt_ref,
                            qt_ref, k_ref, vt_ref, lat_ref, pe_ref, *, H, NOPE, ROPE, VH, QL, KVL, scale):
    sm = sm_ref[...]
    qn = (_rms(sm[:, :QL]) * qnw_ref[...]).astype(BF16)
    qft = lax.dot_general(wqt_ref[...], qn, NT, preferred_element_type=F32)
    qrt = lax.dot_general(wqrt_ref[...], qn, NT, preferred_element_type=F32)
    qpt = qft[H * NOPE:, :] * cost_ref[...] + qrt * sint_ref[...]
    lat = _rms(sm[:, QL:QL + KVL]) * kvnw_ref[...]
    lat_ref[...] = lat
    lat16 = lat.astype(BF16)
    kn = jnp.dot(lat16, wuk_ref[...], preferred_element_type=F32)
    vt = lax.dot_general(wuvt_ref[...], lat16, NT, preferred_element_type=F32)
    o = QL + KVL
    kr = sm[:, o:o + ROPE] * cos_ref[...] + sm[:, o + ROPE:o + 2 * ROPE] * sin_ref[...]
    pe_ref[...] = kr
    for h in range(H):
        qt_ref[h] = (jnp.concatenate([qft[h * NOPE:(h + 1) * NOPE, :], qpt[h * ROPE:(h + 1) * ROPE, :]], axis=0)
                     * scale).astype(qt_ref.dtype)
        k_ref[h] = jnp.concatenate([kn[:, h * NOPE:(h + 1) * NOPE], kr], axis=-1).astype(k_ref.dtype)
        vt_ref[h, 0] = vt[h * VH:(h + 1) * VH, :].astype(vt_ref.dtype)


def _mla_prep_prompt(small, cos, sin, qnw, kvnw, wq, wqr, wuk, wuv, T, tm, H, NOPE, ROPE, VH, QL, KVL, scale):
    n, ws = small.shape
    nt = T // tm
    kern = functools.partial(_mla_prep_prompt_kernel, H=H, NOPE=NOPE, ROPE=ROPE, VH=VH, QL=QL, KVL=KVL, scale=scale)
    full = lambda a: pl.BlockSpec(a.shape, lambda i: (0,) * a.ndim)
    dqk = NOPE + ROPE
    cos_t = jnp.tile(cos.T, (H, 1))
    sin_t = jnp.tile(sin.T, (H, 1))
    wqt, wqrt, wuvt = wq.T, wqr.T, wuv.T
    return pl.pallas_call(
        kern, grid=(n // tm,),
        in_specs=[pl.BlockSpec((tm, ws), lambda i: (i, 0)),
                  pl.BlockSpec((tm, ROPE), lambda i: (i % nt, 0)),
                  pl.BlockSpec((tm, ROPE), lambda i: (i % nt, 0)),
                  pl.BlockSpec((H * ROPE, tm), lambda i: (0, i % nt)),
                  pl.BlockSpec((H * ROPE, tm), lambda i: (0, i % nt)),
                  full(qnw), full(kvnw), full(wqt), full(wqrt), full(wuk), full(wuvt)],
        out_specs=[pl.BlockSpec((H, dqk, tm), lambda i: (0, 0, i)),
                   pl.BlockSpec((H, tm, dqk), lambda i: (0, i, 0)),
                   pl.BlockSpec((H, 1, VH, tm), lambda i: (0, i, 0, 0)),
                   pl.BlockSpec((tm, KVL), lambda i: (i, 0)),
                   pl.BlockSpec((tm, ROPE), lambda i: (i, 0))],
        out_shape=[jax.ShapeDtypeStruct((H, dqk, n), BF16), jax.ShapeDtypeStruct((H, n, dqk), BF16),
                   jax.ShapeDtypeStruct((H, n // tm, VH, tm), BF16), jax.ShapeDtypeStruct((n, KVL), F32),
                   jax.ShapeDtypeStruct((n, ROPE), F32)],
        compiler_params=_cp(("parallel",)))(small, cos, sin, cos_t, sin_t, qnw, kvnw, wqt, wqrt, wuk, wuvt)


def _mla_prep_sample_kernel(sm_ref, cos_ref, sin_ref, qnw_ref, kvnw_ref, wq_ref, wqr_ref,
                            qn_ref, qp_ref, lat_ref, pe_ref, *, H, NOPE, ROPE, QL, KVL, scale):
    sm = sm_ref[...]
    cos = cos_ref[...]
    sin = sin_ref[...]
    qn = _rms(sm[:, :QL]) * qnw_ref[...]
    qf = _bdot(qn, wq_ref[...])
    qr = _bdot(qn, wqr_ref[...])
    qn_ref[...] = qf[:, :H * NOPE].astype(qn_ref.dtype)
    lat_ref[...] = _rms(sm[:, QL:QL + KVL]) * kvnw_ref[...]
    o = QL + KVL
    pe_ref[...] = sm[:, o:o + ROPE] * cos + sm[:, o + ROPE:o + 2 * ROPE] * sin
    for h in range(H):
        p0 = H * NOPE + h * ROPE
        qp = qf[:, p0:p0 + ROPE] * cos + qr[:, h * ROPE:(h + 1) * ROPE] * sin
        qp_ref[:, h * ROPE:(h + 1) * ROPE] = (qp * scale).astype(qp_ref.dtype)


def _mla_prep_sample(small, cos, sin, qnw, kvnw, wq, wqr, H, NOPE, ROPE, QL, KVL, scale):
    n, ws = small.shape
    kern = functools.partial(_mla_prep_sample_kernel, H=H, NOPE=NOPE, ROPE=ROPE, QL=QL, KVL=KVL, scale=scale)
    full = lambda a: pl.BlockSpec(a.shape, lambda i: (0,) * a.ndim)
    return pl.pallas_call(
        kern, grid=(1,),
        in_specs=[full(small), full(cos), full(sin), full(qnw), full(kvnw), full(wq), full(wqr)],
        out_specs=[pl.BlockSpec((n, H * NOPE), lambda i: (0, 0)), pl.BlockSpec((n, H * ROPE), lambda i: (0, 0)),
                   pl.BlockSpec((n, KVL), lambda i: (0, 0)), pl.BlockSpec((n, ROPE), lambda i: (0, 0))],
        out_shape=[jax.ShapeDtypeStruct((n, H * NOPE), BF16), jax.ShapeDtypeStruct((n, H * ROPE), BF16),
                   jax.ShapeDtypeStruct((n, KVL), F32), jax.ShapeDtypeStruct((n, ROPE), F32)],
        compiler_params=_cp(("arbitrary",)))(small, cos, sin, qnw, kvnw, wq, wqr)


def _head_proj_kernel(x_ref, w_ref, o_ref, *, dims, scale):
    o_ref[...] = (_bdot_g(x_ref[...], w_ref[...], dims) * scale).astype(o_ref.dtype)


def _q_latent(q_nope, w_uk, scale):
    H, KVL, NOPE = w_uk.shape
    n = q_nope.shape[0]
    return pl.pallas_call(
        functools.partial(_head_proj_kernel, dims=NT, scale=scale), grid=(H,),
        in_specs=[pl.BlockSpec((n, NOPE), lambda h: (0, h)), pl.BlockSpec((None, KVL, NOPE), lambda h: (h, 0, 0))],
        out_specs=pl.BlockSpec((None, n, KVL), lambda h: (h, 0, 0)),
        out_shape=jax.ShapeDtypeStruct((H, n, KVL), BF16),
        compiler_params=_cp(("parallel",)))(q_nope, w_uk)


def _o_value(o_lat, w_uv):
    H, KVL, VH = w_uv.shape
    n = o_lat.shape[1]
    return pl.pallas_call(
        functools.partial(_head_proj_kernel, dims=(((1,), (0,)), ((), ())), scale=1.0), grid=(H,),
        in_specs=[pl.BlockSpec((None, n, KVL), lambda h: (h, 0, 0)), pl.BlockSpec((None, KVL, VH), lambda h: (h, 0, 0))],
        out_specs=pl.BlockSpec((n, VH), lambda h: (0, h)),
        out_shape=jax.ShapeDtypeStruct((n, H * VH), BF16),
        compiler_params=_cp(("parallel",)))(o_lat, w_uv)


FLASH_SPLIT = 2


def _flash_kernel(qt_ref, k_ref, vt_ref, o_ref, m_ref, l_ref, acc_ref, *, tq):
    qi = pl.program_id(2)
    ns = FLASH_SPLIT
    hq = tq // ns
    m_ref[...] = jnp.full_like(m_ref, -jnp.inf)
    l_ref[...] = jnp.zeros_like(l_ref)
    acc_ref[...] = jnp.zeros_like(acc_ref)
    qt = [qt_ref[:, t * hq:(t + 1) * hq] for t in range(ns)]

    def update(st, vt):
        m_prev = [m_ref[t] for t in range(ns)]
        m_new = [jnp.maximum(m_prev[t], jnp.max(st[t], axis=0, keepdims=True)) for t in range(ns)]
        p = [jnp.exp(st[t] - m_new[t]) for t in range(ns)]
        corr = [jnp.exp(m_prev[t] - m_new[t]) for t in range(ns)]
        pv = [jnp.dot(vt, p[t].astype(BF16), preferred_element_type=F32) for t in range(ns)]
        for t in range(ns):
            m_ref[t] = m_new[t]
            l_ref[t] = corr[t] * l_ref[t] + jnp.sum(p[t], axis=0, keepdims=True)
            acc_ref[t] = acc_ref[t] * corr[t] + pv[t]

    def below_diagonal(j, carry):
        k = k_ref[pl.ds(pl.multiple_of(j * tq, tq), tq), :]
        update([jnp.dot(k, qt[t], preferred_element_type=F32) for t in range(ns)], vt_ref[j])
        return carry

    lax.fori_loop(0, qi, below_diagonal, 0)

    k = k_ref[pl.ds(pl.multiple_of(qi * tq, tq), tq), :]
    st = []
    for t in range(ns):
        s = jnp.dot(k, qt[t], preferred_element_type=F32)
        key = lax.broadcasted_iota(jnp.int32, s.shape, 0)
        qry = t * hq + lax.broadcasted_iota(jnp.int32, s.shape, 1)
        st.append(jnp.where(key <= qry, s, -jnp.inf))
    update(st, vt_ref[qi])
    for t in range(ns):
        o_ref[t * hq:(t + 1) * hq, :] = (acc_ref[t] / l_ref[t]).T.astype(o_ref.dtype)


def _flash_attention(qt, k, vt, B, T, tq):
    H, dqk, n = qt.shape
    VH = vt.shape[2]
    nq = T // tq
    hq = tq // FLASH_SPLIT
    kern = functools.partial(_flash_kernel, tq=tq)
    return pl.pallas_call(
        kern, grid=(B, H, nq),
        in_specs=[pl.BlockSpec((None, dqk, tq), lambda b, h, i: (h, 0, b * nq + i)),
                  pl.BlockSpec((None, T, dqk), lambda b, h, i: (h, b, 0)),
                  pl.BlockSpec((None, nq, VH, tq), lambda b, h, i: (h, b, 0, 0))],
        out_specs=pl.BlockSpec((tq, VH), lambda b, h, i: (b * nq + i, h)),
        out_shape=jax.ShapeDtypeStruct((n, H * VH), BF16),
        scratch_shapes=[pltpu.VMEM((FLASH_SPLIT, 1, hq), F32), pltpu.VMEM((FLASH_SPLIT, 1, hq), F32),
                        pltpu.VMEM((FLASH_SPLIT, VH, hq), F32)],
        compiler_params=_cp(("parallel", "parallel", "arbitrary")))(qt, k, vt)


def _paged_kernel(pt_ref, ql_ref, qp_ref, latn_ref, pen_ref, kv_hbm, pe_hbm, o_ref,
                  kv_buf, pe_buf, sem, m_ref, l_ref, acc_ref, *, R, G, layer):
    b = pl.program_id(0)
    g = pl.program_id(1)
    ng = pl.num_programs(1)
    step = b * ng + g
    slot = step % 2

    def page_copies(bb, gg, sl, real_pages):
        out = []
        for r in range(R):
            for i in range(G):
                page = pt_ref[bb * R + r, gg * G + i] if real_pages else 0
                out.append(pltpu.make_async_copy(kv_hbm.at[layer, page], kv_buf.at[sl, r * G + i], sem.at[0, sl]))
                out.append(pltpu.make_async_copy(pe_hbm.at[layer, page], pe_buf.at[sl, r * G + i], sem.at[1, sl]))
        return out

    last = pl.num_programs(0) * ng - 1

    @pl.when(step == 0)
    def _():
        for cp in page_copies(0, 0, 0, True):
            cp.start()

    for cp in page_copies(b, g, slot, False):
        cp.wait()

    nxt = jnp.minimum(step + 1, last)
    for cp in page_copies(nxt // ng, nxt % ng, 1 - slot, True):
        cp.start()

    @pl.when(g == 0)
    def _():
        m_ref[...] = jnp.full_like(m_ref, -jnp.inf)
        l_ref[...] = jnp.zeros_like(l_ref)
        acc_ref[...] = jnp.zeros_like(acc_ref)

    kvs = [[kv_buf[slot, r * G + i].astype(BF16) for i in range(G)] for r in range(R)]
    P = kvs[0][0].shape[0]
    s, p, corr, m_new = [], [], [], []
    for r in range(R):
        ql = ql_ref[r]
        qp = qp_ref[r]
        s.append(jnp.concatenate(
            [lax.dot_general(ql, kvs[r][i], NT, preferred_element_type=F32)
             + jnp.dot(qp, pe_buf[slot, r * G + i].astype(BF16), preferred_element_type=F32) for i in range(G)],
            axis=-1))
    for r in range(R):
        m_prev = m_ref[r]
        m_new.append(jnp.maximum(m_prev, jnp.max(s[r], axis=-1, keepdims=True)))
        corr.append(jnp.exp(m_prev - m_new[r]))
        p.append(jnp.exp(s[r] - m_new[r]))
    pv = []
    for r in range(R):
        t = None
        for i in range(G):
            d = jnp.dot(p[r][:, i * P:(i + 1) * P].astype(BF16), kvs[r][i], preferred_element_type=F32)
            t = d if t is None else t + d
        pv.append(t)
    for r in range(R):
        m_ref[r] = m_new[r]
        l_ref[r] = corr[r] * l_ref[r] + jnp.sum(p[r], axis=-1, keepdims=True)
        acc_ref[r] = acc_ref[r] * corr[r] + pv[r]

    @pl.when(g == pl.num_programs(1) - 1)
    def _():
        for r in range(R):
            latn = latn_ref[r]
            s_n = (jnp.sum(ql_ref[r].astype(F32) * latn, axis=-1, keepdims=True)
                   + jnp.sum(qp_ref[r].astype(F32) * pen_ref[r], axis=-1, keepdims=True))
            m_old = m_ref[r]
            m2 = jnp.maximum(m_old, s_n)
            c2 = jnp.exp(m_old - m2)
            p2 = jnp.exp(s_n - m2)
            o_ref[r] = ((acc_ref[r] * c2 + p2 * latn) / (l_ref[r] * c2 + p2)).astype(o_ref.dtype)

    @pl.when(step == last)
    def _():
        for cp in page_copies(b, g, 1 - slot, False):
            cp.wait()


def _paged_attention(q_lat, q_pe, lat_new, pe_new, cache_kv, cache_pe_t, page_table, layer):
    Bs, H, KVL = q_lat.shape
    ROPE = q_pe.shape[2]
    n_pages = page_table.shape[1]
    P = cache_kv.shape[2]
    G = min(PAGE_GROUP, n_pages)
    R = min(PAGE_REQUESTS, Bs)

    req = lambda rows, w: pl.BlockSpec((R, rows, w), lambda b, g, pt: (b, 0, 0))
    hbm = pl.BlockSpec(memory_space=pl.ANY)
    grid_spec = pltpu.PrefetchScalarGridSpec(
        num_scalar_prefetch=1, grid=(Bs // R, n_pages // G),
        in_specs=[req(H, KVL), req(H, ROPE), req(1, KVL), req(1, ROPE), hbm, hbm],
        out_specs=req(H, KVL),
        scratch_shapes=[pltpu.VMEM((2, R * G, P, KVL), cache_kv.dtype), pltpu.VMEM((2, R * G, ROPE, P), cache_pe_t.dtype),
                        pltpu.SemaphoreType.DMA((2, 2)),
                        pltpu.VMEM((R, H, 1), F32), pltpu.VMEM((R, H, 1), F32), pltpu.VMEM((R, H, KVL), F32)])
    return pl.pallas_call(
        functools.partial(_paged_kernel, R=R, G=G, layer=layer), grid_spec=grid_spec,
        out_shape=jax.ShapeDtypeStruct((Bs, H, KVL), BF16),
        compiler_params=_cp(("arbitrary", "arbitrary")))(
            page_table, q_lat, q_pe, lat_new.reshape(Bs, 1, KVL), pe_new.reshape(Bs, 1, ROPE), cache_kv, cache_pe_t)


def _post_mixer_kernel(x_ref, og_ref, om_ref, gab_ref, g1_ref, sc2_ref, sh2_ref, wog_ref, wom_ref, wout_ref,
                       nw_ref, wr_ref, br_ref, x1_ref, h2_ref, te_ref, tg_ref, *, D, E):
    gab = gab_ref[...].astype(F32)
    merged = (jax.nn.sigmoid(gab[:, :D]) * _bdot(og_ref[...], wog_ref[...])
              + jax.nn.sigmoid(gab[:, D:]) * _bdot(om_ref[...], wom_ref[...]))
    x1 = x_ref[...] + g1_ref[...] * _bdot(merged, wout_ref[...])
    x1_ref[...] = x1
    h2 = _rms(x1) * nw_ref[...] * (1.0 + sc2_ref[...]) + sh2_ref[...]
    h2_ref[...] = h2
    logits = jnp.dot(h2, wr_ref[...], precision=HI, preferred_element_type=F32) + br_ref[...]
    lane = lax.broadcasted_iota(jnp.int32, logits.shape, 1)
    logits = jnp.where(lane < E, logits, -jnp.inf)
    te = jnp.zeros(logits.shape, jnp.int32)
    ex = jnp.zeros(logits.shape, F32)
    top = None
    for k in range(TOP_K):
        m = jnp.max(logits, axis=-1, keepdims=True)
        idx = jnp.min(jnp.where(logits == m, lane, LANE), axis=-1, keepdims=True)
        top = m if top is None else top
        te = jnp.where(lane == k, idx, te)
        ex = jnp.where(lane == k, jnp.exp(m - top), ex)
        logits = jnp.where(lane == idx, -jnp.inf, logits)
    te_ref[...] = te
    tg_ref[...] = ex / jnp.sum(ex, axis=-1, keepdims=True)


def _post_mixer(x, og, om, gab, grp, wog, wom, wout, norm_w, wr, br, E):
    n, d = x.shape
    tm = grp.tm
    rowblk = lambda w: pl.BlockSpec((tm, w), lambda i: (i, 0))
    full = lambda a: pl.BlockSpec(a.shape, lambda i: (0,) * a.ndim)
    return pl.pallas_call(
        functools.partial(_post_mixer_kernel, D=d, E=E), grid=(n // tm,),
        in_specs=[rowblk(d), rowblk(d), rowblk(d), rowblk(2 * d), grp.spec(2, d), grp.spec(4, d), grp.spec(3, d),
                  full(wog), full(wom), full(wout), full(norm_w), full(wr), full(br)],
        out_specs=[rowblk(d), rowblk(d), rowblk(LANE), rowblk(LANE)],
        out_shape=[jax.ShapeDtypeStruct((n, d), F32), jax.ShapeDtypeStruct((n, d), F32),
                   jax.ShapeDtypeStruct((n, LANE), jnp.int32), jax.ShapeDtypeStruct((n, LANE), F32)],
        compiler_params=_cp(("parallel",)))(x, og, om, gab, grp.mod3, grp.mod3, grp.mod3,
                                            wog, wom, wout, norm_w, wr, br)


def _moe_kernel(be_ref, dst0_ref, dstn_ref, dstp_ref, dstc_ref, h_hbm, wgu_ref, bgu_ref, wdn_ref, bdn_ref, y_hbm,
                xbuf_a, xbuf_b, ybuf_a, ybuf_b, gsem, ssem, wgu_s, wdn_s, *, DE, D, tm, trash_row):
    i = pl.program_id(0)
    nb = pl.num_programs(0)

    k_bits = TOP_K.bit_length() - 1

    def gather_starts(dst_ref, xbuf, s):
        for r in range(tm):
            tok = lax.shift_right_logical(dst_ref[0, 0, r], k_bits)
            pltpu.make_async_copy(h_hbm.at[pl.ds(tok, 1)], xbuf.at[pl.ds(r, 1)], gsem.at[s]).start(priority=r % 2)

    def scatter_starts(dst_of_row, ybuf, s):
        for r in range(tm):
            dst = dst_of_row(r)
            col = pl.multiple_of((dst & (TOP_K - 1)) * D, D)
            pltpu.make_async_copy(ybuf.at[pl.ds(r, 1)],
                                  y_hbm.at[pl.ds(lax.shift_right_logical(dst, k_bits), 1), pl.ds(col, D)],
                                  ssem.at[s]).start(priority=r % 2)

    def gather_wait(xbuf, s):
        pltpu.make_async_copy(h_hbm.at[pl.ds(0, tm)], xbuf, gsem.at[s]).wait()

    def scatter_wait(ybuf, s):
        pltpu.make_async_copy(ybuf, y_hbm.at[pl.ds(0, tm), pl.ds(0, D)], ssem.at[s]).wait()

    first = jnp.logical_or(i == 0, be_ref[i] != be_ref[jnp.maximum(i - 1, 0)])

    @pl.when(first)
    def _():
        wgu_s[...] = wgu_ref[...].astype(BF16)
        wdn_s[...] = wdn_ref[...].astype(BF16)

    @pl.when(i == 0)
    def _():
        ybuf_b[...] = jnp.zeros_like(ybuf_b)
        gather_starts(dst0_ref, xbuf_a, 0)

    def step(x_cur, x_nxt, y_cur, y_prv, cur, nxt):
        @pl.when(i >= 1)
        def _():
            scatter_wait(y_cur, cur)

        gather_wait(x_cur, cur)
        gu = jnp.dot(x_cur[...].astype(BF16), wgu_s[...], preferred_element_type=F32) + bgu_ref[...]
        gt = jnp.minimum(gu[:, :DE], SWIGLU_LIMIT)
        up = jnp.clip(gu[:, DE:], -SWIGLU_LIMIT, SWIGLU_LIMIT)
        act = (up + 1.0) * gt * jax.nn.sigmoid(SWIGLU_ALPHA * gt)
        y_cur[...] = jnp.dot(act.astype(BF16), wdn_s[...], preferred_element_type=F32) + bdn_ref[...]
        gather_starts(dstn_ref, x_nxt, nxt)
        scatter_starts(lambda r: jnp.where(i == 0, trash_row + r, dstp_ref[0, 0, r]), y_prv, nxt)

        @pl.when(i == nb - 1)
        def _():
            scatter_wait(y_prv, nxt)
            scatter_starts(lambda r: dstc_ref[0, 0, r], y_cur, cur)
            scatter_wait(y_cur, cur)
            gather_wait(x_nxt, nxt)

    @pl.when(i % 2 == 0)
    def _():
        step(xbuf_a, xbuf_b, ybuf_a, ybuf_b, 0, 1)

    @pl.when(i % 2 == 1)
    def _():
        step(xbuf_b, xbuf_a, ybuf_b, ybuf_a, 1, 0)


def _experts(h, route, w_gu, b_gu, w_dn, b_dn, layer, tm):
    d = h.shape[1]
    n = h.shape[0] - tm // TOP_K
    slot_dst, blk_e = route
    nb = blk_e.shape[0]
    E, _, de2 = w_gu.shape[1:]
    de = de2 // 2
    L = w_gu.shape[0]
    idx = lambda f: pl.BlockSpec((1, 1, tm), lambda i, be: (f(i), 0, 0), memory_space=pltpu.SMEM)
    wspec = lambda shape: pl.BlockSpec((None, None) + shape, lambda i, be: (layer, be[i], 0, 0))
    grid_spec = pltpu.PrefetchScalarGridSpec(
        num_scalar_prefetch=1, grid=(nb,),
        in_specs=[idx(lambda i: 0), idx(lambda i: jnp.minimum(i + 1, nb - 1)),
                  idx(lambda i: jnp.maximum(i - 1, 0)), idx(lambda i: i),
                  pl.BlockSpec(memory_space=pl.ANY),
                  wspec((d, de2)), wspec((1, de2)), wspec((de, d)), wspec((1, d))],
        out_specs=pl.BlockSpec(memory_space=pl.ANY),
        scratch_shapes=[pltpu.VMEM((tm, d), F32), pltpu.VMEM((tm, d), F32),
                        pltpu.VMEM((tm, d), F32), pltpu.VMEM((tm, d), F32),
                        pltpu.SemaphoreType.DMA((2,)), pltpu.SemaphoreType.DMA((2,)),
                        pltpu.VMEM((d, de2), BF16), pltpu.VMEM((de, d), BF16)])
    dst3 = slot_dst.reshape(nb, 1, tm)
    return pl.pallas_call(
        functools.partial(_moe_kernel, DE=de, D=d, tm=tm, trash_row=n * TOP_K), grid_spec=grid_spec,
        out_shape=jax.ShapeDtypeStruct((n + tm // TOP_K, TOP_K * d), F32),
        compiler_params=_cp(("arbitrary",), VMEM_LIMIT_MOE))(
            blk_e, dst3, dst3, dst3, dst3, h, w_gu, b_gu.reshape(L, E, 1, de2), w_dn, b_dn.reshape(L, E, 1, d))


def _combine_kernel(tg_ref, x1_ref, g2_ref, scf_ref, shf_ref, nwf_ref, y_ref, o_ref, *, D):
    tg = tg_ref[...]
    moe = tg[:, 0:1] * y_ref[:, 0:D]
    for k in range(1, TOP_K):
        moe = moe + tg[:, k:k + 1] * y_ref[:, k * D:(k + 1) * D]
    x2 = x1_ref[...] + g2_ref[...] * moe
    o_ref[...] = _rms(x2) * nwf_ref[...] * (1.0 + scf_ref[...]) + shf_ref[...]


def _combine_final(y_tok, first_block, tg, x1, grp, norm_final_w):
    n, d = x1.shape
    tb = grp.tm
    nb = n // tb
    rf, bpg = grp.modf3.shape[1], grp.bpg
    fspec = lambda j: pl.BlockSpec((None, rf, d), lambda i: (i // bpg, 0, j))
    return pl.pallas_call(
        functools.partial(_combine_kernel, D=d), grid=(nb,),
        in_specs=[pl.BlockSpec((tb, LANE), lambda i: (i, 0)),
                  pl.BlockSpec((tb, d), lambda i: (i, 0)),
                  grp.spec(5, d), fspec(1), fspec(0),
                  pl.BlockSpec((1, d), lambda i: (0, 0)),
                  pl.BlockSpec((tb, TOP_K * d), lambda i: (first_block + i, 0))],
        out_specs=pl.BlockSpec((tb, d), lambda i: (i, 0)),
        out_shape=jax.ShapeDtypeStruct((n, d), F32),
        compiler_params=_cp(("parallel",)))(
            tg, x1, grp.mod3, grp.modf3, grp.modf3, norm_final_w.reshape(1, d), y_tok)


def _route(top_e, E, tm):
    n = top_e.shape[0]
    tok_oh = jnp.sum((top_e[:, :, None] == jnp.arange(E, dtype=jnp.int32)).astype(jnp.int32), axis=1)
    csum = jnp.cumsum(tok_oh, axis=0)
    counts = csum[-1]
    rank = jnp.take_along_axis(csum - tok_oh, top_e, axis=1)
    padded = (counts + tm - 1) // tm * tm
    pad_end = jnp.cumsum(padded)
    slots = ((pad_end - padded)[top_e] + rank).reshape(-1)
    n_blocks = -(-(n * TOP_K) // tm) + E
    n_slots = n_blocks * tm
    assign = jnp.arange(n * TOP_K, dtype=jnp.int32)
    slot_dst = (n * TOP_K + jnp.arange(n_slots, dtype=jnp.int32) % tm).at[slots].set(assign, unique_indices=True)
    blk_start = jnp.arange(n_blocks, dtype=jnp.int32) * tm
    blk_e = jnp.minimum(jnp.sum((pad_end[None, :] <= blk_start[:, None]).astype(jnp.int32), axis=1), E - 1)
    return slot_dst, blk_e.astype(jnp.int32)


def _rope_tables(pos, rope):
    half = rope // 2
    inv = ROPE_THETA ** (-jnp.arange(half, dtype=F32) / half)
    ang = pos.astype(F32)[:, None] * inv[None, :]
    cos, sin = jnp.cos(ang), jnp.sin(ang)
    return jnp.concatenate([cos, cos], axis=-1), jnp.concatenate([sin, sin], axis=-1)


def _rotate_cols(w, rope):
    k, n = w.shape
    w3 = w.reshape(k, n // rope, rope)
    half = rope // 2
    return jnp.concatenate([-w3[..., half:], w3[..., :half]], axis=-1).reshape(k, n)


def kernel(x_prompt, x_sample, c_prompt, c_sample, cache_kv, cache_pe, state_ssm, state_conv, page_table, norm_mix_w, norm_ffn_w, w_ada, b_ada, w_in, conv_w, A_log, dt_bias, gdn_norm_w, w_o_gdn, q_norm_w, kv_norm_w, w_uq, w_uk, w_uv, w_o_mla, w_out, w_router, b_router, w_gu, b_gu, w_dn, b_dn, w_ada_final, b_ada_final, norm_final_w):
    B, T, D = x_prompt.shape
    Bs, Ts, _ = x_sample.shape
    assert Ts == 1
    depth = w_in.shape[0]
    H, DK, DV = state_ssm.shape[2:]
    QK = H * DK
    CONV = state_conv.shape[3]
    assert CONV == 2 * QK + H * DV and H == SUBLANES
    QL = q_norm_w.shape[1]
    HM, KVL, NOPE = w_uk.shape[1:]
    VH = w_uv.shape[3]
    ROPE = cache_pe.shape[3]
    E = w_router.shape[2]
    scale = float(NOPE + ROPE) ** -0.5
    n_p, n_s = B * T, Bs * Ts
    past_len = page_table.shape[1] * cache_kv.shape[2]
    cache_pe_t = jnp.swapaxes(cache_pe, 2, 3)

    c_all = jnp.concatenate([c_prompt, c_sample], axis=0)
    modf = _matmul(c_all, w_ada_final, F32, c_all.shape[0], 1024, b_ada_final)
    tm_p = min(512, T)
    tb = 128
    cos_p, sin_p = _rope_tables(jnp.arange(T), ROPE)
    cos_s, sin_s = _rope_tables(past_len + jnp.arange(Ts), ROPE)

    hp = x_prompt.reshape(n_p, D)
    hs = x_sample.reshape(n_s, D)
    outs = {k: [] for k in ('kv_p', 'pe_p', 'ssm_p', 'conv_p', 'kv_s', 'pe_s', 'ssm_s', 'conv_s')}
    for l in range(depth):
        mod = _matmul(c_all, w_ada[l], F32, c_all.shape[0], 1024, b_ada[l])
        last = l == depth - 1
        grp_p = _Group(mod[:B].reshape(B, 1, 6 * D), modf[:B].reshape(B, 1, 2 * D), tm_p, T // tm_p)
        grp_s = _Group(mod[B:].reshape(1, Bs, 6 * D), modf[B:].reshape(1, Bs, 2 * D), Bs, 1)
        grp_pc = _Group(grp_p.mod3, grp_p.modf3, tb, T // tb)

        offs = [0]
        for s in (CONV, H * DV, H, H, QL, KVL, ROPE, D, D):
            offs.append(offs[-1] + s)
        wi = w_in[l]
        seg = lambda i: wi[:, offs[i]:offs[i + 1]]
        w_qkv = seg(0).astype(BF16)
        w_z = seg(1).astype(BF16)
        w_gab = jnp.concatenate([seg(7), seg(8)], axis=1).astype(BF16)
        n_small = QL + KVL + 2 * ROPE
        ba_col = -(-n_small // LANE)
        w_small = jnp.concatenate(
            [seg(4), seg(5), seg(6), _rotate_cols(seg(6), ROPE), jnp.zeros((D, ba_col * LANE - n_small), F32),
             seg(2), jnp.zeros((D, 8 - H), F32), seg(3), jnp.zeros((D, LANE - 8 - H), F32)], axis=1).astype(BF16)
        cw_t = conv_w[l].T
        gparams = jnp.zeros((2, LANE), F32).at[0, 8:8 + H].set(A_log[l]).at[1, 8:8 + H].set(dt_bias[l])
        wq = w_uq[l].reshape(QL, HM, NOPE + ROPE)
        wq_pe = wq[:, :, NOPE:].reshape(QL, HM * ROPE)
        wq_all = jnp.concatenate([wq[:, :, :NOPE].reshape(QL, HM * NOPE), wq_pe], axis=1).astype(BF16)
        wq_rot = _rotate_cols(wq_pe, ROPE).astype(BF16)
        wuk_all = jnp.transpose(w_uk[l], (1, 0, 2)).reshape(KVL, HM * NOPE).astype(BF16)
        wuv_all = jnp.transpose(w_uv[l], (1, 0, 2)).reshape(KVL, HM * VH).astype(BF16)
        qnw = q_norm_w[l].reshape(1, QL)
        kvnw = kv_norm_w[l].reshape(1, KVL)
        wog = w_o_gdn[l].astype(BF16)
        wom = w_o_mla[l].astype(BF16)
        wout = w_out[l].astype(BF16)
        wr = jnp.pad(w_router[l], ((0, 0), (0, LANE - E)))
        br = jnp.pad(b_router[l], (0, LANE - E)).reshape(1, LANE)
        nfw = norm_ffn_w[l].reshape(1, D)

        h1 = _norm_mod(hp, norm_mix_w[l], grp_p, 1, 0, BF16)
        qkv_p = _matmul(h1, w_qkv, BF16, 1024, 512)
        z_p = _matmul(h1, w_z, BF16, 1024, 512)
        gab_p = _matmul(h1, w_gab, BF16, 1024, 512)
        small_p = _matmul(h1, w_small, F32, 1024, w_small.shape[1])
        og_p, ssm_p = _gdn_prompt(qkv_p, z_p, small_p, ba_col, cw_t, gparams, gdn_norm_w[l], B, T, H, DK, DV)
        q_p, k_p, v_p, lat_p, pe_p = _mla_prep_prompt(small_p, cos_p, sin_p, qnw, kvnw, wq_all, wq_rot, wuk_all,
                                                      wuv_all, T, tm_p, HM, NOPE, ROPE, VH, QL, KVL, scale)
        om_p = _flash_attention(q_p, k_p, v_p, B, T, tm_p)
        x1_p, h2_p, te_p, tg_p = _post_mixer(hp, og_p, om_p, gab_p, grp_p, wog, wom, wout, nfw, wr, br, E)
        outs['kv_p'].append(lat_p.reshape(B, T, KVL))
        outs['pe_p'].append(pe_p.reshape(B, T, ROPE))
        outs['ssm_p'].append(ssm_p)
        outs['conv_p'].append(qkv_p.reshape(B, T, CONV)[:, T - (CONV_WIDTH - 1):, :].astype(F32))

        h1s = _norm_mod(hs, norm_mix_w[l], grp_s, 1, 0, BF16)
        qkv_s = _matmul(h1s, w_qkv, F32, Bs, 512)
        z_s = _matmul(h1s, w_z, BF16, Bs, 512)
        gab_s = _matmul(h1s, w_gab, BF16, Bs, 512)
        small_s = _matmul(h1s, w_small, F32, Bs, w_small.shape[1])
        og_s, ssm_s, conv_s = _gdn_decode(qkv_s, z_s, small_s, ba_col, state_conv[l], state_ssm[l], cw_t, gparams,
                                          gdn_norm_w[l], H, DK, DV)
        qn_s, qp_s, lat_s, pe_s = _mla_prep_sample(small_s, cos_s, sin_s, qnw, kvnw, wq_all, wq_rot,
                                                   HM, NOPE, ROPE, QL, KVL, scale)
        q_lat = jnp.transpose(_q_latent(qn_s, w_uk[l], scale), (1, 0, 2))
        o_lat = _paged_attention(q_lat, qp_s.reshape(n_s, HM, ROPE), lat_s, pe_s, cache_kv, cache_pe_t, page_table, l)
        om_s = _o_value(jnp.transpose(o_lat, (1, 0, 2)), w_uv[l])
        x1_s, h2_s, te_s, tg_s = _post_mixer(hs, og_s, om_s, gab_s, grp_s, wog, wom, wout, nfw, wr, br, E)
        outs['kv_s'].append(lat_s.reshape(Bs, Ts, KVL))
        outs['pe_s'].append(pe_s.reshape(Bs, Ts, ROPE))
        outs['ssm_s'].append(ssm_s)
        outs['conv_s'].append(conv_s)

        tm_e = 256
        h2 = jnp.concatenate([h2_p, h2_s, jnp.zeros((tm_e // TOP_K, D), F32)], axis=0)
        top_e = jnp.concatenate([te_p[:, :TOP_K], te_s[:, :TOP_K]], axis=0)
        y_tok = _experts(h2, _route(top_e, E, tm_e), w_gu, b_gu, w_dn, b_dn, l, tm_e)
        if not last:
            raise NotImplementedError("stacked layers need an un-normalised residual output")
        hp = _combine_final(y_tok, 0, tg_p, x1_p, grp_pc, norm_final_w)
        hs = _combine_final(y_tok, n_p // grp_s.tm, tg_s, x1_s, grp_s, norm_final_w)

    st = lambda k: jnp.stack(outs[k])
    return (hp.reshape(B, T, D), hs.reshape(Bs, Ts, D),
            st('kv_p'), st('pe_p'), st('ssm_p'), st('conv_p'),
            st('kv_s'), st('pe_s'), st('ssm_s'), st('conv_s'))
```

```python
import functools

import jax
import jax.numpy as jnp
from jax import lax
from jax.experimental import pallas as pl
from jax.experimental.pallas import tpu as pltpu

F32 = jnp.float32
BF16 = jnp.bfloat16
HI = lax.Precision.HIGHEST

NORM_EPS = 1e-6
ROPE_THETA = 10000.0
TOP_K = 4
SWIGLU_LIMIT = 7.0
SWIGLU_ALPHA = 1.702
CONV_WIDTH = 4
GDN_CHUNK = 64
PAGE_GROUP = 8
PAGE_REQUESTS = 4
LANE = 128
VMEM_LIMIT = 48 * 1024 * 1024
VMEM_LIMIT_MOE = 58 * 1024 * 1024

NT = (((1,), (1,)), ((), ()))
TN = (((0,), (0,)), ((), ()))


def _cp(sem, vmem=VMEM_LIMIT):
    return pltpu.CompilerParams(dimension_semantics=sem, vmem_limit_bytes=vmem)


def _rms(x):
    return x * lax.rsqrt(jnp.mean(x * x, axis=-1, keepdims=True) + NORM_EPS)


def _softplus(x):
    return jnp.maximum(x, 0.0) + jnp.log1p(jnp.exp(-jnp.abs(x)))


def _silu(x):
    return x * jax.nn.sigmoid(x)


def _bdot(a, b):
    return jnp.dot(a.astype(BF16), b.astype(BF16), preferred_element_type=F32)


def _bdot_g(a, b, dims):
    return lax.dot_general(a.astype(BF16), b.astype(BF16), dims, preferred_element_type=F32)


def _mm_kernel(*refs, has_bias):
    if has_bias:
        x_ref, w_ref, b_ref, o_ref = refs
    else:
        x_ref, w_ref, o_ref = refs
    acc = _bdot(x_ref[...], w_ref[...])
    if has_bias:
        acc = acc + b_ref[...]
    o_ref[...] = acc.astype(o_ref.dtype)


def _matmul(x, w, out_dtype, tm, tn, bias=None):
    M, K = x.shape
    N = w.shape[1]
    tm, tn = min(tm, M), min(tn, N)
    in_specs = [pl.BlockSpec((tm, K), lambda i, j: (i, 0)), pl.BlockSpec((K, tn), lambda i, j: (0, j))]
    args = [x, w]
    if bias is not None:
        in_specs.append(pl.BlockSpec((1, tn), lambda i, j: (0, j)))
        args.append(bias.reshape(1, N))
    return pl.pallas_call(
        functools.partial(_mm_kernel, has_bias=bias is not None),
        grid=(M // tm, N // tn), in_specs=in_specs,
        out_specs=pl.BlockSpec((tm, tn), lambda i, j: (i, j)),
        out_shape=jax.ShapeDtypeStruct((M, N), out_dtype),
        compiler_params=_cp(("parallel", "parallel")))(*args)


class _Group:
    def __init__(self, mod3, modf3, tm, blocks_per_g):
        self.mod3, self.modf3, self.tm, self.bpg = mod3, modf3, tm, blocks_per_g

    def spec(self, j, d):
        r, bpg = self.mod3.shape[1], self.bpg
        return pl.BlockSpec((None, r, d), lambda i: (i // bpg, 0, j))


def _norm_mod_kernel(x_ref, w_ref, sc_ref, sh_ref, o_ref):
    y = _rms(x_ref[...]) * w_ref[...]
    o_ref[...] = (y * (1.0 + sc_ref[...]) + sh_ref[...]).astype(o_ref.dtype)


def _norm_mod(x, w, grp, j_scale, j_shift, out_dtype):
    n, d = x.shape
    tm = grp.tm
    return pl.pallas_call(
        _norm_mod_kernel, grid=(n // tm,),
        in_specs=[pl.BlockSpec((tm, d), lambda i: (i, 0)), pl.BlockSpec((1, d), lambda i: (0, 0)),
                  grp.spec(j_scale, d), grp.spec(j_shift, d)],
        out_specs=pl.BlockSpec((tm, d), lambda i: (i, 0)),
        out_shape=jax.ShapeDtypeStruct((n, d), out_dtype),
        compiler_params=_cp(("parallel",)))(x, w.reshape(1, d), grp.mod3, grp.mod3)


SUBLANES = 8
PREV_ROWS = 16


def _split_bf16(x):
    hi = x.astype(BF16)
    return hi, (x - hi.astype(F32)).astype(BF16)


def _dot3(a, b):
    d = lambda x, y: jnp.dot(x, y, preferred_element_type=F32)
    return d(a[0], b[0]) + d(a[0], b[1]) + d(a[1], b[0])


def _expand_matrix(C):
    k = jnp.arange(C)
    n = jnp.arange(SUBLANES * LANE)
    rem = n % LANE
    hit = ((k[:, None] // SUBLANES == rem[None, :] // SUBLANES) & (k[:, None] % SUBLANES == n[None, :] // LANE)
           & (rem[None, :] < C))
    return hit.astype(BF16)


def _unit_lower_inverses(a_list, at_list, g_ref, C):
    nh = len(a_list)
    row = lax.broadcasted_iota(jnp.int32, (C, C), 0)
    col = lax.broadcasted_iota(jnp.int32, (C, C), 1)
    blockdiag = (row // SUBLANES) == (col // SUBLANES)
    packed = []
    for at in at_list:
        m = jnp.where(blockdiag, at, 0.0)
        d = m[0:SUBLANES]
        for b in range(1, C // SUBLANES):
            d = d + m[SUBLANES * b:SUBLANES * (b + 1)]
        packed.append(d)
    stack = _split_bf16(jnp.concatenate(packed, axis=0))
    g = g_ref[...]
    coef = (jnp.dot(stack[0], g, preferred_element_type=F32) + jnp.dot(stack[1], g, preferred_element_type=F32))
    sub = lax.broadcasted_iota(jnp.int32, (SUBLANES, LANE), 0)
    lane = lax.broadcasted_iota(jnp.int32, (SUBLANES, LANE), 1)
    unit = jnp.where((lane % SUBLANES == sub) & (lane < C), 1.0, 0.0)
    xd = [unit] * nh
    for i in range(1, SUBLANES):
        e_i = jnp.where((lane % SUBLANES == i) & (lane < C), 1.0, 0.0)[0:1]
        for h in range(nh):
            c_i = coef[SUBLANES * h:SUBLANES * (h + 1), LANE * i:LANE * (i + 1)]
            new_row = e_i - jnp.sum(c_i * xd[h], axis=0, keepdims=True)
            xd[h] = jnp.where(sub == i, new_row, xd[h])
    x = [jnp.where(blockdiag, jnp.concatenate([xd[h][:, :C]] * (C // SUBLANES), axis=0), 0.0) for h in range(nh)]
    s = SUBLANES
    while s < C:
        below = ((row // s) % 2 == 1) & ((col // s) == (row // s) - 1)
        xs = [_split_bf16(x[h]) for h in range(nh)]
        m1 = [_dot3(xs[h], _split_bf16(jnp.where(below, a_list[h], 0.0))) for h in range(nh)]
        x = [x[h] - _dot3(_split_bf16(m1[h]), xs[h]) for h in range(nh)]
        s *= 2
    return x


def _gdn_prep_kernel(qkv_ref, prev_ref, sm_ref, cw_ref, gp_ref, g_ref,
                     u_ref, w_ref, qe_ref, kd_ref, aqk_ref, dec_ref, xbuf_ref, *, H, DK, DV, C):
    c = pl.program_id(1)
    QK = H * DK
    P = PREV_ROWS

    prev = prev_ref[...].astype(F32)
    xbuf_ref[0:P, :] = jnp.where(c == 0, 0.0, prev)
    xbuf_ref[P:P + C, :] = qkv_ref[...].astype(F32)
    y = None
    for j in range(CONV_WIDTH):
        o = P - (CONV_WIDTH - 1) + j
        term = xbuf_ref[o:o + C, :] * cw_ref[j:j + 1, :]
        y = term if y is None else y + term
    y = _silu(y)

    sm = sm_ref[...]
    beta_all = jax.nn.sigmoid(sm)
    g_all = -jnp.exp(gp_ref[0:1, :]) * _softplus(sm + gp_ref[1:2, :])
    row = lax.broadcasted_iota(jnp.int32, (C, C), 0)
    col = lax.broadcasted_iota(jnp.int32, (C, C), 1)
    tril = (col <= row).astype(F32)
    gam_all = jnp.dot(tril, g_all, precision=HI, preferred_element_type=F32)
    lane = lax.broadcasted_iota(jnp.int32, sm.shape, 1)
    rows_t = jnp.where(lane < 8, beta_all, gam_all).T

    a_list, at_list, rhs_list = [], [], []
    for h in range(H):
        qh = y[:, h * DK:(h + 1) * DK]
        kh = y[:, QK + h * DK:QK + (h + 1) * DK]
        vh = y[:, 2 * QK + h * DV:2 * QK + (h + 1) * DV]
        qh = qh * lax.rsqrt(jnp.sum(qh * qh, axis=-1, keepdims=True) + NORM_EPS) * (DK ** -0.5)
        kh = kh * lax.rsqrt(jnp.sum(kh * kh, axis=-1, keepdims=True) + NORM_EPS)
        beta_c = beta_all[:, h:h + 1]
        gam_c = gam_all[:, 8 + h:9 + h]
        beta_r = rows_t[h:h + 1, :]
        gam_r = rows_t[8 + h:9 + h, :]
        dm = gam_c - gam_r
        decay = jnp.exp(jnp.where(col <= row, dm, -jnp.inf))
        decay_t = jnp.exp(jnp.where(row < col, -dm, -jnp.inf))
        qk_kk = _bdot_g(jnp.concatenate([qh, kh], axis=0), kh, NT)
        kk = qk_kk[C:]
        a_list.append(jnp.where(col < row, kk * beta_c * decay, 0.0))
        at_list.append(kk * beta_r * decay_t)
        egam = jnp.exp(gam_c)
        rhs_list.append(jnp.concatenate([beta_c * vh, beta_c * egam * kh], axis=1).astype(BF16))
        g_last = gam_c[C - 1:C, :]
        qe_ref[:, h * DK:(h + 1) * DK] = (qh * egam).astype(qe_ref.dtype)
        kd_ref[:, h * DK:(h + 1) * DK] = (kh * jnp.exp(g_last - gam_c)).astype(kd_ref.dtype)
        aqk_ref[:, h * C:(h + 1) * C] = (qk_kk[:C] * decay).astype(aqk_ref.dtype)
        dec_ref[h:h + 1, :] = jnp.broadcast_to(jnp.exp(g_last), (1, LANE))

    t_inv = _unit_lower_inverses(a_list, at_list, g_ref, C)
    for h in range(H):
        t_hi, t_lo = _split_bf16(t_inv[h])
        uw = (jnp.dot(t_hi, rhs_list[h], preferred_element_type=F32)
              + jnp.dot(t_lo, rhs_list[h], preferred_element_type=F32))
        u_ref[:, h * DV:(h + 1) * DV] = uw[:, :DV].astype(u_ref.dtype)
        w_ref[:, h * DK:(h + 1) * DK] = uw[:, DV:].astype(w_ref.dtype)


def _gdn_scan_kernel(u_ref, w_ref, qe_ref, kd_ref, aqk_ref, dec_ref, z_ref, nw_ref, og_ref, s_out_ref, s_ref,
                     *, H, DK, DV, C):
    c = pl.program_id(1)

    @pl.when(c == 0)
    def _():
        s_ref[...] = jnp.zeros_like(s_ref)

    ks = [slice(h * DK, (h + 1) * DK) for h in range(H)]
    vs = [slice(h * DV, (h + 1) * DV) for h in range(H)]
    s16 = [s_ref[h].astype(BF16) for h in range(H)]
    ws = [jnp.dot(w_ref[:, ks[h]], s16[h], preferred_element_type=F32) for h in range(H)]
    qs = [jnp.dot(qe_ref[:, ks[h]], s16[h], preferred_element_type=F32) for h in range(H)]
    v16 = [(u_ref[:, vs[h]].astype(F32) - ws[h]).astype(BF16) for h in range(H)]
    ds = [lax.dot_general(kd_ref[:, ks[h]], v16[h], TN, preferred_element_type=F32) for h in range(H)]
    o = [qs[h] + jnp.dot(aqk_ref[:, h * C:(h + 1) * C], v16[h], preferred_element_type=F32) for h in range(H)]
    for h in range(H):
        s_ref[h] = s_ref[h] * dec_ref[h:h + 1, :] + ds[h]
        zh = z_ref[:, vs[h]].astype(F32)
        og_ref[:, vs[h]] = (_rms(o[h]) * nw_ref[...] * _silu(zh)).astype(og_ref.dtype)

    @pl.when(c == pl.num_programs(1) - 1)
    def _():
        s_out_ref[...] = s_ref[...]


def _gdn_prompt(qkv, z, small, ba_col, cw_t, gparams, norm_w, B, T, H, DK, DV):
    C = GDN_CHUNK
    nc = T // C
    n, W = qkv.shape
    kw = dict(H=H, DK=DK, DV=DV, C=C)
    blk = lambda w: pl.BlockSpec((C, w), lambda b, c: (b * nc + c, 0))
    ppc = C // PREV_ROWS
    u, w, qe, kd, aqk, dec = pl.pallas_call(
        functools.partial(_gdn_prep_kernel, **kw), grid=(B, nc),
        in_specs=[blk(W),
                  pl.BlockSpec((PREV_ROWS, W), lambda b, c: (jnp.maximum((b * nc + c) * ppc - 1, 0), 0)),
                  pl.BlockSpec((C, LANE), lambda b, c: (b * nc + c, ba_col)),
                  pl.BlockSpec((CONV_WIDTH, W), lambda b, c: (0, 0)),
                  pl.BlockSpec((2, LANE), lambda b, c: (0, 0)),
                  pl.BlockSpec((C, SUBLANES * LANE), lambda b, c: (0, 0))],
        out_specs=[blk(H * DV), blk(H * DK), blk(H * DK), blk(H * DK), blk(H * C),
                   pl.BlockSpec((None, SUBLANES, LANE), lambda b, c: (b * nc + c, 0, 0))],
        out_shape=[jax.ShapeDtypeStruct((n, H * DV), BF16), jax.ShapeDtypeStruct((n, H * DK), BF16),
                   jax.ShapeDtypeStruct((n, H * DK), BF16), jax.ShapeDtypeStruct((n, H * DK), BF16),
                   jax.ShapeDtypeStruct((n, H * C), BF16), jax.ShapeDtypeStruct((B * nc, SUBLANES, LANE), F32)],
        scratch_shapes=[pltpu.VMEM((C + PREV_ROWS, W), F32)],
        compiler_params=_cp(("parallel", "parallel")))(qkv, qkv, small, cw_t, gparams, _expand_matrix(C))
    return pl.pallas_call(
        functools.partial(_gdn_scan_kernel, **kw), grid=(B, nc),
        in_specs=[blk(H * DV), blk(H * DK), blk(H * DK), blk(H * DK), blk(H * C),
                  pl.BlockSpec((None, SUBLANES, LANE), lambda b, c: (b * nc + c, 0, 0)),
                  blk(H * DV), pl.BlockSpec((1, DV), lambda b, c: (0, 0))],
        out_specs=[blk(H * DV), pl.BlockSpec((None, H, DK, DV), lambda b, c: (b, 0, 0, 0))],
        out_shape=[jax.ShapeDtypeStruct((n, H * DV), BF16), jax.ShapeDtypeStruct((B, H, DK, DV), F32)],
        scratch_shapes=[pltpu.VMEM((H, DK, DV), F32)],
        compiler_params=_cp(("parallel", "arbitrary")))(u, w, qe, kd, aqk, dec, z, norm_w.reshape(1, DV))


def _gdn_decode_kernel(u_ref, z_ref, ba_ref, buf_ref, s_in_ref, cw_ref, gp_ref, nw_ref,
                       og_ref, s_out_ref, buf_out_ref, *, H, DK, DV):
    QK = H * DK
    u = u_ref[...]
    buf = buf_ref[...]
    y = buf[0:1] * cw_ref[0:1, :]
    y = y + buf[1:2] * cw_ref[1:2, :]
    y = y + buf[2:3] * cw_ref[2:3, :]
    y = y + u * cw_ref[3:4, :]
    buf_out_ref[0:2, :] = buf[1:3]
    buf_out_ref[2:3, :] = u
    y = _silu(y)
    sm = ba_ref[...]
    beta_all = jax.nn.sigmoid(sm)
    g_all = -jnp.exp(gp_ref[0:1, :]) * _softplus(sm + gp_ref[1:2, :])
    rows = []
    for h in range(H):
        kh = y[:, QK + h * DK:QK + (h + 1) * DK]
        rows.append(kh * lax.rsqrt(jnp.sum(kh * kh, axis=-1, keepdims=True) + NORM_EPS))
    for h in range(H):
        qh = y[:, h * DK:(h + 1) * DK]
        rows.append(qh * lax.rsqrt(jnp.sum(qh * qh, axis=-1, keepdims=True) + NORM_EPS) * (DK ** -0.5))
    rows.append(jnp.zeros((DK - 2 * H, DK), F32))
    cols = jnp.concatenate(rows, axis=0).T
    for h in range(H):
        kcol = cols[:, h:h + 1]
        qcol = cols[:, H + h:H + h + 1]
        vh = y[:, 2 * QK + h * DV:2 * QK + (h + 1) * DV]
        s_dec = s_in_ref[h] * jnp.exp(g_all[:, 8 + h:9 + h])
        v_new = beta_all[:, h:h + 1] * (vh - jnp.sum(kcol * s_dec, axis=0, keepdims=True))
        s_new = s_dec + kcol * v_new
        s_out_ref[h] = s_new
        o = jnp.sum(qcol * s_new, axis=0, keepdims=True)
        zh = z_ref[:, h * DV:(h + 1) * DV].astype(F32)
        og_ref[:, h * DV:(h + 1) * DV] = (_rms(o) * nw_ref[...] * _silu(zh)).astype(og_ref.dtype)


def _gdn_decode(qkv, z, small, ba_col, conv_buf, ssm, cw_t, gparams, norm_w, H, DK, DV):
    Bs, W = qkv.shape
    kern = functools.partial(_gdn_decode_kernel, H=H, DK=DK, DV=DV)
    og, s_new, buf_new = pl.pallas_call(
        kern, grid=(Bs,),
        in_specs=[pl.BlockSpec((None, 1, W), lambda b: (b, 0, 0)),
                  pl.BlockSpec((None, 1, H * DV), lambda b: (b, 0, 0)),
                  pl.BlockSpec((None, 1, LANE), lambda b: (b, 0, ba_col)),
                  pl.BlockSpec((None, CONV_WIDTH - 1, W), lambda b: (b, 0, 0)),
                  pl.BlockSpec((None, H, DK, DV), lambda b: (b, 0, 0, 0)),
                  pl.BlockSpec((CONV_WIDTH, W), lambda b: (0, 0)),
                  pl.BlockSpec((2, LANE), lambda b: (0, 0)),
                  pl.BlockSpec((1, DV), lambda b: (0, 0))],
        out_specs=[pl.BlockSpec((None, 1, H * DV), lambda b: (b, 0, 0)),
                   pl.BlockSpec((None, H, DK, DV), lambda b: (b, 0, 0, 0)),
                   pl.BlockSpec((None, CONV_WIDTH - 1, W), lambda b: (b, 0, 0))],
        out_shape=[jax.ShapeDtypeStruct((Bs, 1, H * DV), BF16),
                   jax.ShapeDtypeStruct(ssm.shape, F32),
                   jax.ShapeDtypeStruct(conv_buf.shape, F32)],
        compiler_params=_cp(("parallel",)))(
            qkv.reshape(Bs, 1, W), z.reshape(Bs, 1, H * DV), small.reshape(Bs, 1, small.shape[1]),
            conv_buf, ssm, cw_t, gparams, norm_w.reshape(1, DV))
    return og.reshape(Bs, H * DV), s_new, buf_new


def _mla_prep_prompt_kernel(sm_ref, cos_ref, sin_ref, cost_ref, sint_ref, qnw_ref, kvnw_ref,
                            wqt_ref, wqrt_ref, wuk_ref, wuvt_ref,
                            qt_ref, k_ref, vt_ref, lat_ref, pe_ref, *, H, NOPE, ROPE, VH, QL, KVL, scale):
    sm = sm_ref[...]
    qn = (_rms(sm[:, :QL]) * qnw_ref[...]).astype(BF16)
    qft = lax.dot_general(wqt_ref[...], qn, NT, preferred_element_type=F32)
    qrt = lax.dot_general(wqrt_ref[...], qn, NT, preferred_element_type=F32)
    qpt = qft[H * NOPE:, :] * cost_ref[...] + qrt * sint_ref[...]
    lat = _rms(sm[:, QL:QL + KVL]) * kvnw_ref[...]
    lat_ref[...] = lat
    lat16 = lat.astype(BF16)
    kn = jnp.dot(lat16, wuk_ref[...], preferred_element_type=F32)
    vt = lax.dot_general(wuvt_ref[...], lat16, NT, preferred_element_type=F32)
    o = QL + KVL
    kr = sm[:, o:o + ROPE] * cos_ref[...] + sm[:, o + ROPE:o + 2 * ROPE] * sin_ref[...]
    pe_ref[...] = kr
    for h in range(H):
        qt_ref[h] = (jnp.concatenate([qft[h * NOPE:(h + 1) * NOPE, :], qpt[h * ROPE:(h + 1) * ROPE, :]], axis=0)
                     * scale).astype(qt_ref.dtype)
        k_ref[h] = jnp.concatenate([kn[:, h * NOPE:(h + 1) * NOPE], kr], axis=-1).astype(k_ref.dtype)
        vt_ref[h, 0] = vt[h * VH:(h + 1) * VH, :].astype(vt_ref.dtype)


def _mla_prep_prompt(small, cos, sin, qnw, kvnw, wq, wqr, wuk, wuv, T, tm, H, NOPE, ROPE, VH, QL, KVL, scale):
    n, ws = small.shape
    nt = T // tm
    kern = functools.partial(_mla_prep_prompt_kernel, H=H, NOPE=NOPE, ROPE=ROPE, VH=VH, QL=QL, KVL=KVL, scale=scale)
    full = lambda a: pl.BlockSpec(a.shape, lambda i: (0,) * a.ndim)
    dqk = NOPE + ROPE
    cos_t = jnp.tile(cos.T, (H, 1))
    sin_t = jnp.tile(sin.T, (H, 1))
    wqt, wqrt, wuvt = wq.T, wqr.T, wuv.T
    return pl.pallas_call(
        kern, grid=(n // tm,),
        in_specs=[pl.BlockSpec((tm, ws), lambda i: (i, 0)),
                  pl.BlockSpec((tm, ROPE), lambda i: (i % nt, 0)),
                  pl.BlockSpec((tm, ROPE), lambda i: (i % nt, 0)),
                  pl.BlockSpec((H * ROPE, tm), lambda i: (0, i % nt)),
                  pl.BlockSpec((H * ROPE, tm), lambda i: (0, i % nt)),
                  full(qnw), full(kvnw), full(wqt), full(wqrt), full(wuk), full(wuvt)],
        out_specs=[pl.BlockSpec((H, dqk, tm), lambda i: (0, 0, i)),
                   pl.BlockSpec((H, tm, dqk), lambda i: (0, i, 0)),
                   pl.BlockSpec((H, 1, VH, tm), lambda i: (0, i, 0, 0)),
                   pl.BlockSpec((tm, KVL), lambda i: (i, 0)),
                   pl.BlockSpec((tm, ROPE), lambda i: (i, 0))],
        out_shape=[jax.ShapeDtypeStruct((H, dqk, n), BF16), jax.ShapeDtypeStruct((H, n, dqk), BF16),
                   jax.ShapeDtypeStruct((H, n // tm, VH, tm), BF16), jax.ShapeDtypeStruct((n, KVL), F32),
                   jax.ShapeDtypeStruct((n, ROPE), F32)],
        compiler_params=_cp(("parallel",)))(small, cos, sin, cos_t, sin_t, qnw, kvnw, wqt, wqrt, wuk, wuvt)


def _mla_prep_sample_kernel(sm_ref, cos_ref, sin_ref, qnw_ref, kvnw_ref, wq_ref, wqr_ref,
                            qn_ref, qp_ref, lat_ref, pe_ref, *, H, NOPE, ROPE, QL, KVL, scale):
    sm = sm_ref[...]
    cos = cos_ref[...]
    sin = sin_ref[...]
    qn = _rms(sm[:, :QL]) * qnw_ref[...]
    qf = _bdot(qn, wq_ref[...])
    qr = _bdot(qn, wqr_ref[...])
    qn_ref[...] = qf[:, :H * NOPE].astype(qn_ref.dtype)
    lat_ref[...] = _rms(sm[:, QL:QL + KVL]) * kvnw_ref[...]
    o = QL + KVL
    pe_ref[...] = sm[:, o:o + ROPE] * cos + sm[:, o + ROPE:o + 2 * ROPE] * sin
    for h in range(H):
        p0 = H * NOPE + h * ROPE
        qp = qf[:, p0:p0 + ROPE] * cos + qr[:, h * ROPE:(h + 1) * ROPE] * sin
        qp_ref[:, h * ROPE:(h + 1) * ROPE] = (qp * scale).astype(qp_ref.dtype)


def _mla_prep_sample(small, cos, sin, qnw, kvnw, wq, wqr, H, NOPE, ROPE, QL, KVL, scale):
    n, ws = small.shape
    kern = functools.partial(_mla_prep_sample_kernel, H=H, NOPE=NOPE, ROPE=ROPE, QL=QL, KVL=KVL, scale=scale)
    full = lambda a: pl.BlockSpec(a.shape, lambda i: (0,) * a.ndim)
    return pl.pallas_call(
        kern, grid=(1,),
        in_specs=[full(small), full(cos), full(sin), full(qnw), full(kvnw), full(wq), full(wqr)],
        out_specs=[pl.BlockSpec((n, H * NOPE), lambda i: (0, 0)), pl.BlockSpec((n, H * ROPE), lambda i: (0, 0)),
                   pl.BlockSpec((n, KVL), lambda i: (0, 0)), pl.BlockSpec((n, ROPE), lambda i: (0, 0))],
        out_shape=[jax.ShapeDtypeStruct((n, H * NOPE), BF16), jax.ShapeDtypeStruct((n, H * ROPE), BF16),
                   jax.ShapeDtypeStruct((n, KVL), F32), jax.ShapeDtypeStruct((n, ROPE), F32)],
        compiler_params=_cp(("arbitrary",)))(small, cos, sin, qnw, kvnw, wq, wqr)


def _head_proj_kernel(x_ref, w_ref, o_ref, *, dims, scale):
    o_ref[...] = (_bdot_g(x_ref[...], w_ref[...], dims) * scale).astype(o_ref.dtype)


def _q_latent(q_nope, w_uk, scale):
    H, KVL, NOPE = w_uk.shape
    n = q_nope.shape[0]
    return pl.pallas_call(
        functools.partial(_head_proj_kernel, dims=NT, scale=scale), grid=(H,),
        in_specs=[pl.BlockSpec((n, NOPE), lambda h: (0, h)), pl.BlockSpec((None, KVL, NOPE), lambda h: (h, 0, 0))],
        out_specs=pl.BlockSpec((None, n, KVL), lambda h: (h, 0, 0)),
        out_shape=jax.ShapeDtypeStruct((H, n, KVL), BF16),
        compiler_params=_cp(("parallel",)))(q_nope, w_uk)


def _o_value(o_lat, w_uv):
    H, KVL, VH = w_uv.shape
    n = o_lat.shape[1]
    return pl.pallas_call(
        functools.partial(_head_proj_kernel, dims=(((1,), (0,)), ((), ())), scale=1.0), grid=(H,),
        in_specs=[pl.BlockSpec((None, n, KVL), lambda h: (h, 0, 0)), pl.BlockSpec((None, KVL, VH), lambda h: (h, 0, 0))],
        out_specs=pl.BlockSpec((n, VH), lambda h: (0, h)),
        out_shape=jax.ShapeDtypeStruct((n, H * VH), BF16),
        compiler_params=_cp(("parallel",)))(o_lat, w_uv)


FLASH_SPLIT = 2


def _flash_kernel(qt_ref, k_ref, vt_ref, o_ref, m_ref, l_ref, acc_ref, *, tq):
    qi = pl.program_id(2)
    ns = FLASH_SPLIT
    hq = tq // ns
    m_ref[...] = jnp.full_like(m_ref, -jnp.inf)
    l_ref[...] = jnp.zeros_like(l_ref)
    acc_ref[...] = jnp.zeros_like(acc_ref)
    qt = [qt_ref[:, t * hq:(t + 1) * hq] for t in range(ns)]

    def update(st, vt):
        m_prev = [m_ref[t] for t in range(ns)]
        m_new = [jnp.maximum(m_prev[t], jnp.max(st[t], axis=0, keepdims=True)) for t in range(ns)]
        p = [jnp.exp(st[t] - m_new[t]) for t in range(ns)]
        corr = [jnp.exp(m_prev[t] - m_new[t]) for t in range(ns)]
        pv = [jnp.dot(vt, p[t].astype(BF16), preferred_element_type=F32) for t in range(ns)]
        for t in range(ns):
            m_ref[t] = m_new[t]
            l_ref[t] = corr[t] * l_ref[t] + jnp.sum(p[t], axis=0, keepdims=True)
            acc_ref[t] = acc_ref[t] * corr[t] + pv[t]

    def below_diagonal(j, carry):
        k = k_ref[pl.ds(pl.multiple_of(j * tq, tq), tq), :]
        update([jnp.dot(k, qt[t], preferred_element_type=F32) for t in range(ns)], vt_ref[j])
        return carry

    lax.fori_loop(0, qi, below_diagonal, 0)

    k = k_ref[pl.ds(pl.multiple_of(qi * tq, tq), tq), :]
    st = []
    for t in range(ns):
        s = jnp.dot(k, qt[t], preferred_element_type=F32)
        key = lax.broadcasted_iota(jnp.int32, s.shape, 0)
        qry = t * hq + lax.broadcasted_iota(jnp.int32, s.shape, 1)
        st.append(jnp.where(key <= qry, s, -jnp.inf))
    update(st, vt_ref[qi])
    for t in range(ns):
        o_ref[t * hq:(t + 1) * hq, :] = (acc_ref[t] / l_ref[t]).T.astype(o_ref.dtype)


def _flash_attention(qt, k, vt, B, T, tq):
    H, dqk, n = qt.shape
    VH = vt.shape[2]
    nq = T // tq
    hq = tq // FLASH_SPLIT
    kern = functools.partial(_flash_kernel, tq=tq)
    return pl.pallas_call(
        kern, grid=(B, H, nq),
        in_specs=[pl.BlockSpec((None, dqk, tq), lambda b, h, i: (h, 0, b * nq + i)),
                  pl.BlockSpec((None, T, dqk), lambda b, h, i: (h, b, 0)),
                  pl.BlockSpec((None, nq, VH, tq), lambda b, h, i: (h, b, 0, 0))],
        out_specs=pl.BlockSpec((tq, VH), lambda b, h, i: (b * nq + i, h)),
        out_shape=jax.ShapeDtypeStruct((n, H * VH), BF16),
        scratch_shapes=[pltpu.VMEM((FLASH_SPLIT, 1, hq), F32), pltpu.VMEM((FLASH_SPLIT, 1, hq), F32),
                        pltpu.VMEM((FLASH_SPLIT, VH, hq), F32)],
        compiler_params=_cp(("parallel", "parallel", "arbitrary")))(qt, k, vt)


def _paged_kernel(pt_ref, ql_ref, qp_ref, latn_ref, pen_ref, kv_hbm, pe_hbm, o_ref,
                  kv_buf, pe_buf, sem, m_ref, l_ref, acc_ref, *, R, G, layer):
    b = pl.program_id(0)
    g = pl.program_id(1)
    ng = pl.num_programs(1)
    step = b * ng + g
    slot = step % 2

    def page_copies(bb, gg, sl, real_pages):
        out = []
        for r in range(R):
            for i in range(G):
                page = pt_ref[bb * R + r, gg * G + i] if real_pages else 0
                out.append(pltpu.make_async_copy(kv_hbm.at[layer, page], kv_buf.at[sl, r * G + i], sem.at[0, sl]))
                out.append(pltpu.make_async_copy(pe_hbm.at[layer, page], pe_buf.at[sl, r * G + i], sem.at[1, sl]))
        return out

    @pl.when(step == 0)
    def _():
        for cp in page_copies(0, 0, 0, True):
            cp.start()

    @pl.when(step + 1 < pl.num_programs(0) * ng)
    def _():
        nxt = step + 1
        for cp in page_copies(nxt // ng, nxt % ng, 1 - slot, True):
            cp.start()

    for cp in page_copies(b, g, slot, False):
        cp.wait()

    @pl.when(g == 0)
    def _():
        m_ref[...] = jnp.full_like(m_ref, -jnp.inf)
        l_ref[...] = jnp.zeros_like(l_ref)
        acc_ref[...] = jnp.zeros_like(acc_ref)

    kvs = [[kv_buf[slot, r * G + i].astype(BF16) for i in range(G)] for r in range(R)]
    P = kvs[0][0].shape[0]
    s, p, corr, m_new = [], [], [], []
    for r in range(R):
        ql = ql_ref[r]
        qp = qp_ref[r]
        s.append(jnp.concatenate(
            [lax.dot_general(ql, kvs[r][i], NT, preferred_element_type=F32)
             + jnp.dot(qp, pe_buf[slot, r * G + i].astype(BF16), preferred_element_type=F32) for i in range(G)],
            axis=-1))
    for r in range(R):
        m_prev = m_ref[r]
        m_new.append(jnp.maximum(m_prev, jnp.max(s[r], axis=-1, keepdims=True)))
        corr.append(jnp.exp(m_prev - m_new[r]))
        p.append(jnp.exp(s[r] - m_new[r]))
    pv = []
    for r in range(R):
        t = None
        for i in range(G):
            d = jnp.dot(p[r][:, i * P:(i + 1) * P].astype(BF16), kvs[r][i], preferred_element_type=F32)
            t = d if t is None else t + d
        pv.append(t)
    for r in range(R):
        m_ref[r] = m_new[r]
        l_ref[r] = corr[r] * l_ref[r] + jnp.sum(p[r], axis=-1, keepdims=True)
        acc_ref[r] = acc_ref[r] * corr[r] + pv[r]

    @pl.when(g == pl.num_programs(1) - 1)
    def _():
        for r in range(R):
            latn = latn_ref[r]
            s_n = (jnp.sum(ql_ref[r].astype(F32) * latn, axis=-1, keepdims=True)
                   + jnp.sum(qp_ref[r].astype(F32) * pen_ref[r], axis=-1, keepdims=True))
            m_old = m_ref[r]
            m2 = jnp.maximum(m_old, s_n)
            c2 = jnp.exp(m_old - m2)
            p2 = jnp.exp(s_n - m2)
            o_ref[r] = ((acc_ref[r] * c2 + p2 * latn) / (l_ref[r] * c2 + p2)).astype(o_ref.dtype)


def _paged_attention(q_lat, q_pe, lat_new, pe_new, cache_kv, cache_pe_t, page_table, layer):
    Bs, H, KVL = q_lat.shape
    ROPE = q_pe.shape[2]
    n_pages = page_table.shape[1]
    P = cache_kv.shape[2]
    G = min(PAGE_GROUP, n_pages)
    R = min(PAGE_REQUESTS, Bs)

    req = lambda rows, w: pl.BlockSpec((R, rows, w), lambda b, g, pt: (b, 0, 0))
    hbm = pl.BlockSpec(memory_space=pl.ANY)
    grid_spec = pltpu.PrefetchScalarGridSpec(
        num_scalar_prefetch=1, grid=(Bs // R, n_pages // G),
        in_specs=[req(H, KVL), req(H, ROPE), req(1, KVL), req(1, ROPE), hbm, hbm],
        out_specs=req(H, KVL),
        scratch_shapes=[pltpu.VMEM((2, R * G, P, KVL), cache_kv.dtype), pltpu.VMEM((2, R * G, ROPE, P), cache_pe_t.dtype),
                        pltpu.SemaphoreType.DMA((2, 2)),
                        pltpu.VMEM((R, H, 1), F32), pltpu.VMEM((R, H, 1), F32), pltpu.VMEM((R, H, KVL), F32)])
    return pl.pallas_call(
        functools.partial(_paged_kernel, R=R, G=G, layer=layer), grid_spec=grid_spec,
        out_shape=jax.ShapeDtypeStruct((Bs, H, KVL), BF16),
        compiler_params=_cp(("arbitrary", "arbitrary")))(
            page_table, q_lat, q_pe, lat_new.reshape(Bs, 1, KVL), pe_new.reshape(Bs, 1, ROPE), cache_kv, cache_pe_t)


def _post_mixer_kernel(x_ref, og_ref, om_ref, gab_ref, g1_ref, sc2_ref, sh2_ref, wog_ref, wom_ref, wout_ref,
                       nw_ref, wr_ref, br_ref, x1_ref, h2_ref, te_ref, tg_ref, *, D, E):
    gab = gab_ref[...].astype(F32)
    merged = (jax.nn.sigmoid(gab[:, :D]) * _bdot(og_ref[...], wog_ref[...])
              + jax.nn.sigmoid(gab[:, D:]) * _bdot(om_ref[...], wom_ref[...]))
    x1 = x_ref[...] + g1_ref[...] * _bdot(merged, wout_ref[...])
    x1_ref[...] = x1
    h2 = _rms(x1) * nw_ref[...] * (1.0 + sc2_ref[...]) + sh2_ref[...]
    h2_ref[...] = h2
    logits = _dot3(_split_bf16(h2), _split_bf16(wr_ref[...])) + br_ref[...]
    lane = lax.broadcasted_iota(jnp.int32, logits.shape, 1)
    logits = jnp.where(lane < E, logits, -jnp.inf)
    te = jnp.zeros(logits.shape, jnp.int32)
    ex = jnp.zeros(logits.shape, F32)
    top = None
    for k in range(TOP_K):
        m = jnp.max(logits, axis=-1, keepdims=True)
        idx = jnp.min(jnp.where(logits == m, lane, LANE), axis=-1, keepdims=True)
        top = m if top is None else top
        te = jnp.where(lane == k, idx, te)
        ex = jnp.where(lane == k, jnp.exp(m - top), ex)
        logits = jnp.where(lane == idx, -jnp.inf, logits)
    te_ref[...] = te
    tg_ref[...] = ex / jnp.sum(ex, axis=-1, keepdims=True)


def _post_mixer(x, og, om, gab, grp, wog, wom, wout, norm_w, wr, br, E):
    n, d = x.shape
    tm = grp.tm
    rowblk = lambda w: pl.BlockSpec((tm, w), lambda i: (i, 0))
    full = lambda a: pl.BlockSpec(a.shape, lambda i: (0,) * a.ndim)
    return pl.pallas_call(
        functools.partial(_post_mixer_kernel, D=d, E=E), grid=(n // tm,),
        in_specs=[rowblk(d), rowblk(d), rowblk(d), rowblk(2 * d), grp.spec(2, d), grp.spec(4, d), grp.spec(3, d),
                  full(wog), full(wom), full(wout), full(norm_w), full(wr), full(br)],
        out_specs=[rowblk(d), rowblk(d), rowblk(LANE), rowblk(LANE)],
        out_shape=[jax.ShapeDtypeStruct((n, d), F32), jax.ShapeDtypeStruct((n, d), F32),
                   jax.ShapeDtypeStruct((n, LANE), jnp.int32), jax.ShapeDtypeStruct((n, LANE), F32)],
        compiler_params=_cp(("parallel",)))(x, og, om, gab, grp.mod3, grp.mod3, grp.mod3,
                                            wog, wom, wout, norm_w, wr, br)


def _dispatch_kernel(pe_ref, slot_ref, h_ref, hs_ref, zero_ref, sem, zsem, *, tb, tm, E):
    i = pl.program_id(0)
    n_blocks = hs_ref.shape[0] // tm

    def zero_copy(row0):
        return pltpu.make_async_copy(zero_ref, hs_ref.at[pl.ds(pl.multiple_of(row0, tm), tm)], zsem)

    zero_jobs = [(pe_ref[e + 1] > pe_ref[e], pe_ref[e + 1] - tm) for e in range(E)]
    zero_jobs += [(b * tm >= pe_ref[E], b * tm) for b in range(n_blocks - E, n_blocks)]

    @pl.when(i == 0)
    def _():
        zero_ref[...] = jnp.zeros_like(zero_ref)
        for wanted, row0 in zero_jobs:
            @pl.when(wanted)
            def _():
                zero_copy(row0).start()
        for wanted, row0 in zero_jobs:
            @pl.when(wanted)
            def _():
                zero_copy(row0).wait()

    for r in range(tb):
        for k in range(TOP_K):
            a = r * TOP_K + k
            pltpu.make_async_copy(h_ref.at[pl.ds(r, 1)], hs_ref.at[pl.ds(slot_ref[0, 0, a], 1)],
                                  sem).start(priority=a % 2)
    for k in range(TOP_K):
        pltpu.make_async_copy(h_ref, hs_ref.at[pl.ds(0, tb)], sem).wait()


def _dispatch(h, slots, pad_edges, n_slots, tb, tm):
    n, d = h.shape
    nb = n // tb
    E = pad_edges.shape[0] - 1
    grid_spec = pltpu.PrefetchScalarGridSpec(
        num_scalar_prefetch=1, grid=(nb,),
        in_specs=[pl.BlockSpec((1, 1, tb * TOP_K), lambda i, pe: (i, 0, 0), memory_space=pltpu.SMEM),
                  pl.BlockSpec((tb, d), lambda i, pe: (i, 0))],
        out_specs=pl.BlockSpec(memory_space=pl.ANY),
        scratch_shapes=[pltpu.VMEM((tm, d), h.dtype), pltpu.SemaphoreType.DMA, pltpu.SemaphoreType.DMA])
    return pl.pallas_call(
        functools.partial(_dispatch_kernel, tb=tb, tm=tm, E=E), grid_spec=grid_spec,
        out_shape=jax.ShapeDtypeStruct((n_slots, d), h.dtype),
        compiler_params=_cp(("arbitrary",)))(pad_edges, slots.reshape(nb, 1, tb * TOP_K), h)


def _expert_kernel(be_ref, na_ref, x_ref, wgu_ref, bgu_ref, wdn_ref, bdn_ref, y_ref, wgu_s, wdn_s, *, DE):
    i = pl.program_id(0)
    active = i < na_ref[0]
    first = jnp.logical_or(i == 0, be_ref[i] != be_ref[jnp.maximum(i - 1, 0)])

    @pl.when(jnp.logical_and(active, first))
    def _():
        wgu_s[...] = wgu_ref[...].astype(BF16)
        wdn_s[...] = wdn_ref[...].astype(BF16)

    @pl.when(active)
    def _():
        gu = jnp.dot(x_ref[...].astype(BF16), wgu_s[...], preferred_element_type=F32) + bgu_ref[...]
        gt = jnp.minimum(gu[:, :DE], SWIGLU_LIMIT)
        up = jnp.clip(gu[:, DE:], -SWIGLU_LIMIT, SWIGLU_LIMIT)
        act = (up + 1.0) * gt * jax.nn.sigmoid(SWIGLU_ALPHA * gt)
        y_ref[...] = jnp.dot(act.astype(BF16), wdn_s[...], preferred_element_type=F32) + bdn_ref[...]

    @pl.when(jnp.logical_not(active))
    def _():
        y_ref[...] = jnp.zeros_like(y_ref)


def _experts(hs, blk_e, n_active, w_gu, b_gu, w_dn, b_dn, layer, tm):
    n_slots, d = hs.shape
    E, _, de2 = w_gu.shape[1:]
    de = de2 // 2
    grid_spec = pltpu.PrefetchScalarGridSpec(
        num_scalar_prefetch=2, grid=(n_slots // tm,),
        in_specs=[pl.BlockSpec((tm, d), lambda i, be, na: (jnp.minimum(i, na[0] - 1), 0)),
                  pl.BlockSpec((None, None, d, de2), lambda i, be, na: (layer, be[i], 0, 0)),
                  pl.BlockSpec((None, None, 1, de2), lambda i, be, na: (layer, be[i], 0, 0)),
                  pl.BlockSpec((None, None, de, d), lambda i, be, na: (layer, be[i], 0, 0)),
                  pl.BlockSpec((None, None, 1, d), lambda i, be, na: (layer, be[i], 0, 0))],
        out_specs=pl.BlockSpec((tm, d), lambda i, be, na: (i, 0)),
        scratch_shapes=[pltpu.VMEM((d, de2), BF16), pltpu.VMEM((de, d), BF16)])
    L = w_gu.shape[0]
    return pl.pallas_call(
        functools.partial(_expert_kernel, DE=de), grid_spec=grid_spec,
        out_shape=jax.ShapeDtypeStruct((n_slots, d), F32),
        compiler_params=_cp(("arbitrary",), VMEM_LIMIT_MOE))(
            blk_e, n_active, hs, w_gu, b_gu.reshape(L, E, 1, de2), w_dn, b_dn.reshape(L, E, 1, d))


def _combine_kernel(slotc_ref, slotn_ref, tg_ref, x1_ref, g2_ref, scf_ref, shf_ref, nwf_ref, yp_ref, o_ref,
                    buf, sem, *, tb):
    i = pl.program_id(0)
    slot = i % 2

    def gather_starts(slot_ref, sl):
        for r in range(tb):
            for k in range(TOP_K):
                a = r * TOP_K + k
                pltpu.make_async_copy(yp_ref.at[pl.ds(slot_ref[0, 0, a], 1)], buf.at[sl, k, pl.ds(r, 1)],
                                      sem.at[sl]).start(priority=a % 2)

    @pl.when(i == 0)
    def _():
        gather_starts(slotc_ref, 0)

    @pl.when(i + 1 < pl.num_programs(0))
    def _():
        gather_starts(slotn_ref, 1 - slot)

    for k in range(TOP_K):
        pltpu.make_async_copy(yp_ref.at[pl.ds(0, tb)], buf.at[slot, k], sem.at[slot]).wait()
    tg = tg_ref[...]
    moe = tg[:, 0:1] * buf[slot, 0]
    for k in range(1, TOP_K):
        moe = moe + tg[:, k:k + 1] * buf[slot, k]
    x2 = x1_ref[...] + g2_ref[...] * moe
    o_ref[...] = _rms(x2) * nwf_ref[...] * (1.0 + scf_ref[...]) + shf_ref[...]


def _combine_final(y_pad, slots, tg, x1, grp, norm_final_w):
    n, d = x1.shape
    tb = grp.tm
    nb = n // tb
    rf, bpg = grp.modf3.shape[1], grp.bpg
    fspec = lambda j: pl.BlockSpec((None, rf, d), lambda i: (i // bpg, 0, j))
    slots3 = slots.reshape(nb, 1, tb * TOP_K)
    sspec = lambda f: pl.BlockSpec((1, 1, tb * TOP_K), lambda i: (f(i), 0, 0), memory_space=pltpu.SMEM)
    return pl.pallas_call(
        functools.partial(_combine_kernel, tb=tb), grid=(nb,),
        in_specs=[sspec(lambda i: i), sspec(lambda i: jnp.minimum(i + 1, nb - 1)),
                  pl.BlockSpec((tb, LANE), lambda i: (i, 0)),
                  pl.BlockSpec((tb, d), lambda i: (i, 0)),
                  grp.spec(5, d), fspec(1), fspec(0),
                  pl.BlockSpec((1, d), lambda i: (0, 0)),
                  pl.BlockSpec(memory_space=pl.ANY)],
        out_specs=pl.BlockSpec((tb, d), lambda i: (i, 0)),
        out_shape=jax.ShapeDtypeStruct((n, d), F32),
        scratch_shapes=[pltpu.VMEM((2, TOP_K, tb, d), F32), pltpu.SemaphoreType.DMA((2,))],
        compiler_params=_cp(("arbitrary",)))(
            slots3, slots3, tg, x1, grp.mod3, grp.modf3, grp.modf3, norm_final_w.reshape(1, d), y_pad)


def _route(top_e, E, tm):
    n = top_e.shape[0]
    tok_oh = jnp.sum((top_e[:, :, None] == jnp.arange(E, dtype=jnp.int32)).astype(jnp.int32), axis=1)
    csum = jnp.cumsum(tok_oh, axis=0)
    counts = csum[-1]
    rank = jnp.take_along_axis(csum - tok_oh, top_e, axis=1)
    padded = (counts + tm - 1) // tm * tm
    pad_end = jnp.cumsum(padded)
    slots = (pad_end - padded)[top_e] + rank
    n_blocks = -(-(n * TOP_K) // tm) + E
    blk_start = jnp.arange(n_blocks, dtype=jnp.int32) * tm
    blk_e = jnp.minimum(jnp.sum((pad_end[None, :] <= blk_start[:, None]).astype(jnp.int32), axis=1), E - 1)
    n_active = (pad_end[-1] // tm).reshape(1)
    pad_edges = jnp.concatenate([jnp.zeros((1,), pad_end.dtype), pad_end])
    return (slots.astype(jnp.int32), pad_edges.astype(jnp.int32), blk_e.astype(jnp.int32),
            n_active.astype(jnp.int32), n_blocks * tm)


def _rope_tables(pos, rope):
    half = rope // 2
    inv = ROPE_THETA ** (-jnp.arange(half, dtype=F32) / half)
    ang = pos.astype(F32)[:, None] * inv[None, :]
    cos, sin = jnp.cos(ang), jnp.sin(ang)
    return jnp.concatenate([cos, cos], axis=-1), jnp.concatenate([sin, sin], axis=-1)


def _rotate_cols(w, rope):
    k, n = w.shape
    w3 = w.reshape(k, n // rope, rope)
    half = rope // 2
    return jnp.concatenate([-w3[..., half:], w3[..., :half]], axis=-1).reshape(k, n)


def kernel(x_prompt, x_sample, c_prompt, c_sample, cache_kv, cache_pe, state_ssm, state_conv, page_table, norm_mix_w, norm_ffn_w, w_ada, b_ada, w_in, conv_w, A_log, dt_bias, gdn_norm_w, w_o_gdn, q_norm_w, kv_norm_w, w_uq, w_uk, w_uv, w_o_mla, w_out, w_router, b_router, w_gu, b_gu, w_dn, b_dn, w_ada_final, b_ada_final, norm_final_w):
    B, T, D = x_prompt.shape
    Bs, Ts, _ = x_sample.shape
    assert Ts == 1
    depth = w_in.shape[0]
    H, DK, DV = state_ssm.shape[2:]
    QK = H * DK
    CONV = state_conv.shape[3]
    assert CONV == 2 * QK + H * DV and H == SUBLANES
    QL = q_norm_w.shape[1]
    HM, KVL, NOPE = w_uk.shape[1:]
    VH = w_uv.shape[3]
    ROPE = cache_pe.shape[3]
    E = w_router.shape[2]
    scale = float(NOPE + ROPE) ** -0.5
    n_p, n_s = B * T, Bs * Ts
    past_len = page_table.shape[1] * cache_kv.shape[2]
    cache_pe_t = jnp.swapaxes(cache_pe, 2, 3)

    c_all = jnp.concatenate([c_prompt, c_sample], axis=0)
    modf = _matmul(c_all, w_ada_final, F32, c_all.shape[0], 1024, b_ada_final)
    tm_p = min(512, T)
    tb = 128
    cos_p, sin_p = _rope_tables(jnp.arange(T), ROPE)
    cos_s, sin_s = _rope_tables(past_len + jnp.arange(Ts), ROPE)

    hp = x_prompt.reshape(n_p, D)
    hs = x_sample.reshape(n_s, D)
    outs = {k: [] for k in ('kv_p', 'pe_p', 'ssm_p', 'conv_p', 'kv_s', 'pe_s', 'ssm_s', 'conv_s')}
    for l in range(depth):
        mod = _matmul(c_all, w_ada[l], F32, c_all.shape[0], 1024, b_ada[l])
        last = l == depth - 1
        grp_p = _Group(mod[:B].reshape(B, 1, 6 * D), modf[:B].reshape(B, 1, 2 * D), tm_p, T // tm_p)
        grp_s = _Group(mod[B:].reshape(1, Bs, 6 * D), modf[B:].reshape(1, Bs, 2 * D), Bs, 1)
        grp_pc = _Group(grp_p.mod3, grp_p.modf3, tb, T // tb)

        offs = [0]
        for s in (CONV, H * DV, H, H, QL, KVL, ROPE, D, D):
            offs.append(offs[-1] + s)
        wi = w_in[l]
        seg = lambda i: wi[:, offs[i]:offs[i + 1]]
        w_qkv = seg(0).astype(BF16)
        w_z = seg(1).astype(BF16)
        w_gab = jnp.concatenate([seg(7), seg(8)], axis=1).astype(BF16)
        n_small = QL + KVL + 2 * ROPE
        ba_col = -(-n_small // LANE)
        w_small = jnp.concatenate(
            [seg(4), seg(5), seg(6), _rotate_cols(seg(6), ROPE), jnp.zeros((D, ba_col * LANE - n_small), F32),
             seg(2), jnp.zeros((D, 8 - H), F32), seg(3), jnp.zeros((D, LANE - 8 - H), F32)], axis=1).astype(BF16)
        cw_t = conv_w[l].T
        gparams = jnp.zeros((2, LANE), F32).at[0, 8:8 + H].set(A_log[l]).at[1, 8:8 + H].set(dt_bias[l])
        wq = w_uq[l].reshape(QL, HM, NOPE + ROPE)
        wq_pe = wq[:, :, NOPE:].reshape(QL, HM * ROPE)
        wq_all = jnp.concatenate([wq[:, :, :NOPE].reshape(QL, HM * NOPE), wq_pe], axis=1).astype(BF16)
        wq_rot = _rotate_cols(wq_pe, ROPE).astype(BF16)
        wuk_all = jnp.transpose(w_uk[l], (1, 0, 2)).reshape(KVL, HM * NOPE).astype(BF16)
        wuv_all = jnp.transpose(w_uv[l], (1, 0, 2)).reshape(KVL, HM * VH).astype(BF16)
        qnw = q_norm_w[l].reshape(1, QL)
        kvnw = kv_norm_w[l].reshape(1, KVL)
        wog = w_o_gdn[l].astype(BF16)
        wom = w_o_mla[l].astype(BF16)
        wout = w_out[l].astype(BF16)
        wr = jnp.pad(w_router[l], ((0, 0), (0, LANE - E)))
        br = jnp.pad(b_router[l], (0, LANE - E)).reshape(1, LANE)
        nfw = norm_ffn_w[l].reshape(1, D)

        h1 = _norm_mod(hp, norm_mix_w[l], grp_p, 1, 0, BF16)
        qkv_p = _matmul(h1, w_qkv, BF16, 1024, 512)
        z_p = _matmul(h1, w_z, BF16, 1024, 512)
        gab_p = _matmul(h1, w_gab, BF16, 1024, 512)
        small_p = _matmul(h1, w_small, F32, 1024, w_small.shape[1])
        og_p, ssm_p = _gdn_prompt(qkv_p, z_p, small_p, ba_col, cw_t, gparams, gdn_norm_w[l], B, T, H, DK, DV)
        q_p, k_p, v_p, lat_p, pe_p = _mla_prep_prompt(small_p, cos_p, sin_p, qnw, kvnw, wq_all, wq_rot, wuk_all,
                                                      wuv_all, T, tm_p, HM, NOPE, ROPE, VH, QL, KVL, scale)
        om_p = _flash_attention(q_p, k_p, v_p, B, T, tm_p)
        x1_p, h2_p, te_p, tg_p = _post_mixer(hp, og_p, om_p, gab_p, grp_p, wog, wom, wout, nfw, wr, br, E)
        outs['kv_p'].append(lat_p.reshape(B, T, KVL))
        outs['pe_p'].append(pe_p.reshape(B, T, ROPE))
        outs['ssm_p'].append(ssm_p)
        outs['conv_p'].append(qkv_p.reshape(B, T, CONV)[:, T - (CONV_WIDTH - 1):, :].astype(F32))

        h1s = _norm_mod(hs, norm_mix_w[l], grp_s, 1, 0, BF16)
        qkv_s = _matmul(h1s, w_qkv, F32, Bs, 512)
        z_s = _matmul(h1s, w_z, BF16, Bs, 512)
        gab_s = _matmul(h1s, w_gab, BF16, Bs, 512)
        small_s = _matmul(h1s, w_small, F32, Bs, w_small.shape[1])
        og_s, ssm_s, conv_s = _gdn_decode(qkv_s, z_s, small_s, ba_col, state_conv[l], state_ssm[l], cw_t, gparams,
                                          gdn_norm_w[l], H, DK, DV)
        qn_s, qp_s, lat_s, pe_s = _mla_prep_sample(small_s, cos_s, sin_s, qnw, kvnw, wq_all, wq_rot,
                                                   HM, NOPE, ROPE, QL, KVL, scale)
        q_lat = jnp.transpose(_q_latent(qn_s, w_uk[l], scale), (1, 0, 2))
        o_lat = _paged_attention(q_lat, qp_s.reshape(n_s, HM, ROPE), lat_s, pe_s, cache_kv, cache_pe_t, page_table, l)
        om_s = _o_value(jnp.transpose(o_lat, (1, 0, 2)), w_uv[l])
        x1_s, h2_s, te_s, tg_s = _post_mixer(hs, og_s, om_s, gab_s, grp_s, wog, wom, wout, nfw, wr, br, E)
        outs['kv_s'].append(lat_s.reshape(Bs, Ts, KVL))
        outs['pe_s'].append(pe_s.reshape(Bs, Ts, ROPE))
        outs['ssm_s'].append(ssm_s)
        outs['conv_s'].append(conv_s)

        tm_e = 256
        h2 = jnp.concatenate([h2_p, h2_s], axis=0)
        top_e = jnp.concatenate([te_p[:, :TOP_K], te_s[:, :TOP_K]], axis=0)
        slots, pad_edges, blk_e, n_active, n_slots = _route(top_e, E, tm_e)
        h_sorted = _dispatch(h2, slots, pad_edges, n_slots, tb, tm_e)
        y_pad = _experts(h_sorted, blk_e, n_active, w_gu, b_gu, w_dn, b_dn, l, tm_e)
        if not last:
            raise NotImplementedError("stacked layers need an un-normalised residual output")
        hp = _combine_final(y_pad, slots[:n_p], tg_p, x1_p, grp_pc, norm_final_w)
        hs = _combine_final(y_pad, slots[n_p:], tg_s, x1_s, grp_s, norm_final_w)

    st = lambda k: jnp.stack(outs[k])
    return (hp.reshape(B, T, D), hs.reshape(Bs, Ts, D),
            st('kv_p'), st('pe_p'), st('ssm_p'), st('conv_p'),
            st('kv_s'), st('pe_s'), st('ssm_s'), st('conv_s'))
```

```python
import functools

import jax
import jax.numpy as jnp
from jax import lax
from jax.experimental import pallas as pl
from jax.experimental.pallas import tpu as pltpu

F32 = jnp.float32
BF16 = jnp.bfloat16
HI = lax.Precision.HIGHEST

NORM_EPS = 1e-6
ROPE_THETA = 10000.0
TOP_K = 4
SWIGLU_LIMIT = 7.0
SWIGLU_ALPHA = 1.702
CONV_WIDTH = 4
GDN_CHUNK = 64
PAGE_GROUP = 8
PAGE_REQUESTS = 4
LANE = 128
VMEM_LIMIT = 48 * 1024 * 1024
VMEM_LIMIT_MOE = 58 * 1024 * 1024

NT = (((1,), (1,)), ((), ()))
TN = (((0,), (0,)), ((), ()))


def _cp(sem, vmem=VMEM_LIMIT):
    return pltpu.CompilerParams(dimension_semantics=sem, vmem_limit_bytes=vmem)


def _rms(x):
    return x * lax.rsqrt(jnp.mean(x * x, axis=-1, keepdims=True) + NORM_EPS)


def _softplus(x):
    return jnp.maximum(x, 0.0) + jnp.log1p(jnp.exp(-jnp.abs(x)))


def _silu(x):
    return x * jax.nn.sigmoid(x)


def _bdot(a, b):
    return jnp.dot(a.astype(BF16), b.astype(BF16), preferred_element_type=F32)


def _bdot_g(a, b, dims):
    return lax.dot_general(a.astype(BF16), b.astype(BF16), dims, preferred_element_type=F32)


def _mm_kernel(*refs, has_bias):
    if has_bias:
        x_ref, w_ref, b_ref, o_ref = refs
    else:
        x_ref, w_ref, o_ref = refs
    acc = _bdot(x_ref[...], w_ref[...])
    if has_bias:
        acc = acc + b_ref[...]
    o_ref[...] = acc.astype(o_ref.dtype)


def _matmul(x, w, out_dtype, tm, tn, bias=None):
    M, K = x.shape
    N = w.shape[1]
    tm, tn = min(tm, M), min(tn, N)
    in_specs = [pl.BlockSpec((tm, K), lambda i, j: (i, 0)), pl.BlockSpec((K, tn), lambda i, j: (0, j))]
    args = [x, w]
    if bias is not None:
        in_specs.append(pl.BlockSpec((1, tn), lambda i, j: (0, j)))
        args.append(bias.reshape(1, N))
    return pl.pallas_call(
        functools.partial(_mm_kernel, has_bias=bias is not None),
        grid=(M // tm, N // tn), in_specs=in_specs,
        out_specs=pl.BlockSpec((tm, tn), lambda i, j: (i, j)),
        out_shape=jax.ShapeDtypeStruct((M, N), out_dtype),
        compiler_params=_cp(("parallel", "parallel")))(*args)


class _Group:
    def __init__(self, mod3, modf3, tm, blocks_per_g):
        self.mod3, self.modf3, self.tm, self.bpg = mod3, modf3, tm, blocks_per_g

    def spec(self, j, d):
        r, bpg = self.mod3.shape[1], self.bpg
        return pl.BlockSpec((None, r, d), lambda i: (i // bpg, 0, j))


def _norm_mod_kernel(x_ref, w_ref, sc_ref, sh_ref, o_ref):
    y = _rms(x_ref[...]) * w_ref[...]
    o_ref[...] = (y * (1.0 + sc_ref[...]) + sh_ref[...]).astype(o_ref.dtype)


def _norm_mod(x, w, grp, j_scale, j_shift, out_dtype):
    n, d = x.shape
    tm = grp.tm
    return pl.pallas_call(
        _norm_mod_kernel, grid=(n // tm,),
        in_specs=[pl.BlockSpec((tm, d), lambda i: (i, 0)), pl.BlockSpec((1, d), lambda i: (0, 0)),
                  grp.spec(j_scale, d), grp.spec(j_shift, d)],
        out_specs=pl.BlockSpec((tm, d), lambda i: (i, 0)),
        out_shape=jax.ShapeDtypeStruct((n, d), out_dtype),
        compiler_params=_cp(("parallel",)))(x, w.reshape(1, d), grp.mod3, grp.mod3)


SUBLANES = 8
PREV_ROWS = 16


def _split_bf16(x):
    hi = x.astype(BF16)
    return hi, (x - hi.astype(F32)).astype(BF16)


def _dot3(a, b):
    d = lambda x, y: jnp.dot(x, y, preferred_element_type=F32)
    return d(a[0], b[0]) + d(a[0], b[1]) + d(a[1], b[0])


def _expand_matrix(C):
    k = jnp.arange(C)
    n = jnp.arange(SUBLANES * LANE)
    rem = n % LANE
    hit = ((k[:, None] // SUBLANES == rem[None, :] // SUBLANES) & (k[:, None] % SUBLANES == n[None, :] // LANE)
           & (rem[None, :] < C))
    return hit.astype(BF16)


def _unit_lower_inverses(a_list, at_list, g_ref, C):
    nh = len(a_list)
    row = lax.broadcasted_iota(jnp.int32, (C, C), 0)
    col = lax.broadcasted_iota(jnp.int32, (C, C), 1)
    blockdiag = (row // SUBLANES) == (col // SUBLANES)
    packed = []
    for at in at_list:
        m = jnp.where(blockdiag, at, 0.0)
        d = m[0:SUBLANES]
        for b in range(1, C // SUBLANES):
            d = d + m[SUBLANES * b:SUBLANES * (b + 1)]
        packed.append(d)
    stack = _split_bf16(jnp.concatenate(packed, axis=0))
    g = g_ref[...]
    coef = (jnp.dot(stack[0], g, preferred_element_type=F32) + jnp.dot(stack[1], g, preferred_element_type=F32))
    sub = lax.broadcasted_iota(jnp.int32, (SUBLANES, LANE), 0)
    lane = lax.broadcasted_iota(jnp.int32, (SUBLANES, LANE), 1)
    unit = jnp.where((lane % SUBLANES == sub) & (lane < C), 1.0, 0.0)
    xd = [unit] * nh
    for i in range(1, SUBLANES):
        e_i = jnp.where((lane % SUBLANES == i) & (lane < C), 1.0, 0.0)[0:1]
        for h in range(nh):
            c_i = coef[SUBLANES * h:SUBLANES * (h + 1), LANE * i:LANE * (i + 1)]
            new_row = e_i - jnp.sum(c_i * xd[h], axis=0, keepdims=True)
            xd[h] = jnp.where(sub == i, new_row, xd[h])
    x = [jnp.where(blockdiag, jnp.concatenate([xd[h][:, :C]] * (C // SUBLANES), axis=0), 0.0) for h in range(nh)]
    a16 = [a.astype(BF16) for a in a_list]
    zero = jnp.zeros((C, C), BF16)
    s = SUBLANES
    while s < C:
        below = ((row // s) % 2 == 1) & ((col // s) == (row // s) - 1)
        x16 = [x[h].astype(BF16) for h in range(nh)]
        m1 = [jnp.dot(x16[h], jnp.where(below, a16[h], zero), preferred_element_type=F32) for h in range(nh)]
        x = [x[h] - jnp.dot(m1[h].astype(BF16), x16[h], preferred_element_type=F32) for h in range(nh)]
        s *= 2
    return x


def _gdn_prep_kernel(qkv_ref, prev_ref, sm_ref, cw_ref, gp_ref, g_ref,
                     u_ref, w_ref, qe_ref, kd_ref, aqk_ref, dec_ref, *, H, DK, DV, C):
    c = pl.program_id(1)
    QK = H * DK
    P = PREV_ROWS

    prev = prev_ref[...]
    xin = jnp.concatenate([jnp.where(c == 0, jnp.zeros_like(prev), prev), qkv_ref[...]], axis=0)
    out_row = lax.broadcasted_iota(jnp.int32, (C, P + C), 0)
    in_row = lax.broadcasted_iota(jnp.int32, (C, P + C), 1)
    y = None
    for j in range(CONV_WIDTH):
        shift = (in_row == out_row + (P - (CONV_WIDTH - 1) + j)).astype(BF16)
        term = jnp.dot(shift, xin, preferred_element_type=F32) * cw_ref[j:j + 1, :]
        y = term if y is None else y + term
    y = _silu(y)

    sm = sm_ref[...]
    beta_all = jax.nn.sigmoid(sm)
    g_all = -jnp.exp(gp_ref[0:1, :]) * _softplus(sm + gp_ref[1:2, :])
    row = lax.broadcasted_iota(jnp.int32, (C, C), 0)
    col = lax.broadcasted_iota(jnp.int32, (C, C), 1)
    tril = (col <= row).astype(F32)
    gam_all = jnp.dot(tril, g_all, precision=HI, preferred_element_type=F32)
    lane = lax.broadcasted_iota(jnp.int32, sm.shape, 1)
    rows_t = jnp.where(lane < 8, beta_all, gam_all).T

    a_list, at_list, rhs_list = [], [], []
    for h in range(H):
        qh = y[:, h * DK:(h + 1) * DK]
        kh = y[:, QK + h * DK:QK + (h + 1) * DK]
        vh = y[:, 2 * QK + h * DV:2 * QK + (h + 1) * DV]
        qh = qh * lax.rsqrt(jnp.sum(qh * qh, axis=-1, keepdims=True) + NORM_EPS) * (DK ** -0.5)
        kh = kh * lax.rsqrt(jnp.sum(kh * kh, axis=-1, keepdims=True) + NORM_EPS)
        beta_c = beta_all[:, h:h + 1]
        gam_c = gam_all[:, 8 + h:9 + h]
        beta_r = rows_t[h:h + 1, :]
        gam_r = rows_t[8 + h:9 + h, :]
        dm = gam_c - gam_r
        decay = jnp.exp(jnp.where(col <= row, dm, -jnp.inf))
        decay_t = jnp.exp(jnp.where(row < col, -dm, -jnp.inf))
        qk_kk = _bdot_g(jnp.concatenate([qh, kh], axis=0), kh, NT)
        kk = qk_kk[C:]
        a_list.append(jnp.where(col < row, kk * beta_c * decay, 0.0))
        at_list.append(kk * beta_r * decay_t)
        egam = jnp.exp(gam_c)
        rhs_list.append(jnp.concatenate([beta_c * vh, beta_c * egam * kh], axis=1).astype(BF16))
        g_last = gam_c[C - 1:C, :]
        qe_ref[:, h * DK:(h + 1) * DK] = (qh * egam).astype(qe_ref.dtype)
        kd_ref[:, h * DK:(h + 1) * DK] = (kh * jnp.exp(g_last - gam_c)).astype(kd_ref.dtype)
        aqk_ref[:, h * C:(h + 1) * C] = (qk_kk[:C] * decay).astype(aqk_ref.dtype)
        dec_ref[h:h + 1, :] = jnp.broadcast_to(jnp.exp(g_last), (1, LANE))

    t_inv = _unit_lower_inverses(a_list, at_list, g_ref, C)
    for h in range(H):
        t_hi, t_lo = _split_bf16(t_inv[h])
        uw = (jnp.dot(t_hi, rhs_list[h], preferred_element_type=F32)
              + jnp.dot(t_lo, rhs_list[h], preferred_element_type=F32))
        u_ref[:, h * DV:(h + 1) * DV] = uw[:, :DV].astype(u_ref.dtype)
        w_ref[:, h * DK:(h + 1) * DK] = uw[:, DV:].astype(w_ref.dtype)


def _gdn_scan_kernel(u_ref, w_ref, qe_ref, kd_ref, aqk_ref, dec_ref, z_ref, nw_ref, og_ref, s_out_ref, s_ref,
                     *, H, DK, DV, C):
    c = pl.program_id(1)

    @pl.when(c == 0)
    def _():
        s_ref[...] = jnp.zeros_like(s_ref)

    ks = [slice(h * DK, (h + 1) * DK) for h in range(H)]
    vs = [slice(h * DV, (h + 1) * DV) for h in range(H)]
    s16 = [s_ref[h].astype(BF16) for h in range(H)]
    ws = [jnp.dot(w_ref[:, ks[h]], s16[h], preferred_element_type=F32) for h in range(H)]
    qs = [jnp.dot(qe_ref[:, ks[h]], s16[h], preferred_element_type=F32) for h in range(H)]
    v16 = [(u_ref[:, vs[h]].astype(F32) - ws[h]).astype(BF16) for h in range(H)]
    ds = [lax.dot_general(kd_ref[:, ks[h]], v16[h], TN, preferred_element_type=F32) for h in range(H)]
    o = [qs[h] + jnp.dot(aqk_ref[:, h * C:(h + 1) * C], v16[h], preferred_element_type=F32) for h in range(H)]
    for h in range(H):
        s_ref[h] = s_ref[h] * dec_ref[h:h + 1, :] + ds[h]
        zh = z_ref[:, vs[h]].astype(F32)
        og_ref[:, vs[h]] = (_rms(o[h]) * nw_ref[...] * _silu(zh)).astype(og_ref.dtype)

    @pl.when(c == pl.num_programs(1) - 1)
    def _():
        s_out_ref[...] = s_ref[...]


def _gdn_prompt(qkv, z, small, ba_col, cw_t, gparams, norm_w, B, T, H, DK, DV):
    C = GDN_CHUNK
    nc = T // C
    n, W = qkv.shape
    kw = dict(H=H, DK=DK, DV=DV, C=C)
    blk = lambda w: pl.BlockSpec((C, w), lambda b, c: (b * nc + c, 0))
    ppc = C // PREV_ROWS
    u, w, qe, kd, aqk, dec = pl.pallas_call(
        functools.partial(_gdn_prep_kernel, **kw), grid=(B, nc),
        in_specs=[blk(W),
                  pl.BlockSpec((PREV_ROWS, W), lambda b, c: (jnp.maximum((b * nc + c) * ppc - 1, 0), 0)),
                  pl.BlockSpec((C, LANE), lambda b, c: (b * nc + c, ba_col)),
                  pl.BlockSpec((CONV_WIDTH, W), lambda b, c: (0, 0)),
                  pl.BlockSpec((2, LANE), lambda b, c: (0, 0)),
                  pl.BlockSpec((C, SUBLANES * LANE), lambda b, c: (0, 0))],
        out_specs=[blk(H * DV), blk(H * DK), blk(H * DK), blk(H * DK), blk(H * C),
                   pl.BlockSpec((None, SUBLANES, LANE), lambda b, c: (b * nc + c, 0, 0))],
        out_shape=[jax.ShapeDtypeStruct((n, H * DV), BF16), jax.ShapeDtypeStruct((n, H * DK), BF16),
                   jax.ShapeDtypeStruct((n, H * DK), BF16), jax.ShapeDtypeStruct((n, H * DK), BF16),
                   jax.ShapeDtypeStruct((n, H * C), BF16), jax.ShapeDtypeStruct((B * nc, SUBLANES, LANE), F32)],
        compiler_params=_cp(("parallel", "parallel")))(qkv, qkv, small, cw_t, gparams, _expand_matrix(C))
    return pl.pallas_call(
        functools.partial(_gdn_scan_kernel, **kw), grid=(B, nc),
        in_specs=[blk(H * DV), blk(H * DK), blk(H * DK), blk(H * DK), blk(H * C),
                  pl.BlockSpec((None, SUBLANES, LANE), lambda b, c: (b * nc + c, 0, 0)),
                  blk(H * DV), pl.BlockSpec((1, DV), lambda b, c: (0, 0))],
        out_specs=[blk(H * DV), pl.BlockSpec((None, H, DK, DV), lambda b, c: (b, 0, 0, 0))],
        out_shape=[jax.ShapeDtypeStruct((n, H * DV), BF16), jax.ShapeDtypeStruct((B, H, DK, DV), F32)],
        scratch_shapes=[pltpu.VMEM((H, DK, DV), F32)],
        compiler_params=_cp(("parallel", "arbitrary")))(u, w, qe, kd, aqk, dec, z, norm_w.reshape(1, DV))


DECODE_TOKENS = 1


def _gdn_decode_kernel(u_ref, z_ref, ba_ref, buf_ref, s_in_ref, cw_ref, gp_ref, nw_ref,
                       og_ref, s_out_ref, buf_out_ref, *, H, DK, DV):
    for t in range(u_ref.shape[0]):
        _gdn_decode_token(u_ref.at[t], z_ref.at[t], ba_ref.at[t], buf_ref.at[t], s_in_ref.at[t], cw_ref, gp_ref,
                          nw_ref, og_ref.at[t], s_out_ref.at[t], buf_out_ref.at[t], H=H, DK=DK, DV=DV)


def _gdn_decode_token(u_ref, z_ref, ba_ref, buf_ref, s_in_ref, cw_ref, gp_ref, nw_ref,
                      og_ref, s_out_ref, buf_out_ref, *, H, DK, DV):
    QK = H * DK
    u = u_ref[...]
    buf = buf_ref[...]
    y = buf[0:1] * cw_ref[0:1, :]
    y = y + buf[1:2] * cw_ref[1:2, :]
    y = y + buf[2:3] * cw_ref[2:3, :]
    y = y + u * cw_ref[3:4, :]
    buf_out_ref[0:2, :] = buf[1:3]
    buf_out_ref[2:3, :] = u
    y = _silu(y)
    sm = ba_ref[...]
    beta_all = jax.nn.sigmoid(sm)
    g_all = -jnp.exp(gp_ref[0:1, :]) * _softplus(sm + gp_ref[1:2, :])
    rows = []
    for h in range(H):
        kh = y[:, QK + h * DK:QK + (h + 1) * DK]
        rows.append(kh * lax.rsqrt(jnp.sum(kh * kh, axis=-1, keepdims=True) + NORM_EPS))
    for h in range(H):
        qh = y[:, h * DK:(h + 1) * DK]
        rows.append(qh * lax.rsqrt(jnp.sum(qh * qh, axis=-1, keepdims=True) + NORM_EPS) * (DK ** -0.5))
    rows.append(jnp.zeros((DK - 2 * H, DK), F32))
    cols = jnp.concatenate(rows, axis=0).T
    hs = range(H)
    kcol = [cols[:, h:h + 1] for h in hs]
    qcol = [cols[:, H + h:H + h + 1] for h in hs]
    s_dec = [s_in_ref[h] * jnp.exp(g_all[:, 8 + h:9 + h]) for h in hs]
    ks = [jnp.sum(kcol[h] * s_dec[h], axis=0, keepdims=True) for h in hs]
    v_new = [beta_all[:, h:h + 1] * (y[:, 2 * QK + h * DV:2 * QK + (h + 1) * DV] - ks[h]) for h in hs]
    s_new = [s_dec[h] + kcol[h] * v_new[h] for h in hs]
    o = [jnp.sum(qcol[h] * s_new[h], axis=0, keepdims=True) for h in hs]
    for h in hs:
        s_out_ref[h] = s_new[h]
        zh = z_ref[:, h * DV:(h + 1) * DV].astype(F32)
        og_ref[:, h * DV:(h + 1) * DV] = (_rms(o[h]) * nw_ref[...] * _silu(zh)).astype(og_ref.dtype)


def _gdn_decode(qkv, z, small, ba_col, conv_buf, ssm, cw_t, gparams, norm_w, H, DK, DV):
    Bs, W = qkv.shape
    kern = functools.partial(_gdn_decode_kernel, H=H, DK=DK, DV=DV)
    nt = min(DECODE_TOKENS, Bs)
    og, s_new, buf_new = pl.pallas_call(
        kern, grid=(Bs // nt,),
        in_specs=[pl.BlockSpec((nt, 1, W), lambda b: (b, 0, 0)),
                  pl.BlockSpec((nt, 1, H * DV), lambda b: (b, 0, 0)),
                  pl.BlockSpec((nt, 1, LANE), lambda b: (b, 0, ba_col)),
                  pl.BlockSpec((nt, CONV_WIDTH - 1, W), lambda b: (b, 0, 0)),
                  pl.BlockSpec((nt, H, DK, DV), lambda b: (b, 0, 0, 0)),
                  pl.BlockSpec((CONV_WIDTH, W), lambda b: (0, 0)),
                  pl.BlockSpec((2, LANE), lambda b: (0, 0)),
                  pl.BlockSpec((1, DV), lambda b: (0, 0))],
        out_specs=[pl.BlockSpec((nt, 1, H * DV), lambda b: (b, 0, 0)),
                   pl.BlockSpec((nt, H, DK, DV), lambda b: (b, 0, 0, 0)),
                   pl.BlockSpec((nt, CONV_WIDTH - 1, W), lambda b: (b, 0, 0))],
        out_shape=[jax.ShapeDtypeStruct((Bs, 1, H * DV), BF16),
                   jax.ShapeDtypeStruct(ssm.shape, F32),
                   jax.ShapeDtypeStruct(conv_buf.shape, F32)],
        compiler_params=_cp(("parallel",)))(
            qkv.reshape(Bs, 1, W), z.reshape(Bs, 1, H * DV), small.reshape(Bs, 1, small.shape[1]),
            conv_buf, ssm, cw_t, gparams, norm_w.reshape(1, DV))
    return og.reshape(Bs, H * DV), s_new, buf_new


def _mla_prep_prompt_kernel(sm_ref, cos_ref, sin_ref, cost_ref, sint_ref, qnw_ref, kvnw_ref,
                            wqt_ref, wqrt_ref, wuk_ref, wuvt_ref,
                            qt_ref, k_ref, vt_ref, lat_ref, pe_ref, *, H, NOPE, ROPE, VH, QL, KVL, scale):
    sm = sm_ref[...]
    qn = (_rms(sm[:, :QL]) * qnw_ref[...]).astype(BF16)
    qft = lax.dot_general(wqt_ref[...], qn, NT, preferred_element_type=F32)
    qrt = lax.dot_general(wqrt_ref[...], qn, NT, preferred_element_type=F32)
    qpt = qft[H * NOPE:, :] * cost_ref[...] + qrt * sint_ref[...]
    lat = _rms(sm[:, QL:QL + KVL]) * kvnw_ref[...]
    lat_ref[...] = lat
    lat16 = lat.astype(BF16)
    kn = jnp.dot(lat16, wuk_ref[...], preferred_element_type=F32)
    vt = lax.dot_general(wuvt_ref[...], lat16, NT, preferred_element_type=F32)
    o = QL + KVL
    kr = sm[:, o:o + ROPE] * cos_ref[...] + sm[:, o + ROPE:o + 2 * ROPE] * sin_ref[...]
    pe_ref[...] = kr
    for h in range(H):
        qt_ref[h] = (jnp.concatenate([qft[h * NOPE:(h + 1) * NOPE, :], qpt[h * ROPE:(h + 1) * ROPE, :]], axis=0)
                     * scale).astype(qt_ref.dtype)
        k_ref[h] = jnp.concatenate([kn[:, h * NOPE:(h + 1) * NOPE], kr], axis=-1).astype(k_ref.dtype)
        vt_ref[h, 0] = vt[h * VH:(h + 1) * VH, :].astype(vt_ref.dtype)


def _mla_prep_prompt(small, cos, sin, qnw, kvnw, wq, wqr, wuk, wuv, T, tm, H, NOPE, ROPE, VH, QL, KVL, scale):
    n, ws = small.shape
    nt = T // tm
    kern = functools.partial(_mla_prep_prompt_kernel, H=H, NOPE=NOPE, ROPE=ROPE, VH=VH, QL=QL, KVL=KVL, scale=scale)
    full = lambda a: pl.BlockSpec(a.shape, lambda i: (0,) * a.ndim)
    dqk = NOPE + ROPE
    cos_t = jnp.tile(cos.T, (H, 1))
    sin_t = jnp.tile(sin.T, (H, 1))
    wqt, wqrt, wuvt = wq.T, wqr.T, wuv.T
    return pl.pallas_call(
        kern, grid=(n // tm,),
        in_specs=[pl.BlockSpec((tm, ws), lambda i: (i, 0)),
                  pl.BlockSpec((tm, ROPE), lambda i: (i % nt, 0)),
                  pl.BlockSpec((tm, ROPE), lambda i: (i % nt, 0)),
                  pl.BlockSpec((H * ROPE, tm), lambda i: (0, i % nt)),
                  pl.BlockSpec((H * ROPE, tm), lambda i: (0, i % nt)),
                  full(qnw), full(kvnw), full(wqt), full(wqrt), full(wuk), full(wuvt)],
        out_specs=[pl.BlockSpec((H, dqk, tm), lambda i: (0, 0, i)),
                   pl.BlockSpec((H, tm, dqk), lambda i: (0, i, 0)),
                   pl.BlockSpec((H, 1, VH, tm), lambda i: (0, i, 0, 0)),
                   pl.BlockSpec((tm, KVL), lambda i: (i, 0)),
                   pl.BlockSpec((tm, ROPE), lambda i: (i, 0))],
        out_shape=[jax.ShapeDtypeStruct((H, dqk, n), BF16), jax.ShapeDtypeStruct((H, n, dqk), BF16),
                   jax.ShapeDtypeStruct((H, n // tm, VH, tm), BF16), jax.ShapeDtypeStruct((n, KVL), F32),
                   jax.ShapeDtypeStruct((n, ROPE), F32)],
        compiler_params=_cp(("parallel",)))(small, cos, sin, cos_t, sin_t, qnw, kvnw, wqt, wqrt, wuk, wuvt)


def _mla_prep_sample_kernel(sm_ref, cos_ref, sin_ref, qnw_ref, kvnw_ref, wq_ref, wqr_ref,
                            qn_ref, qp_ref, lat_ref, pe_ref, *, H, NOPE, ROPE, QL, KVL, scale):
    sm = sm_ref[...]
    cos = cos_ref[...]
    sin = sin_ref[...]
    qn = _rms(sm[:, :QL]) * qnw_ref[...]
    qf = _bdot(qn, wq_ref[...])
    qr = _bdot(qn, wqr_ref[...])
    qn_ref[...] = qf[:, :H * NOPE].astype(qn_ref.dtype)
    lat_ref[...] = _rms(sm[:, QL:QL + KVL]) * kvnw_ref[...]
    o = QL + KVL
    pe_ref[...] = sm[:, o:o + ROPE] * cos + sm[:, o + ROPE:o + 2 * ROPE] * sin
    for h in range(H):
        p0 = H * NOPE + h * ROPE
        qp = qf[:, p0:p0 + ROPE] * cos + qr[:, h * ROPE:(h + 1) * ROPE] * sin
        qp_ref[:, h * ROPE:(h + 1) * ROPE] = (qp * scale).astype(qp_ref.dtype)


def _mla_prep_sample(small, cos, sin, qnw, kvnw, wq, wqr, H, NOPE, ROPE, QL, KVL, scale):
    n, ws = small.shape
    kern = functools.partial(_mla_prep_sample_kernel, H=H, NOPE=NOPE, ROPE=ROPE, QL=QL, KVL=KVL, scale=scale)
    full = lambda a: pl.BlockSpec(a.shape, lambda i: (0,) * a.ndim)
    return pl.pallas_call(
        kern, grid=(1,),
        in_specs=[full(small), full(cos), full(sin), full(qnw), full(kvnw), full(wq), full(wqr)],
        out_specs=[pl.BlockSpec((n, H * NOPE), lambda i: (0, 0)), pl.BlockSpec((n, H * ROPE), lambda i: (0, 0)),
                   pl.BlockSpec((n, KVL), lambda i: (0, 0)), pl.BlockSpec((n, ROPE), lambda i: (0, 0))],
        out_shape=[jax.ShapeDtypeStruct((n, H * NOPE), BF16), jax.ShapeDtypeStruct((n, H * ROPE), BF16),
                   jax.ShapeDtypeStruct((n, KVL), F32), jax.ShapeDtypeStruct((n, ROPE), F32)],
        compiler_params=_cp(("arbitrary",)))(small, cos, sin, qnw, kvnw, wq, wqr)


def _head_proj_kernel(x_ref, w_ref, o_ref, *, dims, scale):
    o_ref[...] = (_bdot_g(x_ref[...], w_ref[...], dims) * scale).astype(o_ref.dtype)


def _q_latent(q_nope, w_uk, scale):
    H, KVL, NOPE = w_uk.shape
    n = q_nope.shape[0]
    return pl.pallas_call(
        functools.partial(_head_proj_kernel, dims=NT, scale=scale), grid=(H,),
        in_specs=[pl.BlockSpec((n, NOPE), lambda h: (0, h)), pl.BlockSpec((None, KVL, NOPE), lambda h: (h, 0, 0))],
        out_specs=pl.BlockSpec((None, n, KVL), lambda h: (h, 0, 0)),
        out_shape=jax.ShapeDtypeStruct((H, n, KVL), BF16),
        compiler_params=_cp(("parallel",)))(q_nope, w_uk)


def _o_value(o_lat, w_uv):
    H, KVL, VH = w_uv.shape
    n = o_lat.shape[1]
    return pl.pallas_call(
        functools.partial(_head_proj_kernel, dims=(((1,), (0,)), ((), ())), scale=1.0), grid=(H,),
        in_specs=[pl.BlockSpec((None, n, KVL), lambda h: (h, 0, 0)), pl.BlockSpec((None, KVL, VH), lambda h: (h, 0, 0))],
        out_specs=pl.BlockSpec((n, VH), lambda h: (0, h)),
        out_shape=jax.ShapeDtypeStruct((n, H * VH), BF16),
        compiler_params=_cp(("parallel",)))(o_lat, w_uv)


FLASH_SPLIT = 2


def _flash_kernel(qt_ref, k_ref, vt_ref, o_ref, m_ref, l_ref, acc_ref, *, tq):
    qi = pl.program_id(2)
    ns = FLASH_SPLIT
    hq = tq // ns
    m_ref[...] = jnp.full_like(m_ref, -jnp.inf)
    l_ref[...] = jnp.zeros_like(l_ref)
    acc_ref[...] = jnp.zeros_like(acc_ref)
    qt = [qt_ref[:, t * hq:(t + 1) * hq] for t in range(ns)]

    def update(st, vt):
        m_prev = [m_ref[t] for t in range(ns)]
        m_new = [jnp.maximum(m_prev[t], jnp.max(st[t], axis=0, keepdims=True)) for t in range(ns)]
        p = [jnp.exp(st[t] - m_new[t]) for t in range(ns)]
        corr = [jnp.exp(m_prev[t] - m_new[t]) for t in range(ns)]
        pv = [jnp.dot(vt, p[t].astype(BF16), preferred_element_type=F32) for t in range(ns)]
        for t in range(ns):
            m_ref[t] = m_new[t]
            l_ref[t] = corr[t] * l_ref[t] + jnp.sum(p[t], axis=0, keepdims=True)
            acc_ref[t] = acc_ref[t] * corr[t] + pv[t]

    def below_diagonal(j, carry):
        k = k_ref[pl.ds(pl.multiple_of(j * tq, tq), tq), :]
        update([jnp.dot(k, qt[t], preferred_element_type=F32) for t in range(ns)], vt_ref[j])
        return carry

    lax.fori_loop(0, qi, below_diagonal, 0)

    k = k_ref[pl.ds(pl.multiple_of(qi * tq, tq), tq), :]
    st = []
    for t in range(ns):
        s = jnp.dot(k, qt[t], preferred_element_type=F32)
        key = lax.broadcasted_iota(jnp.int32, s.shape, 0)
        qry = t * hq + lax.broadcasted_iota(jnp.int32, s.shape, 1)
        st.append(jnp.where(key <= qry, s, -jnp.inf))
    update(st, vt_ref[qi])
    for t in range(ns):
        o_ref[t * hq:(t + 1) * hq, :] = (acc_ref[t] / l_ref[t]).T.astype(o_ref.dtype)


def _flash_attention(qt, k, vt, B, T, tq):
    H, dqk, n = qt.shape
    VH = vt.shape[2]
    nq = T // tq
    hq = tq // FLASH_SPLIT
    kern = functools.partial(_flash_kernel, tq=tq)
    return pl.pallas_call(
        kern, grid=(B, H, nq),
        in_specs=[pl.BlockSpec((None, dqk, tq), lambda b, h, i: (h, 0, b * nq + i)),
                  pl.BlockSpec((None, T, dqk), lambda b, h, i: (h, b, 0)),
                  pl.BlockSpec((None, nq, VH, tq), lambda b, h, i: (h, b, 0, 0))],
        out_specs=pl.BlockSpec((tq, VH), lambda b, h, i: (b * nq + i, h)),
        out_shape=jax.ShapeDtypeStruct((n, H * VH), BF16),
        scratch_shapes=[pltpu.VMEM((FLASH_SPLIT, 1, hq), F32), pltpu.VMEM((FLASH_SPLIT, 1, hq), F32),
                        pltpu.VMEM((FLASH_SPLIT, VH, hq), F32)],
        compiler_params=_cp(("parallel", "parallel", "arbitrary")))(qt, k, vt)


def _paged_kernel(pt_ref, ql_ref, qp_ref, latn_ref, pen_ref, kv_hbm, pe_hbm, o_ref,
                  kv_buf, pe_buf, sem, m_ref, l_ref, acc_ref, *, R, G, layer):
    b = pl.program_id(0)
    g = pl.program_id(1)
    ng = pl.num_programs(1)
    step = b * ng + g
    slot = step % 2

    def page_copies(bb, gg, sl, real_pages):
        out = []
        for r in range(R):
            for i in range(G):
                page = pt_ref[bb * R + r, gg * G + i] if real_pages else 0
                out.append(pltpu.make_async_copy(kv_hbm.at[layer, page], kv_buf.at[sl, r * G + i], sem.at[0, sl]))
                out.append(pltpu.make_async_copy(pe_hbm.at[layer, page], pe_buf.at[sl, r * G + i], sem.at[1, sl]))
        return out

    @pl.when(step == 0)
    def _():
        for cp in page_copies(0, 0, 0, True):
            cp.start()

    @pl.when(step + 1 < pl.num_programs(0) * ng)
    def _():
        nxt = step + 1
        for cp in page_copies(nxt // ng, nxt % ng, 1 - slot, True):
            cp.start()

    for cp in page_copies(b, g, slot, False):
        cp.wait()

    @pl.when(g == 0)
    def _():
        m_ref[...] = jnp.full_like(m_ref, -jnp.inf)
        l_ref[...] = jnp.zeros_like(l_ref)
        acc_ref[...] = jnp.zeros_like(acc_ref)

    kvs = [[kv_buf[slot, r * G + i].astype(BF16) for i in range(G)] for r in range(R)]
    P = kvs[0][0].shape[0]
    s, p, corr, m_new = [], [], [], []
    for r in range(R):
        ql = ql_ref[r]
        qp = qp_ref[r]
        s.append(jnp.concatenate(
            [lax.dot_general(ql, kvs[r][i], NT, preferred_element_type=F32)
             + jnp.dot(qp, pe_buf[slot, r * G + i].astype(BF16), preferred_element_type=F32) for i in range(G)],
            axis=-1))
    for r in range(R):
        m_prev = m_ref[r]
        m_new.append(jnp.maximum(m_prev, jnp.max(s[r], axis=-1, keepdims=True)))
        corr.append(jnp.exp(m_prev - m_new[r]))
        p.append(jnp.exp(s[r] - m_new[r]))
    pv = []
    for r in range(R):
        t = None
        for i in range(G):
            d = jnp.dot(p[r][:, i * P:(i + 1) * P].astype(BF16), kvs[r][i], preferred_element_type=F32)
            t = d if t is None else t + d
        pv.append(t)
    for r in range(R):
        m_ref[r] = m_new[r]
        l_ref[r] = corr[r] * l_ref[r] + jnp.sum(p[r], axis=-1, keepdims=True)
        acc_ref[r] = acc_ref[r] * corr[r] + pv[r]

    @pl.when(g == pl.num_programs(1) - 1)
    def _():
        for r in range(R):
            latn = latn_ref[r]
            s_n = (jnp.sum(ql_ref[r].astype(F32) * latn, axis=-1, keepdims=True)
                   + jnp.sum(qp_ref[r].astype(F32) * pen_ref[r], axis=-1, keepdims=True))
            m_old = m_ref[r]
            m2 = jnp.maximum(m_old, s_n)
            c2 = jnp.exp(m_old - m2)
            p2 = jnp.exp(s_n - m2)
            o_ref[r] = ((acc_ref[r] * c2 + p2 * latn) / (l_ref[r] * c2 + p2)).astype(o_ref.dtype)


def _paged_attention(q_lat, q_pe, lat_new, pe_new, cache_kv, cache_pe_t, page_table, layer):
    Bs, H, KVL = q_lat.shape
    ROPE = q_pe.shape[2]
    n_pages = page_table.shape[1]
    P = cache_kv.shape[2]
    G = min(PAGE_GROUP, n_pages)
    R = min(PAGE_REQUESTS, Bs)

    req = lambda rows, w: pl.BlockSpec((R, rows, w), lambda b, g, pt: (b, 0, 0))
    hbm = pl.BlockSpec(memory_space=pl.ANY)
    grid_spec = pltpu.PrefetchScalarGridSpec(
        num_scalar_prefetch=1, grid=(Bs // R, n_pages // G),
        in_specs=[req(H, KVL), req(H, ROPE), req(1, KVL), req(1, ROPE), hbm, hbm],
        out_specs=req(H, KVL),
        scratch_shapes=[pltpu.VMEM((2, R * G, P, KVL), cache_kv.dtype), pltpu.VMEM((2, R * G, ROPE, P), cache_pe_t.dtype),
                        pltpu.SemaphoreType.DMA((2, 2)),
                        pltpu.VMEM((R, H, 1), F32), pltpu.VMEM((R, H, 1), F32), pltpu.VMEM((R, H, KVL), F32)])
    return pl.pallas_call(
        functools.partial(_paged_kernel, R=R, G=G, layer=layer), grid_spec=grid_spec,
        out_shape=jax.ShapeDtypeStruct((Bs, H, KVL), BF16),
        compiler_params=_cp(("arbitrary", "arbitrary")))(
            page_table, q_lat, q_pe, lat_new.reshape(Bs, 1, KVL), pe_new.reshape(Bs, 1, ROPE), cache_kv, cache_pe_t)


def _post_mixer_kernel(x_ref, og_ref, om_ref, gab_ref, g1_ref, sc2_ref, sh2_ref, wog_ref, wom_ref, wout_ref,
                       nw_ref, wr_ref, br_ref, x1_ref, h2_ref, te_ref, tg_ref, *, D, E):
    gab = gab_ref[...].astype(F32)
    merged = (jax.nn.sigmoid(gab[:, :D]) * _bdot(og_ref[...], wog_ref[...])
              + jax.nn.sigmoid(gab[:, D:]) * _bdot(om_ref[...], wom_ref[...]))
    x1 = x_ref[...] + g1_ref[...] * _bdot(merged, wout_ref[...])
    x1_ref[...] = x1
    h2 = _rms(x1) * nw_ref[...] * (1.0 + sc2_ref[...]) + sh2_ref[...]
    h2_ref[...] = h2
    logits = _dot3(_split_bf16(h2), _split_bf16(wr_ref[...])) + br_ref[...]
    lane = lax.broadcasted_iota(jnp.int32, logits.shape, 1)
    logits = jnp.where(lane < E, logits, -jnp.inf)
    te = jnp.zeros(logits.shape, jnp.int32)
    ex = jnp.zeros(logits.shape, F32)
    top = None
    for k in range(TOP_K):
        m = jnp.max(logits, axis=-1, keepdims=True)
        idx = jnp.min(jnp.where(logits == m, lane, LANE), axis=-1, keepdims=True)
        top = m if top is None else top
        te = jnp.where(lane == k, idx, te)
        ex = jnp.where(lane == k, jnp.exp(m - top), ex)
        logits = jnp.where(lane == idx, -jnp.inf, logits)
    te_ref[...] = te
    tg_ref[...] = ex / jnp.sum(ex, axis=-1, keepdims=True)


def _post_mixer(x, og, om, gab, grp, wog, wom, wout, norm_w, wr, br, E):
    n, d = x.shape
    tm = grp.tm
    rowblk = lambda w: pl.BlockSpec((tm, w), lambda i: (i, 0))
    full = lambda a: pl.BlockSpec(a.shape, lambda i: (0,) * a.ndim)
    return pl.pallas_call(
        functools.partial(_post_mixer_kernel, D=d, E=E), grid=(n // tm,),
        in_specs=[rowblk(d), rowblk(d), rowblk(d), rowblk(2 * d), grp.spec(2, d), grp.spec(4, d), grp.spec(3, d),
                  full(wog), full(wom), full(wout), full(norm_w), full(wr), full(br)],
        out_specs=[rowblk(d), rowblk(d), rowblk(LANE), rowblk(LANE)],
        out_shape=[jax.ShapeDtypeStruct((n, d), F32), jax.ShapeDtypeStruct((n, d), F32),
                   jax.ShapeDtypeStruct((n, LANE), jnp.int32), jax.ShapeDtypeStruct((n, LANE), F32)],
        compiler_params=_cp(("parallel",)))(x, og, om, gab, grp.mod3, grp.mod3, grp.mod3,
                                            wog, wom, wout, norm_w, wr, br)


def _dispatch_kernel(pe_ref, slot_ref, h_ref, hs_ref, zero_ref, sem, zsem, *, tb, tm, E):
    i = pl.program_id(0)
    n_blocks = hs_ref.shape[0] // tm

    def zero_copy(row0):
        return pltpu.make_async_copy(zero_ref, hs_ref.at[pl.ds(pl.multiple_of(row0, tm), tm)], zsem)

    zero_jobs = [(pe_ref[e + 1] > pe_ref[e], pe_ref[e + 1] - tm) for e in range(E)]
    zero_jobs += [(b * tm >= pe_ref[E], b * tm) for b in range(n_blocks - E, n_blocks)]

    @pl.when(i == 0)
    def _():
        zero_ref[...] = jnp.zeros_like(zero_ref)
        for wanted, row0 in zero_jobs:
            @pl.when(wanted)
            def _():
                zero_copy(row0).start()
        for wanted, row0 in zero_jobs:
            @pl.when(wanted)
            def _():
                zero_copy(row0).wait()

    for r in range(tb):
        for k in range(TOP_K):
            a = r * TOP_K + k
            pltpu.make_async_copy(h_ref.at[pl.ds(r, 1)], hs_ref.at[pl.ds(slot_ref[0, 0, a], 1)],
                                  sem).start(priority=a % 2)
    for k in range(TOP_K):
        pltpu.make_async_copy(h_ref, hs_ref.at[pl.ds(0, tb)], sem).wait()


def _dispatch(h, slots, pad_edges, n_slots, tb, tm):
    n, d = h.shape
    nb = n // tb
    E = pad_edges.shape[0] - 1
    grid_spec = pltpu.PrefetchScalarGridSpec(
        num_scalar_prefetch=1, grid=(nb,),
        in_specs=[pl.BlockSpec((1, 1, tb * TOP_K), lambda i, pe: (i, 0, 0), memory_space=pltpu.SMEM),
                  pl.BlockSpec((tb, d), lambda i, pe: (i, 0))],
        out_specs=pl.BlockSpec(memory_space=pl.ANY),
        scratch_shapes=[pltpu.VMEM((tm, d), h.dtype), pltpu.SemaphoreType.DMA, pltpu.SemaphoreType.DMA])
    return pl.pallas_call(
        functools.partial(_dispatch_kernel, tb=tb, tm=tm, E=E), grid_spec=grid_spec,
        out_shape=jax.ShapeDtypeStruct((n_slots, d), h.dtype),
        compiler_params=_cp(("arbitrary",)))(pad_edges, slots.reshape(nb, 1, tb * TOP_K), h)


def _expert_kernel(be_ref, na_ref, x_ref, wgu_ref, bgu_ref, wdn_ref, bdn_ref, y_ref, wgu_s, wdn_s, *, DE):
    i = pl.program_id(0)
    active = i < na_ref[0]
    first = jnp.logical_or(i == 0, be_ref[i] != be_ref[jnp.maximum(i - 1, 0)])

    @pl.when(jnp.logical_and(active, first))
    def _():
        wgu_s[...] = wgu_ref[...].astype(BF16)
        wdn_s[...] = wdn_ref[...].astype(BF16)

    @pl.when(active)
    def _():
        gu = jnp.dot(x_ref[...].astype(BF16), wgu_s[...], preferred_element_type=F32) + bgu_ref[...]
        gt = jnp.minimum(gu[:, :DE], SWIGLU_LIMIT)
        up = jnp.clip(gu[:, DE:], -SWIGLU_LIMIT, SWIGLU_LIMIT)
        act = (up + 1.0) * gt * jax.nn.sigmoid(SWIGLU_ALPHA * gt)
        y_ref[...] = jnp.dot(act.astype(BF16), wdn_s[...], preferred_element_type=F32) + bdn_ref[...]

    @pl.when(jnp.logical_not(active))
    def _():
        y_ref[...] = jnp.zeros_like(y_ref)


def _experts(hs, blk_e, n_active, w_gu, b_gu, w_dn, b_dn, layer, tm):
    n_slots, d = hs.shape
    E, _, de2 = w_gu.shape[1:]
    de = de2 // 2
    grid_spec = pltpu.PrefetchScalarGridSpec(
        num_scalar_prefetch=2, grid=(n_slots // tm,),
        in_specs=[pl.BlockSpec((tm, d), lambda i, be, na: (jnp.minimum(i, na[0] - 1), 0)),
                  pl.BlockSpec((None, None, d, de2), lambda i, be, na: (layer, be[i], 0, 0)),
                  pl.BlockSpec((None, None, 1, de2), lambda i, be, na: (layer, be[i], 0, 0)),
                  pl.BlockSpec((None, None, de, d), lambda i, be, na: (layer, be[i], 0, 0)),
                  pl.BlockSpec((None, None, 1, d), lambda i, be, na: (layer, be[i], 0, 0))],
        out_specs=pl.BlockSpec((tm, d), lambda i, be, na: (i, 0)),
        scratch_shapes=[pltpu.VMEM((d, de2), BF16), pltpu.VMEM((de, d), BF16)])
    L = w_gu.shape[0]
    return pl.pallas_call(
        functools.partial(_expert_kernel, DE=de), grid_spec=grid_spec,
        out_shape=jax.ShapeDtypeStruct((n_slots, d), F32),
        compiler_params=_cp(("arbitrary",), VMEM_LIMIT_MOE))(
            blk_e, n_active, hs, w_gu, b_gu.reshape(L, E, 1, de2), w_dn, b_dn.reshape(L, E, 1, d))


def _combine_kernel(slotc_ref, slotn_ref, tg_ref, x1_ref, g2_ref, scf_ref, shf_ref, nwf_ref, yp_ref, o_ref,
                    buf, sem, *, tb):
    i = pl.program_id(0)
    slot = i % 2

    def gather_starts(slot_ref, sl):
        for r in range(tb):
            for k in range(TOP_K):
                a = r * TOP_K + k
                pltpu.make_async_copy(yp_ref.at[pl.ds(slot_ref[0, 0, a], 1)], buf.at[sl, k, pl.ds(r, 1)],
                                      sem.at[sl]).start(priority=a % 2)

    @pl.when(i == 0)
    def _():
        gather_starts(slotc_ref, 0)

    @pl.when(i + 1 < pl.num_programs(0))
    def _():
        gather_starts(slotn_ref, 1 - slot)

    for k in range(TOP_K):
        pltpu.make_async_copy(yp_ref.at[pl.ds(0, tb)], buf.at[slot, k], sem.at[slot]).wait()
    tg = tg_ref[...]
    moe = tg[:, 0:1] * buf[slot, 0]
    for k in range(1, TOP_K):
        moe = moe + tg[:, k:k + 1] * buf[slot, k]
    x2 = x1_ref[...] + g2_ref[...] * moe
    o_ref[...] = _rms(x2) * nwf_ref[...] * (1.0 + scf_ref[...]) + shf_ref[...]


def _combine_final(y_pad, slots, tg, x1, grp, norm_final_w):
    n, d = x1.shape
    tb = grp.tm
    nb = n // tb
    rf, bpg = grp.modf3.shape[1], grp.bpg
    fspec = lambda j: pl.BlockSpec((None, rf, d), lambda i: (i // bpg, 0, j))
    slots3 = slots.reshape(nb, 1, tb * TOP_K)
    sspec = lambda f: pl.BlockSpec((1, 1, tb * TOP_K), lambda i: (f(i), 0, 0), memory_space=pltpu.SMEM)
    return pl.pallas_call(
        functools.partial(_combine_kernel, tb=tb), grid=(nb,),
        in_specs=[sspec(lambda i: i), sspec(lambda i: jnp.minimum(i + 1, nb - 1)),
                  pl.BlockSpec((tb, LANE), lambda i: (i, 0)),
                  pl.BlockSpec((tb, d), lambda i: (i, 0)),
                  grp.spec(5, d), fspec(1), fspec(0),
                  pl.BlockSpec((1, d), lambda i: (0, 0)),
                  pl.BlockSpec(memory_space=pl.ANY)],
        out_specs=pl.BlockSpec((tb, d), lambda i: (i, 0)),
        out_shape=jax.ShapeDtypeStruct((n, d), F32),
        scratch_shapes=[pltpu.VMEM((2, TOP_K, tb, d), F32), pltpu.SemaphoreType.DMA((2,))],
        compiler_params=_cp(("arbitrary",)))(
            slots3, slots3, tg, x1, grp.mod3, grp.modf3, grp.modf3, norm_final_w.reshape(1, d), y_pad)


def _route(top_e, E, tm):
    n = top_e.shape[0]
    tok_oh = jnp.sum((top_e[:, :, None] == jnp.arange(E, dtype=jnp.int32)).astype(jnp.int32), axis=1)
    csum = jnp.cumsum(tok_oh, axis=0)
    counts = csum[-1]
    rank = jnp.take_along_axis(csum - tok_oh, top_e, axis=1)
    padded = (counts + tm - 1) // tm * tm
    pad_end = jnp.cumsum(padded)
    slots = (pad_end - padded)[top_e] + rank
    n_blocks = -(-(n * TOP_K) // tm) + E
    blk_start = jnp.arange(n_blocks, dtype=jnp.int32) * tm
    blk_e = jnp.minimum(jnp.sum((pad_end[None, :] <= blk_start[:, None]).astype(jnp.int32), axis=1), E - 1)
    n_active = (pad_end[-1] // tm).reshape(1)
    pad_edges = jnp.concatenate([jnp.zeros((1,), pad_end.dtype), pad_end])
    return (slots.astype(jnp.int32), pad_edges.astype(jnp.int32), blk_e.astype(jnp.int32),
            n_active.astype(jnp.int32), n_blocks * tm)


def _rope_tables(pos, rope):
    half = rope // 2
    inv = ROPE_THETA ** (-jnp.arange(half, dtype=F32) / half)
    ang = pos.astype(F32)[:, None] * inv[None, :]
    cos, sin = jnp.cos(ang), jnp.sin(ang)
    return jnp.concatenate([cos, cos], axis=-1), jnp.concatenate([sin, sin], axis=-1)


def _rotate_cols(w, rope):
    k, n = w.shape
    w3 = w.reshape(k, n // rope, rope)
    half = rope // 2
    return jnp.concatenate([-w3[..., half:], w3[..., :half]], axis=-1).reshape(k, n)


def kernel(x_prompt, x_sample, c_prompt, c_sample, cache_kv, cache_pe, state_ssm, state_conv, page_table, norm_mix_w, norm_ffn_w, w_ada, b_ada, w_in, conv_w, A_log, dt_bias, gdn_norm_w, w_o_gdn, q_norm_w, kv_norm_w, w_uq, w_uk, w_uv, w_o_mla, w_out, w_router, b_router, w_gu, b_gu, w_dn, b_dn, w_ada_final, b_ada_final, norm_final_w):
    B, T, D = x_prompt.shape
    Bs, Ts, _ = x_sample.shape
    assert Ts == 1
    depth = w_in.shape[0]
    H, DK, DV = state_ssm.shape[2:]
    QK = H * DK
    CONV = state_conv.shape[3]
    assert CONV == 2 * QK + H * DV and H == SUBLANES
    QL = q_norm_w.shape[1]
    HM, KVL, NOPE = w_uk.shape[1:]
    VH = w_uv.shape[3]
    ROPE = cache_pe.shape[3]
    E = w_router.shape[2]
    scale = float(NOPE + ROPE) ** -0.5
    n_p, n_s = B * T, Bs * Ts
    past_len = page_table.shape[1] * cache_kv.shape[2]
    cache_pe_t = jnp.swapaxes(cache_pe, 2, 3)

    c_all = jnp.concatenate([c_prompt, c_sample], axis=0)
    modf = _matmul(c_all, w_ada_final, F32, c_all.shape[0], 1024, b_ada_final)
    tm_p = min(512, T)
    tb = 128
    cos_p, sin_p = _rope_tables(jnp.arange(T), ROPE)
    cos_s, sin_s = _rope_tables(past_len + jnp.arange(Ts), ROPE)

    hp = x_prompt.reshape(n_p, D)
    hs = x_sample.reshape(n_s, D)
    outs = {k: [] for k in ('kv_p', 'pe_p', 'ssm_p', 'conv_p', 'kv_s', 'pe_s', 'ssm_s', 'conv_s')}
    for l in range(depth):
        mod = _matmul(c_all, w_ada[l], F32, c_all.shape[0], 1024, b_ada[l])
        last = l == depth - 1
        grp_p = _Group(mod[:B].reshape(B, 1, 6 * D), modf[:B].reshape(B, 1, 2 * D), tm_p, T // tm_p)
        grp_s = _Group(mod[B:].reshape(1, Bs, 6 * D), modf[B:].reshape(1, Bs, 2 * D), Bs, 1)
        grp_pc = _Group(grp_p.mod3, grp_p.modf3, tb, T // tb)

        offs = [0]
        for s in (CONV, H * DV, H, H, QL, KVL, ROPE, D, D):
            offs.append(offs[-1] + s)
        wi = w_in[l]
        seg = lambda i: wi[:, offs[i]:offs[i + 1]]
        w_qkv = seg(0).astype(BF16)
        w_z = seg(1).astype(BF16)
        w_gab = jnp.concatenate([seg(7), seg(8)], axis=1).astype(BF16)
        n_small = QL + KVL + 2 * ROPE
        ba_col = -(-n_small // LANE)
        w_small = jnp.concatenate(
            [seg(4), seg(5), seg(6), _rotate_cols(seg(6), ROPE), jnp.zeros((D, ba_col * LANE - n_small), F32),
             seg(2), jnp.zeros((D, 8 - H), F32), seg(3), jnp.zeros((D, LANE - 8 - H), F32)], axis=1).astype(BF16)
        cw_t = conv_w[l].T
        gparams = jnp.zeros((2, LANE), F32).at[0, 8:8 + H].set(A_log[l]).at[1, 8:8 + H].set(dt_bias[l])
        wq = w_uq[l].reshape(QL, HM, NOPE + ROPE)
        wq_pe = wq[:, :, NOPE:].reshape(QL, HM * ROPE)
        wq_all = jnp.concatenate([wq[:, :, :NOPE].reshape(QL, HM * NOPE), wq_pe], axis=1).astype(BF16)
        wq_rot = _rotate_cols(wq_pe, ROPE).astype(BF16)
        wuk_all = jnp.transpose(w_uk[l], (1, 0, 2)).reshape(KVL, HM * NOPE).astype(BF16)
        wuv_all = jnp.transpose(w_uv[l], (1, 0, 2)).reshape(KVL, HM * VH).astype(BF16)
        qnw = q_norm_w[l].reshape(1, QL)
        kvnw = kv_norm_w[l].reshape(1, KVL)
        wog = w_o_gdn[l].astype(BF16)
        wom = w_o_mla[l].astype(BF16)
        wout = w_out[l].astype(BF16)
        wr = jnp.pad(w_router[l], ((0, 0), (0, LANE - E)))
        br = jnp.pad(b_router[l], (0, LANE - E)).reshape(1, LANE)
        nfw = norm_ffn_w[l].reshape(1, D)

        h1 = _norm_mod(hp, norm_mix_w[l], grp_p, 1, 0, BF16)
        qkv_p = _matmul(h1, w_qkv, BF16, 1024, 512)
        z_p = _matmul(h1, w_z, BF16, 1024, 512)
        gab_p = _matmul(h1, w_gab, BF16, 1024, 512)
        small_p = _matmul(h1, w_small, F32, 1024, w_small.shape[1])
        og_p, ssm_p = _gdn_prompt(qkv_p, z_p, small_p, ba_col, cw_t, gparams, gdn_norm_w[l], B, T, H, DK, DV)
        q_p, k_p, v_p, lat_p, pe_p = _mla_prep_prompt(small_p, cos_p, sin_p, qnw, kvnw, wq_all, wq_rot, wuk_all,
                                                      wuv_all, T, tm_p, HM, NOPE, ROPE, VH, QL, KVL, scale)
        om_p = _flash_attention(q_p, k_p, v_p, B, T, tm_p)
        x1_p, h2_p, te_p, tg_p = _post_mixer(hp, og_p, om_p, gab_p, grp_p, wog, wom, wout, nfw, wr, br, E)
        outs['kv_p'].append(lat_p.reshape(B, T, KVL))
        outs['pe_p'].append(pe_p.reshape(B, T, ROPE))
        outs['ssm_p'].append(ssm_p)
        outs['conv_p'].append(qkv_p.reshape(B, T, CONV)[:, T - (CONV_WIDTH - 1):, :].astype(F32))

        h1s = _norm_mod(hs, norm_mix_w[l], grp_s, 1, 0, BF16)
        qkv_s = _matmul(h1s, w_qkv, F32, Bs, 512)
        z_s = _matmul(h1s, w_z, BF16, Bs, 512)
        gab_s = _matmul(h1s, w_gab, BF16, Bs, 512)
        small_s = _matmul(h1s, w_small, F32, Bs, w_small.shape[1])
        og_s, ssm_s, conv_s = _gdn_decode(qkv_s, z_s, small_s, ba_col, state_conv[l], state_ssm[l], cw_t, gparams,
                                          gdn_norm_w[l], H, DK, DV)
        qn_s, qp_s, lat_s, pe_s = _mla_prep_sample(small_s, cos_s, sin_s, qnw, kvnw, wq_all, wq_rot,
                                                   HM, NOPE, ROPE, QL, KVL, scale)
        q_lat = jnp.transpose(_q_latent(qn_s, w_uk[l], scale), (1, 0, 2))
        o_lat = _paged_attention(q_lat, qp_s.reshape(n_s, HM, ROPE), lat_s, pe_s, cache_kv, cache_pe_t, page_table, l)
        om_s = _o_value(jnp.transpose(o_lat, (1, 0, 2)), w_uv[l])
        x1_s, h2_s, te_s, tg_s = _post_mixer(hs, og_s, om_s, gab_s, grp_s, wog, wom, wout, nfw, wr, br, E)
        outs['kv_s'].append(lat_s.reshape(Bs, Ts, KVL))
        outs['pe_s'].append(pe_s.reshape(Bs, Ts, ROPE))
        outs['ssm_s'].append(ssm_s)
        outs['conv_s'].append(conv_s)

        tm_e = 256
        h2 = jnp.concatenate([h2_p, h2_s], axis=0)
        top_e = jnp.concatenate([te_p[:, :TOP_K], te_s[:, :TOP_K]], axis=0)
        slots, pad_edges, blk_e, n_active, n_slots = _route(top_e, E, tm_e)
        h_sorted = _dispatch(h2, slots, pad_edges, n_slots, tb, tm_e)
        y_pad = _experts(h_sorted, blk_e, n_active, w_gu, b_gu, w_dn, b_dn, l, tm_e)
        if not last:
            raise NotImplementedError("stacked layers need an un-normalised residual output")
        hp = _combine_final(y_pad, slots[:n_p], tg_p, x1_p, grp_pc, norm_final_w)
        hs = _combine_final(y_pad, slots[n_p:], tg_s, x1_s, grp_s, norm_final_w)

    st = lambda k: jnp.stack(outs[k])
    return (hp.reshape(B, T, D), hs.reshape(Bs, Ts, D),
            st('kv_p'), st('pe_p'), st('ssm_p'), st('conv_p'),
            st('kv_s'), st('pe_s'), st('ssm_s'), st('conv_s'))
```

```python
import functools

import jax
import jax.numpy as jnp
from jax import lax
from jax.experimental import pallas as pl
from jax.experimental.pallas import tpu as pltpu

F32 = jnp.float32
BF16 = jnp.bfloat16
HI = lax.Precision.HIGHEST

NORM_EPS = 1e-6
ROPE_THETA = 10000.0
TOP_K = 4
SWIGLU_LIMIT = 7.0
SWIGLU_ALPHA = 1.702
CONV_WIDTH = 4
GDN_CHUNK = 64
PAGE_GROUP = 8
PAGE_REQUESTS = 4
LANE = 128
VMEM_LIMIT = 48 * 1024 * 1024
VMEM_LIMIT_MOE = 58 * 1024 * 1024

NT = (((1,), (1,)), ((), ()))
TN = (((0,), (0,)), ((), ()))


def _cp(sem, vmem=VMEM_LIMIT):
    return pltpu.CompilerParams(dimension_semantics=sem, vmem_limit_bytes=vmem)


def _rms(x):
    return x * lax.rsqrt(jnp.mean(x * x, axis=-1, keepdims=True) + NORM_EPS)


def _softplus(x):
    return jnp.maximum(x, 0.0) + jnp.log1p(jnp.exp(-jnp.abs(x)))


def _silu(x):
    return x * jax.nn.sigmoid(x)


def _bdot(a, b):
    return jnp.dot(a.astype(BF16), b.astype(BF16), preferred_element_type=F32)


def _bdot_g(a, b, dims):
    return lax.dot_general(a.astype(BF16), b.astype(BF16), dims, preferred_element_type=F32)


def _mm_kernel(*refs, has_bias):
    if has_bias:
        x_ref, w_ref, b_ref, o_ref = refs
    else:
        x_ref, w_ref, o_ref = refs
    acc = _bdot(x_ref[...], w_ref[...])
    if has_bias:
        acc = acc + b_ref[...]
    o_ref[...] = acc.astype(o_ref.dtype)


def _matmul(x, w, out_dtype, tm, tn, bias=None):
    M, K = x.shape
    N = w.shape[1]
    tm, tn = min(tm, M), min(tn, N)
    in_specs = [pl.BlockSpec((tm, K), lambda i, j: (i, 0)), pl.BlockSpec((K, tn), lambda i, j: (0, j))]
    args = [x, w]
    if bias is not None:
        in_specs.append(pl.BlockSpec((1, tn), lambda i, j: (0, j)))
        args.append(bias.reshape(1, N))
    return pl.pallas_call(
        functools.partial(_mm_kernel, has_bias=bias is not None),
        grid=(M // tm, N // tn), in_specs=in_specs,
        out_specs=pl.BlockSpec((tm, tn), lambda i, j: (i, j)),
        out_shape=jax.ShapeDtypeStruct((M, N), out_dtype),
        compiler_params=_cp(("parallel", "parallel")))(*args)


class _Group:
    def __init__(self, mod3, modf3, tm, blocks_per_g):
        self.mod3, self.modf3, self.tm, self.bpg = mod3, modf3, tm, blocks_per_g

    def spec(self, j, d):
        r, bpg = self.mod3.shape[1], self.bpg
        return pl.BlockSpec((None, r, d), lambda i: (i // bpg, 0, j))


def _norm_mod_kernel(x_ref, w_ref, sc_ref, sh_ref, o_ref):
    y = _rms(x_ref[...]) * w_ref[...]
    o_ref[...] = (y * (1.0 + sc_ref[...]) + sh_ref[...]).astype(o_ref.dtype)


def _norm_mod(x, w, grp, j_scale, j_shift, out_dtype):
    n, d = x.shape
    tm = grp.tm
    return pl.pallas_call(
        _norm_mod_kernel, grid=(n // tm,),
        in_specs=[pl.BlockSpec((tm, d), lambda i: (i, 0)), pl.BlockSpec((1, d), lambda i: (0, 0)),
                  grp.spec(j_scale, d), grp.spec(j_shift, d)],
        out_specs=pl.BlockSpec((tm, d), lambda i: (i, 0)),
        out_shape=jax.ShapeDtypeStruct((n, d), out_dtype),
        compiler_params=_cp(("parallel",)))(x, w.reshape(1, d), grp.mod3, grp.mod3)


SUBLANES = 8
PREP_CHUNKS = 2
SCAN_SEQS = 4
PREV_ROWS = 16


def _split_bf16(x):
    hi = x.astype(BF16)
    return hi, (x - hi.astype(F32)).astype(BF16)


def _dot3(a, b):
    d = lambda x, y: jnp.dot(x, y, preferred_element_type=F32)
    return d(a[0], b[0]) + d(a[0], b[1]) + d(a[1], b[0])


def _expand_matrix(C):
    k = jnp.arange(C)
    n = jnp.arange(SUBLANES * LANE)
    rem = n % LANE
    hit = ((k[:, None] // SUBLANES == rem[None, :] // SUBLANES) & (k[:, None] % SUBLANES == n[None, :] // LANE)
           & (rem[None, :] < C))
    return hit.astype(BF16)


def _unit_lower_inverses(a_list, at_list, g_ref, C):
    nh = len(a_list)
    row = lax.broadcasted_iota(jnp.int32, (C, C), 0)
    col = lax.broadcasted_iota(jnp.int32, (C, C), 1)
    blockdiag = (row // SUBLANES) == (col // SUBLANES)
    packed = []
    for at in at_list:
        m = jnp.where(blockdiag, at, 0.0)
        d = m[0:SUBLANES]
        for b in range(1, C // SUBLANES):
            d = d + m[SUBLANES * b:SUBLANES * (b + 1)]
        packed.append(d)
    stack = _split_bf16(jnp.concatenate(packed, axis=0))
    g = g_ref[...]
    coef = (jnp.dot(stack[0], g, preferred_element_type=F32) + jnp.dot(stack[1], g, preferred_element_type=F32))
    sub = lax.broadcasted_iota(jnp.int32, (SUBLANES, LANE), 0)
    lane = lax.broadcasted_iota(jnp.int32, (SUBLANES, LANE), 1)
    unit = jnp.where((lane % SUBLANES == sub) & (lane < C), 1.0, 0.0)
    xd = [unit] * nh
    for i in range(1, SUBLANES):
        e_i = jnp.where((lane % SUBLANES == i) & (lane < C), 1.0, 0.0)[0:1]
        for h in range(nh):
            c_i = coef[SUBLANES * h:SUBLANES * (h + 1), LANE * i:LANE * (i + 1)]
            new_row = e_i - jnp.sum(c_i * xd[h], axis=0, keepdims=True)
            xd[h] = jnp.where(sub == i, new_row, xd[h])
    x = [jnp.where(blockdiag, jnp.concatenate([xd[h][:, :C]] * (C // SUBLANES), axis=0), 0.0) for h in range(nh)]
    a16 = [a.astype(BF16) for a in a_list]
    zero = jnp.zeros((C, C), BF16)
    s = SUBLANES
    while s < C:
        below = ((row // s) % 2 == 1) & ((col // s) == (row // s) - 1)
        x16 = [x[h].astype(BF16) for h in range(nh)]
        m1 = [jnp.dot(x16[h], jnp.where(below, a16[h], zero), preferred_element_type=F32) for h in range(nh)]
        x = [x[h] - jnp.dot(m1[h].astype(BF16), x16[h], preferred_element_type=F32) for h in range(nh)]
        s *= 2
    return x


def _gdn_prep_kernel(qkv_ref, prev_ref, sm_ref, cw_ref, gp_ref, g_ref,
                     u_ref, w_ref, qe_ref, kd_ref, aqk_ref, dec_ref, *, H, DK, DV, C):
    c = pl.program_id(1)
    QK = H * DK
    P = PREV_ROWS
    R = qkv_ref.shape[0]

    prev = prev_ref[...]
    xin = jnp.concatenate([jnp.where(c == 0, jnp.zeros_like(prev), prev), qkv_ref[...]], axis=0)
    out_row = lax.broadcasted_iota(jnp.int32, (R, P + R), 0)
    in_row = lax.broadcasted_iota(jnp.int32, (R, P + R), 1)
    y = None
    for j in range(CONV_WIDTH):
        shift = (in_row == out_row + (P - (CONV_WIDTH - 1) + j)).astype(BF16)
        term = jnp.dot(shift, xin, preferred_element_type=F32) * cw_ref[j:j + 1, :]
        y = term if y is None else y + term
    y = _silu(y)

    sm = sm_ref[...]
    beta_all = jax.nn.sigmoid(sm)
    g_all = -jnp.exp(gp_ref[0:1, :]) * _softplus(sm + gp_ref[1:2, :])
    row_r = lax.broadcasted_iota(jnp.int32, (R, R), 0)
    col_r = lax.broadcasted_iota(jnp.int32, (R, R), 1)
    tril = ((col_r <= row_r) & (col_r // C == row_r // C)).astype(F32)
    gam_all = jnp.dot(tril, g_all, precision=HI, preferred_element_type=F32)
    lane = lax.broadcasted_iota(jnp.int32, sm.shape, 1)
    packed = jnp.where(lane < 8, beta_all, gam_all)
    row = lax.broadcasted_iota(jnp.int32, (C, C), 0)
    col = lax.broadcasted_iota(jnp.int32, (C, C), 1)

    a_list, at_list, rhs_list, places = [], [], [], []
    for ci, h in [(ci, h) for ci in range(R // C) for h in range(H)]:
        rs = slice(ci * C, (ci + 1) * C)
        rows_t = packed[rs].T
        qh = y[rs, h * DK:(h + 1) * DK]
        kh = y[rs, QK + h * DK:QK + (h + 1) * DK]
        vh = y[rs, 2 * QK + h * DV:2 * QK + (h + 1) * DV]
        qh = qh * lax.rsqrt(jnp.sum(qh * qh, axis=-1, keepdims=True) + NORM_EPS) * (DK ** -0.5)
        kh = kh * lax.rsqrt(jnp.sum(kh * kh, axis=-1, keepdims=True) + NORM_EPS)
        beta_c = beta_all[rs, h:h + 1]
        gam_c = gam_all[rs, 8 + h:9 + h]
        beta_r = rows_t[h:h + 1, :]
        gam_r = rows_t[8 + h:9 + h, :]
        dm = gam_c - gam_r
        decay = jnp.exp(jnp.where(col <= row, dm, -jnp.inf))
        decay_t = jnp.exp(jnp.where(row < col, -dm, -jnp.inf))
        qk_kk = _bdot_g(jnp.concatenate([qh, kh], axis=0), kh, NT)
        kk = qk_kk[C:]
        a_list.append(jnp.where(col < row, kk * beta_c * decay, 0.0))
        at_list.append(kk * beta_r * decay_t)
        egam = jnp.exp(gam_c)
        rhs_list.append(jnp.concatenate([beta_c * vh, beta_c * egam * kh], axis=1).astype(BF16))
        g_last = gam_c[C - 1:C, :]
        qe_ref[rs, h * DK:(h + 1) * DK] = (qh * egam).astype(qe_ref.dtype)
        kd_ref[rs, h * DK:(h + 1) * DK] = (kh * jnp.exp(g_last - gam_c)).astype(kd_ref.dtype)
        aqk_ref[rs, h * C:(h + 1) * C] = (qk_kk[:C] * decay).astype(aqk_ref.dtype)
        dec_ref[ci, h:h + 1, :] = jnp.broadcast_to(jnp.exp(g_last), (1, LANE))
        places.append((rs, h))

    t_inv = _unit_lower_inverses(a_list, at_list, g_ref, C)
    for i, (rs, h) in enumerate(places):
        t_hi, t_lo = _split_bf16(t_inv[i])
        uw = (jnp.dot(t_hi, rhs_list[i], preferred_element_type=F32)
              + jnp.dot(t_lo, rhs_list[i], preferred_element_type=F32))
        u_ref[rs, h * DV:(h + 1) * DV] = uw[:, :DV].astype(u_ref.dtype)
        w_ref[rs, h * DK:(h + 1) * DK] = uw[:, DV:].astype(w_ref.dtype)


def _gdn_scan_kernel(u_ref, w_ref, qe_ref, kd_ref, aqk_ref, dec_ref, z_ref, nw_ref, og_ref, s_out_ref, s_ref,
                     *, H, DK, DV, C):
    c = pl.program_id(1)

    @pl.when(c == 0)
    def _():
        s_ref[...] = jnp.zeros_like(s_ref)

    ch = [(g, h) for g in range(u_ref.shape[0]) for h in range(H)]
    ks = [slice(h * DK, (h + 1) * DK) for h in range(H)]
    vs = [slice(h * DV, (h + 1) * DV) for h in range(H)]
    s16 = [s_ref[g, h].astype(BF16) for g, h in ch]
    ws = [jnp.dot(w_ref[g, :, ks[h]], s16[i], preferred_element_type=F32) for i, (g, h) in enumerate(ch)]
    qs = [jnp.dot(qe_ref[g, :, ks[h]], s16[i], preferred_element_type=F32) for i, (g, h) in enumerate(ch)]
    v16 = [(u_ref[g, :, vs[h]].astype(F32) - ws[i]).astype(BF16) for i, (g, h) in enumerate(ch)]
    ds = [lax.dot_general(kd_ref[g, :, ks[h]], v16[i], TN, preferred_element_type=F32) for i, (g, h) in enumerate(ch)]
    o = [qs[i] + jnp.dot(aqk_ref[g, :, h * C:(h + 1) * C], v16[i], preferred_element_type=F32)
         for i, (g, h) in enumerate(ch)]
    for i, (g, h) in enumerate(ch):
        s_ref[g, h] = s_ref[g, h] * dec_ref[g, 0, h:h + 1, :] + ds[i]
        zh = z_ref[g, :, vs[h]].astype(F32)
        og_ref[g, :, vs[h]] = (_rms(o[i]) * nw_ref[...] * _silu(zh)).astype(og_ref.dtype)

    @pl.when(c == pl.num_programs(1) - 1)
    def _():
        s_out_ref[...] = s_ref[...]


def _gdn_prompt(qkv, z, small, ba_col, cw_t, gparams, norm_w, B, T, H, DK, DV):
    C = GDN_CHUNK
    nc = T // C
    n, W = qkv.shape
    kw = dict(H=H, DK=DK, DV=DV, C=C)
    pc = PREP_CHUNKS if nc % PREP_CHUNKS == 0 else 1
    ns = nc // pc
    R = pc * C
    blk = lambda w: pl.BlockSpec((R, w), lambda b, c: (b * ns + c, 0))
    ppc = R // PREV_ROWS
    u, w, qe, kd, aqk, dec = pl.pallas_call(
        functools.partial(_gdn_prep_kernel, **kw), grid=(B, ns),
        in_specs=[blk(W),
                  pl.BlockSpec((PREV_ROWS, W), lambda b, c: (jnp.maximum((b * ns + c) * ppc - 1, 0), 0)),
                  pl.BlockSpec((R, LANE), lambda b, c: (b * ns + c, ba_col)),
                  pl.BlockSpec((CONV_WIDTH, W), lambda b, c: (0, 0)),
                  pl.BlockSpec((2, LANE), lambda b, c: (0, 0)),
                  pl.BlockSpec((C, SUBLANES * LANE), lambda b, c: (0, 0))],
        out_specs=[blk(H * DV), blk(H * DK), blk(H * DK), blk(H * DK), blk(H * C),
                   pl.BlockSpec((pc, SUBLANES, LANE), lambda b, c: (b * ns + c, 0, 0))],
        out_shape=[jax.ShapeDtypeStruct((n, H * DV), BF16), jax.ShapeDtypeStruct((n, H * DK), BF16),
                   jax.ShapeDtypeStruct((n, H * DK), BF16), jax.ShapeDtypeStruct((n, H * DK), BF16),
                   jax.ShapeDtypeStruct((n, H * C), BF16), jax.ShapeDtypeStruct((B * nc, SUBLANES, LANE), F32)],
        compiler_params=_cp(("parallel", "parallel")))(qkv, qkv, small, cw_t, gparams, _expand_matrix(C))
    nb = SCAN_SEQS if B % SCAN_SEQS == 0 else 1
    seq = lambda a: a.reshape(B, T, a.shape[1])
    sblk = lambda w: pl.BlockSpec((nb, C, w), lambda b, c: (b, c, 0))
    og, s_fin = pl.pallas_call(
        functools.partial(_gdn_scan_kernel, **kw), grid=(B // nb, nc),
        in_specs=[sblk(H * DV), sblk(H * DK), sblk(H * DK), sblk(H * DK), sblk(H * C),
                  pl.BlockSpec((nb, 1, SUBLANES, LANE), lambda b, c: (b, c, 0, 0)),
                  sblk(H * DV), pl.BlockSpec((1, DV), lambda b, c: (0, 0))],
        out_specs=[sblk(H * DV), pl.BlockSpec((nb, H, DK, DV), lambda b, c: (b, 0, 0, 0))],
        out_shape=[jax.ShapeDtypeStruct((B, T, H * DV), BF16), jax.ShapeDtypeStruct((B, H, DK, DV), F32)],
        scratch_shapes=[pltpu.VMEM((nb, H, DK, DV), F32)],
        compiler_params=_cp(("parallel", "arbitrary")))(
            seq(u), seq(w), seq(qe), seq(kd), seq(aqk), dec.reshape(B, nc, SUBLANES, LANE), seq(z),
            norm_w.reshape(1, DV))
    return og.reshape(n, H * DV), s_fin


DECODE_TOKENS = 1


def _gdn_decode_kernel(u_ref, z_ref, ba_ref, buf_ref, s_in_ref, cw_ref, gp_ref, nw_ref,
                       og_ref, s_out_ref, buf_out_ref, *, H, DK, DV):
    for t in range(u_ref.shape[0]):
        _gdn_decode_token(u_ref.at[t], z_ref.at[t], ba_ref.at[t], buf_ref.at[t], s_in_ref.at[t], cw_ref, gp_ref,
                          nw_ref, og_ref.at[t], s_out_ref.at[t], buf_out_ref.at[t], H=H, DK=DK, DV=DV)


def _gdn_decode_token(u_ref, z_ref, ba_ref, buf_ref, s_in_ref, cw_ref, gp_ref, nw_ref,
                      og_ref, s_out_ref, buf_out_ref, *, H, DK, DV):
    QK = H * DK
    u = u_ref[...]
    buf = buf_ref[...]
    y = buf[0:1] * cw_ref[0:1, :]
    y = y + buf[1:2] * cw_ref[1:2, :]
    y = y + buf[2:3] * cw_ref[2:3, :]
    y = y + u * cw_ref[3:4, :]
    buf_out_ref[0:2, :] = buf[1:3]
    buf_out_ref[2:3, :] = u
    y = _silu(y)
    sm = ba_ref[...]
    beta_all = jax.nn.sigmoid(sm)
    g_all = -jnp.exp(gp_ref[0:1, :]) * _softplus(sm + gp_ref[1:2, :])
    rows = []
    for h in range(H):
        kh = y[:, QK + h * DK:QK + (h + 1) * DK]
        rows.append(kh * lax.rsqrt(jnp.sum(kh * kh, axis=-1, keepdims=True) + NORM_EPS))
    for h in range(H):
        qh = y[:, h * DK:(h + 1) * DK]
        rows.append(qh * lax.rsqrt(jnp.sum(qh * qh, axis=-1, keepdims=True) + NORM_EPS) * (DK ** -0.5))
    rows.append(jnp.zeros((DK - 2 * H, DK), F32))
    cols = jnp.concatenate(rows, axis=0).T
    hs = range(H)
    kcol = [cols[:, h:h + 1] for h in hs]
    qcol = [cols[:, H + h:H + h + 1] for h in hs]
    s_dec = [s_in_ref[h] * jnp.exp(g_all[:, 8 + h:9 + h]) for h in hs]
    ks = [jnp.sum(kcol[h] * s_dec[h], axis=0, keepdims=True) for h in hs]
    v_new = [beta_all[:, h:h + 1] * (y[:, 2 * QK + h * DV:2 * QK + (h + 1) * DV] - ks[h]) for h in hs]
    s_new = [s_dec[h] + kcol[h] * v_new[h] for h in hs]
    o = [jnp.sum(qcol[h] * s_new[h], axis=0, keepdims=True) for h in hs]
    for h in hs:
        s_out_ref[h] = s_new[h]
        zh = z_ref[:, h * DV:(h + 1) * DV].astype(F32)
        og_ref[:, h * DV:(h + 1) * DV] = (_rms(o[h]) * nw_ref[...] * _silu(zh)).astype(og_ref.dtype)


def _gdn_decode(qkv, z, small, ba_col, conv_buf, ssm, cw_t, gparams, norm_w, H, DK, DV):
    Bs, W = qkv.shape
    kern = functools.partial(_gdn_decode_kernel, H=H, DK=DK, DV=DV)
    nt = min(DECODE_TOKENS, Bs)
    og, s_new, buf_new = pl.pallas_call(
        kern, grid=(Bs // nt,),
        in_specs=[pl.BlockSpec((nt, 1, W), lambda b: (b, 0, 0)),
                  pl.BlockSpec((nt, 1, H * DV), lambda b: (b, 0, 0)),
                  pl.BlockSpec((nt, 1, LANE), lambda b: (b, 0, ba_col)),
                  pl.BlockSpec((nt, CONV_WIDTH - 1, W), lambda b: (b, 0, 0)),
                  pl.BlockSpec((nt, H, DK, DV), lambda b: (b, 0, 0, 0)),
                  pl.BlockSpec((CONV_WIDTH, W), lambda b: (0, 0)),
                  pl.BlockSpec((2, LANE), lambda b: (0, 0)),
                  pl.BlockSpec((1, DV), lambda b: (0, 0))],
        out_specs=[pl.BlockSpec((nt, 1, H * DV), lambda b: (b, 0, 0)),
                   pl.BlockSpec((nt, H, DK, DV), lambda b: (b, 0, 0, 0)),
                   pl.BlockSpec((nt, CONV_WIDTH - 1, W), lambda b: (b, 0, 0))],
        out_shape=[jax.ShapeDtypeStruct((Bs, 1, H * DV), BF16),
                   jax.ShapeDtypeStruct(ssm.shape, F32),
                   jax.ShapeDtypeStruct(conv_buf.shape, F32)],
        compiler_params=_cp(("parallel",)))(
            qkv.reshape(Bs, 1, W), z.reshape(Bs, 1, H * DV), small.reshape(Bs, 1, small.shape[1]),
            conv_buf, ssm, cw_t, gparams, norm_w.reshape(1, DV))
    return og.reshape(Bs, H * DV), s_new, buf_new


def _mla_prep_prompt_kernel(sm_ref, cos_ref, sin_ref, cost_ref, sint_ref, qnw_ref, kvnw_ref,
                            wqt_ref, wqrt_ref, wuk_ref, wuvt_ref,
                            qt_ref, k_ref, vt_ref, lat_ref, pe_ref, *, H, NOPE, ROPE, VH, QL, KVL, scale):
    sm = sm_ref[...]
    qn = (_rms(sm[:, :QL]) * qnw_ref[...]).astype(BF16)
    qft = lax.dot_general(wqt_ref[...], qn, NT, preferred_element_type=F32)
    qrt = lax.dot_general(wqrt_ref[...], qn, NT, preferred_element_type=F32)
    qpt = qft[H * NOPE:, :] * cost_ref[...] + qrt * sint_ref[...]
    lat = _rms(sm[:, QL:QL + KVL]) * kvnw_ref[...]
    lat_ref[...] = lat
    lat16 = lat.astype(BF16)
    kn = jnp.dot(lat16, wuk_ref[...], preferred_element_type=F32)
    vt = lax.dot_general(wuvt_ref[...], lat16, NT, preferred_element_type=F32)
    o = QL + KVL
    kr = sm[:, o:o + ROPE] * cos_ref[...] + sm[:, o + ROPE:o + 2 * ROPE] * sin_ref[...]
    pe_ref[...] = kr
    for h in range(H):
        qt_ref[h] = (jnp.concatenate([qft[h * NOPE:(h + 1) * NOPE, :], qpt[h * ROPE:(h + 1) * ROPE, :]], axis=0)
                     * scale).astype(qt_ref.dtype)
        k_ref[h] = jnp.concatenate([kn[:, h * NOPE:(h + 1) * NOPE], kr], axis=-1).astype(k_ref.dtype)
        vt_ref[h, 0] = vt[h * VH:(h + 1) * VH, :].astype(vt_ref.dtype)


def _mla_prep_prompt(small, cos, sin, qnw, kvnw, wq, wqr, wuk, wuv, T, tm, H, NOPE, ROPE, VH, QL, KVL, scale):
    n, ws = small.shape
    nt = T // tm
    kern = functools.partial(_mla_prep_prompt_kernel, H=H, NOPE=NOPE, ROPE=ROPE, VH=VH, QL=QL, KVL=KVL, scale=scale)
    full = lambda a: pl.BlockSpec(a.shape, lambda i: (0,) * a.ndim)
    dqk = NOPE + ROPE
    cos_t = jnp.tile(cos.T, (H, 1))
    sin_t = jnp.tile(sin.T, (H, 1))
    wqt, wqrt, wuvt = wq.T, wqr.T, wuv.T
    return pl.pallas_call(
        kern, grid=(n // tm,),
        in_specs=[pl.BlockSpec((tm, ws), lambda i: (i, 0)),
                  pl.BlockSpec((tm, ROPE), lambda i: (i % nt, 0)),
                  pl.BlockSpec((tm, ROPE), lambda i: (i % nt, 0)),
                  pl.BlockSpec((H * ROPE, tm), lambda i: (0, i % nt)),
                  pl.BlockSpec((H * ROPE, tm), lambda i: (0, i % nt)),
                  full(qnw), full(kvnw), full(wqt), full(wqrt), full(wuk), full(wuvt)],
        out_specs=[pl.BlockSpec((H, dqk, tm), lambda i: (0, 0, i)),
                   pl.BlockSpec((H, tm, dqk), lambda i: (0, i, 0)),
                   pl.BlockSpec((H, 1, VH, tm), lambda i: (0, i, 0, 0)),
                   pl.BlockSpec((tm, KVL), lambda i: (i, 0)),
                   pl.BlockSpec((tm, ROPE), lambda i: (i, 0))],
        out_shape=[jax.ShapeDtypeStruct((H, dqk, n), BF16), jax.ShapeDtypeStruct((H, n, dqk), BF16),
                   jax.ShapeDtypeStruct((H, n // tm, VH, tm), BF16), jax.ShapeDtypeStruct((n, KVL), F32),
                   jax.ShapeDtypeStruct((n, ROPE), F32)],
        compiler_params=_cp(("parallel",)))(small, cos, sin, cos_t, sin_t, qnw, kvnw, wqt, wqrt, wuk, wuvt)


def _mla_prep_sample_kernel(sm_ref, cos_ref, sin_ref, qnw_ref, kvnw_ref, wq_ref, wqr_ref,
                            qn_ref, qp_ref, lat_ref, pe_ref, *, H, NOPE, ROPE, QL, KVL, scale):
    sm = sm_ref[...]
    cos = cos_ref[...]
    sin = sin_ref[...]
    qn = _rms(sm[:, :QL]) * qnw_ref[...]
    qf = _bdot(qn, wq_ref[...])
    qr = _bdot(qn, wqr_ref[...])
    qn_ref[...] = qf[:, :H * NOPE].astype(qn_ref.dtype)
    lat_ref[...] = _rms(sm[:, QL:QL + KVL]) * kvnw_ref[...]
    o = QL + KVL
    pe_ref[...] = sm[:, o:o + ROPE] * cos + sm[:, o + ROPE:o + 2 * ROPE] * sin
    for h in range(H):
        p0 = H * NOPE + h * ROPE
        qp = qf[:, p0:p0 + ROPE] * cos + qr[:, h * ROPE:(h + 1) * ROPE] * sin
        qp_ref[:, h * ROPE:(h + 1) * ROPE] = (qp * scale).astype(qp_ref.dtype)


def _mla_prep_sample(small, cos, sin, qnw, kvnw, wq, wqr, H, NOPE, ROPE, QL, KVL, scale):
    n, ws = small.shape
    kern = functools.partial(_mla_prep_sample_kernel, H=H, NOPE=NOPE, ROPE=ROPE, QL=QL, KVL=KVL, scale=scale)
    full = lambda a: pl.BlockSpec(a.shape, lambda i: (0,) * a.ndim)
    return pl.pallas_call(
        kern, grid=(1,),
        in_specs=[full(small), full(cos), full(sin), full(qnw), full(kvnw), full(wq), full(wqr)],
        out_specs=[pl.BlockSpec((n, H * NOPE), lambda i: (0, 0)), pl.BlockSpec((n, H * ROPE), lambda i: (0, 0)),
                   pl.BlockSpec((n, KVL), lambda i: (0, 0)), pl.BlockSpec((n, ROPE), lambda i: (0, 0))],
        out_shape=[jax.ShapeDtypeStruct((n, H * NOPE), BF16), jax.ShapeDtypeStruct((n, H * ROPE), BF16),
                   jax.ShapeDtypeStruct((n, KVL), F32), jax.ShapeDtypeStruct((n, ROPE), F32)],
        compiler_params=_cp(("arbitrary",)))(small, cos, sin, qnw, kvnw, wq, wqr)


def _head_proj_kernel(x_ref, w_ref, o_ref, *, dims, scale):
    o_ref[...] = (_bdot_g(x_ref[...], w_ref[...], dims) * scale).astype(o_ref.dtype)


def _q_latent(q_nope, w_uk, scale):
    H, KVL, NOPE = w_uk.shape
    n = q_nope.shape[0]
    return pl.pallas_call(
        functools.partial(_head_proj_kernel, dims=NT, scale=scale), grid=(H,),
        in_specs=[pl.BlockSpec((n, NOPE), lambda h: (0, h)), pl.BlockSpec((None, KVL, NOPE), lambda h: (h, 0, 0))],
        out_specs=pl.BlockSpec((None, n, KVL), lambda h: (h, 0, 0)),
        out_shape=jax.ShapeDtypeStruct((H, n, KVL), BF16),
        compiler_params=_cp(("parallel",)))(q_nope, w_uk)


def _o_value(o_lat, w_uv):
    H, KVL, VH = w_uv.shape
    n = o_lat.shape[1]
    return pl.pallas_call(
        functools.partial(_head_proj_kernel, dims=(((1,), (0,)), ((), ())), scale=1.0), grid=(H,),
        in_specs=[pl.BlockSpec((None, n, KVL), lambda h: (h, 0, 0)), pl.BlockSpec((None, KVL, VH), lambda h: (h, 0, 0))],
        out_specs=pl.BlockSpec((n, VH), lambda h: (0, h)),
        out_shape=jax.ShapeDtypeStruct((n, H * VH), BF16),
        compiler_params=_cp(("parallel",)))(o_lat, w_uv)


FLASH_SPLIT = 2


FLASH_HEADS = 4


def _flash_kernel(qt_ref, k_ref, vt_ref, o_ref, m_ref, l_ref, acc_ref, *, tq):
    qi = pl.program_id(2)
    nh = qt_ref.shape[0]
    ns = FLASH_SPLIT
    hq = tq // ns
    vh = vt_ref.shape[2]
    chains = [(g, t) for g in range(nh) for t in range(ns)]
    m_ref[...] = jnp.full_like(m_ref, -jnp.inf)
    l_ref[...] = jnp.zeros_like(l_ref)
    acc_ref[...] = jnp.zeros_like(acc_ref)
    qt = [qt_ref[g, :, t * hq:(t + 1) * hq] for g, t in chains]

    def scores(off):
        k = [k_ref[g, pl.ds(off, tq), :] for g in range(nh)]
        return [jnp.dot(k[g], qt[c], preferred_element_type=F32) for c, (g, t) in enumerate(chains)]

    def update(st, j):
        cs = range(len(chains))
        vt = [vt_ref[g, j] for g in range(nh)]
        m_prev = [m_ref[c] for c in cs]
        m_new = [jnp.maximum(m_prev[c], jnp.max(st[c], axis=0, keepdims=True)) for c in cs]
        p = [jnp.exp(st[c] - m_new[c]) for c in cs]
        corr = [jnp.exp(m_prev[c] - m_new[c]) for c in cs]
        pv = [jnp.dot(vt[chains[c][0]], p[c].astype(BF16), preferred_element_type=F32) for c in cs]
        for c in cs:
            m_ref[c] = m_new[c]
            l_ref[c] = corr[c] * l_ref[c] + jnp.sum(p[c], axis=0, keepdims=True)
            acc_ref[c] = acc_ref[c] * corr[c] + pv[c]

    def below_diagonal(j, carry):
        update(scores(pl.multiple_of(j * tq, tq)), j)
        return carry

    lax.fori_loop(0, qi, below_diagonal, 0)

    st = scores(pl.multiple_of(qi * tq, tq))
    for c, (g, t) in enumerate(chains):
        key = lax.broadcasted_iota(jnp.int32, st[c].shape, 0)
        qry = t * hq + lax.broadcasted_iota(jnp.int32, st[c].shape, 1)
        st[c] = jnp.where(key <= qry, st[c], -jnp.inf)
    update(st, qi)
    for c, (g, t) in enumerate(chains):
        o_ref[t * hq:(t + 1) * hq, g * vh:(g + 1) * vh] = (acc_ref[c] / l_ref[c]).T.astype(o_ref.dtype)


def _flash_attention(qt, k, vt, B, T, tq):
    H, dqk, n = qt.shape
    VH = vt.shape[2]
    nq = T // tq
    hq = tq // FLASH_SPLIT
    nh = FLASH_HEADS
    nc = nh * FLASH_SPLIT
    kern = functools.partial(_flash_kernel, tq=tq)
    return pl.pallas_call(
        kern, grid=(B, H // nh, nq),
        in_specs=[pl.BlockSpec((nh, dqk, tq), lambda b, h, i: (h, 0, b * nq + i)),
                  pl.BlockSpec((nh, T, dqk), lambda b, h, i: (h, b, 0)),
                  pl.BlockSpec((nh, nq, VH, tq), lambda b, h, i: (h, b, 0, 0))],
        out_specs=pl.BlockSpec((tq, nh * VH), lambda b, h, i: (b * nq + i, h)),
        out_shape=jax.ShapeDtypeStruct((n, H * VH), BF16),
        scratch_shapes=[pltpu.VMEM((nc, 1, hq), F32), pltpu.VMEM((nc, 1, hq), F32), pltpu.VMEM((nc, VH, hq), F32)],
        compiler_params=_cp(("parallel", "parallel", "arbitrary")))(qt, k, vt)


def _paged_kernel(pt_ref, ql_ref, qp_ref, latn_ref, pen_ref, kv_hbm, pe_hbm, o_ref,
                  kv_buf, pe_buf, sem, m_ref, l_ref, acc_ref, *, R, G, layer):
    b = pl.program_id(0)
    g = pl.program_id(1)
    ng = pl.num_programs(1)
    step = b * ng + g
    slot = step % 2

    def page_copies(bb, gg, sl, real_pages):
        out = []
        for r in range(R):
            for i in range(G):
                page = pt_ref[bb * R + r, gg * G + i] if real_pages else 0
                out.append(pltpu.make_async_copy(kv_hbm.at[layer, page], kv_buf.at[sl, r * G + i], sem.at[0, sl]))
                out.append(pltpu.make_async_copy(pe_hbm.at[layer, page], pe_buf.at[sl, r * G + i], sem.at[1, sl]))
        return out

    @pl.when(step == 0)
    def _():
        for cp in page_copies(0, 0, 0, True):
            cp.start()

    @pl.when(step + 1 < pl.num_programs(0) * ng)
    def _():
        nxt = step + 1
        for cp in page_copies(nxt // ng, nxt % ng, 1 - slot, True):
            cp.start()

    for cp in page_copies(b, g, slot, False):
        cp.wait()

    @pl.when(g == 0)
    def _():
        m_ref[...] = jnp.full_like(m_ref, -jnp.inf)
        l_ref[...] = jnp.zeros_like(l_ref)
        acc_ref[...] = jnp.zeros_like(acc_ref)

    kvs = [[kv_buf[slot, r * G + i].astype(BF16) for i in range(G)] for r in range(R)]
    P = kvs[0][0].shape[0]
    s, p, corr, m_new = [], [], [], []
    for r in range(R):
        ql = ql_ref[r]
        qp = qp_ref[r]
        s.append(jnp.concatenate(
            [lax.dot_general(ql, kvs[r][i], NT, preferred_element_type=F32)
             + jnp.dot(qp, pe_buf[slot, r * G + i].astype(BF16), preferred_element_type=F32) for i in range(G)],
            axis=-1))
    for r in range(R):
        m_prev = m_ref[r]
        m_new.append(jnp.maximum(m_prev, jnp.max(s[r], axis=-1, keepdims=True)))
        corr.append(jnp.exp(m_prev - m_new[r]))
        p.append(jnp.exp(s[r] - m_new[r]))
    pv = []
    for r in range(R):
        t = None
        for i in range(G):
            d = jnp.dot(p[r][:, i * P:(i + 1) * P].astype(BF16), kvs[r][i], preferred_element_type=F32)
            t = d if t is None else t + d
        pv.append(t)
    for r in range(R):
        m_ref[r] = m_new[r]
        l_ref[r] = corr[r] * l_ref[r] + jnp.sum(p[r], axis=-1, keepdims=True)
        acc_ref[r] = acc_ref[r] * corr[r] + pv[r]

    @pl.when(g == pl.num_programs(1) - 1)
    def _():
        for r in range(R):
            latn = latn_ref[r]
            s_n = (jnp.sum(ql_ref[r].astype(F32) * latn, axis=-1, keepdims=True)
                   + jnp.sum(qp_ref[r].astype(F32) * pen_ref[r], axis=-1, keepdims=True))
            m_old = m_ref[r]
            m2 = jnp.maximum(m_old, s_n)
            c2 = jnp.exp(m_old - m2)
            p2 = jnp.exp(s_n - m2)
            o_ref[r] = ((acc_ref[r] * c2 + p2 * latn) / (l_ref[r] * c2 + p2)).astype(o_ref.dtype)


def _paged_attention(q_lat, q_pe, lat_new, pe_new, cache_kv, cache_pe_t, page_table, layer):
    Bs, H, KVL = q_lat.shape
    ROPE = q_pe.shape[2]
    n_pages = page_table.shape[1]
    P = cache_kv.shape[2]
    G = min(PAGE_GROUP, n_pages)
    R = min(PAGE_REQUESTS, Bs)

    req = lambda rows, w: pl.BlockSpec((R, rows, w), lambda b, g, pt: (b, 0, 0))
    hbm = pl.BlockSpec(memory_space=pl.ANY)
    grid_spec = pltpu.PrefetchScalarGridSpec(
        num_scalar_prefetch=1, grid=(Bs // R, n_pages // G),
        in_specs=[req(H, KVL), req(H, ROPE), req(1, KVL), req(1, ROPE), hbm, hbm],
        out_specs=req(H, KVL),
        scratch_shapes=[pltpu.VMEM((2, R * G, P, KVL), cache_kv.dtype), pltpu.VMEM((2, R * G, ROPE, P), cache_pe_t.dtype),
                        pltpu.SemaphoreType.DMA((2, 2)),
                        pltpu.VMEM((R, H, 1), F32), pltpu.VMEM((R, H, 1), F32), pltpu.VMEM((R, H, KVL), F32)])
    return pl.pallas_call(
        functools.partial(_paged_kernel, R=R, G=G, layer=layer), grid_spec=grid_spec,
        out_shape=jax.ShapeDtypeStruct((Bs, H, KVL), BF16),
        compiler_params=_cp(("arbitrary", "arbitrary")))(
            page_table, q_lat, q_pe, lat_new.reshape(Bs, 1, KVL), pe_new.reshape(Bs, 1, ROPE), cache_kv, cache_pe_t)


def _post_mixer_kernel(x_ref, og_ref, om_ref, gab_ref, g1_ref, sc2_ref, sh2_ref, wog_ref, wom_ref, wout_ref,
                       nw_ref, wr_ref, br_ref, x1_ref, h2_ref, te_ref, tg_ref, *, D, E):
    gab = gab_ref[...].astype(F32)
    merged = (jax.nn.sigmoid(gab[:, :D]) * _bdot(og_ref[...], wog_ref[...])
              + jax.nn.sigmoid(gab[:, D:]) * _bdot(om_ref[...], wom_ref[...]))
    x1 = x_ref[...] + g1_ref[...] * _bdot(merged, wout_ref[...])
    x1_ref[...] = x1
    h2 = _rms(x1) * nw_ref[...] * (1.0 + sc2_ref[...]) + sh2_ref[...]
    h2_ref[...] = h2
    logits = _dot3(_split_bf16(h2), _split_bf16(wr_ref[...])) + br_ref[...]
    lane = lax.broadcasted_iota(jnp.int32, logits.shape, 1)
    logits = jnp.where(lane < E, logits, -jnp.inf)
    te = jnp.zeros(logits.shape, jnp.int32)
    ex = jnp.zeros(logits.shape, F32)
    top = None
    for k in range(TOP_K):
        m = jnp.max(logits, axis=-1, keepdims=True)
        idx = jnp.min(jnp.where(logits == m, lane, LANE), axis=-1, keepdims=True)
        top = m if top is None else top
        te = jnp.where(lane == k, idx, te)
        ex = jnp.where(lane == k, jnp.exp(m - top), ex)
        logits = jnp.where(lane == idx, -jnp.inf, logits)
    te_ref[...] = te
    tg_ref[...] = ex / jnp.sum(ex, axis=-1, keepdims=True)


def _post_mixer(x, og, om, gab, grp, wog, wom, wout, norm_w, wr, br, E):
    n, d = x.shape
    tm = grp.tm
    rowblk = lambda w: pl.BlockSpec((tm, w), lambda i: (i, 0))
    full = lambda a: pl.BlockSpec(a.shape, lambda i: (0,) * a.ndim)
    return pl.pallas_call(
        functools.partial(_post_mixer_kernel, D=d, E=E), grid=(n // tm,),
        in_specs=[rowblk(d), rowblk(d), rowblk(d), rowblk(2 * d), grp.spec(2, d), grp.spec(4, d), grp.spec(3, d),
                  full(wog), full(wom), full(wout), full(norm_w), full(wr), full(br)],
        out_specs=[rowblk(d), rowblk(d), rowblk(LANE), rowblk(LANE)],
        out_shape=[jax.ShapeDtypeStruct((n, d), F32), jax.ShapeDtypeStruct((n, d), F32),
                   jax.ShapeDtypeStruct((n, LANE), jnp.int32), jax.ShapeDtypeStruct((n, LANE), F32)],
        compiler_params=_cp(("parallel",)))(x, og, om, gab, grp.mod3, grp.mod3, grp.mod3,
                                            wog, wom, wout, norm_w, wr, br)


def _dispatch_kernel(pe_ref, slot_ref, h_ref, hs_ref, zero_ref, sem, zsem, *, tb, tm, E):
    i = pl.program_id(0)
    n_blocks = hs_ref.shape[0] // tm

    def zero_copy(row0):
        return pltpu.make_async_copy(zero_ref, hs_ref.at[pl.ds(pl.multiple_of(row0, tm), tm)], zsem)

    zero_jobs = [(pe_ref[e + 1] > pe_ref[e], pe_ref[e + 1] - tm) for e in range(E)]
    zero_jobs += [(b * tm >= pe_ref[E], b * tm) for b in range(n_blocks - E, n_blocks)]

    @pl.when(i == 0)
    def _():
        zero_ref[...] = jnp.zeros_like(zero_ref)
        for wanted, row0 in zero_jobs:
            @pl.when(wanted)
            def _():
                zero_copy(row0).start()
        for wanted, row0 in zero_jobs:
            @pl.when(wanted)
            def _():
                zero_copy(row0).wait()

    for r in range(tb):
        for k in range(TOP_K):
            a = r * TOP_K + k
            pltpu.make_async_copy(h_ref.at[pl.ds(r, 1)], hs_ref.at[pl.ds(slot_ref[0, 0, a], 1)],
                                  sem).start(priority=a % 2)
    for k in range(TOP_K):
        pltpu.make_async_copy(h_ref, hs_ref.at[pl.ds(0, tb)], sem).wait()


def _dispatch(h, slots, pad_edges, n_slots, tb, tm):
    n, d = h.shape
    nb = n // tb
    E = pad_edges.shape[0] - 1
    grid_spec = pltpu.PrefetchScalarGridSpec(
        num_scalar_prefetch=1, grid=(nb,),
        in_specs=[pl.BlockSpec((1, 1, tb * TOP_K), lambda i, pe: (i, 0, 0), memory_space=pltpu.SMEM),
                  pl.BlockSpec((tb, d), lambda i, pe: (i, 0))],
        out_specs=pl.BlockSpec(memory_space=pl.ANY),
        scratch_shapes=[pltpu.VMEM((tm, d), h.dtype), pltpu.SemaphoreType.DMA, pltpu.SemaphoreType.DMA])
    return pl.pallas_call(
        functools.partial(_dispatch_kernel, tb=tb, tm=tm, E=E), grid_spec=grid_spec,
        out_shape=jax.ShapeDtypeStruct((n_slots, d), h.dtype),
        compiler_params=_cp(("arbitrary",)))(pad_edges, slots.reshape(nb, 1, tb * TOP_K), h)


def _expert_kernel(be_ref, na_ref, x_ref, wgu_ref, bgu_ref, wdn_ref, bdn_ref, y_ref, wgu_s, wdn_s, *, DE):
    i = pl.program_id(0)
    active = i < na_ref[0]
    first = jnp.logical_or(i == 0, be_ref[i] != be_ref[jnp.maximum(i - 1, 0)])

    @pl.when(jnp.logical_and(active, first))
    def _():
        wgu_s[...] = wgu_ref[...].astype(BF16)
        wdn_s[...] = wdn_ref[...].astype(BF16)

    @pl.when(active)
    def _():
        gu = jnp.dot(x_ref[...].astype(BF16), wgu_s[...], preferred_element_type=F32) + bgu_ref[...]
        gt = jnp.minimum(gu[:, :DE], SWIGLU_LIMIT)
        up = jnp.clip(gu[:, DE:], -SWIGLU_LIMIT, SWIGLU_LIMIT)
        act = (up + 1.0) * gt * jax.nn.sigmoid(SWIGLU_ALPHA * gt)
        y_ref[...] = jnp.dot(act.astype(BF16), wdn_s[...], preferred_element_type=F32) + bdn_ref[...]

    @pl.when(jnp.logical_not(active))
    def _():
        y_ref[...] = jnp.zeros_like(y_ref)


def _experts(hs, blk_e, n_active, w_gu, b_gu, w_dn, b_dn, layer, tm):
    n_slots, d = hs.shape
    E, _, de2 = w_gu.shape[1:]
    de = de2 // 2
    grid_spec = pltpu.PrefetchScalarGridSpec(
        num_scalar_prefetch=2, grid=(n_slots // tm,),
        in_specs=[pl.BlockSpec((tm, d), lambda i, be, na: (jnp.minimum(i, na[0] - 1), 0)),
                  pl.BlockSpec((None, None, d, de2), lambda i, be, na: (layer, be[i], 0, 0)),
                  pl.BlockSpec((None, None, 1, de2), lambda i, be, na: (layer, be[i], 0, 0)),
                  pl.BlockSpec((None, None, de, d), lambda i, be, na: (layer, be[i], 0, 0)),
                  pl.BlockSpec((None, None, 1, d), lambda i, be, na: (layer, be[i], 0, 0))],
        out_specs=pl.BlockSpec((tm, d), lambda i, be, na: (i, 0)),
        scratch_shapes=[pltpu.VMEM((d, de2), BF16), pltpu.VMEM((de, d), BF16)])
    L = w_gu.shape[0]
    return pl.pallas_call(
        functools.partial(_expert_kernel, DE=de), grid_spec=grid_spec,
        out_shape=jax.ShapeDtypeStruct((n_slots, d), F32),
        compiler_params=_cp(("arbitrary",), VMEM_LIMIT_MOE))(
            blk_e, n_active, hs, w_gu, b_gu.reshape(L, E, 1, de2), w_dn, b_dn.reshape(L, E, 1, d))


def _combine_kernel(slotc_ref, slotn_ref, tg_ref, x1_ref, g2_ref, scf_ref, shf_ref, nwf_ref, yp_ref, o_ref,
                    buf, sem, *, tb):
    i = pl.program_id(0)
    slot = i % 2

    def gather_starts(slot_ref, sl):
        for r in range(tb):
            for k in range(TOP_K):
                a = r * TOP_K + k
                pltpu.make_async_copy(yp_ref.at[pl.ds(slot_ref[0, 0, a], 1)], buf.at[sl, k, pl.ds(r, 1)],
                                      sem.at[sl]).start(priority=a % 2)

    @pl.when(i == 0)
    def _():
        gather_starts(slotc_ref, 0)

    @pl.when(i + 1 < pl.num_programs(0))
    def _():
        gather_starts(slotn_ref, 1 - slot)

    for k in range(TOP_K):
        pltpu.make_async_copy(yp_ref.at[pl.ds(0, tb)], buf.at[slot, k], sem.at[slot]).wait()
    tg = tg_ref[...]
    moe = tg[:, 0:1] * buf[slot, 0]
    for k in range(1, TOP_K):
        moe = moe + tg[:, k:k + 1] * buf[slot, k]
    x2 = x1_ref[...] + g2_ref[...] * moe
    o_ref[...] = _rms(x2) * nwf_ref[...] * (1.0 + scf_ref[...]) + shf_ref[...]


def _combine_final(y_pad, slots, tg, x1, grp, norm_final_w):
    n, d = x1.shape
    tb = grp.tm
    nb = n // tb
    rf, bpg = grp.modf3.shape[1], grp.bpg
    fspec = lambda j: pl.BlockSpec((None, rf, d), lambda i: (i // bpg, 0, j))
    slots3 = slots.reshape(nb, 1, tb * TOP_K)
    sspec = lambda f: pl.BlockSpec((1, 1, tb * TOP_K), lambda i: (f(i), 0, 0), memory_space=pltpu.SMEM)
    return pl.pallas_call(
        functools.partial(_combine_kernel, tb=tb), grid=(nb,),
        in_specs=[sspec(lambda i: i), sspec(lambda i: jnp.minimum(i + 1, nb - 1)),
                  pl.BlockSpec((tb, LANE), lambda i: (i, 0)),
                  pl.BlockSpec((tb, d), lambda i: (i, 0)),
                  grp.spec(5, d), fspec(1), fspec(0),
                  pl.BlockSpec((1, d), lambda i: (0, 0)),
                  pl.BlockSpec(memory_space=pl.ANY)],
        out_specs=pl.BlockSpec((tb, d), lambda i: (i, 0)),
        out_shape=jax.ShapeDtypeStruct((n, d), F32),
        scratch_shapes=[pltpu.VMEM((2, TOP_K, tb, d), F32), pltpu.SemaphoreType.DMA((2,))],
        compiler_params=_cp(("arbitrary",)))(
            slots3, slots3, tg, x1, grp.mod3, grp.modf3, grp.modf3, norm_final_w.reshape(1, d), y_pad)


def _route(top_e, E, tm):
    n = top_e.shape[0]
    tok_oh = jnp.sum((top_e[:, :, None] == jnp.arange(E, dtype=jnp.int32)).astype(jnp.int32), axis=1)
    csum = jnp.cumsum(tok_oh, axis=0)
    counts = csum[-1]
    rank = jnp.take_along_axis(csum - tok_oh, top_e, axis=1)
    padded = (counts + tm - 1) // tm * tm
    pad_end = jnp.cumsum(padded)
    slots = (pad_end - padded)[top_e] + rank
    n_blocks = -(-(n * TOP_K) // tm) + E
    blk_start = jnp.arange(n_blocks, dtype=jnp.int32) * tm
    blk_e = jnp.minimum(jnp.sum((pad_end[None, :] <= blk_start[:, None]).astype(jnp.int32), axis=1), E - 1)
    n_active = (pad_end[-1] // tm).reshape(1)
    pad_edges = jnp.concatenate([jnp.zeros((1,), pad_end.dtype), pad_end])
    return (slots.astype(jnp.int32), pad_edges.astype(jnp.int32), blk_e.astype(jnp.int32),
            n_active.astype(jnp.int32), n_blocks * tm)


def _rope_tables(pos, rope):
    half = rope // 2
    inv = ROPE_THETA ** (-jnp.arange(half, dtype=F32) / half)
    ang = pos.astype(F32)[:, None] * inv[None, :]
    cos, sin = jnp.cos(ang), jnp.sin(ang)
    return jnp.concatenate([cos, cos], axis=-1), jnp.concatenate([sin, sin], axis=-1)


def _rotate_cols(w, rope):
    k, n = w.shape
    w3 = w.reshape(k, n // rope, rope)
    half = rope // 2
    return jnp.concatenate([-w3[..., half:], w3[..., :half]], axis=-1).reshape(k, n)


def kernel(x_prompt, x_sample, c_prompt, c_sample, cache_kv, cache_pe, state_ssm, state_conv, page_table, norm_mix_w, norm_ffn_w, w_ada, b_ada, w_in, conv_w, A_log, dt_bias, gdn_norm_w, w_o_gdn, q_norm_w, kv_norm_w, w_uq, w_uk, w_uv, w_o_mla, w_out, w_router, b_router, w_gu, b_gu, w_dn, b_dn, w_ada_final, b_ada_final, norm_final_w):
    B, T, D = x_prompt.shape
    Bs, Ts, _ = x_sample.shape
    assert Ts == 1
    depth = w_in.shape[0]
    H, DK, DV = state_ssm.shape[2:]
    QK = H * DK
    CONV = state_conv.shape[3]
    assert CONV == 2 * QK + H * DV and H == SUBLANES
    QL = q_norm_w.shape[1]
    HM, KVL, NOPE = w_uk.shape[1:]
    VH = w_uv.shape[3]
    ROPE = cache_pe.shape[3]
    E = w_router.shape[2]
    scale = float(NOPE + ROPE) ** -0.5
    n_p, n_s = B * T, Bs * Ts
    past_len = page_table.shape[1] * cache_kv.shape[2]
    cache_pe_t = jnp.swapaxes(cache_pe, 2, 3)

    c_all = jnp.concatenate([c_prompt, c_sample], axis=0)
    modf = _matmul(c_all, w_ada_final, F32, c_all.shape[0], 1024, b_ada_final)
    tm_p = min(512, T)
    tb = 128
    cos_p, sin_p = _rope_tables(jnp.arange(T), ROPE)
    cos_s, sin_s = _rope_tables(past_len + jnp.arange(Ts), ROPE)

    hp = x_prompt.reshape(n_p, D)
    hs = x_sample.reshape(n_s, D)
    outs = {k: [] for k in ('kv_p', 'pe_p', 'ssm_p', 'conv_p', 'kv_s', 'pe_s', 'ssm_s', 'conv_s')}
    for l in range(depth):
        mod = _matmul(c_all, w_ada[l], F32, c_all.shape[0], 1024, b_ada[l])
        last = l == depth - 1
        grp_p = _Group(mod[:B].reshape(B, 1, 6 * D), modf[:B].reshape(B, 1, 2 * D), tm_p, T // tm_p)
        grp_s = _Group(mod[B:].reshape(1, Bs, 6 * D), modf[B:].reshape(1, Bs, 2 * D), Bs, 1)
        grp_pc = _Group(grp_p.mod3, grp_p.modf3, tb, T // tb)

        offs = [0]
        for s in (CONV, H * DV, H, H, QL, KVL, ROPE, D, D):
            offs.append(offs[-1] + s)
        wi = w_in[l]
        seg = lambda i: wi[:, offs[i]:offs[i + 1]]
        w_qkv = seg(0).astype(BF16)
        w_z = seg(1).astype(BF16)
        w_gab = jnp.concatenate([seg(7), seg(8)], axis=1).astype(BF16)
        n_small = QL + KVL + 2 * ROPE
        ba_col = -(-n_small // LANE)
        w_small = jnp.concatenate(
            [seg(4), seg(5), seg(6), _rotate_cols(seg(6), ROPE), jnp.zeros((D, ba_col * LANE - n_small), F32),
             seg(2), jnp.zeros((D, 8 - H), F32), seg(3), jnp.zeros((D, LANE - 8 - H), F32)], axis=1).astype(BF16)
        cw_t = conv_w[l].T
        gparams = jnp.zeros((2, LANE), F32).at[0, 8:8 + H].set(A_log[l]).at[1, 8:8 + H].set(dt_bias[l])
        wq = w_uq[l].reshape(QL, HM, NOPE + ROPE)
        wq_pe = wq[:, :, NOPE:].reshape(QL, HM * ROPE)
        wq_all = jnp.concatenate([wq[:, :, :NOPE].reshape(QL, HM * NOPE), wq_pe], axis=1).astype(BF16)
        wq_rot = _rotate_cols(wq_pe, ROPE).astype(BF16)
        wuk_all = jnp.transpose(w_uk[l], (1, 0, 2)).reshape(KVL, HM * NOPE).astype(BF16)
        wuv_all = jnp.transpose(w_uv[l], (1, 0, 2)).reshape(KVL, HM * VH).astype(BF16)
        qnw = q_norm_w[l].reshape(1, QL)
        kvnw = kv_norm_w[l].reshape(1, KVL)
        wog = w_o_gdn[l].astype(BF16)
        wom = w_o_mla[l].astype(BF16)
        wout = w_out[l].astype(BF16)
        wr = jnp.pad(w_router[l], ((0, 0), (0, LANE - E)))
        br = jnp.pad(b_router[l], (0, LANE - E)).reshape(1, LANE)
        nfw = norm_ffn_w[l].reshape(1, D)

        h1 = _norm_mod(hp, norm_mix_w[l], grp_p, 1, 0, BF16)
        qkv_p = _matmul(h1, w_qkv, BF16, 1024, 512)
        z_p = _matmul(h1, w_z, BF16, 1024, 512)
        gab_p = _matmul(h1, w_gab, BF16, 1024, 512)
        small_p = _matmul(h1, w_small, F32, 1024, w_small.shape[1])
        og_p, ssm_p = _gdn_prompt(qkv_p, z_p, small_p, ba_col, cw_t, gparams, gdn_norm_w[l], B, T, H, DK, DV)
        q_p, k_p, v_p, lat_p, pe_p = _mla_prep_prompt(small_p, cos_p, sin_p, qnw, kvnw, wq_all, wq_rot, wuk_all,
                                                      wuv_all, T, tm_p, HM, NOPE, ROPE, VH, QL, KVL, scale)
        om_p = _flash_attention(q_p, k_p, v_p, B, T, tm_p)
        x1_p, h2_p, te_p, tg_p = _post_mixer(hp, og_p, om_p, gab_p, grp_p, wog, wom, wout, nfw, wr, br, E)
        outs['kv_p'].append(lat_p.reshape(B, T, KVL))
        outs['pe_p'].append(pe_p.reshape(B, T, ROPE))
        outs['ssm_p'].append(ssm_p)
        outs['conv_p'].append(qkv_p.reshape(B, T, CONV)[:, T - (CONV_WIDTH - 1):, :].astype(F32))

        h1s = _norm_mod(hs, norm_mix_w[l], grp_s, 1, 0, BF16)
        qkv_s = _matmul(h1s, w_qkv, F32, Bs, 512)
        z_s = _matmul(h1s, w_z, BF16, Bs, 512)
        gab_s = _matmul(h1s, w_gab, BF16, Bs, 512)
        small_s = _matmul(h1s, w_small, F32, Bs, w_small.shape[1])
        og_s, ssm_s, conv_s = _gdn_decode(qkv_s, z_s, small_s, ba_col, state_conv[l], state_ssm[l], cw_t, gparams,
                                          gdn_norm_w[l], H, DK, DV)
        qn_s, qp_s, lat_s, pe_s = _mla_prep_sample(small_s, cos_s, sin_s, qnw, kvnw, wq_all, wq_rot,
                                                   HM, NOPE, ROPE, QL, KVL, scale)
        q_lat = jnp.transpose(_q_latent(qn_s, w_uk[l], scale), (1, 0, 2))
        o_lat = _paged_attention(q_lat, qp_s.reshape(n_s, HM, ROPE), lat_s, pe_s, cache_kv, cache_pe_t, page_table, l)
        om_s = _o_value(jnp.transpose(o_lat, (1, 0, 2)), w_uv[l])
        x1_s, h2_s, te_s, tg_s = _post_mixer(hs, og_s, om_s, gab_s, grp_s, wog, wom, wout, nfw, wr, br, E)
        outs['kv_s'].append(lat_s.reshape(Bs, Ts, KVL))
        outs['pe_s'].append(pe_s.reshape(Bs, Ts, ROPE))
        outs['ssm_s'].append(ssm_s)
        outs['conv_s'].append(conv_s)

        tm_e = 256
        h2 = jnp.concatenate([h2_p, h2_s], axis=0)
        top_e = jnp.concatenate([te_p[:, :TOP_K], te_s[:, :TOP_K]], axis=0)
        slots, pad_edges, blk_e, n_active, n_slots = _route(top_e, E, tm_e)
        h_sorted = _dispatch(h2, slots, pad_edges, n_slots, tb, tm_e)
        y_pad = _experts(h_sorted, blk_e, n_active, w_gu, b_gu, w_dn, b_dn, l, tm_e)
        if not last:
            raise NotImplementedError("stacked layers need an un-normalised residual output")
        hp = _combine_final(y_pad, slots[:n_p], tg_p, x1_p, grp_pc, norm_final_w)
        hs = _combine_final(y_pad, slots[n_p:], tg_s, x1_s, grp_s, norm_final_w)

    st = lambda k: jnp.stack(outs[k])
    return (hp.reshape(B, T, D), hs.reshape(Bs, Ts, D),
            st('kv_p'), st('pe_p'), st('ssm_p'), st('conv_p'),
            st('kv_s'), st('pe_s'), st('ssm_s'), st('conv_s'))
```

```python
import functools

import jax
import jax.numpy as jnp
from jax import lax
from jax.experimental import pallas as pl
from jax.experimental.pallas import tpu as pltpu

F32 = jnp.float32
BF16 = jnp.bfloat16
HI = lax.Precision.HIGHEST

NORM_EPS = 1e-6
ROPE_THETA = 10000.0
TOP_K = 4
SWIGLU_LIMIT = 7.0
SWIGLU_ALPHA = 1.702
CONV_WIDTH = 4
GDN_CHUNK = 64
PAGE_GROUP = 8
PAGE_REQUESTS = 4
LANE = 128
VMEM_LIMIT = 48 * 1024 * 1024
VMEM_LIMIT_MOE = 58 * 1024 * 1024

NT = (((1,), (1,)), ((), ()))
TN = (((0,), (0,)), ((), ()))


def _cp(sem, vmem=VMEM_LIMIT):
    return pltpu.CompilerParams(dimension_semantics=sem, vmem_limit_bytes=vmem)


def _rms(x):
    return x * lax.rsqrt(jnp.mean(x * x, axis=-1, keepdims=True) + NORM_EPS)


def _softplus(x):
    return jnp.maximum(x, 0.0) + jnp.log1p(jnp.exp(-jnp.abs(x)))


def _silu(x):
    return x * jax.nn.sigmoid(x)


def _bdot(a, b):
    return jnp.dot(a.astype(BF16), b.astype(BF16), preferred_element_type=F32)


def _bdot_g(a, b, dims):
    return lax.dot_general(a.astype(BF16), b.astype(BF16), dims, preferred_element_type=F32)


def _mm_kernel(*refs, has_bias):
    if has_bias:
        x_ref, w_ref, b_ref, o_ref = refs
    else:
        x_ref, w_ref, o_ref = refs
    acc = _bdot(x_ref[...], w_ref[...])
    if has_bias:
        acc = acc + b_ref[...]
    o_ref[...] = acc.astype(o_ref.dtype)


def _matmul(x, w, out_dtype, tm, tn, bias=None):
    M, K = x.shape
    N = w.shape[1]
    tm, tn = min(tm, M), min(tn, N)
    in_specs = [pl.BlockSpec((tm, K), lambda i, j: (i, 0)), pl.BlockSpec((K, tn), lambda i, j: (0, j))]
    args = [x, w]
    if bias is not None:
        in_specs.append(pl.BlockSpec((1, tn), lambda i, j: (0, j)))
        args.append(bias.reshape(1, N))
    return pl.pallas_call(
        functools.partial(_mm_kernel, has_bias=bias is not None),
        grid=(M // tm, N // tn), in_specs=in_specs,
        out_specs=pl.BlockSpec((tm, tn), lambda i, j: (i, j)),
        out_shape=jax.ShapeDtypeStruct((M, N), out_dtype),
        compiler_params=_cp(("parallel", "parallel")))(*args)


class _Group:
    def __init__(self, mod3, modf3, tm, blocks_per_g):
        self.mod3, self.modf3, self.tm, self.bpg = mod3, modf3, tm, blocks_per_g

    def spec(self, j, d):
        r, bpg = self.mod3.shape[1], self.bpg
        return pl.BlockSpec((None, r, d), lambda i: (i // bpg, 0, j))


def _norm_mod_kernel(x_ref, w_ref, sc_ref, sh_ref, o_ref):
    y = _rms(x_ref[...]) * w_ref[...]
    o_ref[...] = (y * (1.0 + sc_ref[...]) + sh_ref[...]).astype(o_ref.dtype)


def _norm_mod(x, w, grp, j_scale, j_shift, out_dtype):
    n, d = x.shape
    tm = grp.tm
    return pl.pallas_call(
        _norm_mod_kernel, grid=(n // tm,),
        in_specs=[pl.BlockSpec((tm, d), lambda i: (i, 0)), pl.BlockSpec((1, d), lambda i: (0, 0)),
                  grp.spec(j_scale, d), grp.spec(j_shift, d)],
        out_specs=pl.BlockSpec((tm, d), lambda i: (i, 0)),
        out_shape=jax.ShapeDtypeStruct((n, d), out_dtype),
        compiler_params=_cp(("parallel",)))(x, w.reshape(1, d), grp.mod3, grp.mod3)


SUBLANES = 8
PREP_CHUNKS = 2
SCAN_SEQS = 4
PREV_ROWS = 16


def _split_bf16(x):
    hi = x.astype(BF16)
    return hi, (x - hi.astype(F32)).astype(BF16)


def _dot3(a, b):
    d = lambda x, y: jnp.dot(x, y, preferred_element_type=F32)
    return d(a[0], b[0]) + d(a[0], b[1]) + d(a[1], b[0])


def _expand_matrix(C):
    k = jnp.arange(C)
    n = jnp.arange(SUBLANES * LANE)
    rem = n % LANE
    hit = ((k[:, None] // SUBLANES == rem[None, :] // SUBLANES) & (k[:, None] % SUBLANES == n[None, :] // LANE)
           & (rem[None, :] < C))
    return hit.astype(BF16)


def _unit_lower_inverses(a_list, at_list, g_ref, C):
    nh = len(a_list)
    row = lax.broadcasted_iota(jnp.int32, (C, C), 0)
    col = lax.broadcasted_iota(jnp.int32, (C, C), 1)
    blockdiag = (row // SUBLANES) == (col // SUBLANES)
    packed = []
    for at in at_list:
        m = jnp.where(blockdiag, at, 0.0)
        d = m[0:SUBLANES]
        for b in range(1, C // SUBLANES):
            d = d + m[SUBLANES * b:SUBLANES * (b + 1)]
        packed.append(d)
    stack = _split_bf16(jnp.concatenate(packed, axis=0))
    g = g_ref[...]
    coef = (jnp.dot(stack[0], g, preferred_element_type=F32) + jnp.dot(stack[1], g, preferred_element_type=F32))
    sub = lax.broadcasted_iota(jnp.int32, (SUBLANES, LANE), 0)
    lane = lax.broadcasted_iota(jnp.int32, (SUBLANES, LANE), 1)
    unit = jnp.where((lane % SUBLANES == sub) & (lane < C), 1.0, 0.0)
    xd = [unit] * nh
    for i in range(1, SUBLANES):
        e_i = jnp.where((lane % SUBLANES == i) & (lane < C), 1.0, 0.0)[0:1]
        for h in range(nh):
            c_i = coef[SUBLANES * h:SUBLANES * (h + 1), LANE * i:LANE * (i + 1)]
            new_row = e_i - jnp.sum(c_i * xd[h], axis=0, keepdims=True)
            xd[h] = jnp.where(sub == i, new_row, xd[h])
    x = [jnp.where(blockdiag, jnp.concatenate([xd[h][:, :C]] * (C // SUBLANES), axis=0), 0.0) for h in range(nh)]
    a16 = [a.astype(BF16) for a in a_list]
    zero = jnp.zeros((C, C), BF16)
    s = SUBLANES
    while s < C:
        below = ((row // s) % 2 == 1) & ((col // s) == (row // s) - 1)
        x16 = [x[h].astype(BF16) for h in range(nh)]
        m1 = [jnp.dot(x16[h], jnp.where(below, a16[h], zero), preferred_element_type=F32) for h in range(nh)]
        x = [x[h] - jnp.dot(m1[h].astype(BF16), x16[h], preferred_element_type=F32) for h in range(nh)]
        s *= 2
    return x


def _gdn_prep_kernel(qkv_ref, prev_ref, sm_ref, cw_ref, gp_ref, g_ref,
                     u_ref, w_ref, qe_ref, kd_ref, aqk_ref, dec_ref, *, H, DK, DV, C):
    c = pl.program_id(1)
    QK = H * DK
    P = PREV_ROWS
    R = qkv_ref.shape[0]

    prev = prev_ref[...]
    xin = jnp.concatenate([jnp.where(c == 0, jnp.zeros_like(prev), prev), qkv_ref[...]], axis=0)
    out_row = lax.broadcasted_iota(jnp.int32, (R, P + R), 0)
    in_row = lax.broadcasted_iota(jnp.int32, (R, P + R), 1)
    y = None
    for j in range(CONV_WIDTH):
        shift = (in_row == out_row + (P - (CONV_WIDTH - 1) + j)).astype(BF16)
        term = jnp.dot(shift, xin, preferred_element_type=F32) * cw_ref[j:j + 1, :]
        y = term if y is None else y + term
    y = _silu(y)

    sm = sm_ref[...]
    beta_all = jax.nn.sigmoid(sm)
    g_all = -jnp.exp(gp_ref[0:1, :]) * _softplus(sm + gp_ref[1:2, :])
    row_r = lax.broadcasted_iota(jnp.int32, (R, R), 0)
    col_r = lax.broadcasted_iota(jnp.int32, (R, R), 1)
    tril = ((col_r <= row_r) & (col_r // C == row_r // C)).astype(F32)
    gam_all = jnp.dot(tril, g_all, precision=HI, preferred_element_type=F32)
    lane = lax.broadcasted_iota(jnp.int32, sm.shape, 1)
    packed = jnp.where(lane < 8, beta_all, gam_all)
    row = lax.broadcasted_iota(jnp.int32, (C, C), 0)
    col = lax.broadcasted_iota(jnp.int32, (C, C), 1)

    a_list, at_list, rhs_list, places = [], [], [], []
    for ci, h in [(ci, h) for ci in range(R // C) for h in range(H)]:
        rs = slice(ci * C, (ci + 1) * C)
        rows_t = packed[rs].T
        qh = y[rs, h * DK:(h + 1) * DK]
        kh = y[rs, QK + h * DK:QK + (h + 1) * DK]
        vh = y[rs, 2 * QK + h * DV:2 * QK + (h + 1) * DV]
        qh = qh * lax.rsqrt(jnp.sum(qh * qh, axis=-1, keepdims=True) + NORM_EPS) * (DK ** -0.5)
        kh = kh * lax.rsqrt(jnp.sum(kh * kh, axis=-1, keepdims=True) + NORM_EPS)
        beta_c = beta_all[rs, h:h + 1]
        gam_c = gam_all[rs, 8 + h:9 + h]
        beta_r = rows_t[h:h + 1, :]
        gam_r = rows_t[8 + h:9 + h, :]
        dm = gam_c - gam_r
        decay = jnp.exp(jnp.where(col <= row, dm, -jnp.inf))
        decay_t = jnp.exp(jnp.where(row < col, -dm, -jnp.inf))
        qk_kk = _bdot_g(jnp.concatenate([qh, kh], axis=0), kh, NT)
        kk = qk_kk[C:]
        a_list.append(jnp.where(col < row, kk * beta_c * decay, 0.0))
        at_list.append(kk * beta_r * decay_t)
        egam = jnp.exp(gam_c)
        rhs_list.append(jnp.concatenate([beta_c * vh, beta_c * egam * kh], axis=1).astype(BF16))
        g_last = gam_c[C - 1:C, :]
        qe_ref[rs, h * DK:(h + 1) * DK] = (qh * egam).astype(qe_ref.dtype)
        kd_ref[rs, h * DK:(h + 1) * DK] = (kh * jnp.exp(g_last - gam_c)).astype(kd_ref.dtype)
        aqk_ref[rs, h * C:(h + 1) * C] = (qk_kk[:C] * decay).astype(aqk_ref.dtype)
        dec_ref[ci, h:h + 1, :] = jnp.broadcast_to(jnp.exp(g_last), (1, LANE))
        places.append((rs, h))

    t_inv = _unit_lower_inverses(a_list, at_list, g_ref, C)
    for i, (rs, h) in enumerate(places):
        t_hi, t_lo = _split_bf16(t_inv[i])
        uw = (jnp.dot(t_hi, rhs_list[i], preferred_element_type=F32)
              + jnp.dot(t_lo, rhs_list[i], preferred_element_type=F32))
        u_ref[rs, h * DV:(h + 1) * DV] = uw[:, :DV].astype(u_ref.dtype)
        w_ref[rs, h * DK:(h + 1) * DK] = uw[:, DV:].astype(w_ref.dtype)


def _gdn_scan_kernel(u_ref, w_ref, qe_ref, kd_ref, aqk_ref, dec_ref, z_ref, nw_ref, og_ref, s_out_ref, s_ref,
                     *, H, DK, DV, C):
    c = pl.program_id(1)

    @pl.when(c == 0)
    def _():
        s_ref[...] = jnp.zeros_like(s_ref)

    ch = [(g, h) for g in range(u_ref.shape[0]) for h in range(H)]
    ks = [slice(h * DK, (h + 1) * DK) for h in range(H)]
    vs = [slice(h * DV, (h + 1) * DV) for h in range(H)]
    s16 = [s_ref[g, h].astype(BF16) for g, h in ch]
    ws = [jnp.dot(w_ref[g, :, ks[h]], s16[i], preferred_element_type=F32) for i, (g, h) in enumerate(ch)]
    qs = [jnp.dot(qe_ref[g, :, ks[h]], s16[i], preferred_element_type=F32) for i, (g, h) in enumerate(ch)]
    v16 = [(u_ref[g, :, vs[h]].astype(F32) - ws[i]).astype(BF16) for i, (g, h) in enumerate(ch)]
    ds = [lax.dot_general(kd_ref[g, :, ks[h]], v16[i], TN, preferred_element_type=F32) for i, (g, h) in enumerate(ch)]
    o = [qs[i] + jnp.dot(aqk_ref[g, :, h * C:(h + 1) * C], v16[i], preferred_element_type=F32)
         for i, (g, h) in enumerate(ch)]
    for i, (g, h) in enumerate(ch):
        s_ref[g, h] = s_ref[g, h] * dec_ref[g, 0, h:h + 1, :] + ds[i]
        zh = z_ref[g, :, vs[h]].astype(F32)
        og_ref[g, :, vs[h]] = (_rms(o[i]) * nw_ref[...] * _silu(zh)).astype(og_ref.dtype)

    @pl.when(c == pl.num_programs(1) - 1)
    def _():
        s_out_ref[...] = s_ref[...]


def _gdn_prompt(qkv, z, small, ba_col, cw_t, gparams, norm_w, B, T, H, DK, DV):
    C = GDN_CHUNK
    nc = T // C
    n, W = qkv.shape
    kw = dict(H=H, DK=DK, DV=DV, C=C)
    pc = PREP_CHUNKS if nc % PREP_CHUNKS == 0 else 1
    ns = nc // pc
    R = pc * C
    blk = lambda w: pl.BlockSpec((R, w), lambda b, c: (b * ns + c, 0))
    ppc = R // PREV_ROWS
    u, w, qe, kd, aqk, dec = pl.pallas_call(
        functools.partial(_gdn_prep_kernel, **kw), grid=(B, ns),
        in_specs=[blk(W),
                  pl.BlockSpec((PREV_ROWS, W), lambda b, c: (jnp.maximum((b * ns + c) * ppc - 1, 0), 0)),
                  pl.BlockSpec((R, LANE), lambda b, c: (b * ns + c, ba_col)),
                  pl.BlockSpec((CONV_WIDTH, W), lambda b, c: (0, 0)),
                  pl.BlockSpec((2, LANE), lambda b, c: (0, 0)),
                  pl.BlockSpec((C, SUBLANES * LANE), lambda b, c: (0, 0))],
        out_specs=[blk(H * DV), blk(H * DK), blk(H * DK), blk(H * DK), blk(H * C),
                   pl.BlockSpec((pc, SUBLANES, LANE), lambda b, c: (b * ns + c, 0, 0))],
        out_shape=[jax.ShapeDtypeStruct((n, H * DV), BF16), jax.ShapeDtypeStruct((n, H * DK), BF16),
                   jax.ShapeDtypeStruct((n, H * DK), BF16), jax.ShapeDtypeStruct((n, H * DK), BF16),
                   jax.ShapeDtypeStruct((n, H * C), BF16), jax.ShapeDtypeStruct((B * nc, SUBLANES, LANE), F32)],
        compiler_params=_cp(("parallel", "parallel")))(qkv, qkv, small, cw_t, gparams, _expand_matrix(C))
    nb = SCAN_SEQS if B % SCAN_SEQS == 0 else 1
    seq = lambda a: a.reshape(B, T, a.shape[1])
    sblk = lambda w: pl.BlockSpec((nb, C, w), lambda b, c: (b, c, 0))
    og, s_fin = pl.pallas_call(
        functools.partial(_gdn_scan_kernel, **kw), grid=(B // nb, nc),
        in_specs=[sblk(H * DV), sblk(H * DK), sblk(H * DK), sblk(H * DK), sblk(H * C),
                  pl.BlockSpec((nb, 1, SUBLANES, LANE), lambda b, c: (b, c, 0, 0)),
                  sblk(H * DV), pl.BlockSpec((1, DV), lambda b, c: (0, 0))],
        out_specs=[sblk(H * DV), pl.BlockSpec((nb, H, DK, DV), lambda b, c: (b, 0, 0, 0))],
        out_shape=[jax.ShapeDtypeStruct((B, T, H * DV), BF16), jax.ShapeDtypeStruct((B, H, DK, DV), F32)],
        scratch_shapes=[pltpu.VMEM((nb, H, DK, DV), F32)],
        compiler_params=_cp(("parallel", "arbitrary")))(
            seq(u), seq(w), seq(qe), seq(kd), seq(aqk), dec.reshape(B, nc, SUBLANES, LANE), seq(z),
            norm_w.reshape(1, DV))
    return og.reshape(n, H * DV), s_fin


DECODE_TOKENS = 1


def _gdn_decode_kernel(u_ref, z_ref, ba_ref, buf_ref, s_in_ref, cw_ref, gp_ref, nw_ref,
                       og_ref, s_out_ref, buf_out_ref, *, H, DK, DV):
    for t in range(u_ref.shape[0]):
        _gdn_decode_token(u_ref.at[t], z_ref.at[t], ba_ref.at[t], buf_ref.at[t], s_in_ref.at[t], cw_ref, gp_ref,
                          nw_ref, og_ref.at[t], s_out_ref.at[t], buf_out_ref.at[t], H=H, DK=DK, DV=DV)


def _gdn_decode_token(u_ref, z_ref, ba_ref, buf_ref, s_in_ref, cw_ref, gp_ref, nw_ref,
                      og_ref, s_out_ref, buf_out_ref, *, H, DK, DV):
    QK = H * DK
    u = u_ref[...]
    buf = buf_ref[...]
    y = buf[0:1] * cw_ref[0:1, :]
    y = y + buf[1:2] * cw_ref[1:2, :]
    y = y + buf[2:3] * cw_ref[2:3, :]
    y = y + u * cw_ref[3:4, :]
    buf_out_ref[0:2, :] = buf[1:3]
    buf_out_ref[2:3, :] = u
    y = _silu(y)
    sm = ba_ref[...]
    beta_all = jax.nn.sigmoid(sm)
    g_all = -jnp.exp(gp_ref[0:1, :]) * _softplus(sm + gp_ref[1:2, :])
    rows = []
    for h in range(H):
        kh = y[:, QK + h * DK:QK + (h + 1) * DK]
        rows.append(kh * lax.rsqrt(jnp.sum(kh * kh, axis=-1, keepdims=True) + NORM_EPS))
    for h in range(H):
        qh = y[:, h * DK:(h + 1) * DK]
        rows.append(qh * lax.rsqrt(jnp.sum(qh * qh, axis=-1, keepdims=True) + NORM_EPS) * (DK ** -0.5))
    rows.append(jnp.zeros((DK - 2 * H, DK), F32))
    cols = jnp.concatenate(rows, axis=0).T
    hs = range(H)
    kcol = [cols[:, h:h + 1] for h in hs]
    qcol = [cols[:, H + h:H + h + 1] for h in hs]
    s_dec = [s_in_ref[h] * jnp.exp(g_all[:, 8 + h:9 + h]) for h in hs]
    ks = [jnp.sum(kcol[h] * s_dec[h], axis=0, keepdims=True) for h in hs]
    v_new = [beta_all[:, h:h + 1] * (y[:, 2 * QK + h * DV:2 * QK + (h + 1) * DV] - ks[h]) for h in hs]
    s_new = [s_dec[h] + kcol[h] * v_new[h] for h in hs]
    o = [jnp.sum(qcol[h] * s_new[h], axis=0, keepdims=True) for h in hs]
    for h in hs:
        s_out_ref[h] = s_new[h]
        zh = z_ref[:, h * DV:(h + 1) * DV].astype(F32)
        og_ref[:, h * DV:(h + 1) * DV] = (_rms(o[h]) * nw_ref[...] * _silu(zh)).astype(og_ref.dtype)


def _gdn_decode(qkv, z, small, ba_col, conv_buf, ssm, cw_t, gparams, norm_w, H, DK, DV):
    Bs, W = qkv.shape
    kern = functools.partial(_gdn_decode_kernel, H=H, DK=DK, DV=DV)
    nt = min(DECODE_TOKENS, Bs)
    og, s_new, buf_new = pl.pallas_call(
        kern, grid=(Bs // nt,),
        in_specs=[pl.BlockSpec((nt, 1, W), lambda b: (b, 0, 0)),
                  pl.BlockSpec((nt, 1, H * DV), lambda b: (b, 0, 0)),
                  pl.BlockSpec((nt, 1, LANE), lambda b: (b, 0, ba_col)),
                  pl.BlockSpec((nt, CONV_WIDTH - 1, W), lambda b: (b, 0, 0)),
                  pl.BlockSpec((nt, H, DK, DV), lambda b: (b, 0, 0, 0)),
                  pl.BlockSpec((CONV_WIDTH, W), lambda b: (0, 0)),
                  pl.BlockSpec((2, LANE), lambda b: (0, 0)),
                  pl.BlockSpec((1, DV), lambda b: (0, 0))],
        out_specs=[pl.BlockSpec((nt, 1, H * DV), lambda b: (b, 0, 0)),
                   pl.BlockSpec((nt, H, DK, DV), lambda b: (b, 0, 0, 0)),
                   pl.BlockSpec((nt, CONV_WIDTH - 1, W), lambda b: (b, 0, 0))],
        out_shape=[jax.ShapeDtypeStruct((Bs, 1, H * DV), BF16),
                   jax.ShapeDtypeStruct(ssm.shape, F32),
                   jax.ShapeDtypeStruct(conv_buf.shape, F32)],
        compiler_params=_cp(("parallel",)))(
            qkv.reshape(Bs, 1, W), z.reshape(Bs, 1, H * DV), small.reshape(Bs, 1, small.shape[1]),
            conv_buf, ssm, cw_t, gparams, norm_w.reshape(1, DV))
    return og.reshape(Bs, H * DV), s_new, buf_new


def _mla_prep_prompt_kernel(sm_ref, cos_ref, sin_ref, cost_ref, sint_ref, qnw_ref, kvnw_ref,
                            wqt_ref, wqrt_ref, wuk_ref, wuvt_ref,
                            qt_ref, k_ref, vt_ref, lat_ref, pe_ref, *, H, NOPE, ROPE, VH, QL, KVL, scale):
    sm = sm_ref[...]
    qn = (_rms(sm[:, :QL]) * qnw_ref[...]).astype(BF16)
    qft = lax.dot_general(wqt_ref[...], qn, NT, preferred_element_type=F32)
    qrt = lax.dot_general(wqrt_ref[...], qn, NT, preferred_element_type=F32)
    qpt = qft[H * NOPE:, :] * cost_ref[...] + qrt * sint_ref[...]
    lat = _rms(sm[:, QL:QL + KVL]) * kvnw_ref[...]
    lat_ref[...] = lat
    lat16 = lat.astype(BF16)
    kn = jnp.dot(lat16, wuk_ref[...], preferred_element_type=F32)
    vt = lax.dot_general(wuvt_ref[...], lat16, NT, preferred_element_type=F32)
    o = QL + KVL
    kr = sm[:, o:o + ROPE] * cos_ref[...] + sm[:, o + ROPE:o + 2 * ROPE] * sin_ref[...]
    pe_ref[...] = kr
    for h in range(H):
        qt_ref[h] = (jnp.concatenate([qft[h * NOPE:(h + 1) * NOPE, :], qpt[h * ROPE:(h + 1) * ROPE, :]], axis=0)
                     * scale).astype(qt_ref.dtype)
        k_ref[h] = jnp.concatenate([kn[:, h * NOPE:(h + 1) * NOPE], kr], axis=-1).astype(k_ref.dtype)
        vt_ref[h, 0] = vt[h * VH:(h + 1) * VH, :].astype(vt_ref.dtype)


def _mla_prep_prompt(small, cos, sin, qnw, kvnw, wq, wqr, wuk, wuv, T, tm, H, NOPE, ROPE, VH, QL, KVL, scale):
    n, ws = small.shape
    nt = T // tm
    kern = functools.partial(_mla_prep_prompt_kernel, H=H, NOPE=NOPE, ROPE=ROPE, VH=VH, QL=QL, KVL=KVL, scale=scale)
    full = lambda a: pl.BlockSpec(a.shape, lambda i: (0,) * a.ndim)
    dqk = NOPE + ROPE
    cos_t = jnp.tile(cos.T, (H, 1))
    sin_t = jnp.tile(sin.T, (H, 1))
    wqt, wqrt, wuvt = wq.T, wqr.T, wuv.T
    return pl.pallas_call(
        kern, grid=(n // tm,),
        in_specs=[pl.BlockSpec((tm, ws), lambda i: (i, 0)),
                  pl.BlockSpec((tm, ROPE), lambda i: (i % nt, 0)),
                  pl.BlockSpec((tm, ROPE), lambda i: (i % nt, 0)),
                  pl.BlockSpec((H * ROPE, tm), lambda i: (0, i % nt)),
                  pl.BlockSpec((H * ROPE, tm), lambda i: (0, i % nt)),
                  full(qnw), full(kvnw), full(wqt), full(wqrt), full(wuk), full(wuvt)],
        out_specs=[pl.BlockSpec((H, dqk, tm), lambda i: (0, 0, i)),
                   pl.BlockSpec((H, tm, dqk), lambda i: (0, i, 0)),
                   pl.BlockSpec((H, 1, VH, tm), lambda i: (0, i, 0, 0)),
                   pl.BlockSpec((tm, KVL), lambda i: (i, 0)),
                   pl.BlockSpec((tm, ROPE), lambda i: (i, 0))],
        out_shape=[jax.ShapeDtypeStruct((H, dqk, n), BF16), jax.ShapeDtypeStruct((H, n, dqk), BF16),
                   jax.ShapeDtypeStruct((H, n // tm, VH, tm), BF16), jax.ShapeDtypeStruct((n, KVL), F32),
                   jax.ShapeDtypeStruct((n, ROPE), F32)],
        compiler_params=_cp(("parallel",)))(small, cos, sin, cos_t, sin_t, qnw, kvnw, wqt, wqrt, wuk, wuvt)


def _mla_prep_sample_kernel(sm_ref, cos_ref, sin_ref, qnw_ref, kvnw_ref, wq_ref, wqr_ref,
                            qn_ref, qp_ref, lat_ref, pe_ref, *, H, NOPE, ROPE, QL, KVL, scale):
    sm = sm_ref[...]
    cos = cos_ref[...]
    sin = sin_ref[...]
    qn = _rms(sm[:, :QL]) * qnw_ref[...]
    qf = _bdot(qn, wq_ref[...])
    qr = _bdot(qn, wqr_ref[...])
    qn_ref[...] = qf[:, :H * NOPE].astype(qn_ref.dtype)
    lat_ref[...] = _rms(sm[:, QL:QL + KVL]) * kvnw_ref[...]
    o = QL + KVL
    pe_ref[...] = sm[:, o:o + ROPE] * cos + sm[:, o + ROPE:o + 2 * ROPE] * sin
    for h in range(H):
        p0 = H * NOPE + h * ROPE
        qp = qf[:, p0:p0 + ROPE] * cos + qr[:, h * ROPE:(h + 1) * ROPE] * sin
        qp_ref[:, h * ROPE:(h + 1) * ROPE] = (qp * scale).astype(qp_ref.dtype)


def _mla_prep_sample(small, cos, sin, qnw, kvnw, wq, wqr, H, NOPE, ROPE, QL, KVL, scale):
    n, ws = small.shape
    kern = functools.partial(_mla_prep_sample_kernel, H=H, NOPE=NOPE, ROPE=ROPE, QL=QL, KVL=KVL, scale=scale)
    full = lambda a: pl.BlockSpec(a.shape, lambda i: (0,) * a.ndim)
    return pl.pallas_call(
        kern, grid=(1,),
        in_specs=[full(small), full(cos), full(sin), full(qnw), full(kvnw), full(wq), full(wqr)],
        out_specs=[pl.BlockSpec((n, H * NOPE), lambda i: (0, 0)), pl.BlockSpec((n, H * ROPE), lambda i: (0, 0)),
                   pl.BlockSpec((n, KVL), lambda i: (0, 0)), pl.BlockSpec((n, ROPE), lambda i: (0, 0))],
        out_shape=[jax.ShapeDtypeStruct((n, H * NOPE), BF16), jax.ShapeDtypeStruct((n, H * ROPE), BF16),
                   jax.ShapeDtypeStruct((n, KVL), F32), jax.ShapeDtypeStruct((n, ROPE), F32)],
        compiler_params=_cp(("arbitrary",)))(small, cos, sin, qnw, kvnw, wq, wqr)


def _head_proj_kernel(x_ref, w_ref, o_ref, *, dims, scale):
    o_ref[...] = (_bdot_g(x_ref[...], w_ref[...], dims) * scale).astype(o_ref.dtype)


def _q_latent(q_nope, w_uk, scale):
    H, KVL, NOPE = w_uk.shape
    n = q_nope.shape[0]
    return pl.pallas_call(
        functools.partial(_head_proj_kernel, dims=NT, scale=scale), grid=(H,),
        in_specs=[pl.BlockSpec((n, NOPE), lambda h: (0, h)), pl.BlockSpec((None, KVL, NOPE), lambda h: (h, 0, 0))],
        out_specs=pl.BlockSpec((None, n, KVL), lambda h: (h, 0, 0)),
        out_shape=jax.ShapeDtypeStruct((H, n, KVL), BF16),
        compiler_params=_cp(("parallel",)))(q_nope, w_uk)


def _o_value(o_lat, w_uv):
    H, KVL, VH = w_uv.shape
    n = o_lat.shape[1]
    return pl.pallas_call(
        functools.partial(_head_proj_kernel, dims=(((1,), (0,)), ((), ())), scale=1.0), grid=(H,),
        in_specs=[pl.BlockSpec((None, n, KVL), lambda h: (h, 0, 0)), pl.BlockSpec((None, KVL, VH), lambda h: (h, 0, 0))],
        out_specs=pl.BlockSpec((n, VH), lambda h: (0, h)),
        out_shape=jax.ShapeDtypeStruct((n, H * VH), BF16),
        compiler_params=_cp(("parallel",)))(o_lat, w_uv)


FLASH_SPLIT = 2


FLASH_HEADS = 4


def _flash_kernel(qt_ref, k_ref, vt_ref, o_ref, m_ref, l_ref, acc_ref, *, tq):
    qi = pl.program_id(2)
    nh = qt_ref.shape[0]
    ns = FLASH_SPLIT
    hq = tq // ns
    vh = vt_ref.shape[2]
    chains = [(g, t) for g in range(nh) for t in range(ns)]
    m_ref[...] = jnp.full_like(m_ref, -jnp.inf)
    l_ref[...] = jnp.zeros_like(l_ref)
    acc_ref[...] = jnp.zeros_like(acc_ref)
    qt = [qt_ref[g, :, t * hq:(t + 1) * hq] for g, t in chains]

    def scores(off):
        k = [k_ref[g, pl.ds(off, tq), :] for g in range(nh)]
        return [jnp.dot(k[g], qt[c], preferred_element_type=F32) for c, (g, t) in enumerate(chains)]

    def update(st, j):
        cs = range(len(chains))
        vt = [vt_ref[g, j] for g in range(nh)]
        m_prev = [m_ref[c] for c in cs]
        m_new = [jnp.maximum(m_prev[c], jnp.max(st[c], axis=0, keepdims=True)) for c in cs]
        p = [jnp.exp(st[c] - m_new[c]) for c in cs]
        corr = [jnp.exp(m_prev[c] - m_new[c]) for c in cs]
        pv = [jnp.dot(vt[chains[c][0]], p[c].astype(BF16), preferred_element_type=F32) for c in cs]
        for c in cs:
            m_ref[c] = m_new[c]
            l_ref[c] = corr[c] * l_ref[c] + jnp.sum(p[c], axis=0, keepdims=True)
            acc_ref[c] = acc_ref[c] * corr[c] + pv[c]

    def below_diagonal(j, carry):
        update(scores(pl.multiple_of(j * tq, tq)), j)
        return carry

    lax.fori_loop(0, qi, below_diagonal, 0)

    st = scores(pl.multiple_of(qi * tq, tq))
    for c, (g, t) in enumerate(chains):
        key = lax.broadcasted_iota(jnp.int32, st[c].shape, 0)
        qry = t * hq + lax.broadcasted_iota(jnp.int32, st[c].shape, 1)
        st[c] = jnp.where(key <= qry, st[c], -jnp.inf)
    update(st, qi)
    for c, (g, t) in enumerate(chains):
        o_ref[t * hq:(t + 1) * hq, g * vh:(g + 1) * vh] = (acc_ref[c] / l_ref[c]).T.astype(o_ref.dtype)


def _flash_attention(qt, k, vt, B, T, tq):
    H, dqk, n = qt.shape
    VH = vt.shape[2]
    nq = T // tq
    hq = tq // FLASH_SPLIT
    nh = FLASH_HEADS
    nc = nh * FLASH_SPLIT
    kern = functools.partial(_flash_kernel, tq=tq)
    return pl.pallas_call(
        kern, grid=(B, H // nh, nq),
        in_specs=[pl.BlockSpec((nh, dqk, tq), lambda b, h, i: (h, 0, b * nq + i)),
                  pl.BlockSpec((nh, T, dqk), lambda b, h, i: (h, b, 0)),
                  pl.BlockSpec((nh, nq, VH, tq), lambda b, h, i: (h, b, 0, 0))],
        out_specs=pl.BlockSpec((tq, nh * VH), lambda b, h, i: (b * nq + i, h)),
        out_shape=jax.ShapeDtypeStruct((n, H * VH), BF16),
        scratch_shapes=[pltpu.VMEM((nc, 1, hq), F32), pltpu.VMEM((nc, 1, hq), F32), pltpu.VMEM((nc, VH, hq), F32)],
        compiler_params=_cp(("parallel", "parallel", "arbitrary")))(qt, k, vt)


def _paged_kernel(pt_ref, ql_ref, qp_ref, latn_ref, pen_ref, kv_hbm, pe_hbm, o_ref,
                  kv_buf, pe_buf, sem, m_ref, l_ref, acc_ref, *, R, G, layer):
    b = pl.program_id(0)
    g = pl.program_id(1)
    ng = pl.num_programs(1)
    step = b * ng + g
    slot = step % 2

    def page_copies(bb, gg, sl, real_pages):
        out = []
        for r in range(R):
            for i in range(G):
                page = pt_ref[bb * R + r, gg * G + i] if real_pages else 0
                out.append(pltpu.make_async_copy(kv_hbm.at[layer, page], kv_buf.at[sl, r * G + i], sem.at[0, sl]))
                out.append(pltpu.make_async_copy(pe_hbm.at[layer, page], pe_buf.at[sl, r * G + i], sem.at[1, sl]))
        return out

    @pl.when(step == 0)
    def _():
        for cp in page_copies(0, 0, 0, True):
            cp.start()

    @pl.when(step + 1 < pl.num_programs(0) * ng)
    def _():
        nxt = step + 1
        for cp in page_copies(nxt // ng, nxt % ng, 1 - slot, True):
            cp.start()

    for cp in page_copies(b, g, slot, False):
        cp.wait()

    @pl.when(g == 0)
    def _():
        m_ref[...] = jnp.full_like(m_ref, -jnp.inf)
        l_ref[...] = jnp.zeros_like(l_ref)
        acc_ref[...] = jnp.zeros_like(acc_ref)

    kvs = [[kv_buf[slot, r * G + i].astype(BF16) for i in range(G)] for r in range(R)]
    P = kvs[0][0].shape[0]
    s, p, corr, m_new = [], [], [], []
    for r in range(R):
        ql = ql_ref[r]
        qp = qp_ref[r]
        s.append(jnp.concatenate(
            [lax.dot_general(ql, kvs[r][i], NT, preferred_element_type=F32)
             + jnp.dot(qp, pe_buf[slot, r * G + i].astype(BF16), preferred_element_type=F32) for i in range(G)],
            axis=-1))
    for r in range(R):
        m_prev = m_ref[r]
        m_new.append(jnp.maximum(m_prev, jnp.max(s[r], axis=-1, keepdims=True)))
        corr.append(jnp.exp(m_prev - m_new[r]))
        p.append(jnp.exp(s[r] - m_new[r]))
    pv = []
    for r in range(R):
        t = None
        for i in range(G):
            d = jnp.dot(p[r][:, i * P:(i + 1) * P].astype(BF16), kvs[r][i], preferred_element_type=F32)
            t = d if t is None else t + d
        pv.append(t)
    for r in range(R):
        m_ref[r] = m_new[r]
        l_ref[r] = corr[r] * l_ref[r] + jnp.sum(p[r], axis=-1, keepdims=True)
        acc_ref[r] = acc_ref[r] * corr[r] + pv[r]

    @pl.when(g == pl.num_programs(1) - 1)
    def _():
        for r in range(R):
            latn = latn_ref[r]
            s_n = (jnp.sum(ql_ref[r].astype(F32) * latn, axis=-1, keepdims=True)
                   + jnp.sum(qp_ref[r].astype(F32) * pen_ref[r], axis=-1, keepdims=True))
            m_old = m_ref[r]
            m2 = jnp.maximum(m_old, s_n)
            c2 = jnp.exp(m_old - m2)
            p2 = jnp.exp(s_n - m2)
            o_ref[r] = ((acc_ref[r] * c2 + p2 * latn) / (l_ref[r] * c2 + p2)).astype(o_ref.dtype)


def _paged_attention(q_lat, q_pe, lat_new, pe_new, cache_kv, cache_pe_t, page_table, layer):
    Bs, H, KVL = q_lat.shape
    ROPE = q_pe.shape[2]
    n_pages = page_table.shape[1]
    P = cache_kv.shape[2]
    G = min(PAGE_GROUP, n_pages)
    R = min(PAGE_REQUESTS, Bs)

    req = lambda rows, w: pl.BlockSpec((R, rows, w), lambda b, g, pt: (b, 0, 0))
    hbm = pl.BlockSpec(memory_space=pl.ANY)
    grid_spec = pltpu.PrefetchScalarGridSpec(
        num_scalar_prefetch=1, grid=(Bs // R, n_pages // G),
        in_specs=[req(H, KVL), req(H, ROPE), req(1, KVL), req(1, ROPE), hbm, hbm],
        out_specs=req(H, KVL),
        scratch_shapes=[pltpu.VMEM((2, R * G, P, KVL), cache_kv.dtype), pltpu.VMEM((2, R * G, ROPE, P), cache_pe_t.dtype),
                        pltpu.SemaphoreType.DMA((2, 2)),
                        pltpu.VMEM((R, H, 1), F32), pltpu.VMEM((R, H, 1), F32), pltpu.VMEM((R, H, KVL), F32)])
    return pl.pallas_call(
        functools.partial(_paged_kernel, R=R, G=G, layer=layer), grid_spec=grid_spec,
        out_shape=jax.ShapeDtypeStruct((Bs, H, KVL), BF16),
        compiler_params=_cp(("arbitrary", "arbitrary")))(
            page_table, q_lat, q_pe, lat_new.reshape(Bs, 1, KVL), pe_new.reshape(Bs, 1, ROPE), cache_kv, cache_pe_t)


def _post_mixer_kernel(x_ref, og_ref, om_ref, gab_ref, g1_ref, sc2_ref, sh2_ref, wog_ref, wom_ref, wout_ref,
                       nw_ref, wr_ref, br_ref, x1_ref, h2_ref, te_ref, tg_ref, *, D, E):
    gab = gab_ref[...].astype(F32)
    merged = (jax.nn.sigmoid(gab[:, :D]) * _bdot(og_ref[...], wog_ref[...])
              + jax.nn.sigmoid(gab[:, D:]) * _bdot(om_ref[...], wom_ref[...]))
    x1 = x_ref[...] + g1_ref[...] * _bdot(merged, wout_ref[...])
    x1_ref[...] = x1
    h2 = _rms(x1) * nw_ref[...] * (1.0 + sc2_ref[...]) + sh2_ref[...]
    h2_ref[...] = h2
    logits = _dot3(_split_bf16(h2), _split_bf16(wr_ref[...])) + br_ref[...]
    lane = lax.broadcasted_iota(jnp.int32, logits.shape, 1)
    logits = jnp.where(lane < E, logits, -jnp.inf)
    te = jnp.zeros(logits.shape, jnp.int32)
    ex = jnp.zeros(logits.shape, F32)
    top = None
    for k in range(TOP_K):
        m = jnp.max(logits, axis=-1, keepdims=True)
        idx = jnp.min(jnp.where(logits == m, lane, LANE), axis=-1, keepdims=True)
        top = m if top is None else top
        te = jnp.where(lane == k, idx, te)
        ex = jnp.where(lane == k, jnp.exp(m - top), ex)
        logits = jnp.where(lane == idx, -jnp.inf, logits)
    te_ref[...] = te
    tg_ref[...] = ex / jnp.sum(ex, axis=-1, keepdims=True)


def _post_mixer(x, og, om, gab, grp, wog, wom, wout, norm_w, wr, br, E):
    n, d = x.shape
    tm = grp.tm
    rowblk = lambda w: pl.BlockSpec((tm, w), lambda i: (i, 0))
    full = lambda a: pl.BlockSpec(a.shape, lambda i: (0,) * a.ndim)
    return pl.pallas_call(
        functools.partial(_post_mixer_kernel, D=d, E=E), grid=(n // tm,),
        in_specs=[rowblk(d), rowblk(d), rowblk(d), rowblk(2 * d), grp.spec(2, d), grp.spec(4, d), grp.spec(3, d),
                  full(wog), full(wom), full(wout), full(norm_w), full(wr), full(br)],
        out_specs=[rowblk(d), rowblk(d), rowblk(LANE), rowblk(LANE)],
        out_shape=[jax.ShapeDtypeStruct((n, d), F32), jax.ShapeDtypeStruct((n, d), F32),
                   jax.ShapeDtypeStruct((n, LANE), jnp.int32), jax.ShapeDtypeStruct((n, LANE), F32)],
        compiler_params=_cp(("parallel",)))(x, og, om, gab, grp.mod3, grp.mod3, grp.mod3,
                                            wog, wom, wout, norm_w, wr, br)


def _dispatch_kernel(pe_ref, slot_ref, ha_ref, hb_ref, hs_ref, zero_ref, sem, zsem, *, tb, tm, E, nb_a):
    i = pl.program_id(0)
    n_blocks = hs_ref.shape[0] // tm

    def zero_copy(row0):
        return pltpu.make_async_copy(zero_ref, hs_ref.at[pl.ds(pl.multiple_of(row0, tm), tm)], zsem)

    zero_jobs = [(pe_ref[e + 1] > pe_ref[e], pe_ref[e + 1] - tm) for e in range(E)]
    zero_jobs += [(b * tm >= pe_ref[E], b * tm) for b in range(n_blocks - E, n_blocks)]

    @pl.when(i == 0)
    def _():
        zero_ref[...] = jnp.zeros_like(zero_ref)
        for wanted, row0 in zero_jobs:
            @pl.when(wanted)
            def _():
                zero_copy(row0).start()
        for wanted, row0 in zero_jobs:
            @pl.when(wanted)
            def _():
                zero_copy(row0).wait()

    def scatter_rows(h_ref):
        for r in range(tb):
            for k in range(TOP_K):
                a = r * TOP_K + k
                pltpu.make_async_copy(h_ref.at[pl.ds(r, 1)], hs_ref.at[pl.ds(slot_ref[0, 0, a], 1)],
                                      sem).start(priority=a % 2)
        for k in range(TOP_K):
            pltpu.make_async_copy(h_ref, hs_ref.at[pl.ds(0, tb)], sem).wait()

    @pl.when(i < nb_a)
    def _():
        scatter_rows(ha_ref)

    @pl.when(i >= nb_a)
    def _():
        scatter_rows(hb_ref)


def _dispatch(ha, hb, slots3, pad_edges, n_slots, tb, tm):
    d = ha.shape[1]
    nb_a, nb_b = ha.shape[0] // tb, hb.shape[0] // tb
    nb = nb_a + nb_b
    E = pad_edges.shape[0] - 1
    grid_spec = pltpu.PrefetchScalarGridSpec(
        num_scalar_prefetch=1, grid=(nb,),
        in_specs=[pl.BlockSpec((1, 1, tb * TOP_K), lambda i, pe: (i, 0, 0), memory_space=pltpu.SMEM),
                  pl.BlockSpec((tb, d), lambda i, pe: (jnp.minimum(i, nb_a - 1), 0)),
                  pl.BlockSpec((tb, d), lambda i, pe: (jnp.maximum(i - nb_a, 0), 0))],
        out_specs=pl.BlockSpec(memory_space=pl.ANY),
        scratch_shapes=[pltpu.VMEM((tm, d), ha.dtype), pltpu.SemaphoreType.DMA, pltpu.SemaphoreType.DMA])
    return pl.pallas_call(
        functools.partial(_dispatch_kernel, tb=tb, tm=tm, E=E, nb_a=nb_a), grid_spec=grid_spec,
        out_shape=jax.ShapeDtypeStruct((n_slots, d), ha.dtype),
        compiler_params=_cp(("arbitrary",)))(pad_edges, slots3, ha, hb)


def _expert_kernel(be_ref, na_ref, x_ref, wgu_ref, bgu_ref, wdn_ref, bdn_ref, y_ref, wgu_s, wdn_s, *, DE):
    i = pl.program_id(0)
    active = i < na_ref[0]
    first = jnp.logical_or(i == 0, be_ref[i] != be_ref[jnp.maximum(i - 1, 0)])

    @pl.when(jnp.logical_and(active, first))
    def _():
        wgu_s[...] = wgu_ref[...].astype(BF16)
        wdn_s[...] = wdn_ref[...].astype(BF16)

    @pl.when(active)
    def _():
        gu = jnp.dot(x_ref[...].astype(BF16), wgu_s[...], preferred_element_type=F32) + bgu_ref[...]
        gt = jnp.minimum(gu[:, :DE], SWIGLU_LIMIT)
        up = jnp.clip(gu[:, DE:], -SWIGLU_LIMIT, SWIGLU_LIMIT)
        act = (up + 1.0) * gt * jax.nn.sigmoid(SWIGLU_ALPHA * gt)
        y_ref[...] = jnp.dot(act.astype(BF16), wdn_s[...], preferred_element_type=F32) + bdn_ref[...]

    @pl.when(jnp.logical_not(active))
    def _():
        y_ref[...] = jnp.zeros_like(y_ref)


def _experts(hs, blk_e, n_active, w_gu, b_gu, w_dn, b_dn, layer, tm):
    n_slots, d = hs.shape
    E, _, de2 = w_gu.shape[1:]
    de = de2 // 2
    grid_spec = pltpu.PrefetchScalarGridSpec(
        num_scalar_prefetch=2, grid=(n_slots // tm,),
        in_specs=[pl.BlockSpec((tm, d), lambda i, be, na: (jnp.minimum(i, na[0] - 1), 0)),
                  pl.BlockSpec((None, None, d, de2), lambda i, be, na: (layer, be[i], 0, 0)),
                  pl.BlockSpec((None, None, 1, de2), lambda i, be, na: (layer, be[i], 0, 0)),
                  pl.BlockSpec((None, None, de, d), lambda i, be, na: (layer, be[i], 0, 0)),
                  pl.BlockSpec((None, None, 1, d), lambda i, be, na: (layer, be[i], 0, 0))],
        out_specs=pl.BlockSpec((tm, d), lambda i, be, na: (i, 0)),
        scratch_shapes=[pltpu.VMEM((d, de2), BF16), pltpu.VMEM((de, d), BF16)])
    L = w_gu.shape[0]
    return pl.pallas_call(
        functools.partial(_expert_kernel, DE=de), grid_spec=grid_spec,
        out_shape=jax.ShapeDtypeStruct((n_slots, d), F32),
        compiler_params=_cp(("arbitrary",), VMEM_LIMIT_MOE))(
            blk_e, n_active, hs, w_gu, b_gu.reshape(L, E, 1, de2), w_dn, b_dn.reshape(L, E, 1, d))


def _combine_kernel(slotc_ref, slotn_ref, tg_ref, x1_ref, g2_ref, scf_ref, shf_ref, nwf_ref, yp_ref, o_ref,
                    buf, sem, *, tb):
    i = pl.program_id(0)
    slot = i % 2

    def gather_starts(slot_ref, sl):
        for r in range(tb):
            for k in range(TOP_K):
                a = r * TOP_K + k
                pltpu.make_async_copy(yp_ref.at[pl.ds(slot_ref[0, 0, a], 1)], buf.at[sl, k, pl.ds(r, 1)],
                                      sem.at[sl]).start(priority=a % 2)

    @pl.when(i == 0)
    def _():
        gather_starts(slotc_ref, 0)

    @pl.when(i + 1 < pl.num_programs(0))
    def _():
        gather_starts(slotn_ref, 1 - slot)

    for k in range(TOP_K):
        pltpu.make_async_copy(yp_ref.at[pl.ds(0, tb)], buf.at[slot, k], sem.at[slot]).wait()
    tg = tg_ref[...]
    moe = tg[:, 0:1] * buf[slot, 0]
    for k in range(1, TOP_K):
        moe = moe + tg[:, k:k + 1] * buf[slot, k]
    x2 = x1_ref[...] + g2_ref[...] * moe
    o_ref[...] = _rms(x2) * nwf_ref[...] * (1.0 + scf_ref[...]) + shf_ref[...]


def _combine_final(y_pad, slots3, first_block, tg, x1, grp, norm_final_w):
    n, d = x1.shape
    tb = grp.tm
    nb = n // tb
    rf, bpg = grp.modf3.shape[1], grp.bpg
    fspec = lambda j: pl.BlockSpec((None, rf, d), lambda i: (i // bpg, 0, j))
    sspec = lambda f: pl.BlockSpec((1, 1, tb * TOP_K), lambda i: (first_block + f(i), 0, 0),
                                   memory_space=pltpu.SMEM)
    return pl.pallas_call(
        functools.partial(_combine_kernel, tb=tb), grid=(nb,),
        in_specs=[sspec(lambda i: i), sspec(lambda i: jnp.minimum(i + 1, nb - 1)),
                  pl.BlockSpec((tb, LANE), lambda i: (i, 0)),
                  pl.BlockSpec((tb, d), lambda i: (i, 0)),
                  grp.spec(5, d), fspec(1), fspec(0),
                  pl.BlockSpec((1, d), lambda i: (0, 0)),
                  pl.BlockSpec(memory_space=pl.ANY)],
        out_specs=pl.BlockSpec((tb, d), lambda i: (i, 0)),
        out_shape=jax.ShapeDtypeStruct((n, d), F32),
        scratch_shapes=[pltpu.VMEM((2, TOP_K, tb, d), F32), pltpu.SemaphoreType.DMA((2,))],
        compiler_params=_cp(("arbitrary",)))(
            slots3, slots3, tg, x1, grp.mod3, grp.modf3, grp.modf3, norm_final_w.reshape(1, d), y_pad)


def _route(top_e, E, tm):
    n = top_e.shape[0]
    hit = top_e[:, :, None] == jnp.arange(E, dtype=jnp.int32)
    tok_oh = jnp.sum(hit.astype(jnp.int32), axis=1)
    csum = jnp.cumsum(tok_oh, axis=0)
    counts = csum[-1]
    padded = (counts + tm - 1) // tm * tm
    pad_end = jnp.cumsum(padded)
    slots = jnp.sum(jnp.where(hit, (csum - tok_oh + (pad_end - padded)[None, :])[:, None, :], 0), axis=2)
    n_blocks = -(-(n * TOP_K) // tm) + E
    blk_start = jnp.arange(n_blocks, dtype=jnp.int32) * tm
    blk_e = jnp.minimum(jnp.sum((pad_end[None, :] <= blk_start[:, None]).astype(jnp.int32), axis=1), E - 1)
    n_active = (pad_end[-1] // tm).reshape(1)
    pad_edges = jnp.concatenate([jnp.zeros((1,), pad_end.dtype), pad_end])
    return (slots.astype(jnp.int32), pad_edges.astype(jnp.int32), blk_e.astype(jnp.int32),
            n_active.astype(jnp.int32), n_blocks * tm)


def _rope_tables(pos, rope):
    half = rope // 2
    inv = ROPE_THETA ** (-jnp.arange(half, dtype=F32) / half)
    ang = pos.astype(F32)[:, None] * inv[None, :]
    cos, sin = jnp.cos(ang), jnp.sin(ang)
    return jnp.concatenate([cos, cos], axis=-1), jnp.concatenate([sin, sin], axis=-1)


def _rotate_cols(w, rope):
    k, n = w.shape
    w3 = w.reshape(k, n // rope, rope)
    half = rope // 2
    return jnp.concatenate([-w3[..., half:], w3[..., :half]], axis=-1).reshape(k, n)


def kernel(x_prompt, x_sample, c_prompt, c_sample, cache_kv, cache_pe, state_ssm, state_conv, page_table, norm_mix_w, norm_ffn_w, w_ada, b_ada, w_in, conv_w, A_log, dt_bias, gdn_norm_w, w_o_gdn, q_norm_w, kv_norm_w, w_uq, w_uk, w_uv, w_o_mla, w_out, w_router, b_router, w_gu, b_gu, w_dn, b_dn, w_ada_final, b_ada_final, norm_final_w):
    B, T, D = x_prompt.shape
    Bs, Ts, _ = x_sample.shape
    assert Ts == 1
    depth = w_in.shape[0]
    H, DK, DV = state_ssm.shape[2:]
    QK = H * DK
    CONV = state_conv.shape[3]
    assert CONV == 2 * QK + H * DV and H == SUBLANES
    QL = q_norm_w.shape[1]
    HM, KVL, NOPE = w_uk.shape[1:]
    VH = w_uv.shape[3]
    ROPE = cache_pe.shape[3]
    E = w_router.shape[2]
    scale = float(NOPE + ROPE) ** -0.5
    n_p, n_s = B * T, Bs * Ts
    past_len = page_table.shape[1] * cache_kv.shape[2]
    cache_pe_t = jnp.swapaxes(cache_pe, 2, 3)

    c_all = jnp.concatenate([c_prompt, c_sample], axis=0)
    modf = _matmul(c_all, w_ada_final, F32, c_all.shape[0], 1024, b_ada_final)
    tm_p = min(512, T)
    tb = 128
    cos_p, sin_p = _rope_tables(jnp.arange(T), ROPE)
    cos_s, sin_s = _rope_tables(past_len + jnp.arange(Ts), ROPE)

    hp = x_prompt.reshape(n_p, D)
    hs = x_sample.reshape(n_s, D)
    outs = {k: [] for k in ('kv_p', 'pe_p', 'ssm_p', 'conv_p', 'kv_s', 'pe_s', 'ssm_s', 'conv_s')}
    for l in range(depth):
        mod = _matmul(c_all, w_ada[l], F32, c_all.shape[0], 1024, b_ada[l])
        last = l == depth - 1
        grp_p = _Group(mod[:B].reshape(B, 1, 6 * D), modf[:B].reshape(B, 1, 2 * D), tm_p, T // tm_p)
        grp_s = _Group(mod[B:].reshape(1, Bs, 6 * D), modf[B:].reshape(1, Bs, 2 * D), Bs, 1)
        grp_pc = _Group(grp_p.mod3, grp_p.modf3, tb, T // tb)

        offs = [0]
        for s in (CONV, H * DV, H, H, QL, KVL, ROPE, D, D):
            offs.append(offs[-1] + s)
        wi = w_in[l]
        seg = lambda i: wi[:, offs[i]:offs[i + 1]]
        w_qkv = seg(0).astype(BF16)
        w_z = seg(1).astype(BF16)
        w_gab = jnp.concatenate([seg(7), seg(8)], axis=1).astype(BF16)
        n_small = QL + KVL + 2 * ROPE
        ba_col = -(-n_small // LANE)
        w_small = jnp.concatenate(
            [seg(4), seg(5), seg(6), _rotate_cols(seg(6), ROPE), jnp.zeros((D, ba_col * LANE - n_small), F32),
             seg(2), jnp.zeros((D, 8 - H), F32), seg(3), jnp.zeros((D, LANE - 8 - H), F32)], axis=1).astype(BF16)
        cw_t = conv_w[l].T
        gparams = jnp.zeros((2, LANE), F32).at[0, 8:8 + H].set(A_log[l]).at[1, 8:8 + H].set(dt_bias[l])
        wq = w_uq[l].reshape(QL, HM, NOPE + ROPE)
        wq_pe = wq[:, :, NOPE:].reshape(QL, HM * ROPE)
        wq_all = jnp.concatenate([wq[:, :, :NOPE].reshape(QL, HM * NOPE), wq_pe], axis=1).astype(BF16)
        wq_rot = _rotate_cols(wq_pe, ROPE).astype(BF16)
        wuk_all = jnp.transpose(w_uk[l], (1, 0, 2)).reshape(KVL, HM * NOPE).astype(BF16)
        wuv_all = jnp.transpose(w_uv[l], (1, 0, 2)).reshape(KVL, HM * VH).astype(BF16)
        qnw = q_norm_w[l].reshape(1, QL)
        kvnw = kv_norm_w[l].reshape(1, KVL)
        wog = w_o_gdn[l].astype(BF16)
        wom = w_o_mla[l].astype(BF16)
        wout = w_out[l].astype(BF16)
        wr = jnp.pad(w_router[l], ((0, 0), (0, LANE - E)))
        br = jnp.pad(b_router[l], (0, LANE - E)).reshape(1, LANE)
        nfw = norm_ffn_w[l].reshape(1, D)

        h1 = _norm_mod(hp, norm_mix_w[l], grp_p, 1, 0, BF16)
        qkv_p = _matmul(h1, w_qkv, BF16, 1024, 512)
        z_p = _matmul(h1, w_z, BF16, 1024, 512)
        gab_p = _matmul(h1, w_gab, BF16, 1024, 512)
        small_p = _matmul(h1, w_small, F32, 1024, w_small.shape[1])
        og_p, ssm_p = _gdn_prompt(qkv_p, z_p, small_p, ba_col, cw_t, gparams, gdn_norm_w[l], B, T, H, DK, DV)
        q_p, k_p, v_p, lat_p, pe_p = _mla_prep_prompt(small_p, cos_p, sin_p, qnw, kvnw, wq_all, wq_rot, wuk_all,
                                                      wuv_all, T, tm_p, HM, NOPE, ROPE, VH, QL, KVL, scale)
        om_p = _flash_attention(q_p, k_p, v_p, B, T, tm_p)
        x1_p, h2_p, te_p, tg_p = _post_mixer(hp, og_p, om_p, gab_p, grp_p, wog, wom, wout, nfw, wr, br, E)
        outs['kv_p'].append(lat_p.reshape(B, T, KVL))
        outs['pe_p'].append(pe_p.reshape(B, T, ROPE))
        outs['ssm_p'].append(ssm_p)
        outs['conv_p'].append(qkv_p.reshape(B, T, CONV)[:, T - (CONV_WIDTH - 1):, :].astype(F32))

        h1s = _norm_mod(hs, norm_mix_w[l], grp_s, 1, 0, BF16)
        qkv_s = _matmul(h1s, w_qkv, F32, Bs, 512)
        z_s = _matmul(h1s, w_z, BF16, Bs, 512)
        gab_s = _matmul(h1s, w_gab, BF16, Bs, 512)
        small_s = _matmul(h1s, w_small, F32, Bs, w_small.shape[1])
        og_s, ssm_s, conv_s = _gdn_decode(qkv_s, z_s, small_s, ba_col, state_conv[l], state_ssm[l], cw_t, gparams,
                                          gdn_norm_w[l], H, DK, DV)
        qn_s, qp_s, lat_s, pe_s = _mla_prep_sample(small_s, cos_s, sin_s, qnw, kvnw, wq_all, wq_rot,
                                                   HM, NOPE, ROPE, QL, KVL, scale)
        q_lat = jnp.transpose(_q_latent(qn_s, w_uk[l], scale), (1, 0, 2))
        o_lat = _paged_attention(q_lat, qp_s.reshape(n_s, HM, ROPE), lat_s, pe_s, cache_kv, cache_pe_t, page_table, l)
        om_s = _o_value(jnp.transpose(o_lat, (1, 0, 2)), w_uv[l])
        x1_s, h2_s, te_s, tg_s = _post_mixer(hs, og_s, om_s, gab_s, grp_s, wog, wom, wout, nfw, wr, br, E)
        outs['kv_s'].append(lat_s.reshape(Bs, Ts, KVL))
        outs['pe_s'].append(pe_s.reshape(Bs, Ts, ROPE))
        outs['ssm_s'].append(ssm_s)
        outs['conv_s'].append(conv_s)

        tm_e = 256
        top_e = jnp.concatenate([te_p[:, :TOP_K], te_s[:, :TOP_K]], axis=0)
        slots, pad_edges, blk_e, n_active, n_slots = _route(top_e, E, tm_e)
        slots3 = slots.reshape((n_p + n_s) // tb, 1, tb * TOP_K)
        h_sorted = _dispatch(h2_p, h2_s, slots3, pad_edges, n_slots, tb, tm_e)
        y_pad = _experts(h_sorted, blk_e, n_active, w_gu, b_gu, w_dn, b_dn, l, tm_e)
        if not last:
            raise NotImplementedError("stacked layers need an un-normalised residual output")
        assert grp_s.tm == tb
        hp = _combine_final(y_pad, slots3, 0, tg_p, x1_p, grp_pc, norm_final_w)
        hs = _combine_final(y_pad, slots3, n_p // tb, tg_s, x1_s, grp_s, norm_final_w)

    st = lambda k: jnp.stack(outs[k])
    return (hp.reshape(B, T, D), hs.reshape(Bs, Ts, D),
            st('kv_p'), st('pe_p'), st('ssm_p'), st('conv_p'),
            st('kv_s'), st('pe_s'), st('ssm_s'), st('conv_s'))
```

```python
import functools

import jax
import jax.numpy as jnp
from jax import lax
from jax.experimental import pallas as pl
from jax.experimental.pallas import tpu as pltpu

F32 = jnp.float32
BF16 = jnp.bfloat16
HI = lax.Precision.HIGHEST

NORM_EPS = 1e-6
ROPE_THETA = 10000.0
TOP_K = 4
SWIGLU_LIMIT = 7.0
SWIGLU_ALPHA = 1.702
CONV_WIDTH = 4
GDN_CHUNK = 64
PAGE_GROUP = 8
PAGE_REQUESTS = 4
LANE = 128
VMEM_LIMIT = 48 * 1024 * 1024
VMEM_LIMIT_MOE = 58 * 1024 * 1024

NT = (((1,), (1,)), ((), ()))
TN = (((0,), (0,)), ((), ()))


def _cp(sem, vmem=VMEM_LIMIT):
    return pltpu.CompilerParams(dimension_semantics=sem, vmem_limit_bytes=vmem)


def _rms(x):
    return x * lax.rsqrt(jnp.mean(x * x, axis=-1, keepdims=True) + NORM_EPS)


def _softplus(x):
    return jnp.maximum(x, 0.0) + jnp.log1p(jnp.exp(-jnp.abs(x)))


def _silu(x):
    return x * jax.nn.sigmoid(x)


def _bdot(a, b):
    return jnp.dot(a.astype(BF16), b.astype(BF16), preferred_element_type=F32)


def _bdot_g(a, b, dims):
    return lax.dot_general(a.astype(BF16), b.astype(BF16), dims, preferred_element_type=F32)


def _mm_kernel(*refs, has_bias):
    if has_bias:
        x_ref, w_ref, b_ref, o_ref = refs
    else:
        x_ref, w_ref, o_ref = refs
    acc = _bdot(x_ref[...], w_ref[...])
    if has_bias:
        acc = acc + b_ref[...]
    o_ref[...] = acc.astype(o_ref.dtype)


def _matmul(x, w, out_dtype, tm, tn, bias=None):
    M, K = x.shape
    N = w.shape[1]
    tm, tn = min(tm, M), min(tn, N)
    in_specs = [pl.BlockSpec((tm, K), lambda i, j: (i, 0)), pl.BlockSpec((K, tn), lambda i, j: (0, j))]
    args = [x, w]
    if bias is not None:
        in_specs.append(pl.BlockSpec((1, tn), lambda i, j: (0, j)))
        args.append(bias.reshape(1, N))
    return pl.pallas_call(
        functools.partial(_mm_kernel, has_bias=bias is not None),
        grid=(M // tm, N // tn), in_specs=in_specs,
        out_specs=pl.BlockSpec((tm, tn), lambda i, j: (i, j)),
        out_shape=jax.ShapeDtypeStruct((M, N), out_dtype),
        compiler_params=_cp(("parallel", "parallel")))(*args)


class _Group:
    def __init__(self, mod3, modf3, tm, blocks_per_g):
        self.mod3, self.modf3, self.tm, self.bpg = mod3, modf3, tm, blocks_per_g

    def spec(self, j, d):
        r, bpg = self.mod3.shape[1], self.bpg
        return pl.BlockSpec((None, r, d), lambda i: (i // bpg, 0, j))


def _norm_mod_kernel(x_ref, w_ref, sc_ref, sh_ref, o_ref):
    y = _rms(x_ref[...]) * w_ref[...]
    o_ref[...] = (y * (1.0 + sc_ref[...]) + sh_ref[...]).astype(o_ref.dtype)


def _norm_mod(x, w, grp, j_scale, j_shift, out_dtype):
    n, d = x.shape
    tm = grp.tm
    return pl.pallas_call(
        _norm_mod_kernel, grid=(n // tm,),
        in_specs=[pl.BlockSpec((tm, d), lambda i: (i, 0)), pl.BlockSpec((1, d), lambda i: (0, 0)),
                  grp.spec(j_scale, d), grp.spec(j_shift, d)],
        out_specs=pl.BlockSpec((tm, d), lambda i: (i, 0)),
        out_shape=jax.ShapeDtypeStruct((n, d), out_dtype),
        compiler_params=_cp(("parallel",)))(x, w.reshape(1, d), grp.mod3, grp.mod3)


SUBLANES = 8
PREP_CHUNKS = 2
SCAN_SEQS = 4
PREV_ROWS = 16


def _split_bf16(x):
    hi = x.astype(BF16)
    return hi, (x - hi.astype(F32)).astype(BF16)


def _dot3(a, b):
    d = lambda x, y: jnp.dot(x, y, preferred_element_type=F32)
    return d(a[0], b[0]) + d(a[0], b[1]) + d(a[1], b[0])


def _expand_matrix(C):
    k = jnp.arange(C)
    n = jnp.arange(SUBLANES * LANE)
    rem = n % LANE
    hit = ((k[:, None] // SUBLANES == rem[None, :] // SUBLANES) & (k[:, None] % SUBLANES == n[None, :] // LANE)
           & (rem[None, :] < C))
    return hit.astype(BF16)


def _unit_lower_inverses(a_list, at_list, g_ref, C):
    nh = len(a_list)
    row = lax.broadcasted_iota(jnp.int32, (C, C), 0)
    col = lax.broadcasted_iota(jnp.int32, (C, C), 1)
    blockdiag = (row // SUBLANES) == (col // SUBLANES)
    packed = []
    for at in at_list:
        m = jnp.where(blockdiag, at, 0.0)
        d = m[0:SUBLANES]
        for b in range(1, C // SUBLANES):
            d = d + m[SUBLANES * b:SUBLANES * (b + 1)]
        packed.append(d)
    stack = _split_bf16(jnp.concatenate(packed, axis=0))
    g = g_ref[...]
    coef = (jnp.dot(stack[0], g, preferred_element_type=F32) + jnp.dot(stack[1], g, preferred_element_type=F32))
    sub = lax.broadcasted_iota(jnp.int32, (SUBLANES, LANE), 0)
    lane = lax.broadcasted_iota(jnp.int32, (SUBLANES, LANE), 1)
    unit = jnp.where((lane % SUBLANES == sub) & (lane < C), 1.0, 0.0)
    xd = [unit] * nh
    for i in range(1, SUBLANES):
        e_i = jnp.where((lane % SUBLANES == i) & (lane < C), 1.0, 0.0)[0:1]
        for h in range(nh):
            c_i = coef[SUBLANES * h:SUBLANES * (h + 1), LANE * i:LANE * (i + 1)]
            new_row = e_i - jnp.sum(c_i * xd[h], axis=0, keepdims=True)
            xd[h] = jnp.where(sub == i, new_row, xd[h])
    x = [jnp.where(blockdiag, jnp.concatenate([xd[h][:, :C]] * (C // SUBLANES), axis=0), 0.0) for h in range(nh)]
    a16 = [a.astype(BF16) for a in a_list]
    zero = jnp.zeros((C, C), BF16)
    s = SUBLANES
    while s < C:
        below = ((row // s) % 2 == 1) & ((col // s) == (row // s) - 1)
        x16 = [x[h].astype(BF16) for h in range(nh)]
        m1 = [jnp.dot(x16[h], jnp.where(below, a16[h], zero), preferred_element_type=F32) for h in range(nh)]
        x = [x[h] - jnp.dot(m1[h].astype(BF16), x16[h], preferred_element_type=F32) for h in range(nh)]
        s *= 2
    return x


def _gdn_prep_kernel(qkv_ref, prev_ref, sm_ref, cw_ref, gp_ref, g_ref,
                     u_ref, w_ref, qe_ref, kd_ref, aqk_ref, dec_ref, *, H, DK, DV, C):
    c = pl.program_id(1)
    QK = H * DK
    P = PREV_ROWS
    R = qkv_ref.shape[0]

    prev = prev_ref[...]
    xin = jnp.concatenate([jnp.where(c == 0, jnp.zeros_like(prev), prev), qkv_ref[...]], axis=0)
    out_row = lax.broadcasted_iota(jnp.int32, (R, P + R), 0)
    in_row = lax.broadcasted_iota(jnp.int32, (R, P + R), 1)
    y = None
    for j in range(CONV_WIDTH):
        shift = (in_row == out_row + (P - (CONV_WIDTH - 1) + j)).astype(BF16)
        term = jnp.dot(shift, xin, preferred_element_type=F32) * cw_ref[j:j + 1, :]
        y = term if y is None else y + term
    y = _silu(y)

    sm = sm_ref[...]
    beta_all = jax.nn.sigmoid(sm)
    g_all = -jnp.exp(gp_ref[0:1, :]) * _softplus(sm + gp_ref[1:2, :])
    row_r = lax.broadcasted_iota(jnp.int32, (R, R), 0)
    col_r = lax.broadcasted_iota(jnp.int32, (R, R), 1)
    tril = ((col_r <= row_r) & (col_r // C == row_r // C)).astype(F32)
    gam_all = jnp.dot(tril, g_all, precision=HI, preferred_element_type=F32)
    lane = lax.broadcasted_iota(jnp.int32, sm.shape, 1)
    packed = jnp.where(lane < 8, beta_all, gam_all)
    row = lax.broadcasted_iota(jnp.int32, (C, C), 0)
    col = lax.broadcasted_iota(jnp.int32, (C, C), 1)

    a_list, at_list, rhs_list, places = [], [], [], []
    for ci, h in [(ci, h) for ci in range(R // C) for h in range(H)]:
        rs = slice(ci * C, (ci + 1) * C)
        rows_t = packed[rs].T
        qh = y[rs, h * DK:(h + 1) * DK]
        kh = y[rs, QK + h * DK:QK + (h + 1) * DK]
        vh = y[rs, 2 * QK + h * DV:2 * QK + (h + 1) * DV]
        qh = qh * lax.rsqrt(jnp.sum(qh * qh, axis=-1, keepdims=True) + NORM_EPS) * (DK ** -0.5)
        kh = kh * lax.rsqrt(jnp.sum(kh * kh, axis=-1, keepdims=True) + NORM_EPS)
        beta_c = beta_all[rs, h:h + 1]
        gam_c = gam_all[rs, 8 + h:9 + h]
        beta_r = rows_t[h:h + 1, :]
        gam_r = rows_t[8 + h:9 + h, :]
        dm = gam_c - gam_r
        decay = jnp.exp(jnp.where(col <= row, dm, -jnp.inf))
        decay_t = jnp.exp(jnp.where(row < col, -dm, -jnp.inf))
        qk_kk = _bdot_g(jnp.concatenate([qh, kh], axis=0), kh, NT)
        kk = qk_kk[C:]
        a_list.append(jnp.where(col < row, kk * beta_c * decay, 0.0))
        at_list.append(kk * beta_r * decay_t)
        egam = jnp.exp(gam_c)
        rhs_list.append(jnp.concatenate([beta_c * vh, beta_c * egam * kh], axis=1).astype(BF16))
        g_last = gam_c[C - 1:C, :]
        qe_ref[rs, h * DK:(h + 1) * DK] = (qh * egam).astype(qe_ref.dtype)
        kd_ref[rs, h * DK:(h + 1) * DK] = (kh * jnp.exp(g_last - gam_c)).astype(kd_ref.dtype)
        aqk_ref[rs, h * C:(h + 1) * C] = (qk_kk[:C] * decay).astype(aqk_ref.dtype)
        dec_ref[ci, h:h + 1, :] = jnp.broadcast_to(jnp.exp(g_last), (1, LANE))
        places.append((rs, h))

    t_inv = _unit_lower_inverses(a_list, at_list, g_ref, C)
    for i, (rs, h) in enumerate(places):
        t_hi, t_lo = _split_bf16(t_inv[i])
        uw = (jnp.dot(t_hi, rhs_list[i], preferred_element_type=F32)
              + jnp.dot(t_lo, rhs_list[i], preferred_element_type=F32))
        u_ref[rs, h * DV:(h + 1) * DV] = uw[:, :DV].astype(u_ref.dtype)
        w_ref[rs, h * DK:(h + 1) * DK] = uw[:, DV:].astype(w_ref.dtype)


def _gdn_scan_kernel(u_ref, w_ref, qe_ref, kd_ref, aqk_ref, dec_ref, z_ref, nw_ref, og_ref, s_out_ref, s_ref,
                     *, H, DK, DV, C):
    c = pl.program_id(1)

    @pl.when(c == 0)
    def _():
        s_ref[...] = jnp.zeros_like(s_ref)

    ch = [(g, h) for g in range(u_ref.shape[0]) for h in range(H)]
    ks = [slice(h * DK, (h + 1) * DK) for h in range(H)]
    vs = [slice(h * DV, (h + 1) * DV) for h in range(H)]
    s16 = [s_ref[g, h].astype(BF16) for g, h in ch]
    ws = [jnp.dot(w_ref[g, :, ks[h]], s16[i], preferred_element_type=F32) for i, (g, h) in enumerate(ch)]
    qs = [jnp.dot(qe_ref[g, :, ks[h]], s16[i], preferred_element_type=F32) for i, (g, h) in enumerate(ch)]
    v16 = [(u_ref[g, :, vs[h]].astype(F32) - ws[i]).astype(BF16) for i, (g, h) in enumerate(ch)]
    ds = [lax.dot_general(kd_ref[g, :, ks[h]], v16[i], TN, preferred_element_type=F32) for i, (g, h) in enumerate(ch)]
    o = [qs[i] + jnp.dot(aqk_ref[g, :, h * C:(h + 1) * C], v16[i], preferred_element_type=F32)
         for i, (g, h) in enumerate(ch)]
    for i, (g, h) in enumerate(ch):
        s_ref[g, h] = s_ref[g, h] * dec_ref[g, 0, h:h + 1, :] + ds[i]
        zh = z_ref[g, :, vs[h]].astype(F32)
        og_ref[g, :, vs[h]] = (_rms(o[i]) * nw_ref[...] * _silu(zh)).astype(og_ref.dtype)

    @pl.when(c == pl.num_programs(1) - 1)
    def _():
        s_out_ref[...] = s_ref[...]


def _gdn_prompt(qkv, z, small, ba_col, cw_t, gparams, norm_w, B, T, H, DK, DV):
    C = GDN_CHUNK
    nc = T // C
    n, W = qkv.shape
    kw = dict(H=H, DK=DK, DV=DV, C=C)
    pc = PREP_CHUNKS if nc % PREP_CHUNKS == 0 else 1
    ns = nc // pc
    R = pc * C
    blk = lambda w: pl.BlockSpec((R, w), lambda b, c: (b * ns + c, 0))
    ppc = R // PREV_ROWS
    u, w, qe, kd, aqk, dec = pl.pallas_call(
        functools.partial(_gdn_prep_kernel, **kw), grid=(B, ns),
        in_specs=[blk(W),
                  pl.BlockSpec((PREV_ROWS, W), lambda b, c: (jnp.maximum((b * ns + c) * ppc - 1, 0), 0)),
                  pl.BlockSpec((R, LANE), lambda b, c: (b * ns + c, ba_col)),
                  pl.BlockSpec((CONV_WIDTH, W), lambda b, c: (0, 0)),
                  pl.BlockSpec((2, LANE), lambda b, c: (0, 0)),
                  pl.BlockSpec((C, SUBLANES * LANE), lambda b, c: (0, 0))],
        out_specs=[blk(H * DV), blk(H * DK), blk(H * DK), blk(H * DK), blk(H * C),
                   pl.BlockSpec((pc, SUBLANES, LANE), lambda b, c: (b * ns + c, 0, 0))],
        out_shape=[jax.ShapeDtypeStruct((n, H * DV), BF16), jax.ShapeDtypeStruct((n, H * DK), BF16),
                   jax.ShapeDtypeStruct((n, H * DK), BF16), jax.ShapeDtypeStruct((n, H * DK), BF16),
                   jax.ShapeDtypeStruct((n, H * C), BF16), jax.ShapeDtypeStruct((B * nc, SUBLANES, LANE), F32)],
        compiler_params=_cp(("parallel", "parallel")))(qkv, qkv, small, cw_t, gparams, _expand_matrix(C))
    nb = SCAN_SEQS if B % SCAN_SEQS == 0 else 1
    seq = lambda a: a.reshape(B, T, a.shape[1])
    sblk = lambda w: pl.BlockSpec((nb, C, w), lambda b, c: (b, c, 0))
    og, s_fin = pl.pallas_call(
        functools.partial(_gdn_scan_kernel, **kw), grid=(B // nb, nc),
        in_specs=[sblk(H * DV), sblk(H * DK), sblk(H * DK), sblk(H * DK), sblk(H * C),
                  pl.BlockSpec((nb, 1, SUBLANES, LANE), lambda b, c: (b, c, 0, 0)),
                  sblk(H * DV), pl.BlockSpec((1, DV), lambda b, c: (0, 0))],
        out_specs=[sblk(H * DV), pl.BlockSpec((nb, H, DK, DV), lambda b, c: (b, 0, 0, 0))],
        out_shape=[jax.ShapeDtypeStruct((B, T, H * DV), BF16), jax.ShapeDtypeStruct((B, H, DK, DV), F32)],
        scratch_shapes=[pltpu.VMEM((nb, H, DK, DV), F32)],
        compiler_params=_cp(("parallel", "arbitrary")))(
            seq(u), seq(w), seq(qe), seq(kd), seq(aqk), dec.reshape(B, nc, SUBLANES, LANE), seq(z),
            norm_w.reshape(1, DV))
    return og.reshape(n, H * DV), s_fin


DECODE_TOKENS = 8


def _gdn_decode_kernel(u_ref, z_ref, ba_ref, buf_ref, s_in_ref, cw_ref, gp_ref, nw_ref,
                       og_ref, s_out_ref, buf_out_ref, *, H, DK, DV):
    QK = H * DK
    nt = u_ref.shape[0]
    u = u_ref[...]
    b0, b1, b2 = buf_ref[:, 0, :], buf_ref[:, 1, :], buf_ref[:, 2, :]
    y = b0 * cw_ref[0:1, :]
    y = y + b1 * cw_ref[1:2, :]
    y = y + b2 * cw_ref[2:3, :]
    y = y + u * cw_ref[3:4, :]
    buf_out_ref[:, 0, :] = b1
    buf_out_ref[:, 1, :] = b2
    buf_out_ref[:, 2, :] = u
    y = _silu(y)
    sm = ba_ref[...]
    beta_all = jax.nn.sigmoid(sm)
    dec_all = jnp.exp(-jnp.exp(gp_ref[0:1, :]) * _softplus(sm + gp_ref[1:2, :]))
    z_all = z_ref[...].astype(F32)
    cols = []
    for h in range(H):
        kh = y[:, QK + h * DK:QK + (h + 1) * DK]
        qh = y[:, h * DK:(h + 1) * DK]
        kh = kh * lax.rsqrt(jnp.sum(kh * kh, axis=-1, keepdims=True) + NORM_EPS)
        qh = qh * lax.rsqrt(jnp.sum(qh * qh, axis=-1, keepdims=True) + NORM_EPS) * (DK ** -0.5)
        cols.append(jnp.concatenate([kh, qh, jnp.zeros((DK - 2 * nt, DK), F32)], axis=0).T)
    for t in range(nt):
        hs = range(H)
        kcol = [cols[h][:, t:t + 1] for h in hs]
        qcol = [cols[h][:, nt + t:nt + t + 1] for h in hs]
        s_dec = [s_in_ref[t, h] * dec_all[t:t + 1, 8 + h:9 + h] for h in hs]
        ks = [jnp.sum(kcol[h] * s_dec[h], axis=0, keepdims=True) for h in hs]
        v_new = [beta_all[t:t + 1, h:h + 1] * (y[t:t + 1, 2 * QK + h * DV:2 * QK + (h + 1) * DV] - ks[h]) for h in hs]
        s_new = [s_dec[h] + kcol[h] * v_new[h] for h in hs]
        o = [jnp.sum(qcol[h] * s_new[h], axis=0, keepdims=True) for h in hs]
        for h in hs:
            s_out_ref[t, h] = s_new[h]
            zh = z_all[t:t + 1, h * DV:(h + 1) * DV]
            og_ref[t:t + 1, h * DV:(h + 1) * DV] = (_rms(o[h]) * nw_ref[...] * _silu(zh)).astype(og_ref.dtype)


def _gdn_decode(qkv, z, small, ba_col, conv_buf, ssm, cw_t, gparams, norm_w, H, DK, DV):
    Bs, W = qkv.shape
    kern = functools.partial(_gdn_decode_kernel, H=H, DK=DK, DV=DV)
    nt = DECODE_TOKENS
    assert Bs % nt == 0 and qkv.dtype == F32 and z.dtype == F32
    return pl.pallas_call(
        kern, grid=(Bs // nt,),
        in_specs=[pl.BlockSpec((nt, W), lambda b: (b, 0)),
                  pl.BlockSpec((nt, H * DV), lambda b: (b, 0)),
                  pl.BlockSpec((nt, LANE), lambda b: (b, ba_col)),
                  pl.BlockSpec((nt, CONV_WIDTH - 1, W), lambda b: (b, 0, 0)),
                  pl.BlockSpec((nt, H, DK, DV), lambda b: (b, 0, 0, 0)),
                  pl.BlockSpec((CONV_WIDTH, W), lambda b: (0, 0)),
                  pl.BlockSpec((2, LANE), lambda b: (0, 0)),
                  pl.BlockSpec((1, DV), lambda b: (0, 0))],
        out_specs=[pl.BlockSpec((nt, H * DV), lambda b: (b, 0)),
                   pl.BlockSpec((nt, H, DK, DV), lambda b: (b, 0, 0, 0)),
                   pl.BlockSpec((nt, CONV_WIDTH - 1, W), lambda b: (b, 0, 0))],
        out_shape=[jax.ShapeDtypeStruct((Bs, H * DV), F32),
                   jax.ShapeDtypeStruct(ssm.shape, F32),
                   jax.ShapeDtypeStruct(conv_buf.shape, F32)],
        compiler_params=_cp(("parallel",)))(qkv, z, small, conv_buf, ssm, cw_t, gparams, norm_w.reshape(1, DV))


def _mla_prep_prompt_kernel(sm_ref, cos_ref, sin_ref, cost_ref, sint_ref, qnw_ref, kvnw_ref,
                            wqt_ref, wqrt_ref, wuk_ref, wuvt_ref,
                            qt_ref, k_ref, vt_ref, lat_ref, pe_ref, *, H, NOPE, ROPE, VH, QL, KVL, scale):
    sm = sm_ref[...]
    qn = (_rms(sm[:, :QL]) * qnw_ref[...]).astype(BF16)
    qft = lax.dot_general(wqt_ref[...], qn, NT, preferred_element_type=F32)
    qrt = lax.dot_general(wqrt_ref[...], qn, NT, preferred_element_type=F32)
    qpt = qft[H * NOPE:, :] * cost_ref[...] + qrt * sint_ref[...]
    lat = _rms(sm[:, QL:QL + KVL]) * kvnw_ref[...]
    lat_ref[...] = lat
    lat16 = lat.astype(BF16)
    kn = jnp.dot(lat16, wuk_ref[...], preferred_element_type=F32)
    vt = lax.dot_general(wuvt_ref[...], lat16, NT, preferred_element_type=F32)
    o = QL + KVL
    kr = sm[:, o:o + ROPE] * cos_ref[...] + sm[:, o + ROPE:o + 2 * ROPE] * sin_ref[...]
    pe_ref[...] = kr
    for h in range(H):
        qt_ref[h] = (jnp.concatenate([qft[h * NOPE:(h + 1) * NOPE, :], qpt[h * ROPE:(h + 1) * ROPE, :]], axis=0)
                     * scale).astype(qt_ref.dtype)
        k_ref[h] = jnp.concatenate([kn[:, h * NOPE:(h + 1) * NOPE], kr], axis=-1).astype(k_ref.dtype)
        vt_ref[h, 0] = vt[h * VH:(h + 1) * VH, :].astype(vt_ref.dtype)


def _mla_prep_prompt(small, cos, sin, qnw, kvnw, wq, wqr, wuk, wuv, T, tm, H, NOPE, ROPE, VH, QL, KVL, scale):
    n, ws = small.shape
    nt = T // tm
    kern = functools.partial(_mla_prep_prompt_kernel, H=H, NOPE=NOPE, ROPE=ROPE, VH=VH, QL=QL, KVL=KVL, scale=scale)
    full = lambda a: pl.BlockSpec(a.shape, lambda i: (0,) * a.ndim)
    dqk = NOPE + ROPE
    cos_t = jnp.tile(cos.T, (H, 1))
    sin_t = jnp.tile(sin.T, (H, 1))
    wqt, wqrt, wuvt = wq.T, wqr.T, wuv.T
    return pl.pallas_call(
        kern, grid=(n // tm,),
        in_specs=[pl.BlockSpec((tm, ws), lambda i: (i, 0)),
                  pl.BlockSpec((tm, ROPE), lambda i: (i % nt, 0)),
                  pl.BlockSpec((tm, ROPE), lambda i: (i % nt, 0)),
                  pl.BlockSpec((H * ROPE, tm), lambda i: (0, i % nt)),
                  pl.BlockSpec((H * ROPE, tm), lambda i: (0, i % nt)),
                  full(qnw), full(kvnw), full(wqt), full(wqrt), full(wuk), full(wuvt)],
        out_specs=[pl.BlockSpec((H, dqk, tm), lambda i: (0, 0, i)),
                   pl.BlockSpec((H, tm, dqk), lambda i: (0, i, 0)),
                   pl.BlockSpec((H, 1, VH, tm), lambda i: (0, i, 0, 0)),
                   pl.BlockSpec((tm, KVL), lambda i: (i, 0)),
                   pl.BlockSpec((tm, ROPE), lambda i: (i, 0))],
        out_shape=[jax.ShapeDtypeStruct((H, dqk, n), BF16), jax.ShapeDtypeStruct((H, n, dqk), BF16),
                   jax.ShapeDtypeStruct((H, n // tm, VH, tm), BF16), jax.ShapeDtypeStruct((n, KVL), F32),
                   jax.ShapeDtypeStruct((n, ROPE), F32)],
        compiler_params=_cp(("parallel",)))(small, cos, sin, cos_t, sin_t, qnw, kvnw, wqt, wqrt, wuk, wuvt)


def _mla_prep_sample_kernel(sm_ref, cos_ref, sin_ref, qnw_ref, kvnw_ref, wq_ref, wqr_ref,
                            qn_ref, qp_ref, lat_ref, pe_ref, *, H, NOPE, ROPE, QL, KVL, scale):
    sm = sm_ref[...]
    cos = cos_ref[...]
    sin = sin_ref[...]
    qn = _rms(sm[:, :QL]) * qnw_ref[...]
    qf = _bdot(qn, wq_ref[...])
    qr = _bdot(qn, wqr_ref[...])
    qn_ref[...] = qf[:, :H * NOPE].astype(qn_ref.dtype)
    lat_ref[...] = _rms(sm[:, QL:QL + KVL]) * kvnw_ref[...]
    o = QL + KVL
    pe_ref[...] = sm[:, o:o + ROPE] * cos + sm[:, o + ROPE:o + 2 * ROPE] * sin
    for h in range(H):
        p0 = H * NOPE + h * ROPE
        qp = qf[:, p0:p0 + ROPE] * cos + qr[:, h * ROPE:(h + 1) * ROPE] * sin
        qp_ref[:, h * ROPE:(h + 1) * ROPE] = (qp * scale).astype(qp_ref.dtype)


def _mla_prep_sample(small, cos, sin, qnw, kvnw, wq, wqr, H, NOPE, ROPE, QL, KVL, scale):
    n, ws = small.shape
    kern = functools.partial(_mla_prep_sample_kernel, H=H, NOPE=NOPE, ROPE=ROPE, QL=QL, KVL=KVL, scale=scale)
    full = lambda a: pl.BlockSpec(a.shape, lambda i: (0,) * a.ndim)
    return pl.pallas_call(
        kern, grid=(1,),
        in_specs=[full(small), full(cos), full(sin), full(qnw), full(kvnw), full(wq), full(wqr)],
        out_specs=[pl.BlockSpec((n, H * NOPE), lambda i: (0, 0)), pl.BlockSpec((n, H * ROPE), lambda i: (0, 0)),
                   pl.BlockSpec((n, KVL), lambda i: (0, 0)), pl.BlockSpec((n, ROPE), lambda i: (0, 0))],
        out_shape=[jax.ShapeDtypeStruct((n, H * NOPE), BF16), jax.ShapeDtypeStruct((n, H * ROPE), BF16),
                   jax.ShapeDtypeStruct((n, KVL), F32), jax.ShapeDtypeStruct((n, ROPE), F32)],
        compiler_params=_cp(("arbitrary",)))(small, cos, sin, qnw, kvnw, wq, wqr)


def _head_proj_kernel(x_ref, w_ref, o_ref, *, dims, scale):
    o_ref[...] = (_bdot_g(x_ref[...], w_ref[...], dims) * scale).astype(o_ref.dtype)


def _q_latent(q_nope, w_uk, scale):
    H, KVL, NOPE = w_uk.shape
    n = q_nope.shape[0]
    return pl.pallas_call(
        functools.partial(_head_proj_kernel, dims=NT, scale=scale), grid=(H,),
        in_specs=[pl.BlockSpec((n, NOPE), lambda h: (0, h)), pl.BlockSpec((None, KVL, NOPE), lambda h: (h, 0, 0))],
        out_specs=pl.BlockSpec((None, n, KVL), lambda h: (h, 0, 0)),
        out_shape=jax.ShapeDtypeStruct((H, n, KVL), BF16),
        compiler_params=_cp(("parallel",)))(q_nope, w_uk)


def _o_value(o_lat, w_uv):
    H, KVL, VH = w_uv.shape
    n = o_lat.shape[1]
    return pl.pallas_call(
        functools.partial(_head_proj_kernel, dims=(((1,), (0,)), ((), ())), scale=1.0), grid=(H,),
        in_specs=[pl.BlockSpec((None, n, KVL), lambda h: (h, 0, 0)), pl.BlockSpec((None, KVL, VH), lambda h: (h, 0, 0))],
        out_specs=pl.BlockSpec((n, VH), lambda h: (0, h)),
        out_shape=jax.ShapeDtypeStruct((n, H * VH), BF16),
        compiler_params=_cp(("parallel",)))(o_lat, w_uv)


FLASH_SPLIT = 2


FLASH_HEADS = 4


def _flash_kernel(qt_ref, k_ref, vt_ref, o_ref, m_ref, l_ref, acc_ref, *, tq):
    qi = pl.program_id(2)
    nh = qt_ref.shape[0]
    ns = FLASH_SPLIT
    hq = tq // ns
    vh = vt_ref.shape[2]
    chains = [(g, t) for g in range(nh) for t in range(ns)]
    m_ref[...] = jnp.full_like(m_ref, -jnp.inf)
    l_ref[...] = jnp.zeros_like(l_ref)
    acc_ref[...] = jnp.zeros_like(acc_ref)
    qt = [qt_ref[g, :, t * hq:(t + 1) * hq] for g, t in chains]

    def scores(off):
        k = [k_ref[g, pl.ds(off, tq), :] for g in range(nh)]
        return [jnp.dot(k[g], qt[c], preferred_element_type=F32) for c, (g, t) in enumerate(chains)]

    def update(st, j):
        cs = range(len(chains))
        vt = [vt_ref[g, j] for g in range(nh)]
        m_prev = [m_ref[c] for c in cs]
        m_new = [jnp.maximum(m_prev[c], jnp.max(st[c], axis=0, keepdims=True)) for c in cs]
        p = [jnp.exp(st[c] - m_new[c]) for c in cs]
        corr = [jnp.exp(m_prev[c] - m_new[c]) for c in cs]
        pv = [jnp.dot(vt[chains[c][0]], p[c].astype(BF16), preferred_element_type=F32) for c in cs]
        for c in cs:
            m_ref[c] = m_new[c]
            l_ref[c] = corr[c] * l_ref[c] + jnp.sum(p[c], axis=0, keepdims=True)
            acc_ref[c] = acc_ref[c] * corr[c] + pv[c]

    def below_diagonal(j, carry):
        update(scores(pl.multiple_of(j * tq, tq)), j)
        return carry

    lax.fori_loop(0, qi, below_diagonal, 0)

    st = scores(pl.multiple_of(qi * tq, tq))
    for c, (g, t) in enumerate(chains):
        key = lax.broadcasted_iota(jnp.int32, st[c].shape, 0)
        qry = t * hq + lax.broadcasted_iota(jnp.int32, st[c].shape, 1)
        st[c] = jnp.where(key <= qry, st[c], -jnp.inf)
    update(st, qi)
    for c, (g, t) in enumerate(chains):
        o_ref[t * hq:(t + 1) * hq, g * vh:(g + 1) * vh] = (acc_ref[c] / l_ref[c]).T.astype(o_ref.dtype)


def _flash_attention(qt, k, vt, B, T, tq):
    H, dqk, n = qt.shape
    VH = vt.shape[2]
    nq = T // tq
    hq = tq // FLASH_SPLIT
    nh = FLASH_HEADS
    nc = nh * FLASH_SPLIT
    kern = functools.partial(_flash_kernel, tq=tq)
    return pl.pallas_call(
        kern, grid=(B, H // nh, nq),
        in_specs=[pl.BlockSpec((nh, dqk, tq), lambda b, h, i: (h, 0, b * nq + i)),
                  pl.BlockSpec((nh, T, dqk), lambda b, h, i: (h, b, 0)),
                  pl.BlockSpec((nh, nq, VH, tq), lambda b, h, i: (h, b, 0, 0))],
        out_specs=pl.BlockSpec((tq, nh * VH), lambda b, h, i: (b * nq + i, h)),
        out_shape=jax.ShapeDtypeStruct((n, H * VH), BF16),
        scratch_shapes=[pltpu.VMEM((nc, 1, hq), F32), pltpu.VMEM((nc, 1, hq), F32), pltpu.VMEM((nc, VH, hq), F32)],
        compiler_params=_cp(("parallel", "parallel", "arbitrary")))(qt, k, vt)


def _paged_kernel(pt_ref, ql_ref, qp_ref, latn_ref, pen_ref, kv_hbm, pe_hbm, o_ref,
                  kv_buf, pe_buf, sem, m_ref, l_ref, acc_ref, *, R, G, layer):
    b = pl.program_id(0)
    g = pl.program_id(1)
    ng = pl.num_programs(1)
    step = b * ng + g
    slot = step % 2

    def page_copies(bb, gg, sl, real_pages):
        out = []
        for r in range(R):
            for i in range(G):
                page = pt_ref[bb * R + r, gg * G + i] if real_pages else 0
                out.append(pltpu.make_async_copy(kv_hbm.at[layer, page], kv_buf.at[sl, r * G + i], sem.at[0, sl]))
                out.append(pltpu.make_async_copy(pe_hbm.at[layer, page], pe_buf.at[sl, r * G + i], sem.at[1, sl]))
        return out

    @pl.when(step == 0)
    def _():
        for cp in page_copies(0, 0, 0, True):
            cp.start()

    @pl.when(step + 1 < pl.num_programs(0) * ng)
    def _():
        nxt = step + 1
        for cp in page_copies(nxt // ng, nxt % ng, 1 - slot, True):
            cp.start()

    for cp in page_copies(b, g, slot, False):
        cp.wait()

    @pl.when(g == 0)
    def _():
        m_ref[...] = jnp.full_like(m_ref, -jnp.inf)
        l_ref[...] = jnp.zeros_like(l_ref)
        acc_ref[...] = jnp.zeros_like(acc_ref)

    kvs = [[kv_buf[slot, r * G + i].astype(BF16) for i in range(G)] for r in range(R)]
    P = kvs[0][0].shape[0]
    s, p, corr, m_new = [], [], [], []
    for r in range(R):
        ql = ql_ref[r]
        qp = qp_ref[r]
        s.append(jnp.concatenate(
            [lax.dot_general(ql, kvs[r][i], NT, preferred_element_type=F32)
             + jnp.dot(qp, pe_buf[slot, r * G + i].astype(BF16), preferred_element_type=F32) for i in range(G)],
            axis=-1))
    for r in range(R):
        m_prev = m_ref[r]
        m_new.append(jnp.maximum(m_prev, jnp.max(s[r], axis=-1, keepdims=True)))
        corr.append(jnp.exp(m_prev - m_new[r]))
        p.append(jnp.exp(s[r] - m_new[r]))
    pv = []
    for r in range(R):
        t = None
        for i in range(G):
            d = jnp.dot(p[r][:, i * P:(i + 1) * P].astype(BF16), kvs[r][i], preferred_element_type=F32)
            t = d if t is None else t + d
        pv.append(t)
    for r in range(R):
        m_ref[r] = m_new[r]
        l_ref[r] = corr[r] * l_ref[r] + jnp.sum(p[r], axis=-1, keepdims=True)
        acc_ref[r] = acc_ref[r] * corr[r] + pv[r]

    @pl.when(g == pl.num_programs(1) - 1)
    def _():
        for r in range(R):
            latn = latn_ref[r]
            s_n = (jnp.sum(ql_ref[r].astype(F32) * latn, axis=-1, keepdims=True)
                   + jnp.sum(qp_ref[r].astype(F32) * pen_ref[r], axis=-1, keepdims=True))
            m_old = m_ref[r]
            m2 = jnp.maximum(m_old, s_n)
            c2 = jnp.exp(m_old - m2)
            p2 = jnp.exp(s_n - m2)
            o_ref[r] = ((acc_ref[r] * c2 + p2 * latn) / (l_ref[r] * c2 + p2)).astype(o_ref.dtype)


def _paged_attention(q_lat, q_pe, lat_new, pe_new, cache_kv, cache_pe_t, page_table, layer):
    Bs, H, KVL = q_lat.shape
    ROPE = q_pe.shape[2]
    n_pages = page_table.shape[1]
    P = cache_kv.shape[2]
    G = min(PAGE_GROUP, n_pages)
    R = min(PAGE_REQUESTS, Bs)

    req = lambda rows, w: pl.BlockSpec((R, rows, w), lambda b, g, pt: (b, 0, 0))
    hbm = pl.BlockSpec(memory_space=pl.ANY)
    grid_spec = pltpu.PrefetchScalarGridSpec(
        num_scalar_prefetch=1, grid=(Bs // R, n_pages // G),
        in_specs=[req(H, KVL), req(H, ROPE), req(1, KVL), req(1, ROPE), hbm, hbm],
        out_specs=req(H, KVL),
        scratch_shapes=[pltpu.VMEM((2, R * G, P, KVL), cache_kv.dtype), pltpu.VMEM((2, R * G, ROPE, P), cache_pe_t.dtype),
                        pltpu.SemaphoreType.DMA((2, 2)),
                        pltpu.VMEM((R, H, 1), F32), pltpu.VMEM((R, H, 1), F32), pltpu.VMEM((R, H, KVL), F32)])
    return pl.pallas_call(
        functools.partial(_paged_kernel, R=R, G=G, layer=layer), grid_spec=grid_spec,
        out_shape=jax.ShapeDtypeStruct((Bs, H, KVL), BF16),
        compiler_params=_cp(("arbitrary", "arbitrary")))(
            page_table, q_lat, q_pe, lat_new.reshape(Bs, 1, KVL), pe_new.reshape(Bs, 1, ROPE), cache_kv, cache_pe_t)


def _post_mixer_kernel(x_ref, og_ref, om_ref, gab_ref, g1_ref, sc2_ref, sh2_ref, wog_ref, wom_ref, wout_ref,
                       nw_ref, wr_ref, br_ref, x1_ref, h2_ref, te_ref, tg_ref, *, D, E):
    gab = gab_ref[...].astype(F32)
    merged = (jax.nn.sigmoid(gab[:, :D]) * _bdot(og_ref[...], wog_ref[...])
              + jax.nn.sigmoid(gab[:, D:]) * _bdot(om_ref[...], wom_ref[...]))
    x1 = x_ref[...] + g1_ref[...] * _bdot(merged, wout_ref[...])
    x1_ref[...] = x1
    h2 = _rms(x1) * nw_ref[...] * (1.0 + sc2_ref[...]) + sh2_ref[...]
    h2_ref[...] = h2
    logits = _dot3(_split_bf16(h2), _split_bf16(wr_ref[...])) + br_ref[...]
    lane = lax.broadcasted_iota(jnp.int32, logits.shape, 1)
    logits = jnp.where(lane < E, logits, -jnp.inf)
    te = jnp.zeros(logits.shape, jnp.int32)
    ex = jnp.zeros(logits.shape, F32)
    top = None
    for k in range(TOP_K):
        m = jnp.max(logits, axis=-1, keepdims=True)
        idx = jnp.min(jnp.where(logits == m, lane, LANE), axis=-1, keepdims=True)
        top = m if top is None else top
        te = jnp.where(lane == k, idx, te)
        ex = jnp.where(lane == k, jnp.exp(m - top), ex)
        logits = jnp.where(lane == idx, -jnp.inf, logits)
    te_ref[...] = te
    tg_ref[...] = ex / jnp.sum(ex, axis=-1, keepdims=True)


def _post_mixer(x, og, om, gab, grp, wog, wom, wout, norm_w, wr, br, E):
    n, d = x.shape
    tm = grp.tm
    rowblk = lambda w: pl.BlockSpec((tm, w), lambda i: (i, 0))
    full = lambda a: pl.BlockSpec(a.shape, lambda i: (0,) * a.ndim)
    return pl.pallas_call(
        functools.partial(_post_mixer_kernel, D=d, E=E), grid=(n // tm,),
        in_specs=[rowblk(d), rowblk(d), rowblk(d), rowblk(2 * d), grp.spec(2, d), grp.spec(4, d), grp.spec(3, d),
                  full(wog), full(wom), full(wout), full(norm_w), full(wr), full(br)],
        out_specs=[rowblk(d), rowblk(d), rowblk(LANE), rowblk(LANE)],
        out_shape=[jax.ShapeDtypeStruct((n, d), F32), jax.ShapeDtypeStruct((n, d), F32),
                   jax.ShapeDtypeStruct((n, LANE), jnp.int32), jax.ShapeDtypeStruct((n, LANE), F32)],
        compiler_params=_cp(("parallel",)))(x, og, om, gab, grp.mod3, grp.mod3, grp.mod3,
                                            wog, wom, wout, norm_w, wr, br)


def _dispatch_kernel(pe_ref, slot_ref, ha_ref, hb_ref, hs_ref, zero_ref, sem, zsem, *, tb, tm, E, nb_a):
    i = pl.program_id(0)
    n_blocks = hs_ref.shape[0] // tm

    def zero_copy(row0):
        return pltpu.make_async_copy(zero_ref, hs_ref.at[pl.ds(pl.multiple_of(row0, tm), tm)], zsem)

    zero_jobs = [(pe_ref[e + 1] > pe_ref[e], pe_ref[e + 1] - tm) for e in range(E)]
    zero_jobs += [(b * tm >= pe_ref[E], b * tm) for b in range(n_blocks - E, n_blocks)]

    @pl.when(i == 0)
    def _():
        zero_ref[...] = jnp.zeros_like(zero_ref)
        for wanted, row0 in zero_jobs:
            @pl.when(wanted)
            def _():
                zero_copy(row0).start()
        for wanted, row0 in zero_jobs:
            @pl.when(wanted)
            def _():
                zero_copy(row0).wait()

    def scatter_rows(h_ref):
        for r in range(tb):
            for k in range(TOP_K):
                a = r * TOP_K + k
                pltpu.make_async_copy(h_ref.at[pl.ds(r, 1)], hs_ref.at[pl.ds(slot_ref[0, 0, a], 1)],
                                      sem).start(priority=a % 2)
        for k in range(TOP_K):
            pltpu.make_async_copy(h_ref, hs_ref.at[pl.ds(0, tb)], sem).wait()

    @pl.when(i < nb_a)
    def _():
        scatter_rows(ha_ref)

    @pl.when(i >= nb_a)
    def _():
        scatter_rows(hb_ref)


def _dispatch(ha, hb, slots3, pad_edges, n_slots, tb, tm):
    d = ha.shape[1]
    nb_a, nb_b = ha.shape[0] // tb, hb.shape[0] // tb
    nb = nb_a + nb_b
    E = pad_edges.shape[0] - 1
    grid_spec = pltpu.PrefetchScalarGridSpec(
        num_scalar_prefetch=1, grid=(nb,),
        in_specs=[pl.BlockSpec((1, 1, tb * TOP_K), lambda i, pe: (i, 0, 0), memory_space=pltpu.SMEM),
                  pl.BlockSpec((tb, d), lambda i, pe: (jnp.minimum(i, nb_a - 1), 0)),
                  pl.BlockSpec((tb, d), lambda i, pe: (jnp.maximum(i - nb_a, 0), 0))],
        out_specs=pl.BlockSpec(memory_space=pl.ANY),
        scratch_shapes=[pltpu.VMEM((tm, d), ha.dtype), pltpu.SemaphoreType.DMA, pltpu.SemaphoreType.DMA])
    return pl.pallas_call(
        functools.partial(_dispatch_kernel, tb=tb, tm=tm, E=E, nb_a=nb_a), grid_spec=grid_spec,
        out_shape=jax.ShapeDtypeStruct((n_slots, d), ha.dtype),
        compiler_params=_cp(("arbitrary",)))(pad_edges, slots3, ha, hb)


def _expert_kernel(be_ref, na_ref, x_ref, wgu_ref, bgu_ref, wdn_ref, bdn_ref, y_ref, wgu_s, wdn_s, *, DE):
    i = pl.program_id(0)
    active = i < na_ref[0]
    first = jnp.logical_or(i == 0, be_ref[i] != be_ref[jnp.maximum(i - 1, 0)])

    @pl.when(jnp.logical_and(active, first))
    def _():
        wgu_s[...] = wgu_ref[...].astype(BF16)
        wdn_s[...] = wdn_ref[...].astype(BF16)

    @pl.when(active)
    def _():
        gu = jnp.dot(x_ref[...].astype(BF16), wgu_s[...], preferred_element_type=F32) + bgu_ref[...]
        gt = jnp.minimum(gu[:, :DE], SWIGLU_LIMIT)
        up = jnp.clip(gu[:, DE:], -SWIGLU_LIMIT, SWIGLU_LIMIT)
        act = (up + 1.0) * gt * jax.nn.sigmoid(SWIGLU_ALPHA * gt)
        y_ref[...] = jnp.dot(act.astype(BF16), wdn_s[...], preferred_element_type=F32) + bdn_ref[...]

    @pl.when(jnp.logical_not(active))
    def _():
        y_ref[...] = jnp.zeros_like(y_ref)


def _experts(hs, blk_e, n_active, w_gu, b_gu, w_dn, b_dn, layer, tm):
    n_slots, d = hs.shape
    E, _, de2 = w_gu.shape[1:]
    de = de2 // 2
    grid_spec = pltpu.PrefetchScalarGridSpec(
        num_scalar_prefetch=2, grid=(n_slots // tm,),
        in_specs=[pl.BlockSpec((tm, d), lambda i, be, na: (jnp.minimum(i, na[0] - 1), 0)),
                  pl.BlockSpec((None, None, d, de2), lambda i, be, na: (layer, be[i], 0, 0)),
                  pl.BlockSpec((None, None, 1, de2), lambda i, be, na: (layer, be[i], 0, 0)),
                  pl.BlockSpec((None, None, de, d), lambda i, be, na: (layer, be[i], 0, 0)),
                  pl.BlockSpec((None, None, 1, d), lambda i, be, na: (layer, be[i], 0, 0))],
        out_specs=pl.BlockSpec((tm, d), lambda i, be, na: (i, 0)),
        scratch_shapes=[pltpu.VMEM((d, de2), BF16), pltpu.VMEM((de, d), BF16)])
    L = w_gu.shape[0]
    return pl.pallas_call(
        functools.partial(_expert_kernel, DE=de), grid_spec=grid_spec,
        out_shape=jax.ShapeDtypeStruct((n_slots, d), F32),
        compiler_params=_cp(("arbitrary",), VMEM_LIMIT_MOE))(
            blk_e, n_active, hs, w_gu, b_gu.reshape(L, E, 1, de2), w_dn, b_dn.reshape(L, E, 1, d))


def _combine_kernel(slotc_ref, slotn_ref, tg_ref, x1_ref, g2_ref, scf_ref, shf_ref, nwf_ref, yp_ref, o_ref,
                    buf, sem, *, tb):
    i = pl.program_id(0)
    slot = i % 2

    def gather_starts(slot_ref, sl):
        for r in range(tb):
            for k in range(TOP_K):
                a = r * TOP_K + k
                pltpu.make_async_copy(yp_ref.at[pl.ds(slot_ref[0, 0, a], 1)], buf.at[sl, k, pl.ds(r, 1)],
                                      sem.at[sl]).start(priority=a % 2)

    @pl.when(i == 0)
    def _():
        gather_starts(slotc_ref, 0)

    @pl.when(i + 1 < pl.num_programs(0))
    def _():
        gather_starts(slotn_ref, 1 - slot)

    for k in range(TOP_K):
        pltpu.make_async_copy(yp_ref.at[pl.ds(0, tb)], buf.at[slot, k], sem.at[slot]).wait()
    tg = tg_ref[...]
    moe = tg[:, 0:1] * buf[slot, 0]
    for k in range(1, TOP_K):
        moe = moe + tg[:, k:k + 1] * buf[slot, k]
    x2 = x1_ref[...] + g2_ref[...] * moe
    o_ref[...] = _rms(x2) * nwf_ref[...] * (1.0 + scf_ref[...]) + shf_ref[...]


def _combine_final(y_pad, slots3, first_block, tg, x1, grp, norm_final_w):
    n, d = x1.shape
    tb = grp.tm
    nb = n // tb
    rf, bpg = grp.modf3.shape[1], grp.bpg
    fspec = lambda j: pl.BlockSpec((None, rf, d), lambda i: (i // bpg, 0, j))
    sspec = lambda f: pl.BlockSpec((1, 1, tb * TOP_K), lambda i: (first_block + f(i), 0, 0),
                                   memory_space=pltpu.SMEM)
    return pl.pallas_call(
        functools.partial(_combine_kernel, tb=tb), grid=(nb,),
        in_specs=[sspec(lambda i: i), sspec(lambda i: jnp.minimum(i + 1, nb - 1)),
                  pl.BlockSpec((tb, LANE), lambda i: (i, 0)),
                  pl.BlockSpec((tb, d), lambda i: (i, 0)),
                  grp.spec(5, d), fspec(1), fspec(0),
                  pl.BlockSpec((1, d), lambda i: (0, 0)),
                  pl.BlockSpec(memory_space=pl.ANY)],
        out_specs=pl.BlockSpec((tb, d), lambda i: (i, 0)),
        out_shape=jax.ShapeDtypeStruct((n, d), F32),
        scratch_shapes=[pltpu.VMEM((2, TOP_K, tb, d), F32), pltpu.SemaphoreType.DMA((2,))],
        compiler_params=_cp(("arbitrary",)))(
            slots3, slots3, tg, x1, grp.mod3, grp.modf3, grp.modf3, norm_final_w.reshape(1, d), y_pad)


def _route(top_e, E, tm):
    n = top_e.shape[0]
    hit = top_e[:, :, None] == jnp.arange(E, dtype=jnp.int32)
    tok_oh = jnp.sum(hit.astype(jnp.int32), axis=1)
    csum = jnp.cumsum(tok_oh, axis=0)
    counts = csum[-1]
    padded = (counts + tm - 1) // tm * tm
    pad_end = jnp.cumsum(padded)
    slots = jnp.sum(jnp.where(hit, (csum - tok_oh + (pad_end - padded)[None, :])[:, None, :], 0), axis=2)
    n_blocks = -(-(n * TOP_K) // tm) + E
    blk_start = jnp.arange(n_blocks, dtype=jnp.int32) * tm
    blk_e = jnp.minimum(jnp.sum((pad_end[None, :] <= blk_start[:, None]).astype(jnp.int32), axis=1), E - 1)
    n_active = (pad_end[-1] // tm).reshape(1)
    pad_edges = jnp.concatenate([jnp.zeros((1,), pad_end.dtype), pad_end])
    return (slots.astype(jnp.int32), pad_edges.astype(jnp.int32), blk_e.astype(jnp.int32),
            n_active.astype(jnp.int32), n_blocks * tm)


def _rope_tables(pos, rope):
    half = rope // 2
    inv = ROPE_THETA ** (-jnp.arange(half, dtype=F32) / half)
    ang = pos.astype(F32)[:, None] * inv[None, :]
    cos, sin = jnp.cos(ang), jnp.sin(ang)
    return jnp.concatenate([cos, cos], axis=-1), jnp.concatenate([sin, sin], axis=-1)


def _rotate_cols(w, rope):
    k, n = w.shape
    w3 = w.reshape(k, n // rope, rope)
    half = rope // 2
    return jnp.concatenate([-w3[..., half:], w3[..., :half]], axis=-1).reshape(k, n)


def kernel(x_prompt, x_sample, c_prompt, c_sample, cache_kv, cache_pe, state_ssm, state_conv, page_table, norm_mix_w, norm_ffn_w, w_ada, b_ada, w_in, conv_w, A_log, dt_bias, gdn_norm_w, w_o_gdn, q_norm_w, kv_norm_w, w_uq, w_uk, w_uv, w_o_mla, w_out, w_router, b_router, w_gu, b_gu, w_dn, b_dn, w_ada_final, b_ada_final, norm_final_w):
    B, T, D = x_prompt.shape
    Bs, Ts, _ = x_sample.shape
    assert Ts == 1
    depth = w_in.shape[0]
    H, DK, DV = state_ssm.shape[2:]
    QK = H * DK
    CONV = state_conv.shape[3]
    assert CONV == 2 * QK + H * DV and H == SUBLANES
    QL = q_norm_w.shape[1]
    HM, KVL, NOPE = w_uk.shape[1:]
    VH = w_uv.shape[3]
    ROPE = cache_pe.shape[3]
    E = w_router.shape[2]
    scale = float(NOPE + ROPE) ** -0.5
    n_p, n_s = B * T, Bs * Ts
    past_len = page_table.shape[1] * cache_kv.shape[2]
    cache_pe_t = jnp.swapaxes(cache_pe, 2, 3)

    c_all = jnp.concatenate([c_prompt, c_sample], axis=0)
    modf = _matmul(c_all, w_ada_final, F32, c_all.shape[0], 1024, b_ada_final)
    tm_p = min(512, T)
    tb = 128
    cos_p, sin_p = _rope_tables(jnp.arange(T), ROPE)
    cos_s, sin_s = _rope_tables(past_len + jnp.arange(Ts), ROPE)

    hp = x_prompt.reshape(n_p, D)
    hs = x_sample.reshape(n_s, D)
    outs = {k: [] for k in ('kv_p', 'pe_p', 'ssm_p', 'conv_p', 'kv_s', 'pe_s', 'ssm_s', 'conv_s')}
    for l in range(depth):
        mod = _matmul(c_all, w_ada[l], F32, c_all.shape[0], 1024, b_ada[l])
        last = l == depth - 1
        grp_p = _Group(mod[:B].reshape(B, 1, 6 * D), modf[:B].reshape(B, 1, 2 * D), tm_p, T // tm_p)
        grp_s = _Group(mod[B:].reshape(1, Bs, 6 * D), modf[B:].reshape(1, Bs, 2 * D), Bs, 1)
        grp_pc = _Group(grp_p.mod3, grp_p.modf3, tb, T // tb)

        offs = [0]
        for s in (CONV, H * DV, H, H, QL, KVL, ROPE, D, D):
            offs.append(offs[-1] + s)
        wi = w_in[l]
        seg = lambda i: wi[:, offs[i]:offs[i + 1]]
        w_qkv = seg(0).astype(BF16)
        w_z = seg(1).astype(BF16)
        w_gab = jnp.concatenate([seg(7), seg(8)], axis=1).astype(BF16)
        n_small = QL + KVL + 2 * ROPE
        ba_col = -(-n_small // LANE)
        w_small = jnp.concatenate(
            [seg(4), seg(5), seg(6), _rotate_cols(seg(6), ROPE), jnp.zeros((D, ba_col * LANE - n_small), F32),
             seg(2), jnp.zeros((D, 8 - H), F32), seg(3), jnp.zeros((D, LANE - 8 - H), F32)], axis=1).astype(BF16)
        cw_t = conv_w[l].T
        gparams = jnp.zeros((2, LANE), F32).at[0, 8:8 + H].set(A_log[l]).at[1, 8:8 + H].set(dt_bias[l])
        wq = w_uq[l].reshape(QL, HM, NOPE + ROPE)
        wq_pe = wq[:, :, NOPE:].reshape(QL, HM * ROPE)
        wq_all = jnp.concatenate([wq[:, :, :NOPE].reshape(QL, HM * NOPE), wq_pe], axis=1).astype(BF16)
        wq_rot = _rotate_cols(wq_pe, ROPE).astype(BF16)
        wuk_all = jnp.transpose(w_uk[l], (1, 0, 2)).reshape(KVL, HM * NOPE).astype(BF16)
        wuv_all = jnp.transpose(w_uv[l], (1, 0, 2)).reshape(KVL, HM * VH).astype(BF16)
        qnw = q_norm_w[l].reshape(1, QL)
        kvnw = kv_norm_w[l].reshape(1, KVL)
        wog = w_o_gdn[l].astype(BF16)
        wom = w_o_mla[l].astype(BF16)
        wout = w_out[l].astype(BF16)
        wr = jnp.pad(w_router[l], ((0, 0), (0, LANE - E)))
        br = jnp.pad(b_router[l], (0, LANE - E)).reshape(1, LANE)
        nfw = norm_ffn_w[l].reshape(1, D)

        h1 = _norm_mod(hp, norm_mix_w[l], grp_p, 1, 0, BF16)
        qkv_p = _matmul(h1, w_qkv, BF16, 1024, 512)
        z_p = _matmul(h1, w_z, BF16, 1024, 512)
        gab_p = _matmul(h1, w_gab, BF16, 1024, 512)
        small_p = _matmul(h1, w_small, F32, 1024, w_small.shape[1])
        og_p, ssm_p = _gdn_prompt(qkv_p, z_p, small_p, ba_col, cw_t, gparams, gdn_norm_w[l], B, T, H, DK, DV)
        q_p, k_p, v_p, lat_p, pe_p = _mla_prep_prompt(small_p, cos_p, sin_p, qnw, kvnw, wq_all, wq_rot, wuk_all,
                                                      wuv_all, T, tm_p, HM, NOPE, ROPE, VH, QL, KVL, scale)
        om_p = _flash_attention(q_p, k_p, v_p, B, T, tm_p)
        x1_p, h2_p, te_p, tg_p = _post_mixer(hp, og_p, om_p, gab_p, grp_p, wog, wom, wout, nfw, wr, br, E)
        outs['kv_p'].append(lat_p.reshape(B, T, KVL))
        outs['pe_p'].append(pe_p.reshape(B, T, ROPE))
        outs['ssm_p'].append(ssm_p)
        outs['conv_p'].append(qkv_p.reshape(B, T, CONV)[:, T - (CONV_WIDTH - 1):, :].astype(F32))

        h1s = _norm_mod(hs, norm_mix_w[l], grp_s, 1, 0, BF16)
        qkv_s = _matmul(h1s, w_qkv, F32, Bs, 512)
        z_s = _matmul(h1s, w_z, F32, Bs, 512)
        gab_s = _matmul(h1s, w_gab, BF16, Bs, 512)
        small_s = _matmul(h1s, w_small, F32, Bs, w_small.shape[1])
        og_s, ssm_s, conv_s = _gdn_decode(qkv_s, z_s, small_s, ba_col, state_conv[l], state_ssm[l], cw_t, gparams,
                                          gdn_norm_w[l], H, DK, DV)
        qn_s, qp_s, lat_s, pe_s = _mla_prep_sample(small_s, cos_s, sin_s, qnw, kvnw, wq_all, wq_rot,
                                                   HM, NOPE, ROPE, QL, KVL, scale)
        q_lat = jnp.transpose(_q_latent(qn_s, w_uk[l], scale), (1, 0, 2))
        o_lat = _paged_attention(q_lat, qp_s.reshape(n_s, HM, ROPE), lat_s, pe_s, cache_kv, cache_pe_t, page_table, l)
        om_s = _o_value(jnp.transpose(o_lat, (1, 0, 2)), w_uv[l])
        x1_s, h2_s, te_s, tg_s = _post_mixer(hs, og_s, om_s, gab_s, grp_s, wog, wom, wout, nfw, wr, br, E)
        outs['kv_s'].append(lat_s.reshape(Bs, Ts, KVL))
        outs['pe_s'].append(pe_s.reshape(Bs, Ts, ROPE))
        outs['ssm_s'].append(ssm_s)
        outs['conv_s'].append(conv_s)

        tm_e = 256
        top_e = jnp.concatenate([te_p[:, :TOP_K], te_s[:, :TOP_K]], axis=0)
        slots, pad_edges, blk_e, n_active, n_slots = _route(top_e, E, tm_e)
        slots3 = slots.reshape((n_p + n_s) // tb, 1, tb * TOP_K)
        h_sorted = _dispatch(h2_p, h2_s, slots3, pad_edges, n_slots, tb, tm_e)
        y_pad = _experts(h_sorted, blk_e, n_active, w_gu, b_gu, w_dn, b_dn, l, tm_e)
        if not last:
            raise NotImplementedError("stacked layers need an un-normalised residual output")
        assert grp_s.tm == tb
        hp = _combine_final(y_pad, slots3, 0, tg_p, x1_p, grp_pc, norm_final_w)
        hs = _combine_final(y_pad, slots3, n_p // tb, tg_s, x1_s, grp_s, norm_final_w)

    st = lambda k: jnp.stack(outs[k])
    return (hp.reshape(B, T, D), hs.reshape(Bs, Ts, D),
            st('kv_p'), st('pe_p'), st('ssm_p'), st('conv_p'),
            st('kv_s'), st('pe_s'), st('ssm_s'), st('conv_s'))
```

```python
import functools

import jax
import jax.numpy as jnp
from jax import lax
from jax.experimental import pallas as pl
from jax.experimental.pallas import tpu as pltpu

F32 = jnp.float32
BF16 = jnp.bfloat16
HI = lax.Precision.HIGHEST

NORM_EPS = 1e-6
ROPE_THETA = 10000.0
TOP_K = 4
SWIGLU_LIMIT = 7.0
SWIGLU_ALPHA = 1.702
CONV_WIDTH = 4
GDN_CHUNK = 64
PAGE_GROUP = 16
PAGE_REQUESTS = 4
LANE = 128
VMEM_LIMIT = 48 * 1024 * 1024
VMEM_LIMIT_LARGE = 58 * 1024 * 1024

NT = (((1,), (1,)), ((), ()))
TN = (((0,), (0,)), ((), ()))


def _cp(sem, vmem=VMEM_LIMIT):
    return pltpu.CompilerParams(dimension_semantics=sem, vmem_limit_bytes=vmem)


def _rms(x):
    return x * lax.rsqrt(jnp.mean(x * x, axis=-1, keepdims=True) + NORM_EPS)


def _softplus(x):
    return jnp.maximum(x, 0.0) + jnp.log1p(jnp.exp(-jnp.abs(x)))


def _silu(x):
    return x * jax.nn.sigmoid(x)


def _bdot(a, b):
    return jnp.dot(a.astype(BF16), b.astype(BF16), preferred_element_type=F32)


def _bdot_g(a, b, dims):
    return lax.dot_general(a.astype(BF16), b.astype(BF16), dims, preferred_element_type=F32)


def _mm_kernel(*refs, has_bias):
    if has_bias:
        x_ref, w_ref, b_ref, o_ref = refs
    else:
        x_ref, w_ref, o_ref = refs
    acc = _bdot(x_ref[...], w_ref[...])
    if has_bias:
        acc = acc + b_ref[...]
    o_ref[...] = acc.astype(o_ref.dtype)


def _matmul(x, w, out_dtype, tm, tn, bias=None):
    M, K = x.shape
    N = w.shape[1]
    tm, tn = min(tm, M), min(tn, N)
    in_specs = [pl.BlockSpec((tm, K), lambda i, j: (i, 0)), pl.BlockSpec((K, tn), lambda i, j: (0, j))]
    args = [x, w]
    if bias is not None:
        in_specs.append(pl.BlockSpec((1, tn), lambda i, j: (0, j)))
        args.append(bias.reshape(1, N))
    return pl.pallas_call(
        functools.partial(_mm_kernel, has_bias=bias is not None),
        grid=(M // tm, N // tn), in_specs=in_specs,
        out_specs=pl.BlockSpec((tm, tn), lambda i, j: (i, j)),
        out_shape=jax.ShapeDtypeStruct((M, N), out_dtype),
        compiler_params=_cp(("parallel", "parallel")))(*args)


class _Group:
    def __init__(self, mod3, modf3, tm, blocks_per_g):
        self.mod3, self.modf3, self.tm, self.bpg = mod3, modf3, tm, blocks_per_g

    def spec(self, j, d):
        r, bpg = self.mod3.shape[1], self.bpg
        return pl.BlockSpec((None, r, d), lambda i: (i // bpg, 0, j))


def _norm_mod_kernel(x_ref, w_ref, sc_ref, sh_ref, o_ref):
    y = _rms(x_ref[...]) * w_ref[...]
    o_ref[...] = (y * (1.0 + sc_ref[...]) + sh_ref[...]).astype(o_ref.dtype)


def _norm_mod(x, w, grp, j_scale, j_shift, out_dtype):
    n, d = x.shape
    tm = grp.tm
    return pl.pallas_call(
        _norm_mod_kernel, grid=(n // tm,),
        in_specs=[pl.BlockSpec((tm, d), lambda i: (i, 0)), pl.BlockSpec((1, d), lambda i: (0, 0)),
                  grp.spec(j_scale, d), grp.spec(j_shift, d)],
        out_specs=pl.BlockSpec((tm, d), lambda i: (i, 0)),
        out_shape=jax.ShapeDtypeStruct((n, d), out_dtype),
        compiler_params=_cp(("parallel",)))(x, w.reshape(1, d), grp.mod3, grp.mod3)


SUBLANES = 8
PREP_CHUNKS = 2
SCAN_SEQS = 4
PREV_ROWS = 16


def _split_bf16(x):
    hi = x.astype(BF16)
    return hi, (x - hi.astype(F32)).astype(BF16)


def _dot3(a, b):
    d = lambda x, y: jnp.dot(x, y, preferred_element_type=F32)
    return d(a[0], b[0]) + d(a[0], b[1]) + d(a[1], b[0])


def _expand_matrix(C):
    k = jnp.arange(C)
    n = jnp.arange(SUBLANES * LANE)
    rem = n % LANE
    hit = ((k[:, None] // SUBLANES == rem[None, :] // SUBLANES) & (k[:, None] % SUBLANES == n[None, :] // LANE)
           & (rem[None, :] < C))
    return hit.astype(BF16)


def _unit_lower_inverses(a_list, at_list, g_ref, C):
    nh = len(a_list)
    row = lax.broadcasted_iota(jnp.int32, (C, C), 0)
    col = lax.broadcasted_iota(jnp.int32, (C, C), 1)
    blockdiag = (row // SUBLANES) == (col // SUBLANES)
    packed = []
    for at in at_list:
        m = jnp.where(blockdiag, at, 0.0)
        d = m[0:SUBLANES]
        for b in range(1, C // SUBLANES):
            d = d + m[SUBLANES * b:SUBLANES * (b + 1)]
        packed.append(d)
    stack = _split_bf16(jnp.concatenate(packed, axis=0))
    g = g_ref[...]
    coef = (jnp.dot(stack[0], g, preferred_element_type=F32) + jnp.dot(stack[1], g, preferred_element_type=F32))
    sub = lax.broadcasted_iota(jnp.int32, (SUBLANES, LANE), 0)
    lane = lax.broadcasted_iota(jnp.int32, (SUBLANES, LANE), 1)
    unit = jnp.where((lane % SUBLANES == sub) & (lane < C), 1.0, 0.0)
    xd = [unit] * nh
    for i in range(1, SUBLANES):
        e_i = jnp.where((lane % SUBLANES == i) & (lane < C), 1.0, 0.0)[0:1]
        for h in range(nh):
            c_i = coef[SUBLANES * h:SUBLANES * (h + 1), LANE * i:LANE * (i + 1)]
            new_row = e_i - jnp.sum(c_i * xd[h], axis=0, keepdims=True)
            xd[h] = jnp.where(sub == i, new_row, xd[h])
    x = [jnp.where(blockdiag, jnp.concatenate([xd[h][:, :C]] * (C // SUBLANES), axis=0), 0.0) for h in range(nh)]
    a16 = [a.astype(BF16) for a in a_list]
    zero = jnp.zeros((C, C), BF16)
    s = SUBLANES
    while s < C:
        below = ((row // s) % 2 == 1) & ((col // s) == (row // s) - 1)
        x16 = [x[h].astype(BF16) for h in range(nh)]
        m1 = [jnp.dot(x16[h], jnp.where(below, a16[h], zero), preferred_element_type=F32) for h in range(nh)]
        x = [x[h] - jnp.dot(m1[h].astype(BF16), x16[h], preferred_element_type=F32) for h in range(nh)]
        s *= 2
    return x


def _gdn_prep_kernel(qkv_ref, prev_ref, sm_ref, cw_ref, gp_ref, g_ref,
                     u_ref, w_ref, qe_ref, kd_ref, aqk_ref, dec_ref, *, H, DK, DV, C):
    c = pl.program_id(1)
    QK = H * DK
    P = PREV_ROWS
    R = qkv_ref.shape[0]

    prev = prev_ref[...]
    xin = jnp.concatenate([jnp.where(c == 0, jnp.zeros_like(prev), prev), qkv_ref[...]], axis=0)
    out_row = lax.broadcasted_iota(jnp.int32, (R, P + R), 0)
    in_row = lax.broadcasted_iota(jnp.int32, (R, P + R), 1)
    y = None
    for j in range(CONV_WIDTH):
        shift = (in_row == out_row + (P - (CONV_WIDTH - 1) + j)).astype(BF16)
        term = jnp.dot(shift, xin, preferred_element_type=F32) * cw_ref[j:j + 1, :]
        y = term if y is None else y + term
    y = _silu(y)

    sm = sm_ref[...]
    beta_all = jax.nn.sigmoid(sm)
    g_all = -jnp.exp(gp_ref[0:1, :]) * _softplus(sm + gp_ref[1:2, :])
    row_r = lax.broadcasted_iota(jnp.int32, (R, R), 0)
    col_r = lax.broadcasted_iota(jnp.int32, (R, R), 1)
    tril = ((col_r <= row_r) & (col_r // C == row_r // C)).astype(F32)
    gam_all = jnp.dot(tril, g_all, precision=HI, preferred_element_type=F32)
    lane = lax.broadcasted_iota(jnp.int32, sm.shape, 1)
    packed = jnp.where(lane < 8, beta_all, gam_all)
    row = lax.broadcasted_iota(jnp.int32, (C, C), 0)
    col = lax.broadcasted_iota(jnp.int32, (C, C), 1)

    a_list, at_list, rhs_list, places = [], [], [], []
    for ci, h in [(ci, h) for ci in range(R // C) for h in range(H)]:
        rs = slice(ci * C, (ci + 1) * C)
        rows_t = packed[rs].T
        qh = y[rs, h * DK:(h + 1) * DK]
        kh = y[rs, QK + h * DK:QK + (h + 1) * DK]
        vh = y[rs, 2 * QK + h * DV:2 * QK + (h + 1) * DV]
        qh = qh * lax.rsqrt(jnp.sum(qh * qh, axis=-1, keepdims=True) + NORM_EPS) * (DK ** -0.5)
        kh = kh * lax.rsqrt(jnp.sum(kh * kh, axis=-1, keepdims=True) + NORM_EPS)
        beta_c = beta_all[rs, h:h + 1]
        gam_c = gam_all[rs, 8 + h:9 + h]
        beta_r = rows_t[h:h + 1, :]
        gam_r = rows_t[8 + h:9 + h, :]
        dm = gam_c - gam_r
        decay = jnp.exp(jnp.where(col <= row, dm, -jnp.inf))
        decay_t = jnp.exp(jnp.where(row < col, -dm, -jnp.inf))
        qk_kk = _bdot_g(jnp.concatenate([qh, kh], axis=0), kh, NT)
        kk = qk_kk[C:]
        a_list.append(jnp.where(col < row, kk * beta_c * decay, 0.0))
        at_list.append(kk * beta_r * decay_t)
        egam = jnp.exp(gam_c)
        rhs_list.append(jnp.concatenate([beta_c * vh, beta_c * egam * kh], axis=1).astype(BF16))
        g_last = gam_c[C - 1:C, :]
        qe_ref[rs, h * DK:(h + 1) * DK] = (qh * egam).astype(qe_ref.dtype)
        kd_ref[rs, h * DK:(h + 1) * DK] = (kh * jnp.exp(g_last - gam_c)).astype(kd_ref.dtype)
        aqk_ref[rs, h * C:(h + 1) * C] = (qk_kk[:C] * decay).astype(aqk_ref.dtype)
        dec_ref[ci, h:h + 1, :] = jnp.broadcast_to(jnp.exp(g_last), (1, LANE))
        places.append((rs, h))

    t_inv = _unit_lower_inverses(a_list, at_list, g_ref, C)
    for i, (rs, h) in enumerate(places):
        t_hi, t_lo = _split_bf16(t_inv[i])
        uw = (jnp.dot(t_hi, rhs_list[i], preferred_element_type=F32)
              + jnp.dot(t_lo, rhs_list[i], preferred_element_type=F32))
        u_ref[rs, h * DV:(h + 1) * DV] = uw[:, :DV].astype(u_ref.dtype)
        w_ref[rs, h * DK:(h + 1) * DK] = uw[:, DV:].astype(w_ref.dtype)


def _gdn_scan_kernel(u_ref, w_ref, qe_ref, kd_ref, aqk_ref, dec_ref, z_ref, nw_ref, og_ref, s_out_ref, s_ref,
                     *, H, DK, DV, C):
    c = pl.program_id(1)

    @pl.when(c == 0)
    def _():
        s_ref[...] = jnp.zeros_like(s_ref)

    ch = [(g, h) for g in range(u_ref.shape[0]) for h in range(H)]
    ks = [slice(h * DK, (h + 1) * DK) for h in range(H)]
    vs = [slice(h * DV, (h + 1) * DV) for h in range(H)]
    s16 = [s_ref[g, h].astype(BF16) for g, h in ch]
    ws = [jnp.dot(w_ref[g, :, ks[h]], s16[i], preferred_element_type=F32) for i, (g, h) in enumerate(ch)]
    qs = [jnp.dot(qe_ref[g, :, ks[h]], s16[i], preferred_element_type=F32) for i, (g, h) in enumerate(ch)]
    v16 = [(u_ref[g, :, vs[h]].astype(F32) - ws[i]).astype(BF16) for i, (g, h) in enumerate(ch)]
    ds = [lax.dot_general(kd_ref[g, :, ks[h]], v16[i], TN, preferred_element_type=F32) for i, (g, h) in enumerate(ch)]
    o = [qs[i] + jnp.dot(aqk_ref[g, :, h * C:(h + 1) * C], v16[i], preferred_element_type=F32)
         for i, (g, h) in enumerate(ch)]
    for i, (g, h) in enumerate(ch):
        s_ref[g, h] = s_ref[g, h] * dec_ref[g, 0, h:h + 1, :] + ds[i]
        zh = z_ref[g, :, vs[h]].astype(F32)
        og_ref[g, :, vs[h]] = (_rms(o[i]) * nw_ref[...] * _silu(zh)).astype(og_ref.dtype)

    @pl.when(c == pl.num_programs(1) - 1)
    def _():
        s_out_ref[...] = s_ref[...]


def _gdn_prompt(qkv, z, small, ba_col, cw_t, gparams, norm_w, B, T, H, DK, DV):
    C = GDN_CHUNK
    nc = T // C
    n, W = qkv.shape
    kw = dict(H=H, DK=DK, DV=DV, C=C)
    pc = PREP_CHUNKS if nc % PREP_CHUNKS == 0 else 1
    ns = nc // pc
    R = pc * C
    blk = lambda w: pl.BlockSpec((R, w), lambda b, c: (b * ns + c, 0))
    ppc = R // PREV_ROWS
    u, w, qe, kd, aqk, dec = pl.pallas_call(
        functools.partial(_gdn_prep_kernel, **kw), grid=(B, ns),
        in_specs=[blk(W),
                  pl.BlockSpec((PREV_ROWS, W), lambda b, c: (jnp.maximum((b * ns + c) * ppc - 1, 0), 0)),
                  pl.BlockSpec((R, LANE), lambda b, c: (b * ns + c, ba_col)),
                  pl.BlockSpec((CONV_WIDTH, W), lambda b, c: (0, 0)),
                  pl.BlockSpec((2, LANE), lambda b, c: (0, 0)),
                  pl.BlockSpec((C, SUBLANES * LANE), lambda b, c: (0, 0))],
        out_specs=[blk(H * DV), blk(H * DK), blk(H * DK), blk(H * DK), blk(H * C),
                   pl.BlockSpec((pc, SUBLANES, LANE), lambda b, c: (b * ns + c, 0, 0))],
        out_shape=[jax.ShapeDtypeStruct((n, H * DV), BF16), jax.ShapeDtypeStruct((n, H * DK), BF16),
                   jax.ShapeDtypeStruct((n, H * DK), BF16), jax.ShapeDtypeStruct((n, H * DK), BF16),
                   jax.ShapeDtypeStruct((n, H * C), BF16), jax.ShapeDtypeStruct((B * nc, SUBLANES, LANE), F32)],
        compiler_params=_cp(("parallel", "parallel")))(qkv, qkv, small, cw_t, gparams, _expand_matrix(C))
    nb = SCAN_SEQS if B % SCAN_SEQS == 0 else 1
    seq = lambda a: a.reshape(B, T, a.shape[1])
    sblk = lambda w: pl.BlockSpec((nb, C, w), lambda b, c: (b, c, 0))
    og, s_fin = pl.pallas_call(
        functools.partial(_gdn_scan_kernel, **kw), grid=(B // nb, nc),
        in_specs=[sblk(H * DV), sblk(H * DK), sblk(H * DK), sblk(H * DK), sblk(H * C),
                  pl.BlockSpec((nb, 1, SUBLANES, LANE), lambda b, c: (b, c, 0, 0)),
                  sblk(H * DV), pl.BlockSpec((1, DV), lambda b, c: (0, 0))],
        out_specs=[sblk(H * DV), pl.BlockSpec((nb, H, DK, DV), lambda b, c: (b, 0, 0, 0))],
        out_shape=[jax.ShapeDtypeStruct((B, T, H * DV), BF16), jax.ShapeDtypeStruct((B, H, DK, DV), F32)],
        scratch_shapes=[pltpu.VMEM((nb, H, DK, DV), F32)],
        compiler_params=_cp(("parallel", "arbitrary")))(
            seq(u), seq(w), seq(qe), seq(kd), seq(aqk), dec.reshape(B, nc, SUBLANES, LANE), seq(z),
            norm_w.reshape(1, DV))
    return og.reshape(n, H * DV), s_fin


DECODE_TOKENS = 8


def _gdn_decode_kernel(u_ref, z_ref, ba_ref, buf_ref, s_in_ref, cw_ref, gp_ref, nw_ref,
                       og_ref, s_out_ref, buf_out_ref, *, H, DK, DV):
    QK = H * DK
    nt = u_ref.shape[0]
    u = u_ref[...]
    b0, b1, b2 = buf_ref[:, 0, :], buf_ref[:, 1, :], buf_ref[:, 2, :]
    y = b0 * cw_ref[0:1, :]
    y = y + b1 * cw_ref[1:2, :]
    y = y + b2 * cw_ref[2:3, :]
    y = y + u * cw_ref[3:4, :]
    buf_out_ref[:, 0, :] = b1
    buf_out_ref[:, 1, :] = b2
    buf_out_ref[:, 2, :] = u
    y = _silu(y)
    sm = ba_ref[...]
    beta_all = jax.nn.sigmoid(sm)
    dec_all = jnp.exp(-jnp.exp(gp_ref[0:1, :]) * _softplus(sm + gp_ref[1:2, :]))
    z_all = z_ref[...].astype(F32)
    cols = []
    for h in range(H):
        kh = y[:, QK + h * DK:QK + (h + 1) * DK]
        qh = y[:, h * DK:(h + 1) * DK]
        kh = kh * lax.rsqrt(jnp.sum(kh * kh, axis=-1, keepdims=True) + NORM_EPS)
        qh = qh * lax.rsqrt(jnp.sum(qh * qh, axis=-1, keepdims=True) + NORM_EPS) * (DK ** -0.5)
        cols.append(jnp.concatenate([kh, qh, jnp.zeros((DK - 2 * nt, DK), F32)], axis=0).T)
    for t in range(nt):
        hs = range(H)
        kcol = [cols[h][:, t:t + 1] for h in hs]
        qcol = [cols[h][:, nt + t:nt + t + 1] for h in hs]
        s_dec = [s_in_ref[t, h] * dec_all[t:t + 1, 8 + h:9 + h] for h in hs]
        ks = [jnp.sum(kcol[h] * s_dec[h], axis=0, keepdims=True) for h in hs]
        v_new = [beta_all[t:t + 1, h:h + 1] * (y[t:t + 1, 2 * QK + h * DV:2 * QK + (h + 1) * DV] - ks[h]) for h in hs]
        s_new = [s_dec[h] + kcol[h] * v_new[h] for h in hs]
        o = [jnp.sum(qcol[h] * s_new[h], axis=0, keepdims=True) for h in hs]
        for h in hs:
            s_out_ref[t, h] = s_new[h]
            zh = z_all[t:t + 1, h * DV:(h + 1) * DV]
            og_ref[t:t + 1, h * DV:(h + 1) * DV] = (_rms(o[h]) * nw_ref[...] * _silu(zh)).astype(og_ref.dtype)


def _gdn_decode(qkv, z, small, ba_col, conv_buf, ssm, cw_t, gparams, norm_w, H, DK, DV):
    Bs, W = qkv.shape
    kern = functools.partial(_gdn_decode_kernel, H=H, DK=DK, DV=DV)
    nt = DECODE_TOKENS
    assert Bs % nt == 0 and qkv.dtype == F32 and z.dtype == F32
    return pl.pallas_call(
        kern, grid=(Bs // nt,),
        in_specs=[pl.BlockSpec((nt, W), lambda b: (b, 0)),
                  pl.BlockSpec((nt, H * DV), lambda b: (b, 0)),
                  pl.BlockSpec((nt, LANE), lambda b: (b, ba_col)),
                  pl.BlockSpec((nt, CONV_WIDTH - 1, W), lambda b: (b, 0, 0)),
                  pl.BlockSpec((nt, H, DK, DV), lambda b: (b, 0, 0, 0)),
                  pl.BlockSpec((CONV_WIDTH, W), lambda b: (0, 0)),
                  pl.BlockSpec((2, LANE), lambda b: (0, 0)),
                  pl.BlockSpec((1, DV), lambda b: (0, 0))],
        out_specs=[pl.BlockSpec((nt, H * DV), lambda b: (b, 0)),
                   pl.BlockSpec((nt, H, DK, DV), lambda b: (b, 0, 0, 0)),
                   pl.BlockSpec((nt, CONV_WIDTH - 1, W), lambda b: (b, 0, 0))],
        out_shape=[jax.ShapeDtypeStruct((Bs, H * DV), F32),
                   jax.ShapeDtypeStruct(ssm.shape, F32),
                   jax.ShapeDtypeStruct(conv_buf.shape, F32)],
        compiler_params=_cp(("parallel",)))(qkv, z, small, conv_buf, ssm, cw_t, gparams, norm_w.reshape(1, DV))


def _mla_prep_prompt_kernel(sm_ref, cos_ref, sin_ref, cost_ref, sint_ref, qnw_ref, kvnw_ref,
                            wqt_ref, wqrt_ref, wuk_ref, wuvt_ref,
                            qt_ref, k_ref, vt_ref, lat_ref, pe_ref, *, H, NOPE, ROPE, VH, QL, KVL, scale):
    sm = sm_ref[...]
    qn = (_rms(sm[:, :QL]) * qnw_ref[...]).astype(BF16)
    qft = lax.dot_general(wqt_ref[...], qn, NT, preferred_element_type=F32)
    qrt = lax.dot_general(wqrt_ref[...], qn, NT, preferred_element_type=F32)
    qpt = qft[H * NOPE:, :] * cost_ref[...] + qrt * sint_ref[...]
    lat = _rms(sm[:, QL:QL + KVL]) * kvnw_ref[...]
    lat_ref[...] = lat
    lat16 = lat.astype(BF16)
    kn = jnp.dot(lat16, wuk_ref[...], preferred_element_type=F32)
    vt = lax.dot_general(wuvt_ref[...], lat16, NT, preferred_element_type=F32)
    o = QL + KVL
    kr = sm[:, o:o + ROPE] * cos_ref[...] + sm[:, o + ROPE:o + 2 * ROPE] * sin_ref[...]
    pe_ref[...] = kr
    for h in range(H):
        qt_ref[h] = (jnp.concatenate([qft[h * NOPE:(h + 1) * NOPE, :], qpt[h * ROPE:(h + 1) * ROPE, :]], axis=0)
                     * scale).astype(qt_ref.dtype)
        k_ref[h] = jnp.concatenate([kn[:, h * NOPE:(h + 1) * NOPE], kr], axis=-1).astype(k_ref.dtype)
        vt_ref[h, 0] = vt[h * VH:(h + 1) * VH, :].astype(vt_ref.dtype)


def _mla_prep_prompt(small, cos, sin, qnw, kvnw, wq, wqr, wuk, wuv, T, tm, H, NOPE, ROPE, VH, QL, KVL, scale):
    n, ws = small.shape
    nt = T // tm
    kern = functools.partial(_mla_prep_prompt_kernel, H=H, NOPE=NOPE, ROPE=ROPE, VH=VH, QL=QL, KVL=KVL, scale=scale)
    full = lambda a: pl.BlockSpec(a.shape, lambda i: (0,) * a.ndim)
    dqk = NOPE + ROPE
    cos_t = jnp.tile(cos.T, (H, 1))
    sin_t = jnp.tile(sin.T, (H, 1))
    wqt, wqrt, wuvt = wq.T, wqr.T, wuv.T
    return pl.pallas_call(
        kern, grid=(n // tm,),
        in_specs=[pl.BlockSpec((tm, ws), lambda i: (i, 0)),
                  pl.BlockSpec((tm, ROPE), lambda i: (i % nt, 0)),
                  pl.BlockSpec((tm, ROPE), lambda i: (i % nt, 0)),
                  pl.BlockSpec((H * ROPE, tm), lambda i: (0, i % nt)),
                  pl.BlockSpec((H * ROPE, tm), lambda i: (0, i % nt)),
                  full(qnw), full(kvnw), full(wqt), full(wqrt), full(wuk), full(wuvt)],
        out_specs=[pl.BlockSpec((H, dqk, tm), lambda i: (0, 0, i)),
                   pl.BlockSpec((H, tm, dqk), lambda i: (0, i, 0)),
                   pl.BlockSpec((H, 1, VH, tm), lambda i: (0, i, 0, 0)),
                   pl.BlockSpec((tm, KVL), lambda i: (i, 0)),
                   pl.BlockSpec((tm, ROPE), lambda i: (i, 0))],
        out_shape=[jax.ShapeDtypeStruct((H, dqk, n), BF16), jax.ShapeDtypeStruct((H, n, dqk), BF16),
                   jax.ShapeDtypeStruct((H, n // tm, VH, tm), BF16), jax.ShapeDtypeStruct((n, KVL), F32),
                   jax.ShapeDtypeStruct((n, ROPE), F32)],
        compiler_params=_cp(("parallel",)))(small, cos, sin, cos_t, sin_t, qnw, kvnw, wqt, wqrt, wuk, wuvt)


def _mla_prep_sample_kernel(sm_ref, cos_ref, sin_ref, qnw_ref, kvnw_ref, wq_ref, wqr_ref,
                            qn_ref, qp_ref, lat_ref, pe_ref, *, H, NOPE, ROPE, QL, KVL, scale):
    sm = sm_ref[...]
    cos = cos_ref[...]
    sin = sin_ref[...]
    qn = _rms(sm[:, :QL]) * qnw_ref[...]
    qf = _bdot(qn, wq_ref[...])
    qr = _bdot(qn, wqr_ref[...])
    qn_ref[...] = qf[:, :H * NOPE].astype(qn_ref.dtype)
    lat_ref[...] = _rms(sm[:, QL:QL + KVL]) * kvnw_ref[...]
    o = QL + KVL
    pe_ref[...] = sm[:, o:o + ROPE] * cos + sm[:, o + ROPE:o + 2 * ROPE] * sin
    for h in range(H):
        p0 = H * NOPE + h * ROPE
        qp = qf[:, p0:p0 + ROPE] * cos + qr[:, h * ROPE:(h + 1) * ROPE] * sin
        qp_ref[:, h * ROPE:(h + 1) * ROPE] = (qp * scale).astype(qp_ref.dtype)


def _mla_prep_sample(small, cos, sin, qnw, kvnw, wq, wqr, H, NOPE, ROPE, QL, KVL, scale):
    n, ws = small.shape
    kern = functools.partial(_mla_prep_sample_kernel, H=H, NOPE=NOPE, ROPE=ROPE, QL=QL, KVL=KVL, scale=scale)
    full = lambda a: pl.BlockSpec(a.shape, lambda i: (0,) * a.ndim)
    return pl.pallas_call(
        kern, grid=(1,),
        in_specs=[full(small), full(cos), full(sin), full(qnw), full(kvnw), full(wq), full(wqr)],
        out_specs=[pl.BlockSpec((n, H * NOPE), lambda i: (0, 0)), pl.BlockSpec((n, H * ROPE), lambda i: (0, 0)),
                   pl.BlockSpec((n, KVL), lambda i: (0, 0)), pl.BlockSpec((n, ROPE), lambda i: (0, 0))],
        out_shape=[jax.ShapeDtypeStruct((n, H * NOPE), BF16), jax.ShapeDtypeStruct((n, H * ROPE), BF16),
                   jax.ShapeDtypeStruct((n, KVL), F32), jax.ShapeDtypeStruct((n, ROPE), F32)],
        compiler_params=_cp(("arbitrary",)))(small, cos, sin, qnw, kvnw, wq, wqr)


def _head_proj_kernel(x_ref, w_ref, o_ref, *, dims, scale):
    o_ref[...] = (_bdot_g(x_ref[...], w_ref[...], dims) * scale).astype(o_ref.dtype)


def _q_latent(q_nope, w_uk, scale):
    H, KVL, NOPE = w_uk.shape
    n = q_nope.shape[0]
    return pl.pallas_call(
        functools.partial(_head_proj_kernel, dims=NT, scale=scale), grid=(H,),
        in_specs=[pl.BlockSpec((n, NOPE), lambda h: (0, h)), pl.BlockSpec((None, KVL, NOPE), lambda h: (h, 0, 0))],
        out_specs=pl.BlockSpec((None, n, KVL), lambda h: (h, 0, 0)),
        out_shape=jax.ShapeDtypeStruct((H, n, KVL), BF16),
        compiler_params=_cp(("parallel",)))(q_nope, w_uk)


def _o_value(o_lat, w_uv):
    H, KVL, VH = w_uv.shape
    n = o_lat.shape[1]
    return pl.pallas_call(
        functools.partial(_head_proj_kernel, dims=(((1,), (0,)), ((), ())), scale=1.0), grid=(H,),
        in_specs=[pl.BlockSpec((None, n, KVL), lambda h: (h, 0, 0)), pl.BlockSpec((None, KVL, VH), lambda h: (h, 0, 0))],
        out_specs=pl.BlockSpec((n, VH), lambda h: (0, h)),
        out_shape=jax.ShapeDtypeStruct((n, H * VH), BF16),
        compiler_params=_cp(("parallel",)))(o_lat, w_uv)


FLASH_SPLIT = 2


FLASH_HEADS = 4


def _flash_kernel(qt_ref, k_ref, vt_ref, o_ref, m_ref, l_ref, acc_ref, *, tq):
    qi = pl.program_id(2)
    nh = qt_ref.shape[0]
    ns = FLASH_SPLIT
    hq = tq // ns
    vh = vt_ref.shape[2]
    chains = [(g, t) for g in range(nh) for t in range(ns)]
    m_ref[...] = jnp.full_like(m_ref, -jnp.inf)
    l_ref[...] = jnp.zeros_like(l_ref)
    acc_ref[...] = jnp.zeros_like(acc_ref)
    qt = [qt_ref[g, :, t * hq:(t + 1) * hq] for g, t in chains]

    def scores(off):
        k = [k_ref[g, pl.ds(off, tq), :] for g in range(nh)]
        return [jnp.dot(k[g], qt[c], preferred_element_type=F32) for c, (g, t) in enumerate(chains)]

    def update(st, j):
        cs = range(len(chains))
        vt = [vt_ref[g, j] for g in range(nh)]
        m_prev = [m_ref[c] for c in cs]
        m_new = [jnp.maximum(m_prev[c], jnp.max(st[c], axis=0, keepdims=True)) for c in cs]
        p = [jnp.exp(st[c] - m_new[c]) for c in cs]
        corr = [jnp.exp(m_prev[c] - m_new[c]) for c in cs]
        pv = [jnp.dot(vt[chains[c][0]], p[c].astype(BF16), preferred_element_type=F32) for c in cs]
        for c in cs:
            m_ref[c] = m_new[c]
            l_ref[c] = corr[c] * l_ref[c] + jnp.sum(p[c], axis=0, keepdims=True)
            acc_ref[c] = acc_ref[c] * corr[c] + pv[c]

    def below_diagonal(j, carry):
        update(scores(pl.multiple_of(j * tq, tq)), j)
        return carry

    lax.fori_loop(0, qi, below_diagonal, 0)

    st = scores(pl.multiple_of(qi * tq, tq))
    for c, (g, t) in enumerate(chains):
        key = lax.broadcasted_iota(jnp.int32, st[c].shape, 0)
        qry = t * hq + lax.broadcasted_iota(jnp.int32, st[c].shape, 1)
        st[c] = jnp.where(key <= qry, st[c], -jnp.inf)
    update(st, qi)
    for c, (g, t) in enumerate(chains):
        o_ref[t * hq:(t + 1) * hq, g * vh:(g + 1) * vh] = (acc_ref[c] / l_ref[c]).T.astype(o_ref.dtype)


def _flash_attention(qt, k, vt, B, T, tq):
    H, dqk, n = qt.shape
    VH = vt.shape[2]
    nq = T // tq
    hq = tq // FLASH_SPLIT
    nh = FLASH_HEADS
    nc = nh * FLASH_SPLIT
    kern = functools.partial(_flash_kernel, tq=tq)
    return pl.pallas_call(
        kern, grid=(B, H // nh, nq),
        in_specs=[pl.BlockSpec((nh, dqk, tq), lambda b, h, i: (h, 0, b * nq + i)),
                  pl.BlockSpec((nh, T, dqk), lambda b, h, i: (h, b, 0)),
                  pl.BlockSpec((nh, nq, VH, tq), lambda b, h, i: (h, b, 0, 0))],
        out_specs=pl.BlockSpec((tq, nh * VH), lambda b, h, i: (b * nq + i, h)),
        out_shape=jax.ShapeDtypeStruct((n, H * VH), BF16),
        scratch_shapes=[pltpu.VMEM((nc, 1, hq), F32), pltpu.VMEM((nc, 1, hq), F32), pltpu.VMEM((nc, VH, hq), F32)],
        compiler_params=_cp(("parallel", "parallel", "arbitrary")))(qt, k, vt)


def _paged_kernel(pt_ref, ql_ref, qp_ref, latn_ref, pen_ref, kv_hbm, pe_hbm, o_ref,
                  kv_buf, pe_buf, sem, m_ref, l_ref, acc_ref, *, R, G, layer):
    b = pl.program_id(0)
    g = pl.program_id(1)
    ng = pl.num_programs(1)
    step = b * ng + g
    slot = step % 2

    def page_copies(bb, gg, sl, real_pages):
        out = []
        for r in range(R):
            for i in range(G):
                page = pt_ref[bb * R + r, gg * G + i] if real_pages else 0
                out.append(pltpu.make_async_copy(kv_hbm.at[layer, page], kv_buf.at[sl, r * G + i], sem.at[0, sl]))
                out.append(pltpu.make_async_copy(pe_hbm.at[layer, page], pe_buf.at[sl, r * G + i], sem.at[1, sl]))
        return out

    @pl.when(step == 0)
    def _():
        for cp in page_copies(0, 0, 0, True):
            cp.start()

    @pl.when(step + 1 < pl.num_programs(0) * ng)
    def _():
        nxt = step + 1
        for cp in page_copies(nxt // ng, nxt % ng, 1 - slot, True):
            cp.start()

    for cp in page_copies(b, g, slot, False):
        cp.wait()

    @pl.when(g == 0)
    def _():
        m_ref[...] = jnp.full_like(m_ref, -jnp.inf)
        l_ref[...] = jnp.zeros_like(l_ref)
        acc_ref[...] = jnp.zeros_like(acc_ref)

    kvs = [[kv_buf[slot, r * G + i].astype(BF16) for i in range(G)] for r in range(R)]
    P = kvs[0][0].shape[0]
    s, p, corr, m_new = [], [], [], []
    for r in range(R):
        ql = ql_ref[r]
        qp = qp_ref[r]
        s.append(jnp.concatenate(
            [lax.dot_general(ql, kvs[r][i], NT, preferred_element_type=F32)
             + jnp.dot(qp, pe_buf[slot, r * G + i].astype(BF16), preferred_element_type=F32) for i in range(G)],
            axis=-1))
    for r in range(R):
        m_prev = m_ref[r]
        m_new.append(jnp.maximum(m_prev, jnp.max(s[r], axis=-1, keepdims=True)))
        corr.append(jnp.exp(m_prev - m_new[r]))
        p.append(jnp.exp(s[r] - m_new[r]))
    pv = []
    for r in range(R):
        t = None
        for i in range(G):
            d = jnp.dot(p[r][:, i * P:(i + 1) * P].astype(BF16), kvs[r][i], preferred_element_type=F32)
            t = d if t is None else t + d
        pv.append(t)
    for r in range(R):
        m_ref[r] = m_new[r]
        l_ref[r] = corr[r] * l_ref[r] + jnp.sum(p[r], axis=-1, keepdims=True)
        acc_ref[r] = acc_ref[r] * corr[r] + pv[r]

    @pl.when(g == pl.num_programs(1) - 1)
    def _():
        for r in range(R):
            latn = latn_ref[r]
            s_n = (jnp.sum(ql_ref[r].astype(F32) * latn, axis=-1, keepdims=True)
                   + jnp.sum(qp_ref[r].astype(F32) * pen_ref[r], axis=-1, keepdims=True))
            m_old = m_ref[r]
            m2 = jnp.maximum(m_old, s_n)
            c2 = jnp.exp(m_old - m2)
            p2 = jnp.exp(s_n - m2)
            o_ref[r] = ((acc_ref[r] * c2 + p2 * latn) / (l_ref[r] * c2 + p2)).astype(o_ref.dtype)


def _paged_attention(q_lat, q_pe, lat_new, pe_new, cache_kv, cache_pe_t, page_table, layer):
    Bs, H, KVL = q_lat.shape
    ROPE = q_pe.shape[2]
    n_pages = page_table.shape[1]
    P = cache_kv.shape[2]
    G = min(PAGE_GROUP, n_pages)
    R = min(PAGE_REQUESTS, Bs)

    req = lambda rows, w: pl.BlockSpec((R, rows, w), lambda b, g, pt: (b, 0, 0))
    hbm = pl.BlockSpec(memory_space=pl.ANY)
    grid_spec = pltpu.PrefetchScalarGridSpec(
        num_scalar_prefetch=1, grid=(Bs // R, n_pages // G),
        in_specs=[req(H, KVL), req(H, ROPE), req(1, KVL), req(1, ROPE), hbm, hbm],
        out_specs=req(H, KVL),
        scratch_shapes=[pltpu.VMEM((2, R * G, P, KVL), cache_kv.dtype), pltpu.VMEM((2, R * G, ROPE, P), cache_pe_t.dtype),
                        pltpu.SemaphoreType.DMA((2, 2)),
                        pltpu.VMEM((R, H, 1), F32), pltpu.VMEM((R, H, 1), F32), pltpu.VMEM((R, H, KVL), F32)])
    return pl.pallas_call(
        functools.partial(_paged_kernel, R=R, G=G, layer=layer), grid_spec=grid_spec,
        out_shape=jax.ShapeDtypeStruct((Bs, H, KVL), BF16),
        compiler_params=_cp(("arbitrary", "arbitrary"), VMEM_LIMIT_LARGE))(
            page_table, q_lat, q_pe, lat_new.reshape(Bs, 1, KVL), pe_new.reshape(Bs, 1, ROPE), cache_kv, cache_pe_t)


def _post_mixer_kernel(x_ref, og_ref, om_ref, gab_ref, g1_ref, sc2_ref, sh2_ref, wog_ref, wom_ref, wout_ref,
                       nw_ref, wr_ref, br_ref, x1_ref, h2_ref, te_ref, tg_ref, *, D, E):
    gab = gab_ref[...].astype(F32)
    merged = (jax.nn.sigmoid(gab[:, :D]) * _bdot(og_ref[...], wog_ref[...])
              + jax.nn.sigmoid(gab[:, D:]) * _bdot(om_ref[...], wom_ref[...]))
    x1 = x_ref[...] + g1_ref[...] * _bdot(merged, wout_ref[...])
    x1_ref[...] = x1
    h2 = _rms(x1) * nw_ref[...] * (1.0 + sc2_ref[...]) + sh2_ref[...]
    h2_ref[...] = h2
    logits = _dot3(_split_bf16(h2), _split_bf16(wr_ref[...])) + br_ref[...]
    lane = lax.broadcasted_iota(jnp.int32, logits.shape, 1)
    logits = jnp.where(lane < E, logits, -jnp.inf)
    te = jnp.zeros(logits.shape, jnp.int32)
    ex = jnp.zeros(logits.shape, F32)
    top = None
    for k in range(TOP_K):
        m = jnp.max(logits, axis=-1, keepdims=True)
        idx = jnp.min(jnp.where(logits == m, lane, LANE), axis=-1, keepdims=True)
        top = m if top is None else top
        te = jnp.where(lane == k, idx, te)
        ex = jnp.where(lane == k, jnp.exp(m - top), ex)
        logits = jnp.where(lane == idx, -jnp.inf, logits)
    te_ref[...] = te
    tg_ref[...] = ex / jnp.sum(ex, axis=-1, keepdims=True)


def _post_mixer(x, og, om, gab, grp, wog, wom, wout, norm_w, wr, br, E):
    n, d = x.shape
    tm = grp.tm
    rowblk = lambda w: pl.BlockSpec((tm, w), lambda i: (i, 0))
    full = lambda a: pl.BlockSpec(a.shape, lambda i: (0,) * a.ndim)
    return pl.pallas_call(
        functools.partial(_post_mixer_kernel, D=d, E=E), grid=(n // tm,),
        in_specs=[rowblk(d), rowblk(d), rowblk(d), rowblk(2 * d), grp.spec(2, d), grp.spec(4, d), grp.spec(3, d),
                  full(wog), full(wom), full(wout), full(norm_w), full(wr), full(br)],
        out_specs=[rowblk(d), rowblk(d), rowblk(LANE), rowblk(LANE)],
        out_shape=[jax.ShapeDtypeStruct((n, d), F32), jax.ShapeDtypeStruct((n, d), F32),
                   jax.ShapeDtypeStruct((n, LANE), jnp.int32), jax.ShapeDtypeStruct((n, LANE), F32)],
        compiler_params=_cp(("parallel",)))(x, og, om, gab, grp.mod3, grp.mod3, grp.mod3,
                                            wog, wom, wout, norm_w, wr, br)


def _dispatch_kernel(pe_ref, slot_ref, ha_ref, hb_ref, hs_ref, zero_ref, sem, zsem, *, tb, tm, E, nb_a):
    i = pl.program_id(0)
    n_blocks = hs_ref.shape[0] // tm

    def zero_copy(row0):
        return pltpu.make_async_copy(zero_ref, hs_ref.at[pl.ds(pl.multiple_of(row0, tm), tm)], zsem)

    zero_jobs = [(pe_ref[e + 1] > pe_ref[e], pe_ref[e + 1] - tm) for e in range(E)]
    zero_jobs += [(b * tm >= pe_ref[E], b * tm) for b in range(n_blocks - E, n_blocks)]

    @pl.when(i == 0)
    def _():
        zero_ref[...] = jnp.zeros_like(zero_ref)
        for wanted, row0 in zero_jobs:
            @pl.when(wanted)
            def _():
                zero_copy(row0).start()
        for wanted, row0 in zero_jobs:
            @pl.when(wanted)
            def _():
                zero_copy(row0).wait()

    def scatter_rows(h_ref):
        for r in range(tb):
            for k in range(TOP_K):
                a = r * TOP_K + k
                pltpu.make_async_copy(h_ref.at[pl.ds(r, 1)], hs_ref.at[pl.ds(slot_ref[0, 0, a], 1)],
                                      sem).start(priority=a % 2)
        for k in range(TOP_K):
            pltpu.make_async_copy(h_ref, hs_ref.at[pl.ds(0, tb)], sem).wait()

    @pl.when(i < nb_a)
    def _():
        scatter_rows(ha_ref)

    @pl.when(i >= nb_a)
    def _():
        scatter_rows(hb_ref)


def _dispatch(ha, hb, slots3, pad_edges, n_slots, tb, tm):
    d = ha.shape[1]
    nb_a, nb_b = ha.shape[0] // tb, hb.shape[0] // tb
    nb = nb_a + nb_b
    E = pad_edges.shape[0] - 1
    grid_spec = pltpu.PrefetchScalarGridSpec(
        num_scalar_prefetch=1, grid=(nb,),
        in_specs=[pl.BlockSpec((1, 1, tb * TOP_K), lambda i, pe: (i, 0, 0), memory_space=pltpu.SMEM),
                  pl.BlockSpec((tb, d), lambda i, pe: (jnp.minimum(i, nb_a - 1), 0)),
                  pl.BlockSpec((tb, d), lambda i, pe: (jnp.maximum(i - nb_a, 0), 0))],
        out_specs=pl.BlockSpec(memory_space=pl.ANY),
        scratch_shapes=[pltpu.VMEM((tm, d), ha.dtype), pltpu.SemaphoreType.DMA, pltpu.SemaphoreType.DMA])
    return pl.pallas_call(
        functools.partial(_dispatch_kernel, tb=tb, tm=tm, E=E, nb_a=nb_a), grid_spec=grid_spec,
        out_shape=jax.ShapeDtypeStruct((n_slots, d), ha.dtype),
        compiler_params=_cp(("arbitrary",)))(pad_edges, slots3, ha, hb)


def _expert_kernel(be_ref, na_ref, x_ref, wgu_ref, bgu_ref, wdn_ref, bdn_ref, y_ref, wgu_s, wdn_s, *, DE):
    i = pl.program_id(0)
    active = i < na_ref[0]
    first = jnp.logical_or(i == 0, be_ref[i] != be_ref[jnp.maximum(i - 1, 0)])

    @pl.when(jnp.logical_and(active, first))
    def _():
        wgu_s[...] = wgu_ref[...].astype(BF16)
        wdn_s[...] = wdn_ref[...].astype(BF16)

    @pl.when(active)
    def _():
        gu = jnp.dot(x_ref[...].astype(BF16), wgu_s[...], preferred_element_type=F32) + bgu_ref[...]
        gt = jnp.minimum(gu[:, :DE], SWIGLU_LIMIT)
        up = jnp.clip(gu[:, DE:], -SWIGLU_LIMIT, SWIGLU_LIMIT)
        act = (up + 1.0) * gt * jax.nn.sigmoid(SWIGLU_ALPHA * gt)
        y_ref[...] = jnp.dot(act.astype(BF16), wdn_s[...], preferred_element_type=F32) + bdn_ref[...]

    @pl.when(jnp.logical_not(active))
    def _():
        y_ref[...] = jnp.zeros_like(y_ref)


def _experts(hs, blk_e, n_active, w_gu, b_gu, w_dn, b_dn, layer, tm):
    n_slots, d = hs.shape
    E, _, de2 = w_gu.shape[1:]
    de = de2 // 2
    grid_spec = pltpu.PrefetchScalarGridSpec(
        num_scalar_prefetch=2, grid=(n_slots // tm,),
        in_specs=[pl.BlockSpec((tm, d), lambda i, be, na: (jnp.minimum(i, na[0] - 1), 0)),
                  pl.BlockSpec((None, None, d, de2), lambda i, be, na: (layer, be[i], 0, 0)),
                  pl.BlockSpec((None, None, 1, de2), lambda i, be, na: (layer, be[i], 0, 0)),
                  pl.BlockSpec((None, None, de, d), lambda i, be, na: (layer, be[i], 0, 0)),
                  pl.BlockSpec((None, None, 1, d), lambda i, be, na: (layer, be[i], 0, 0))],
        out_specs=pl.BlockSpec((tm, d), lambda i, be, na: (i, 0)),
        scratch_shapes=[pltpu.VMEM((d, de2), BF16), pltpu.VMEM((de, d), BF16)])
    L = w_gu.shape[0]
    return pl.pallas_call(
        functools.partial(_expert_kernel, DE=de), grid_spec=grid_spec,
        out_shape=jax.ShapeDtypeStruct((n_slots, d), F32),
        compiler_params=_cp(("arbitrary",), VMEM_LIMIT_LARGE))(
            blk_e, n_active, hs, w_gu, b_gu.reshape(L, E, 1, de2), w_dn, b_dn.reshape(L, E, 1, d))


def _combine_kernel(slotc_ref, slotn_ref, tg_ref, x1_ref, g2_ref, scf_ref, shf_ref, nwf_ref, yp_ref, o_ref,
                    buf, sem, *, tb):
    i = pl.program_id(0)
    slot = i % 2

    def gather_starts(slot_ref, sl):
        for r in range(tb):
            for k in range(TOP_K):
                a = r * TOP_K + k
                pltpu.make_async_copy(yp_ref.at[pl.ds(slot_ref[0, 0, a], 1)], buf.at[sl, k, pl.ds(r, 1)],
                                      sem.at[sl]).start(priority=a % 2)

    @pl.when(i == 0)
    def _():
        gather_starts(slotc_ref, 0)

    @pl.when(i + 1 < pl.num_programs(0))
    def _():
        gather_starts(slotn_ref, 1 - slot)

    for k in range(TOP_K):
        pltpu.make_async_copy(yp_ref.at[pl.ds(0, tb)], buf.at[slot, k], sem.at[slot]).wait()
    tg = tg_ref[...]
    moe = tg[:, 0:1] * buf[slot, 0]
    for k in range(1, TOP_K):
        moe = moe + tg[:, k:k + 1] * buf[slot, k]
    x2 = x1_ref[...] + g2_ref[...] * moe
    o_ref[...] = _rms(x2) * nwf_ref[...] * (1.0 + scf_ref[...]) + shf_ref[...]


def _combine_final(y_pad, slots3, first_block, tg, x1, grp, norm_final_w):
    n, d = x1.shape
    tb = grp.tm
    nb = n // tb
    rf, bpg = grp.modf3.shape[1], grp.bpg
    fspec = lambda j: pl.BlockSpec((None, rf, d), lambda i: (i // bpg, 0, j))
    sspec = lambda f: pl.BlockSpec((1, 1, tb * TOP_K), lambda i: (first_block + f(i), 0, 0),
                                   memory_space=pltpu.SMEM)
    return pl.pallas_call(
        functools.partial(_combine_kernel, tb=tb), grid=(nb,),
        in_specs=[sspec(lambda i: i), sspec(lambda i: jnp.minimum(i + 1, nb - 1)),
                  pl.BlockSpec((tb, LANE), lambda i: (i, 0)),
                  pl.BlockSpec((tb, d), lambda i: (i, 0)),
                  grp.spec(5, d), fspec(1), fspec(0),
                  pl.BlockSpec((1, d), lambda i: (0, 0)),
                  pl.BlockSpec(memory_space=pl.ANY)],
        out_specs=pl.BlockSpec((tb, d), lambda i: (i, 0)),
        out_shape=jax.ShapeDtypeStruct((n, d), F32),
        scratch_shapes=[pltpu.VMEM((2, TOP_K, tb, d), F32), pltpu.SemaphoreType.DMA((2,))],
        compiler_params=_cp(("arbitrary",)))(
            slots3, slots3, tg, x1, grp.mod3, grp.modf3, grp.modf3, norm_final_w.reshape(1, d), y_pad)


def _route(top_e, E, tm):
    n = top_e.shape[0]
    hit = top_e[:, :, None] == jnp.arange(E, dtype=jnp.int32)
    tok_oh = jnp.sum(hit.astype(jnp.int32), axis=1)
    csum = jnp.cumsum(tok_oh, axis=0)
    counts = csum[-1]
    padded = (counts + tm - 1) // tm * tm
    pad_end = jnp.cumsum(padded)
    slots = jnp.sum(jnp.where(hit, (csum - tok_oh + (pad_end - padded)[None, :])[:, None, :], 0), axis=2)
    n_blocks = -(-(n * TOP_K) // tm) + E
    blk_start = jnp.arange(n_blocks, dtype=jnp.int32) * tm
    blk_e = jnp.minimum(jnp.sum((pad_end[None, :] <= blk_start[:, None]).astype(jnp.int32), axis=1), E - 1)
    n_active = (pad_end[-1] // tm).reshape(1)
    pad_edges = jnp.concatenate([jnp.zeros((1,), pad_end.dtype), pad_end])
    return (slots.astype(jnp.int32), pad_edges.astype(jnp.int32), blk_e.astype(jnp.int32),
            n_active.astype(jnp.int32), n_blocks * tm)


def _rope_tables(pos, rope):
    half = rope // 2
    inv = ROPE_THETA ** (-jnp.arange(half, dtype=F32) / half)
    ang = pos.astype(F32)[:, None] * inv[None, :]
    cos, sin = jnp.cos(ang), jnp.sin(ang)
    return jnp.concatenate([cos, cos], axis=-1), jnp.concatenate([sin, sin], axis=-1)


def _rotate_cols(w, rope):
    k, n = w.shape
    w3 = w.reshape(k, n // rope, rope)
    half = rope // 2
    return jnp.concatenate([-w3[..., half:], w3[..., :half]], axis=-1).reshape(k, n)


def kernel(x_prompt, x_sample, c_prompt, c_sample, cache_kv, cache_pe, state_ssm, state_conv, page_table, norm_mix_w, norm_ffn_w, w_ada, b_ada, w_in, conv_w, A_log, dt_bias, gdn_norm_w, w_o_gdn, q_norm_w, kv_norm_w, w_uq, w_uk, w_uv, w_o_mla, w_out, w_router, b_router, w_gu, b_gu, w_dn, b_dn, w_ada_final, b_ada_final, norm_final_w):
    B, T, D = x_prompt.shape
    Bs, Ts, _ = x_sample.shape
    assert Ts == 1
    depth = w_in.shape[0]
    H, DK, DV = state_ssm.shape[2:]
    QK = H * DK
    CONV = state_conv.shape[3]
    assert CONV == 2 * QK + H * DV and H == SUBLANES
    QL = q_norm_w.shape[1]
    HM, KVL, NOPE = w_uk.shape[1:]
    VH = w_uv.shape[3]
    ROPE = cache_pe.shape[3]
    E = w_router.shape[2]
    scale = float(NOPE + ROPE) ** -0.5
    n_p, n_s = B * T, Bs * Ts
    past_len = page_table.shape[1] * cache_kv.shape[2]
    cache_pe_t = jnp.swapaxes(cache_pe, 2, 3)

    c_all = jnp.concatenate([c_prompt, c_sample], axis=0)
    modf = _matmul(c_all, w_ada_final, F32, c_all.shape[0], 1024, b_ada_final)
    tm_p = min(512, T)
    tb = 128
    cos_p, sin_p = _rope_tables(jnp.arange(T), ROPE)
    cos_s, sin_s = _rope_tables(past_len + jnp.arange(Ts), ROPE)

    hp = x_prompt.reshape(n_p, D)
    hs = x_sample.reshape(n_s, D)
    outs = {k: [] for k in ('kv_p', 'pe_p', 'ssm_p', 'conv_p', 'kv_s', 'pe_s', 'ssm_s', 'conv_s')}
    for l in range(depth):
        mod = _matmul(c_all, w_ada[l], F32, c_all.shape[0], 1024, b_ada[l])
        last = l == depth - 1
        grp_p = _Group(mod[:B].reshape(B, 1, 6 * D), modf[:B].reshape(B, 1, 2 * D), tm_p, T // tm_p)
        grp_s = _Group(mod[B:].reshape(1, Bs, 6 * D), modf[B:].reshape(1, Bs, 2 * D), Bs, 1)
        grp_pc = _Group(grp_p.mod3, grp_p.modf3, tb, T // tb)

        offs = [0]
        for s in (CONV, H * DV, H, H, QL, KVL, ROPE, D, D):
            offs.append(offs[-1] + s)
        wi = w_in[l]
        seg = lambda i: wi[:, offs[i]:offs[i + 1]]
        w_qkv = seg(0).astype(BF16)
        w_z = seg(1).astype(BF16)
        w_gab = jnp.concatenate([seg(7), seg(8)], axis=1).astype(BF16)
        n_small = QL + KVL + 2 * ROPE
        ba_col = -(-n_small // LANE)
        w_small = jnp.concatenate(
            [seg(4), seg(5), seg(6), _rotate_cols(seg(6), ROPE), jnp.zeros((D, ba_col * LANE - n_small), F32),
             seg(2), jnp.zeros((D, 8 - H), F32), seg(3), jnp.zeros((D, LANE - 8 - H), F32)], axis=1).astype(BF16)
        cw_t = conv_w[l].T
        gparams = jnp.zeros((2, LANE), F32).at[0, 8:8 + H].set(A_log[l]).at[1, 8:8 + H].set(dt_bias[l])
        wq = w_uq[l].reshape(QL, HM, NOPE + ROPE)
        wq_pe = wq[:, :, NOPE:].reshape(QL, HM * ROPE)
        wq_all = jnp.concatenate([wq[:, :, :NOPE].reshape(QL, HM * NOPE), wq_pe], axis=1).astype(BF16)
        wq_rot = _rotate_cols(wq_pe, ROPE).astype(BF16)
        wuk_all = jnp.transpose(w_uk[l], (1, 0, 2)).reshape(KVL, HM * NOPE).astype(BF16)
        wuv_all = jnp.transpose(w_uv[l], (1, 0, 2)).reshape(KVL, HM * VH).astype(BF16)
        qnw = q_norm_w[l].reshape(1, QL)
        kvnw = kv_norm_w[l].reshape(1, KVL)
        wog = w_o_gdn[l].astype(BF16)
        wom = w_o_mla[l].astype(BF16)
        wout = w_out[l].astype(BF16)
        wr = jnp.pad(w_router[l], ((0, 0), (0, LANE - E)))
        br = jnp.pad(b_router[l], (0, LANE - E)).reshape(1, LANE)
        nfw = norm_ffn_w[l].reshape(1, D)

        h1 = _norm_mod(hp, norm_mix_w[l], grp_p, 1, 0, BF16)
        qkv_p = _matmul(h1, w_qkv, BF16, 1024, 512)
        z_p = _matmul(h1, w_z, BF16, 1024, 512)
        gab_p = _matmul(h1, w_gab, BF16, 1024, 512)
        small_p = _matmul(h1, w_small, F32, 1024, w_small.shape[1])
        og_p, ssm_p = _gdn_prompt(qkv_p, z_p, small_p, ba_col, cw_t, gparams, gdn_norm_w[l], B, T, H, DK, DV)
        q_p, k_p, v_p, lat_p, pe_p = _mla_prep_prompt(small_p, cos_p, sin_p, qnw, kvnw, wq_all, wq_rot, wuk_all,
                                                      wuv_all, T, tm_p, HM, NOPE, ROPE, VH, QL, KVL, scale)
        om_p = _flash_attention(q_p, k_p, v_p, B, T, tm_p)
        x1_p, h2_p, te_p, tg_p = _post_mixer(hp, og_p, om_p, gab_p, grp_p, wog, wom, wout, nfw, wr, br, E)
        outs['kv_p'].append(lat_p.reshape(B, T, KVL))
        outs['pe_p'].append(pe_p.reshape(B, T, ROPE))
        outs['ssm_p'].append(ssm_p)
        outs['conv_p'].append(qkv_p.reshape(B, T, CONV)[:, T - (CONV_WIDTH - 1):, :].astype(F32))

        h1s = _norm_mod(hs, norm_mix_w[l], grp_s, 1, 0, BF16)
        qkv_s = _matmul(h1s, w_qkv, F32, Bs, 512)
        z_s = _matmul(h1s, w_z, F32, Bs, 512)
        gab_s = _matmul(h1s, w_gab, BF16, Bs, 512)
        small_s = _matmul(h1s, w_small, F32, Bs, w_small.shape[1])
        og_s, ssm_s, conv_s = _gdn_decode(qkv_s, z_s, small_s, ba_col, state_conv[l], state_ssm[l], cw_t, gparams,
                                          gdn_norm_w[l], H, DK, DV)
        qn_s, qp_s, lat_s, pe_s = _mla_prep_sample(small_s, cos_s, sin_s, qnw, kvnw, wq_all, wq_rot,
                                                   HM, NOPE, ROPE, QL, KVL, scale)
        q_lat = jnp.transpose(_q_latent(qn_s, w_uk[l], scale), (1, 0, 2))
        o_lat = _paged_attention(q_lat, qp_s.reshape(n_s, HM, ROPE), lat_s, pe_s, cache_kv, cache_pe_t, page_table, l)
        om_s = _o_value(jnp.transpose(o_lat, (1, 0, 2)), w_uv[l])
        x1_s, h2_s, te_s, tg_s = _post_mixer(hs, og_s, om_s, gab_s, grp_s, wog, wom, wout, nfw, wr, br, E)
        outs['kv_s'].append(lat_s.reshape(Bs, Ts, KVL))
        outs['pe_s'].append(pe_s.reshape(Bs, Ts, ROPE))
        outs['ssm_s'].append(ssm_s)
        outs['conv_s'].append(conv_s)

        tm_e = 256
        top_e = jnp.concatenate([te_p[:, :TOP_K], te_s[:, :TOP_K]], axis=0)
        slots, pad_edges, blk_e, n_active, n_slots = _route(top_e, E, tm_e)
        slots3 = slots.reshape((n_p + n_s) // tb, 1, tb * TOP_K)
        h_sorted = _dispatch(h2_p, h2_s, slots3, pad_edges, n_slots, tb, tm_e)
        y_pad = _experts(h_sorted, blk_e, n_active, w_gu, b_gu, w_dn, b_dn, l, tm_e)
        if not last:
            raise NotImplementedError("stacked layers need an un-normalised residual output")
        assert grp_s.tm == tb
        hp = _combine_final(y_pad, slots3, 0, tg_p, x1_p, grp_pc, norm_final_w)
        hs = _combine_final(y_pad, slots3, n_p // tb, tg_s, x1_s, grp_s, norm_final_w)

    st = lambda k: jnp.stack(outs[k])
    return (hp.reshape(B, T, D), hs.reshape(Bs, Ts, D),
            st('kv_p'), st('pe_p'), st('ssm_p'), st('conv_p'),
            st('kv_s'), st('pe_s'), st('ssm_s'), st('conv_s'))
```

```python
import functools

import jax
import jax.numpy as jnp
from jax import lax
from jax.experimental import pallas as pl
from jax.experimental.pallas import tpu as pltpu

F32 = jnp.float32
BF16 = jnp.bfloat16
HI = lax.Precision.HIGHEST

NORM_EPS = 1e-6
ROPE_THETA = 10000.0
TOP_K = 4
SWIGLU_LIMIT = 7.0
SWIGLU_ALPHA = 1.702
CONV_WIDTH = 4
GDN_CHUNK = 64
PAGE_GROUP = 16
PAGE_REQUESTS = 4
LANE = 128
VMEM_LIMIT = 48 * 1024 * 1024
VMEM_LIMIT_LARGE = 58 * 1024 * 1024

NT = (((1,), (1,)), ((), ()))
TN = (((0,), (0,)), ((), ()))


def _cp(sem, vmem=VMEM_LIMIT):
    return pltpu.CompilerParams(dimension_semantics=sem, vmem_limit_bytes=vmem)


def _rms(x):
    return x * lax.rsqrt(jnp.mean(x * x, axis=-1, keepdims=True) + NORM_EPS)


def _softplus(x):
    return jnp.maximum(x, 0.0) + jnp.log1p(jnp.exp(-jnp.abs(x)))


def _silu(x):
    return x * jax.nn.sigmoid(x)


def _bdot(a, b):
    return jnp.dot(a.astype(BF16), b.astype(BF16), preferred_element_type=F32)


def _bdot_g(a, b, dims):
    return lax.dot_general(a.astype(BF16), b.astype(BF16), dims, preferred_element_type=F32)


def _mm_kernel(*refs, has_bias):
    if has_bias:
        x_ref, w_ref, b_ref, o_ref = refs
    else:
        x_ref, w_ref, o_ref = refs
    acc = _bdot(x_ref[...], w_ref[...])
    if has_bias:
        acc = acc + b_ref[...]
    o_ref[...] = acc.astype(o_ref.dtype)


def _matmul(x, w, out_dtype, tm, tn, bias=None):
    M, K = x.shape
    N = w.shape[1]
    tm, tn = min(tm, M), min(tn, N)
    in_specs = [pl.BlockSpec((tm, K), lambda i, j: (i, 0)), pl.BlockSpec((K, tn), lambda i, j: (0, j))]
    args = [x, w]
    if bias is not None:
        in_specs.append(pl.BlockSpec((1, tn), lambda i, j: (0, j)))
        args.append(bias.reshape(1, N))
    return pl.pallas_call(
        functools.partial(_mm_kernel, has_bias=bias is not None),
        grid=(M // tm, N // tn), in_specs=in_specs,
        out_specs=pl.BlockSpec((tm, tn), lambda i, j: (i, j)),
        out_shape=jax.ShapeDtypeStruct((M, N), out_dtype),
        compiler_params=_cp(("parallel", "parallel")))(*args)


class _Group:
    def __init__(self, mod3, modf3, tm, blocks_per_g):
        self.mod3, self.modf3, self.tm, self.bpg = mod3, modf3, tm, blocks_per_g

    def spec(self, j, d):
        r, bpg = self.mod3.shape[1], self.bpg
        return pl.BlockSpec((None, r, d), lambda i: (i // bpg, 0, j))


def _norm_mod_kernel(x_ref, w_ref, sc_ref, sh_ref, o_ref):
    y = _rms(x_ref[...]) * w_ref[...]
    o_ref[...] = (y * (1.0 + sc_ref[...]) + sh_ref[...]).astype(o_ref.dtype)


def _norm_mod(x, w, grp, j_scale, j_shift, out_dtype):
    n, d = x.shape
    tm = grp.tm
    return pl.pallas_call(
        _norm_mod_kernel, grid=(n // tm,),
        in_specs=[pl.BlockSpec((tm, d), lambda i: (i, 0)), pl.BlockSpec((1, d), lambda i: (0, 0)),
                  grp.spec(j_scale, d), grp.spec(j_shift, d)],
        out_specs=pl.BlockSpec((tm, d), lambda i: (i, 0)),
        out_shape=jax.ShapeDtypeStruct((n, d), out_dtype),
        compiler_params=_cp(("parallel",)))(x, w.reshape(1, d), grp.mod3, grp.mod3)


SUBLANES = 8
PREP_CHUNKS = 2
SCAN_SEQS = 8
PREV_ROWS = 16


def _split_bf16(x):
    hi = x.astype(BF16)
    return hi, (x - hi.astype(F32)).astype(BF16)


def _dot3(a, b):
    d = lambda x, y: jnp.dot(x, y, preferred_element_type=F32)
    return d(a[0], b[0]) + d(a[0], b[1]) + d(a[1], b[0])


def _expand_matrix(C):
    k = jnp.arange(C)
    n = jnp.arange(SUBLANES * LANE)
    rem = n % LANE
    hit = ((k[:, None] // SUBLANES == rem[None, :] // SUBLANES) & (k[:, None] % SUBLANES == n[None, :] // LANE)
           & (rem[None, :] < C))
    return hit.astype(BF16)


def _unit_lower_inverses(a_list, at_list, g_ref, C):
    nh = len(a_list)
    row = lax.broadcasted_iota(jnp.int32, (C, C), 0)
    col = lax.broadcasted_iota(jnp.int32, (C, C), 1)
    blockdiag = (row // SUBLANES) == (col // SUBLANES)
    packed = []
    for at in at_list:
        m = jnp.where(blockdiag, at, 0.0)
        d = m[0:SUBLANES]
        for b in range(1, C // SUBLANES):
            d = d + m[SUBLANES * b:SUBLANES * (b + 1)]
        packed.append(d)
    stack = _split_bf16(jnp.concatenate(packed, axis=0))
    g = g_ref[...]
    coef = (jnp.dot(stack[0], g, preferred_element_type=F32) + jnp.dot(stack[1], g, preferred_element_type=F32))
    sub = lax.broadcasted_iota(jnp.int32, (SUBLANES, LANE), 0)
    lane = lax.broadcasted_iota(jnp.int32, (SUBLANES, LANE), 1)
    unit = jnp.where((lane % SUBLANES == sub) & (lane < C), 1.0, 0.0)
    xd = [unit] * nh
    for i in range(1, SUBLANES):
        e_i = jnp.where((lane % SUBLANES == i) & (lane < C), 1.0, 0.0)[0:1]
        for h in range(nh):
            c_i = coef[SUBLANES * h:SUBLANES * (h + 1), LANE * i:LANE * (i + 1)]
            new_row = e_i - jnp.sum(c_i * xd[h], axis=0, keepdims=True)
            xd[h] = jnp.where(sub == i, new_row, xd[h])
    x = [jnp.where(blockdiag, jnp.concatenate([xd[h][:, :C]] * (C // SUBLANES), axis=0), 0.0) for h in range(nh)]
    a16 = [a.astype(BF16) for a in a_list]
    zero = jnp.zeros((C, C), BF16)
    s = SUBLANES
    while s < C:
        below = ((row // s) % 2 == 1) & ((col // s) == (row // s) - 1)
        x16 = [x[h].astype(BF16) for h in range(nh)]
        m1 = [jnp.dot(x16[h], jnp.where(below, a16[h], zero), preferred_element_type=F32) for h in range(nh)]
        x = [x[h] - jnp.dot(m1[h].astype(BF16), x16[h], preferred_element_type=F32) for h in range(nh)]
        s *= 2
    return x


def _gdn_prep_kernel(qkv_ref, prev_ref, sm_ref, cw_ref, gp_ref, g_ref,
                     u_ref, w_ref, qe_ref, kd_ref, aqk_ref, dec_ref, *, H, DK, DV, C):
    c = pl.program_id(1)
    QK = H * DK
    P = PREV_ROWS
    R = qkv_ref.shape[0]

    prev = prev_ref[...]
    xin = jnp.concatenate([jnp.where(c == 0, jnp.zeros_like(prev), prev), qkv_ref[...]], axis=0)
    out_row = lax.broadcasted_iota(jnp.int32, (R, P + R), 0)
    in_row = lax.broadcasted_iota(jnp.int32, (R, P + R), 1)
    y = None
    for j in range(CONV_WIDTH):
        shift = (in_row == out_row + (P - (CONV_WIDTH - 1) + j)).astype(BF16)
        term = jnp.dot(shift, xin, preferred_element_type=F32) * cw_ref[j:j + 1, :]
        y = term if y is None else y + term
    y = _silu(y)

    sm = sm_ref[...]
    beta_all = jax.nn.sigmoid(sm)
    g_all = -jnp.exp(gp_ref[0:1, :]) * _softplus(sm + gp_ref[1:2, :])
    row_r = lax.broadcasted_iota(jnp.int32, (R, R), 0)
    col_r = lax.broadcasted_iota(jnp.int32, (R, R), 1)
    tril = ((col_r <= row_r) & (col_r // C == row_r // C)).astype(F32)
    gam_all = jnp.dot(tril, g_all, precision=HI, preferred_element_type=F32)
    lane = lax.broadcasted_iota(jnp.int32, sm.shape, 1)
    packed = jnp.where(lane < 8, beta_all, gam_all)
    row = lax.broadcasted_iota(jnp.int32, (C, C), 0)
    col = lax.broadcasted_iota(jnp.int32, (C, C), 1)

    a_list, at_list, rhs_list, places = [], [], [], []
    for ci, h in [(ci, h) for ci in range(R // C) for h in range(H)]:
        rs = slice(ci * C, (ci + 1) * C)
        rows_t = packed[rs].T
        qh = y[rs, h * DK:(h + 1) * DK]
        kh = y[rs, QK + h * DK:QK + (h + 1) * DK]
        vh = y[rs, 2 * QK + h * DV:2 * QK + (h + 1) * DV]
        qh = qh * lax.rsqrt(jnp.sum(qh * qh, axis=-1, keepdims=True) + NORM_EPS) * (DK ** -0.5)
        kh = kh * lax.rsqrt(jnp.sum(kh * kh, axis=-1, keepdims=True) + NORM_EPS)
        beta_c = beta_all[rs, h:h + 1]
        gam_c = gam_all[rs, 8 + h:9 + h]
        beta_r = rows_t[h:h + 1, :]
        gam_r = rows_t[8 + h:9 + h, :]
        dm = gam_c - gam_r
        decay = jnp.exp(jnp.where(col <= row, dm, -jnp.inf))
        decay_t = jnp.exp(jnp.where(row < col, -dm, -jnp.inf))
        qk_kk = _bdot_g(jnp.concatenate([qh, kh], axis=0), kh, NT)
        kk = qk_kk[C:]
        a_list.append(jnp.where(col < row, kk * beta_c * decay, 0.0))
        at_list.append(kk * beta_r * decay_t)
        egam = jnp.exp(gam_c)
        rhs_list.append(jnp.concatenate([beta_c * vh, beta_c * egam * kh], axis=1).astype(BF16))
        g_last = gam_c[C - 1:C, :]
        qe_ref[rs, h * DK:(h + 1) * DK] = (qh * egam).astype(qe_ref.dtype)
        kd_ref[rs, h * DK:(h + 1) * DK] = (kh * jnp.exp(g_last - gam_c)).astype(kd_ref.dtype)
        aqk_ref[rs, h * C:(h + 1) * C] = (qk_kk[:C] * decay).astype(aqk_ref.dtype)
        dec_ref[ci, h:h + 1, :] = jnp.broadcast_to(jnp.exp(g_last), (1, LANE))
        places.append((rs, h))

    t_inv = _unit_lower_inverses(a_list, at_list, g_ref, C)
    for i, (rs, h) in enumerate(places):
        t_hi, t_lo = _split_bf16(t_inv[i])
        uw = (jnp.dot(t_hi, rhs_list[i], preferred_element_type=F32)
              + jnp.dot(t_lo, rhs_list[i], preferred_element_type=F32))
        u_ref[rs, h * DV:(h + 1) * DV] = uw[:, :DV].astype(u_ref.dtype)
        w_ref[rs, h * DK:(h + 1) * DK] = uw[:, DV:].astype(w_ref.dtype)


def _gdn_scan_kernel(u_ref, w_ref, qe_ref, kd_ref, aqk_ref, dec_ref, z_ref, nw_ref, og_ref, s_out_ref, s_ref,
                     *, H, DK, DV, C):
    c = pl.program_id(1)

    @pl.when(c == 0)
    def _():
        s_ref[...] = jnp.zeros_like(s_ref)

    ch = [(g, h) for g in range(u_ref.shape[0]) for h in range(H)]
    ks = [slice(h * DK, (h + 1) * DK) for h in range(H)]
    vs = [slice(h * DV, (h + 1) * DV) for h in range(H)]
    s16 = [s_ref[g, h].astype(BF16) for g, h in ch]
    ws = [jnp.dot(w_ref[g, :, ks[h]], s16[i], preferred_element_type=F32) for i, (g, h) in enumerate(ch)]
    qs = [jnp.dot(qe_ref[g, :, ks[h]], s16[i], preferred_element_type=F32) for i, (g, h) in enumerate(ch)]
    v16 = [(u_ref[g, :, vs[h]].astype(F32) - ws[i]).astype(BF16) for i, (g, h) in enumerate(ch)]
    ds = [lax.dot_general(kd_ref[g, :, ks[h]], v16[i], TN, preferred_element_type=F32) for i, (g, h) in enumerate(ch)]
    o = [qs[i] + jnp.dot(aqk_ref[g, :, h * C:(h + 1) * C], v16[i], preferred_element_type=F32)
         for i, (g, h) in enumerate(ch)]
    for i, (g, h) in enumerate(ch):
        s_ref[g, h] = s_ref[g, h] * dec_ref[g, 0, h:h + 1, :] + ds[i]
        zh = z_ref[g, :, vs[h]].astype(F32)
        og_ref[g, :, vs[h]] = (_rms(o[i]) * nw_ref[...] * _silu(zh)).astype(og_ref.dtype)

    @pl.when(c == pl.num_programs(1) - 1)
    def _():
        s_out_ref[...] = s_ref[...]


def _gdn_prompt(qkv, z, small, ba_col, cw_t, gparams, norm_w, B, T, H, DK, DV):
    C = GDN_CHUNK
    nc = T // C
    n, W = qkv.shape
    kw = dict(H=H, DK=DK, DV=DV, C=C)
    pc = PREP_CHUNKS if nc % PREP_CHUNKS == 0 else 1
    ns = nc // pc
    R = pc * C
    blk = lambda w: pl.BlockSpec((R, w), lambda b, c: (b * ns + c, 0))
    ppc = R // PREV_ROWS
    u, w, qe, kd, aqk, dec = pl.pallas_call(
        functools.partial(_gdn_prep_kernel, **kw), grid=(B, ns),
        in_specs=[blk(W),
                  pl.BlockSpec((PREV_ROWS, W), lambda b, c: (jnp.maximum((b * ns + c) * ppc - 1, 0), 0)),
                  pl.BlockSpec((R, LANE), lambda b, c: (b * ns + c, ba_col)),
                  pl.BlockSpec((CONV_WIDTH, W), lambda b, c: (0, 0)),
                  pl.BlockSpec((2, LANE), lambda b, c: (0, 0)),
                  pl.BlockSpec((C, SUBLANES * LANE), lambda b, c: (0, 0))],
        out_specs=[blk(H * DV), blk(H * DK), blk(H * DK), blk(H * DK), blk(H * C),
                   pl.BlockSpec((pc, SUBLANES, LANE), lambda b, c: (b * ns + c, 0, 0))],
        out_shape=[jax.ShapeDtypeStruct((n, H * DV), BF16), jax.ShapeDtypeStruct((n, H * DK), BF16),
                   jax.ShapeDtypeStruct((n, H * DK), BF16), jax.ShapeDtypeStruct((n, H * DK), BF16),
                   jax.ShapeDtypeStruct((n, H * C), BF16), jax.ShapeDtypeStruct((B * nc, SUBLANES, LANE), F32)],
        compiler_params=_cp(("parallel", "parallel")))(qkv, qkv, small, cw_t, gparams, _expand_matrix(C))
    nb = SCAN_SEQS if B % SCAN_SEQS == 0 else 1
    seq = lambda a: a.reshape(B, T, a.shape[1])
    sblk = lambda w: pl.BlockSpec((nb, C, w), lambda b, c: (b, c, 0))
    og, s_fin = pl.pallas_call(
        functools.partial(_gdn_scan_kernel, **kw), grid=(B // nb, nc),
        in_specs=[sblk(H * DV), sblk(H * DK), sblk(H * DK), sblk(H * DK), sblk(H * C),
                  pl.BlockSpec((nb, 1, SUBLANES, LANE), lambda b, c: (b, c, 0, 0)),
                  sblk(H * DV), pl.BlockSpec((1, DV), lambda b, c: (0, 0))],
        out_specs=[sblk(H * DV), pl.BlockSpec((nb, H, DK, DV), lambda b, c: (b, 0, 0, 0))],
        out_shape=[jax.ShapeDtypeStruct((B, T, H * DV), BF16), jax.ShapeDtypeStruct((B, H, DK, DV), F32)],
        scratch_shapes=[pltpu.VMEM((nb, H, DK, DV), F32)],
        compiler_params=_cp(("parallel", "arbitrary")))(
            seq(u), seq(w), seq(qe), seq(kd), seq(aqk), dec.reshape(B, nc, SUBLANES, LANE), seq(z),
            norm_w.reshape(1, DV))
    return og.reshape(n, H * DV), s_fin


DECODE_TOKENS = 8


def _gdn_decode_kernel(u_ref, z_ref, ba_ref, buf_ref, s_in_ref, cw_ref, gp_ref, nw_ref,
                       og_ref, s_out_ref, buf_out_ref, *, H, DK, DV):
    QK = H * DK
    nt = u_ref.shape[0]
    u = u_ref[...]
    b0, b1, b2 = buf_ref[:, 0, :], buf_ref[:, 1, :], buf_ref[:, 2, :]
    y = b0 * cw_ref[0:1, :]
    y = y + b1 * cw_ref[1:2, :]
    y = y + b2 * cw_ref[2:3, :]
    y = y + u * cw_ref[3:4, :]
    buf_out_ref[:, 0, :] = b1
    buf_out_ref[:, 1, :] = b2
    buf_out_ref[:, 2, :] = u
    y = _silu(y)
    sm = ba_ref[...]
    beta_all = jax.nn.sigmoid(sm)
    dec_all = jnp.exp(-jnp.exp(gp_ref[0:1, :]) * _softplus(sm + gp_ref[1:2, :]))
    z_all = z_ref[...].astype(F32)
    cols = []
    for h in range(H):
        kh = y[:, QK + h * DK:QK + (h + 1) * DK]
        qh = y[:, h * DK:(h + 1) * DK]
        kh = kh * lax.rsqrt(jnp.sum(kh * kh, axis=-1, keepdims=True) + NORM_EPS)
        qh = qh * lax.rsqrt(jnp.sum(qh * qh, axis=-1, keepdims=True) + NORM_EPS) * (DK ** -0.5)
        cols.append(jnp.concatenate([kh, qh, jnp.zeros((DK - 2 * nt, DK), F32)], axis=0).T)
    for t in range(nt):
        hs = range(H)
        kcol = [cols[h][:, t:t + 1] for h in hs]
        qcol = [cols[h][:, nt + t:nt + t + 1] for h in hs]
        s_dec = [s_in_ref[t, h] * dec_all[t:t + 1, 8 + h:9 + h] for h in hs]
        ks = [jnp.sum(kcol[h] * s_dec[h], axis=0, keepdims=True) for h in hs]
        v_new = [beta_all[t:t + 1, h:h + 1] * (y[t:t + 1, 2 * QK + h * DV:2 * QK + (h + 1) * DV] - ks[h]) for h in hs]
        s_new = [s_dec[h] + kcol[h] * v_new[h] for h in hs]
        o = [jnp.sum(qcol[h] * s_new[h], axis=0, keepdims=True) for h in hs]
        for h in hs:
            s_out_ref[t, h] = s_new[h]
            zh = z_all[t:t + 1, h * DV:(h + 1) * DV]
            og_ref[t:t + 1, h * DV:(h + 1) * DV] = (_rms(o[h]) * nw_ref[...] * _silu(zh)).astype(og_ref.dtype)


def _gdn_decode(qkv, z, small, ba_col, conv_buf, ssm, cw_t, gparams, norm_w, H, DK, DV):
    Bs, W = qkv.shape
    kern = functools.partial(_gdn_decode_kernel, H=H, DK=DK, DV=DV)
    nt = DECODE_TOKENS
    assert Bs % nt == 0 and qkv.dtype == F32 and z.dtype == F32
    return pl.pallas_call(
        kern, grid=(Bs // nt,),
        in_specs=[pl.BlockSpec((nt, W), lambda b: (b, 0)),
                  pl.BlockSpec((nt, H * DV), lambda b: (b, 0)),
                  pl.BlockSpec((nt, LANE), lambda b: (b, ba_col)),
                  pl.BlockSpec((nt, CONV_WIDTH - 1, W), lambda b: (b, 0, 0)),
                  pl.BlockSpec((nt, H, DK, DV), lambda b: (b, 0, 0, 0)),
                  pl.BlockSpec((CONV_WIDTH, W), lambda b: (0, 0)),
                  pl.BlockSpec((2, LANE), lambda b: (0, 0)),
                  pl.BlockSpec((1, DV), lambda b: (0, 0))],
        out_specs=[pl.BlockSpec((nt, H * DV), lambda b: (b, 0)),
                   pl.BlockSpec((nt, H, DK, DV), lambda b: (b, 0, 0, 0)),
                   pl.BlockSpec((nt, CONV_WIDTH - 1, W), lambda b: (b, 0, 0))],
        out_shape=[jax.ShapeDtypeStruct((Bs, H * DV), F32),
                   jax.ShapeDtypeStruct(ssm.shape, F32),
                   jax.ShapeDtypeStruct(conv_buf.shape, F32)],
        compiler_params=_cp(("parallel",)))(qkv, z, small, conv_buf, ssm, cw_t, gparams, norm_w.reshape(1, DV))


def _mla_prep_prompt_kernel(sm_ref, cos_ref, sin_ref, cost_ref, sint_ref, qnw_ref, kvnw_ref,
                            wqt_ref, wqrt_ref, wuk_ref, wuvt_ref,
                            qt_ref, k_ref, vt_ref, lat_ref, pe_ref, *, H, NOPE, ROPE, VH, QL, KVL, scale):
    sm = sm_ref[...]
    qn = (_rms(sm[:, :QL]) * qnw_ref[...]).astype(BF16)
    qft = lax.dot_general(wqt_ref[...], qn, NT, preferred_element_type=F32)
    qrt = lax.dot_general(wqrt_ref[...], qn, NT, preferred_element_type=F32)
    qpt = qft[H * NOPE:, :] * cost_ref[...] + qrt * sint_ref[...]
    lat = _rms(sm[:, QL:QL + KVL]) * kvnw_ref[...]
    lat_ref[...] = lat
    lat16 = lat.astype(BF16)
    kn = jnp.dot(lat16, wuk_ref[...], preferred_element_type=F32)
    vt = lax.dot_general(wuvt_ref[...], lat16, NT, preferred_element_type=F32)
    o = QL + KVL
    kr = sm[:, o:o + ROPE] * cos_ref[...] + sm[:, o + ROPE:o + 2 * ROPE] * sin_ref[...]
    pe_ref[...] = kr
    for h in range(H):
        qt_ref[h] = (jnp.concatenate([qft[h * NOPE:(h + 1) * NOPE, :], qpt[h * ROPE:(h + 1) * ROPE, :]], axis=0)
                     * scale).astype(qt_ref.dtype)
        k_ref[h] = jnp.concatenate([kn[:, h * NOPE:(h + 1) * NOPE], kr], axis=-1).astype(k_ref.dtype)
        vt_ref[h, 0] = vt[h * VH:(h + 1) * VH, :].astype(vt_ref.dtype)


def _mla_prep_prompt(small, cos, sin, qnw, kvnw, wq, wqr, wuk, wuv, T, tm, H, NOPE, ROPE, VH, QL, KVL, scale):
    n, ws = small.shape
    nt = T // tm
    kern = functools.partial(_mla_prep_prompt_kernel, H=H, NOPE=NOPE, ROPE=ROPE, VH=VH, QL=QL, KVL=KVL, scale=scale)
    full = lambda a: pl.BlockSpec(a.shape, lambda i: (0,) * a.ndim)
    dqk = NOPE + ROPE
    cos_t = jnp.tile(cos.T, (H, 1))
    sin_t = jnp.tile(sin.T, (H, 1))
    wqt, wqrt, wuvt = wq.T, wqr.T, wuv.T
    return pl.pallas_call(
        kern, grid=(n // tm,),
        in_specs=[pl.BlockSpec((tm, ws), lambda i: (i, 0)),
                  pl.BlockSpec((tm, ROPE), lambda i: (i % nt, 0)),
                  pl.BlockSpec((tm, ROPE), lambda i: (i % nt, 0)),
                  pl.BlockSpec((H * ROPE, tm), lambda i: (0, i % nt)),
                  pl.BlockSpec((H * ROPE, tm), lambda i: (0, i % nt)),
                  full(qnw), full(kvnw), full(wqt), full(wqrt), full(wuk), full(wuvt)],
        out_specs=[pl.BlockSpec((H, dqk, tm), lambda i: (0, 0, i)),
                   pl.BlockSpec((H, tm, dqk), lambda i: (0, i, 0)),
                   pl.BlockSpec((H, 1, VH, tm), lambda i: (0, i, 0, 0)),
                   pl.BlockSpec((tm, KVL), lambda i: (i, 0)),
                   pl.BlockSpec((tm, ROPE), lambda i: (i, 0))],
        out_shape=[jax.ShapeDtypeStruct((H, dqk, n), BF16), jax.ShapeDtypeStruct((H, n, dqk), BF16),
                   jax.ShapeDtypeStruct((H, n // tm, VH, tm), BF16), jax.ShapeDtypeStruct((n, KVL), F32),
                   jax.ShapeDtypeStruct((n, ROPE), F32)],
        compiler_params=_cp(("parallel",)))(small, cos, sin, cos_t, sin_t, qnw, kvnw, wqt, wqrt, wuk, wuvt)


def _mla_prep_sample_kernel(sm_ref, cos_ref, sin_ref, qnw_ref, kvnw_ref, wq_ref, wqr_ref,
                            qn_ref, qp_ref, lat_ref, pe_ref, *, H, NOPE, ROPE, QL, KVL, scale):
    sm = sm_ref[...]
    cos = cos_ref[...]
    sin = sin_ref[...]
    qn = _rms(sm[:, :QL]) * qnw_ref[...]
    qf = _bdot(qn, wq_ref[...])
    qr = _bdot(qn, wqr_ref[...])
    qn_ref[...] = qf[:, :H * NOPE].astype(qn_ref.dtype)
    lat_ref[...] = _rms(sm[:, QL:QL + KVL]) * kvnw_ref[...]
    o = QL + KVL
    pe_ref[...] = sm[:, o:o + ROPE] * cos + sm[:, o + ROPE:o + 2 * ROPE] * sin
    for h in range(H):
        p0 = H * NOPE + h * ROPE
        qp = qf[:, p0:p0 + ROPE] * cos + qr[:, h * ROPE:(h + 1) * ROPE] * sin
        qp_ref[:, h * ROPE:(h + 1) * ROPE] = (qp * scale).astype(qp_ref.dtype)


def _mla_prep_sample(small, cos, sin, qnw, kvnw, wq, wqr, H, NOPE, ROPE, QL, KVL, scale):
    n, ws = small.shape
    kern = functools.partial(_mla_prep_sample_kernel, H=H, NOPE=NOPE, ROPE=ROPE, QL=QL, KVL=KVL, scale=scale)
    full = lambda a: pl.BlockSpec(a.shape, lambda i: (0,) * a.ndim)
    return pl.pallas_call(
        kern, grid=(1,),
        in_specs=[full(small), full(cos), full(sin), full(qnw), full(kvnw), full(wq), full(wqr)],
        out_specs=[pl.BlockSpec((n, H * NOPE), lambda i: (0, 0)), pl.BlockSpec((n, H * ROPE), lambda i: (0, 0)),
                   pl.BlockSpec((n, KVL), lambda i: (0, 0)), pl.BlockSpec((n, ROPE), lambda i: (0, 0))],
        out_shape=[jax.ShapeDtypeStruct((n, H * NOPE), BF16), jax.ShapeDtypeStruct((n, H * ROPE), BF16),
                   jax.ShapeDtypeStruct((n, KVL), F32), jax.ShapeDtypeStruct((n, ROPE), F32)],
        compiler_params=_cp(("arbitrary",)))(small, cos, sin, qnw, kvnw, wq, wqr)


def _head_proj_kernel(x_ref, w_ref, o_ref, *, dims, scale):
    o_ref[...] = (_bdot_g(x_ref[...], w_ref[...], dims) * scale).astype(o_ref.dtype)


def _q_latent(q_nope, w_uk, scale):
    H, KVL, NOPE = w_uk.shape
    n = q_nope.shape[0]
    return pl.pallas_call(
        functools.partial(_head_proj_kernel, dims=NT, scale=scale), grid=(H,),
        in_specs=[pl.BlockSpec((n, NOPE), lambda h: (0, h)), pl.BlockSpec((None, KVL, NOPE), lambda h: (h, 0, 0))],
        out_specs=pl.BlockSpec((None, n, KVL), lambda h: (h, 0, 0)),
        out_shape=jax.ShapeDtypeStruct((H, n, KVL), BF16),
        compiler_params=_cp(("parallel",)))(q_nope, w_uk)


def _o_value(o_lat, w_uv):
    H, KVL, VH = w_uv.shape
    n = o_lat.shape[1]
    return pl.pallas_call(
        functools.partial(_head_proj_kernel, dims=(((1,), (0,)), ((), ())), scale=1.0), grid=(H,),
        in_specs=[pl.BlockSpec((None, n, KVL), lambda h: (h, 0, 0)), pl.BlockSpec((None, KVL, VH), lambda h: (h, 0, 0))],
        out_specs=pl.BlockSpec((n, VH), lambda h: (0, h)),
        out_shape=jax.ShapeDtypeStruct((n, H * VH), BF16),
        compiler_params=_cp(("parallel",)))(o_lat, w_uv)


FLASH_SPLIT = 2


FLASH_HEADS = 8


def _flash_kernel(qt_ref, k_ref, vt_ref, o_ref, m_ref, l_ref, acc_ref, *, tq):
    qi = pl.program_id(2)
    nh = qt_ref.shape[0]
    ns = FLASH_SPLIT
    hq = tq // ns
    vh = vt_ref.shape[2]
    chains = [(g, t) for g in range(nh) for t in range(ns)]
    m_ref[...] = jnp.full_like(m_ref, -jnp.inf)
    l_ref[...] = jnp.zeros_like(l_ref)
    acc_ref[...] = jnp.zeros_like(acc_ref)
    qt = [qt_ref[g, :, t * hq:(t + 1) * hq] for g, t in chains]

    def scores(off):
        k = [k_ref[g, pl.ds(off, tq), :] for g in range(nh)]
        return [jnp.dot(k[g], qt[c], preferred_element_type=F32) for c, (g, t) in enumerate(chains)]

    def update(st, j):
        cs = range(len(chains))
        vt = [vt_ref[g, j] for g in range(nh)]
        m_prev = [m_ref[c] for c in cs]
        m_new = [jnp.maximum(m_prev[c], jnp.max(st[c], axis=0, keepdims=True)) for c in cs]
        p = [jnp.exp(st[c] - m_new[c]) for c in cs]
        corr = [jnp.exp(m_prev[c] - m_new[c]) for c in cs]
        pv = [jnp.dot(vt[chains[c][0]], p[c].astype(BF16), preferred_element_type=F32) for c in cs]
        for c in cs:
            m_ref[c] = m_new[c]
            l_ref[c] = corr[c] * l_ref[c] + jnp.sum(p[c], axis=0, keepdims=True)
            acc_ref[c] = acc_ref[c] * corr[c] + pv[c]

    def below_diagonal(j, carry):
        update(scores(pl.multiple_of(j * tq, tq)), j)
        return carry

    lax.fori_loop(0, qi, below_diagonal, 0)

    st = scores(pl.multiple_of(qi * tq, tq))
    for c, (g, t) in enumerate(chains):
        key = lax.broadcasted_iota(jnp.int32, st[c].shape, 0)
        qry = t * hq + lax.broadcasted_iota(jnp.int32, st[c].shape, 1)
        st[c] = jnp.where(key <= qry, st[c], -jnp.inf)
    update(st, qi)
    for c, (g, t) in enumerate(chains):
        o_ref[t * hq:(t + 1) * hq, g * vh:(g + 1) * vh] = (acc_ref[c] / l_ref[c]).T.astype(o_ref.dtype)


def _flash_attention(qt, k, vt, B, T, tq):
    H, dqk, n = qt.shape
    VH = vt.shape[2]
    nq = T // tq
    hq = tq // FLASH_SPLIT
    nh = FLASH_HEADS
    nc = nh * FLASH_SPLIT
    kern = functools.partial(_flash_kernel, tq=tq)
    return pl.pallas_call(
        kern, grid=(B, H // nh, nq),
        in_specs=[pl.BlockSpec((nh, dqk, tq), lambda b, h, i: (h, 0, b * nq + i)),
                  pl.BlockSpec((nh, T, dqk), lambda b, h, i: (h, b, 0)),
                  pl.BlockSpec((nh, nq, VH, tq), lambda b, h, i: (h, b, 0, 0))],
        out_specs=pl.BlockSpec((tq, nh * VH), lambda b, h, i: (b * nq + i, h)),
        out_shape=jax.ShapeDtypeStruct((n, H * VH), BF16),
        scratch_shapes=[pltpu.VMEM((nc, 1, hq), F32), pltpu.VMEM((nc, 1, hq), F32), pltpu.VMEM((nc, VH, hq), F32)],
        compiler_params=_cp(("parallel", "parallel", "arbitrary")))(qt, k, vt)


def _paged_kernel(pt_ref, ql_ref, qp_ref, latn_ref, pen_ref, kv_hbm, pe_hbm, o_ref,
                  kv_buf, pe_buf, sem, m_ref, l_ref, acc_ref, *, R, G, layer):
    b = pl.program_id(0)
    g = pl.program_id(1)
    ng = pl.num_programs(1)
    step = b * ng + g
    slot = step % 2

    def page_copies(bb, gg, sl, real_pages):
        out = []
        for r in range(R):
            for i in range(G):
                page = pt_ref[bb * R + r, gg * G + i] if real_pages else 0
                out.append(pltpu.make_async_copy(kv_hbm.at[layer, page], kv_buf.at[sl, r * G + i], sem.at[0, sl]))
                out.append(pltpu.make_async_copy(pe_hbm.at[layer, page], pe_buf.at[sl, r * G + i], sem.at[1, sl]))
        return out

    @pl.when(step == 0)
    def _():
        for cp in page_copies(0, 0, 0, True):
            cp.start()

    @pl.when(step + 1 < pl.num_programs(0) * ng)
    def _():
        nxt = step + 1
        for cp in page_copies(nxt // ng, nxt % ng, 1 - slot, True):
            cp.start()

    for cp in page_copies(b, g, slot, False):
        cp.wait()

    @pl.when(g == 0)
    def _():
        m_ref[...] = jnp.full_like(m_ref, -jnp.inf)
        l_ref[...] = jnp.zeros_like(l_ref)
        acc_ref[...] = jnp.zeros_like(acc_ref)

    kvs = [[kv_buf[slot, r * G + i].astype(BF16) for i in range(G)] for r in range(R)]
    P = kvs[0][0].shape[0]
    s, p, corr, m_new = [], [], [], []
    for r in range(R):
        ql = ql_ref[r]
        qp = qp_ref[r]
        s.append(jnp.concatenate(
            [lax.dot_general(ql, kvs[r][i], NT, preferred_element_type=F32)
             + jnp.dot(qp, pe_buf[slot, r * G + i].astype(BF16), preferred_element_type=F32) for i in range(G)],
            axis=-1))
    for r in range(R):
        m_prev = m_ref[r]
        m_new.append(jnp.maximum(m_prev, jnp.max(s[r], axis=-1, keepdims=True)))
        corr.append(jnp.exp(m_prev - m_new[r]))
        p.append(jnp.exp(s[r] - m_new[r]))
    pv = []
    for r in range(R):
        t = None
        for i in range(G):
            d = jnp.dot(p[r][:, i * P:(i + 1) * P].astype(BF16), kvs[r][i], preferred_element_type=F32)
            t = d if t is None else t + d
        pv.append(t)
    for r in range(R):
        m_ref[r] = m_new[r]
        l_ref[r] = corr[r] * l_ref[r] + jnp.sum(p[r], axis=-1, keepdims=True)
        acc_ref[r] = acc_ref[r] * corr[r] + pv[r]

    @pl.when(g == pl.num_programs(1) - 1)
    def _():
        for r in range(R):
            latn = latn_ref[r]
            s_n = (jnp.sum(ql_ref[r].astype(F32) * latn, axis=-1, keepdims=True)
                   + jnp.sum(qp_ref[r].astype(F32) * pen_ref[r], axis=-1, keepdims=True))
            m_old = m_ref[r]
            m2 = jnp.maximum(m_old, s_n)
            c2 = jnp.exp(m_old - m2)
            p2 = jnp.exp(s_n - m2)
            o_ref[r] = ((acc_ref[r] * c2 + p2 * latn) / (l_ref[r] * c2 + p2)).astype(o_ref.dtype)


def _paged_attention(q_lat, q_pe, lat_new, pe_new, cache_kv, cache_pe_t, page_table, layer):
    Bs, H, KVL = q_lat.shape
    ROPE = q_pe.shape[2]
    n_pages = page_table.shape[1]
    P = cache_kv.shape[2]
    G = min(PAGE_GROUP, n_pages)
    R = min(PAGE_REQUESTS, Bs)

    req = lambda rows, w: pl.BlockSpec((R, rows, w), lambda b, g, pt: (b, 0, 0))
    hbm = pl.BlockSpec(memory_space=pl.ANY)
    grid_spec = pltpu.PrefetchScalarGridSpec(
        num_scalar_prefetch=1, grid=(Bs // R, n_pages // G),
        in_specs=[req(H, KVL), req(H, ROPE), req(1, KVL), req(1, ROPE), hbm, hbm],
        out_specs=req(H, KVL),
        scratch_shapes=[pltpu.VMEM((2, R * G, P, KVL), cache_kv.dtype), pltpu.VMEM((2, R * G, ROPE, P), cache_pe_t.dtype),
                        pltpu.SemaphoreType.DMA((2, 2)),
                        pltpu.VMEM((R, H, 1), F32), pltpu.VMEM((R, H, 1), F32), pltpu.VMEM((R, H, KVL), F32)])
    return pl.pallas_call(
        functools.partial(_paged_kernel, R=R, G=G, layer=layer), grid_spec=grid_spec,
        out_shape=jax.ShapeDtypeStruct((Bs, H, KVL), BF16),
        compiler_params=_cp(("arbitrary", "arbitrary"), VMEM_LIMIT_LARGE))(
            page_table, q_lat, q_pe, lat_new.reshape(Bs, 1, KVL), pe_new.reshape(Bs, 1, ROPE), cache_kv, cache_pe_t)


def _post_mixer_kernel(x_ref, og_ref, om_ref, gab_ref, g1_ref, sc2_ref, sh2_ref, wog_ref, wom_ref, wout_ref,
                       nw_ref, wr_ref, br_ref, x1_ref, h2_ref, te_ref, tg_ref, *, D, E):
    gab = gab_ref[...].astype(F32)
    merged = (jax.nn.sigmoid(gab[:, :D]) * _bdot(og_ref[...], wog_ref[...])
              + jax.nn.sigmoid(gab[:, D:]) * _bdot(om_ref[...], wom_ref[...]))
    x1 = x_ref[...] + g1_ref[...] * _bdot(merged, wout_ref[...])
    x1_ref[...] = x1
    h2 = _rms(x1) * nw_ref[...] * (1.0 + sc2_ref[...]) + sh2_ref[...]
    h2_ref[...] = h2
    logits = _dot3(_split_bf16(h2), _split_bf16(wr_ref[...])) + br_ref[...]
    lane = lax.broadcasted_iota(jnp.int32, logits.shape, 1)
    logits = jnp.where(lane < E, logits, -jnp.inf)
    te = jnp.zeros(logits.shape, jnp.int32)
    ex = jnp.zeros(logits.shape, F32)
    top = None
    for k in range(TOP_K):
        m = jnp.max(logits, axis=-1, keepdims=True)
        idx = jnp.min(jnp.where(logits == m, lane, LANE), axis=-1, keepdims=True)
        top = m if top is None else top
        te = jnp.where(lane == k, idx, te)
        ex = jnp.where(lane == k, jnp.exp(m - top), ex)
        logits = jnp.where(lane == idx, -jnp.inf, logits)
    te_ref[...] = te
    tg_ref[...] = ex / jnp.sum(ex, axis=-1, keepdims=True)


def _post_mixer(x, og, om, gab, grp, wog, wom, wout, norm_w, wr, br, E):
    n, d = x.shape
    tm = grp.tm
    rowblk = lambda w: pl.BlockSpec((tm, w), lambda i: (i, 0))
    full = lambda a: pl.BlockSpec(a.shape, lambda i: (0,) * a.ndim)
    return pl.pallas_call(
        functools.partial(_post_mixer_kernel, D=d, E=E), grid=(n // tm,),
        in_specs=[rowblk(d), rowblk(d), rowblk(d), rowblk(2 * d), grp.spec(2, d), grp.spec(4, d), grp.spec(3, d),
                  full(wog), full(wom), full(wout), full(norm_w), full(wr), full(br)],
        out_specs=[rowblk(d), rowblk(d), rowblk(LANE), rowblk(LANE)],
        out_shape=[jax.ShapeDtypeStruct((n, d), F32), jax.ShapeDtypeStruct((n, d), F32),
                   jax.ShapeDtypeStruct((n, LANE), jnp.int32), jax.ShapeDtypeStruct((n, LANE), F32)],
        compiler_params=_cp(("parallel",)))(x, og, om, gab, grp.mod3, grp.mod3, grp.mod3,
                                            wog, wom, wout, norm_w, wr, br)


def _dispatch_kernel(pe_ref, slot_ref, ha_ref, hb_ref, hs_ref, zero_ref, sem, zsem, *, tb, tm, E, nb_a):
    i = pl.program_id(0)
    n_blocks = hs_ref.shape[0] // tm

    def zero_copy(row0):
        return pltpu.make_async_copy(zero_ref, hs_ref.at[pl.ds(pl.multiple_of(row0, tm), tm)], zsem)

    zero_jobs = [(pe_ref[e + 1] > pe_ref[e], pe_ref[e + 1] - tm) for e in range(E)]
    zero_jobs += [(b * tm >= pe_ref[E], b * tm) for b in range(n_blocks - E, n_blocks)]

    @pl.when(i == 0)
    def _():
        zero_ref[...] = jnp.zeros_like(zero_ref)
        for wanted, row0 in zero_jobs:
            @pl.when(wanted)
            def _():
                zero_copy(row0).start()
        for wanted, row0 in zero_jobs:
            @pl.when(wanted)
            def _():
                zero_copy(row0).wait()

    def scatter_rows(h_ref):
        for r in range(tb):
            for k in range(TOP_K):
                a = r * TOP_K + k
                pltpu.make_async_copy(h_ref.at[pl.ds(r, 1)], hs_ref.at[pl.ds(slot_ref[0, 0, a], 1)],
                                      sem).start(priority=a % 2)
        for k in range(TOP_K):
            pltpu.make_async_copy(h_ref, hs_ref.at[pl.ds(0, tb)], sem).wait()

    @pl.when(i < nb_a)
    def _():
        scatter_rows(ha_ref)

    @pl.when(i >= nb_a)
    def _():
        scatter_rows(hb_ref)


def _dispatch(ha, hb, slots3, pad_edges, n_slots, tb, tm):
    d = ha.shape[1]
    nb_a, nb_b = ha.shape[0] // tb, hb.shape[0] // tb
    nb = nb_a + nb_b
    E = pad_edges.shape[0] - 1
    grid_spec = pltpu.PrefetchScalarGridSpec(
        num_scalar_prefetch=1, grid=(nb,),
        in_specs=[pl.BlockSpec((1, 1, tb * TOP_K), lambda i, pe: (i, 0, 0), memory_space=pltpu.SMEM),
                  pl.BlockSpec((tb, d), lambda i, pe: (jnp.minimum(i, nb_a - 1), 0)),
                  pl.BlockSpec((tb, d), lambda i, pe: (jnp.maximum(i - nb_a, 0), 0))],
        out_specs=pl.BlockSpec(memory_space=pl.ANY),
        scratch_shapes=[pltpu.VMEM((tm, d), ha.dtype), pltpu.SemaphoreType.DMA, pltpu.SemaphoreType.DMA])
    return pl.pallas_call(
        functools.partial(_dispatch_kernel, tb=tb, tm=tm, E=E, nb_a=nb_a), grid_spec=grid_spec,
        out_shape=jax.ShapeDtypeStruct((n_slots, d), ha.dtype),
        compiler_params=_cp(("arbitrary",)))(pad_edges, slots3, ha, hb)


def _expert_kernel(be_ref, na_ref, x_ref, wgu_ref, bgu_ref, wdn_ref, bdn_ref, y_ref, wgu_s, wdn_s, *, DE):
    i = pl.program_id(0)
    active = i < na_ref[0]
    first = jnp.logical_or(i == 0, be_ref[i] != be_ref[jnp.maximum(i - 1, 0)])

    @pl.when(jnp.logical_and(active, first))
    def _():
        wgu_s[...] = wgu_ref[...].astype(BF16)
        wdn_s[...] = wdn_ref[...].astype(BF16)

    @pl.when(active)
    def _():
        gu = jnp.dot(x_ref[...].astype(BF16), wgu_s[...], preferred_element_type=F32) + bgu_ref[...]
        gt = jnp.minimum(gu[:, :DE], SWIGLU_LIMIT)
        up = jnp.clip(gu[:, DE:], -SWIGLU_LIMIT, SWIGLU_LIMIT)
        act = (up + 1.0) * gt * jax.nn.sigmoid(SWIGLU_ALPHA * gt)
        y_ref[...] = jnp.dot(act.astype(BF16), wdn_s[...], preferred_element_type=F32) + bdn_ref[...]

    @pl.when(jnp.logical_not(active))
    def _():
        y_ref[...] = jnp.zeros_like(y_ref)


def _experts(hs, blk_e, n_active, w_gu, b_gu, w_dn, b_dn, layer, tm):
    n_slots, d = hs.shape
    E, _, de2 = w_gu.shape[1:]
    de = de2 // 2
    grid_spec = pltpu.PrefetchScalarGridSpec(
        num_scalar_prefetch=2, grid=(n_slots // tm,),
        in_specs=[pl.BlockSpec((tm, d), lambda i, be, na: (jnp.minimum(i, na[0] - 1), 0)),
                  pl.BlockSpec((None, None, d, de2), lambda i, be, na: (layer, be[i], 0, 0)),
                  pl.BlockSpec((None, None, 1, de2), lambda i, be, na: (layer, be[i], 0, 0)),
                  pl.BlockSpec((None, None, de, d), lambda i, be, na: (layer, be[i], 0, 0)),
                  pl.BlockSpec((None, None, 1, d), lambda i, be, na: (layer, be[i], 0, 0))],
        out_specs=pl.BlockSpec((tm, d), lambda i, be, na: (i, 0)),
        scratch_shapes=[pltpu.VMEM((d, de2), BF16), pltpu.VMEM((de, d), BF16)])
    L = w_gu.shape[0]
    return pl.pallas_call(
        functools.partial(_expert_kernel, DE=de), grid_spec=grid_spec,
        out_shape=jax.ShapeDtypeStruct((n_slots, d), F32),
        compiler_params=_cp(("arbitrary",), VMEM_LIMIT_LARGE))(
            blk_e, n_active, hs, w_gu, b_gu.reshape(L, E, 1, de2), w_dn, b_dn.reshape(L, E, 1, d))


def _combine_kernel(slotc_ref, slotn_ref, tg_ref, x1_ref, g2_ref, scf_ref, shf_ref, nwf_ref, yp_ref, o_ref,
                    buf, sem, *, tb):
    i = pl.program_id(0)
    slot = i % 2

    def gather_starts(slot_ref, sl):
        for r in range(tb):
            for k in range(TOP_K):
                a = r * TOP_K + k
                pltpu.make_async_copy(yp_ref.at[pl.ds(slot_ref[0, 0, a], 1)], buf.at[sl, k, pl.ds(r, 1)],
                                      sem.at[sl]).start(priority=a % 2)

    @pl.when(i == 0)
    def _():
        gather_starts(slotc_ref, 0)

    @pl.when(i + 1 < pl.num_programs(0))
    def _():
        gather_starts(slotn_ref, 1 - slot)

    for k in range(TOP_K):
        pltpu.make_async_copy(yp_ref.at[pl.ds(0, tb)], buf.at[slot, k], sem.at[slot]).wait()
    tg = tg_ref[...]
    moe = tg[:, 0:1] * buf[slot, 0]
    for k in range(1, TOP_K):
        moe = moe + tg[:, k:k + 1] * buf[slot, k]
    x2 = x1_ref[...] + g2_ref[...] * moe
    o_ref[...] = _rms(x2) * nwf_ref[...] * (1.0 + scf_ref[...]) + shf_ref[...]


def _combine_final(y_pad, slots3, first_block, tg, x1, grp, norm_final_w):
    n, d = x1.shape
    tb = grp.tm
    nb = n // tb
    rf, bpg = grp.modf3.shape[1], grp.bpg
    fspec = lambda j: pl.BlockSpec((None, rf, d), lambda i: (i // bpg, 0, j))
    sspec = lambda f: pl.BlockSpec((1, 1, tb * TOP_K), lambda i: (first_block + f(i), 0, 0),
                                   memory_space=pltpu.SMEM)
    return pl.pallas_call(
        functools.partial(_combine_kernel, tb=tb), grid=(nb,),
        in_specs=[sspec(lambda i: i), sspec(lambda i: jnp.minimum(i + 1, nb - 1)),
                  pl.BlockSpec((tb, LANE), lambda i: (i, 0)),
                  pl.BlockSpec((tb, d), lambda i: (i, 0)),
                  grp.spec(5, d), fspec(1), fspec(0),
                  pl.BlockSpec((1, d), lambda i: (0, 0)),
                  pl.BlockSpec(memory_space=pl.ANY)],
        out_specs=pl.BlockSpec((tb, d), lambda i: (i, 0)),
        out_shape=jax.ShapeDtypeStruct((n, d), F32),
        scratch_shapes=[pltpu.VMEM((2, TOP_K, tb, d), F32), pltpu.SemaphoreType.DMA((2,))],
        compiler_params=_cp(("arbitrary",)))(
            slots3, slots3, tg, x1, grp.mod3, grp.modf3, grp.modf3, norm_final_w.reshape(1, d), y_pad)


def _route(top_e, E, tm):
    n = top_e.shape[0]
    hit = top_e[:, :, None] == jnp.arange(E, dtype=jnp.int32)
    tok_oh = jnp.sum(hit.astype(jnp.int32), axis=1)
    csum = jnp.cumsum(tok_oh, axis=0)
    counts = csum[-1]
    padded = (counts + tm - 1) // tm * tm
    pad_end = jnp.cumsum(padded)
    slots = jnp.sum(jnp.where(hit, (csum - tok_oh + (pad_end - padded)[None, :])[:, None, :], 0), axis=2)
    n_blocks = -(-(n * TOP_K) // tm) + E
    blk_start = jnp.arange(n_blocks, dtype=jnp.int32) * tm
    blk_e = jnp.minimum(jnp.sum((pad_end[None, :] <= blk_start[:, None]).astype(jnp.int32), axis=1), E - 1)
    n_active = (pad_end[-1] // tm).reshape(1)
    pad_edges = jnp.concatenate([jnp.zeros((1,), pad_end.dtype), pad_end])
    return (slots.astype(jnp.int32), pad_edges.astype(jnp.int32), blk_e.astype(jnp.int32),
            n_active.astype(jnp.int32), n_blocks * tm)


def _rope_tables(pos, rope):
    half = rope // 2
    inv = ROPE_THETA ** (-jnp.arange(half, dtype=F32) / half)
    ang = pos.astype(F32)[:, None] * inv[None, :]
    cos, sin = jnp.cos(ang), jnp.sin(ang)
    return jnp.concatenate([cos, cos], axis=-1), jnp.concatenate([sin, sin], axis=-1)


def _rotate_cols(w, rope):
    k, n = w.shape
    w3 = w.reshape(k, n // rope, rope)
    half = rope // 2
    return jnp.concatenate([-w3[..., half:], w3[..., :half]], axis=-1).reshape(k, n)


def kernel(x_prompt, x_sample, c_prompt, c_sample, cache_kv, cache_pe, state_ssm, state_conv, page_table, norm_mix_w, norm_ffn_w, w_ada, b_ada, w_in, conv_w, A_log, dt_bias, gdn_norm_w, w_o_gdn, q_norm_w, kv_norm_w, w_uq, w_uk, w_uv, w_o_mla, w_out, w_router, b_router, w_gu, b_gu, w_dn, b_dn, w_ada_final, b_ada_final, norm_final_w):
    B, T, D = x_prompt.shape
    Bs, Ts, _ = x_sample.shape
    assert Ts == 1
    depth = w_in.shape[0]
    H, DK, DV = state_ssm.shape[2:]
    QK = H * DK
    CONV = state_conv.shape[3]
    assert CONV == 2 * QK + H * DV and H == SUBLANES
    QL = q_norm_w.shape[1]
    HM, KVL, NOPE = w_uk.shape[1:]
    VH = w_uv.shape[3]
    ROPE = cache_pe.shape[3]
    E = w_router.shape[2]
    scale = float(NOPE + ROPE) ** -0.5
    n_p, n_s = B * T, Bs * Ts
    past_len = page_table.shape[1] * cache_kv.shape[2]
    cache_pe_t = jnp.swapaxes(cache_pe, 2, 3)

    c_all = jnp.concatenate([c_prompt, c_sample], axis=0)
    modf = _matmul(c_all, w_ada_final, F32, c_all.shape[0], 1024, b_ada_final)
    tm_p = min(512, T)
    tb = 128
    cos_p, sin_p = _rope_tables(jnp.arange(T), ROPE)
    cos_s, sin_s = _rope_tables(past_len + jnp.arange(Ts), ROPE)

    hp = x_prompt.reshape(n_p, D)
    hs = x_sample.reshape(n_s, D)
    outs = {k: [] for k in ('kv_p', 'pe_p', 'ssm_p', 'conv_p', 'kv_s', 'pe_s', 'ssm_s', 'conv_s')}
    for l in range(depth):
        mod = _matmul(c_all, w_ada[l], F32, c_all.shape[0], 1024, b_ada[l])
        last = l == depth - 1
        grp_p = _Group(mod[:B].reshape(B, 1, 6 * D), modf[:B].reshape(B, 1, 2 * D), tm_p, T // tm_p)
        grp_s = _Group(mod[B:].reshape(1, Bs, 6 * D), modf[B:].reshape(1, Bs, 2 * D), Bs, 1)
        grp_pc = _Group(grp_p.mod3, grp_p.modf3, tb, T // tb)

        offs = [0]
        for s in (CONV, H * DV, H, H, QL, KVL, ROPE, D, D):
            offs.append(offs[-1] + s)
        wi = w_in[l]
        seg = lambda i: wi[:, offs[i]:offs[i + 1]]
        w_qkv = seg(0).astype(BF16)
        w_z = seg(1).astype(BF16)
        w_gab = jnp.concatenate([seg(7), seg(8)], axis=1).astype(BF16)
        n_small = QL + KVL + 2 * ROPE
        ba_col = -(-n_small // LANE)
        w_small = jnp.concatenate(
            [seg(4), seg(5), seg(6), _rotate_cols(seg(6), ROPE), jnp.zeros((D, ba_col * LANE - n_small), F32),
             seg(2), jnp.zeros((D, 8 - H), F32), seg(3), jnp.zeros((D, LANE - 8 - H), F32)], axis=1).astype(BF16)
        cw_t = conv_w[l].T
        gparams = jnp.zeros((2, LANE), F32).at[0, 8:8 + H].set(A_log[l]).at[1, 8:8 + H].set(dt_bias[l])
        wq = w_uq[l].reshape(QL, HM, NOPE + ROPE)
        wq_pe = wq[:, :, NOPE:].reshape(QL, HM * ROPE)
        wq_all = jnp.concatenate([wq[:, :, :NOPE].reshape(QL, HM * NOPE), wq_pe], axis=1).astype(BF16)
        wq_rot = _rotate_cols(wq_pe, ROPE).astype(BF16)
        wuk_all = jnp.transpose(w_uk[l], (1, 0, 2)).reshape(KVL, HM * NOPE).astype(BF16)
        wuv_all = jnp.transpose(w_uv[l], (1, 0, 2)).reshape(KVL, HM * VH).astype(BF16)
        qnw = q_norm_w[l].reshape(1, QL)
        kvnw = kv_norm_w[l].reshape(1, KVL)
        wog = w_o_gdn[l].astype(BF16)
        wom = w_o_mla[l].astype(BF16)
        wout = w_out[l].astype(BF16)
        wr = jnp.pad(w_router[l], ((0, 0), (0, LANE - E)))
        br = jnp.pad(b_router[l], (0, LANE - E)).reshape(1, LANE)
        nfw = norm_ffn_w[l].reshape(1, D)

        h1 = _norm_mod(hp, norm_mix_w[l], grp_p, 1, 0, BF16)
        qkv_p = _matmul(h1, w_qkv, BF16, 1024, 512)
        z_p = _matmul(h1, w_z, BF16, 1024, 512)
        gab_p = _matmul(h1, w_gab, BF16, 1024, 512)
        small_p = _matmul(h1, w_small, F32, 1024, w_small.shape[1])
        og_p, ssm_p = _gdn_prompt(qkv_p, z_p, small_p, ba_col, cw_t, gparams, gdn_norm_w[l], B, T, H, DK, DV)
        q_p, k_p, v_p, lat_p, pe_p = _mla_prep_prompt(small_p, cos_p, sin_p, qnw, kvnw, wq_all, wq_rot, wuk_all,
                                                      wuv_all, T, tm_p, HM, NOPE, ROPE, VH, QL, KVL, scale)
        om_p = _flash_attention(q_p, k_p, v_p, B, T, tm_p)
        x1_p, h2_p, te_p, tg_p = _post_mixer(hp, og_p, om_p, gab_p, grp_p, wog, wom, wout, nfw, wr, br, E)
        outs['kv_p'].append(lat_p.reshape(B, T, KVL))
        outs['pe_p'].append(pe_p.reshape(B, T, ROPE))
        outs['ssm_p'].append(ssm_p)
        outs['conv_p'].append(qkv_p.reshape(B, T, CONV)[:, T - (CONV_WIDTH - 1):, :].astype(F32))

        h1s = _norm_mod(hs, norm_mix_w[l], grp_s, 1, 0, BF16)
        qkv_s = _matmul(h1s, w_qkv, F32, Bs, 512)
        z_s = _matmul(h1s, w_z, F32, Bs, 512)
        gab_s = _matmul(h1s, w_gab, BF16, Bs, 512)
        small_s = _matmul(h1s, w_small, F32, Bs, w_small.shape[1])
        og_s, ssm_s, conv_s = _gdn_decode(qkv_s, z_s, small_s, ba_col, state_conv[l], state_ssm[l], cw_t, gparams,
                                          gdn_norm_w[l], H, DK, DV)
        qn_s, qp_s, lat_s, pe_s = _mla_prep_sample(small_s, cos_s, sin_s, qnw, kvnw, wq_all, wq_rot,
                                                   HM, NOPE, ROPE, QL, KVL, scale)
        q_lat = jnp.transpose(_q_latent(qn_s, w_uk[l], scale), (1, 0, 2))
        o_lat = _paged_attention(q_lat, qp_s.reshape(n_s, HM, ROPE), lat_s, pe_s, cache_kv, cache_pe_t, page_table, l)
        om_s = _o_value(jnp.transpose(o_lat, (1, 0, 2)), w_uv[l])
        x1_s, h2_s, te_s, tg_s = _post_mixer(hs, og_s, om_s, gab_s, grp_s, wog, wom, wout, nfw, wr, br, E)
        outs['kv_s'].append(lat_s.reshape(Bs, Ts, KVL))
        outs['pe_s'].append(pe_s.reshape(Bs, Ts, ROPE))
        outs['ssm_s'].append(ssm_s)
        outs['conv_s'].append(conv_s)

        tm_e = 256
        top_e = jnp.concatenate([te_p[:, :TOP_K], te_s[:, :TOP_K]], axis=0)
        slots, pad_edges, blk_e, n_active, n_slots = _route(top_e, E, tm_e)
        slots3 = slots.reshape((n_p + n_s) // tb, 1, tb * TOP_K)
        h_sorted = _dispatch(h2_p, h2_s, slots3, pad_edges, n_slots, tb, tm_e)
        y_pad = _experts(h_sorted, blk_e, n_active, w_gu, b_gu, w_dn, b_dn, l, tm_e)
        if not last:
            raise NotImplementedError("stacked layers need an un-normalised residual output")
        assert grp_s.tm == tb
        hp = _combine_final(y_pad, slots3, 0, tg_p, x1_p, grp_pc, norm_final_w)
        hs = _combine_final(y_pad, slots3, n_p // tb, tg_s, x1_s, grp_s, norm_final_w)

    st = lambda k: jnp.stack(outs[k])
    return (hp.reshape(B, T, D), hs.reshape(Bs, Ts, D),
            st('kv_p'), st('pe_p'), st('ssm_p'), st('conv_p'),
            st('kv_s'), st('pe_s'), st('ssm_s'), st('conv_s'))
```
